```python
import jax, jax.numpy as jnp
from jax import lax
import numpy as np

D_MODEL = 1024
BATCH = 8
SEQ = 2048
DEPTH = 1

D_MIX = D_MODEL
RET_WIDTH = D_MIX // 2
POOL_WIDTH = D_MIX - RET_WIDTH
RET_HEADS = 4
RET_HEAD_DIM = RET_WIDTH // RET_HEADS
RET_CHUNK = 128
ROPE_BASE = 10000.0
POOL_WINDOWS = (2, 4, 8, 16)
POOL_GROUPS = len(POOL_WINDOWS)
POOL_GROUP_DIM = POOL_WIDTH // POOL_GROUPS
IN_COLS = 4 * RET_WIDTH + POOL_WIDTH
N_GROUPS = 4
EXPERTS_PER_GROUP = 8
N_EXPERTS = N_GROUPS * EXPERTS_PER_GROUP
TOP_K_INNER = 2
D_EXPERT = 256
ALPHA = (2.0 * DEPTH) ** 0.25
BETA = (8.0 * DEPTH) ** -0.25
LN_EPS = 1e-5

kernel_name = "hymba_retention_pool_hiermoe_deepnorm_adaln"


def _layer_norm(x, w=None, b=None):
    xf = x.astype(jnp.float32)
    mu = jnp.mean(xf, axis=-1, keepdims=True)
    var = jnp.mean(jnp.square(xf - mu), axis=-1, keepdims=True)
    y = (xf - mu) * lax.rsqrt(var + LN_EPS)
    if w is not None:
        y = y * w.astype(jnp.float32) + b.astype(jnp.float32)
    return y.astype(x.dtype)


def _rotary(t, cos, sin):
    t1, t2 = jnp.split(t, 2, axis=-1)
    return jnp.concatenate([t1 * cos - t2 * sin, t1 * sin + t2 * cos], axis=-1)


def _retention(q, k, v):
    B, S, H, Dh = q.shape
    nc = S // RET_CHUNK
    lg = jnp.log1p(-jnp.exp2(-5.0 - jnp.arange(H, dtype=jnp.float32)))
    qc = q.reshape(B, nc, RET_CHUNK, H, Dh)
    kc = k.reshape(B, nc, RET_CHUNK, H, Dh)
    vc = v.reshape(B, nc, RET_CHUNK, H, Dh)
    i = jnp.arange(RET_CHUNK, dtype=jnp.float32)
    rel = i[:, None] - i[None, :]
    dmat = jnp.where(rel[None] >= 0, jnp.exp(jnp.maximum(rel, 0.0)[None] * lg[:, None, None]), 0.0)
    scores = jnp.einsum('bnchd,bnmhd->bnhcm', qc, kc) * dmat[None, None]
    intra = jnp.einsum('bnhcm,bnmhd->bnchd', scores, vc)
    k_decay = jnp.exp((RET_CHUNK - 1.0 - i)[None, :] * lg[:, None])
    kv = jnp.einsum('bnmhd,hm,bnmhe->nbhde', kc, k_decay, vc)
    chunk_decay = jnp.exp(RET_CHUNK * lg)[None, :, None, None]

    def step(state, kv_n):
        return state * chunk_decay + kv_n, state

    _, prev = lax.scan(step, jnp.zeros((B, H, Dh, Dh), jnp.float32), kv)
    q_decay = jnp.exp((i + 1.0)[:, None] * lg[None, :])
    cross = jnp.einsum('bnchd,nbhde,ch->bnche', qc, prev, q_decay)
    return (intra + cross).reshape(B, S, H, Dh)


def _multi_scale_pool(p):
    B, S, _ = p.shape
    pf = p.astype(jnp.float32)
    cs = jnp.concatenate([jnp.zeros((B, 1, POOL_WIDTH), jnp.float32), jnp.cumsum(pf, axis=1)], axis=1)
    t = jnp.arange(S)
    outs = []
    for g, w in enumerate(POOL_WINDOWS):
        lo_c, hi_c = g * POOL_GROUP_DIM, (g + 1) * POOL_GROUP_DIM
        csg = cs[:, :, lo_c:hi_c]
        upper = csg[:, 1:]
        lower = jnp.pad(csg[:, :S - w + 1], ((0, 0), (w - 1, 0), (0, 0)))
        count = jnp.minimum(t + 1, w).astype(jnp.float32)[None, :, None]
        outs.append((upper - lower) / count - pf[:, :, lo_c:hi_c])
    return jnp.stack(outs, axis=2)


def _hier_moe(u, w_group, b_group, w_router, b_router, w1, w3, w2):
    T = u.shape[0]
    gp = jax.nn.softmax((u @ w_group + b_group).astype(jnp.float32), axis=-1)
    g_prob, g_idx = lax.top_k(gp, 1)
    el = (u @ w_router + b_router).astype(jnp.float32).reshape(T, N_GROUPS, EXPERTS_PER_GROUP)
    sel = jnp.take_along_axis(el, g_idx[:, :, None], axis=1)[:, 0]
    top_v, top_i = lax.top_k(sel, TOP_K_INNER)
    cw = g_prob * jax.nn.softmax(top_v, axis=-1)
    within = jnp.einsum('tk,tke->te', cw, jax.nn.one_hot(top_i, EXPERTS_PER_GROUP, dtype=jnp.float32))
    gates = (jax.nn.one_hot(g_idx[:, 0], N_GROUPS, dtype=jnp.float32)[:, :, None] * within[:, None, :]).astype(u.dtype)
    y = jnp.zeros_like(u)
    for grp in range(N_GROUPS):
        h = jax.nn.silu(jnp.einsum('td,edf->tef', u, w1[grp])) * jnp.einsum('td,edf->tef', u, w3[grp])
        h = h * gates[:, grp, :, None]
        y = y + jnp.einsum('tef,efd->td', h, w2[grp])
    return y


def setup_inputs(seed: int = 0) -> dict:
    key = jax.random.key(seed)
    ks = jax.random.split(key, 24)
    D = D_MODEL
    nrm = jax.random.normal
    col_gain = jnp.concatenate([
        jnp.ones((2 * RET_WIDTH,), jnp.float32),
        jnp.full((RET_WIDTH,), BETA, jnp.float32),
        jnp.ones((RET_WIDTH,), jnp.float32),
        jnp.full((POOL_WIDTH,), BETA, jnp.float32),
    ])
    return {
        "x": nrm(ks[0], (BATCH, SEQ, D), jnp.float32),
        "c": nrm(ks[1], (BATCH, D), jnp.float32),
        "positions": jnp.broadcast_to(jnp.arange(SEQ, dtype=jnp.int32), (BATCH, SEQ)),
        "w_ada": nrm(ks[2], (DEPTH, D, 6 * D), jnp.float32) * D ** -0.5,
        "b_ada": 0.02 * nrm(ks[3], (DEPTH, 6 * D), jnp.float32),
        "w_in": nrm(ks[4], (DEPTH, D, IN_COLS), jnp.float32) * D ** -0.5 * col_gain,
        "ret_gn_w": 1.0 + 0.02 * nrm(ks[5], (DEPTH, RET_WIDTH), jnp.float32),
        "w_pool": nrm(ks[6], (DEPTH, POOL_GROUPS, POOL_GROUP_DIM, POOL_GROUP_DIM), jnp.float32) * POOL_GROUP_DIM ** -0.5,
        "pool_scale": 1.0 + 0.02 * nrm(ks[7], (DEPTH, POOL_WIDTH), jnp.float32),
        "w_out": nrm(ks[8], (DEPTH, D_MIX, D), jnp.float32) * D_MIX ** -0.5 * BETA,
        "ln1_w": 1.0 + 0.02 * nrm(ks[9], (DEPTH, D), jnp.float32),
        "ln1_b": 0.02 * nrm(ks[10], (DEPTH, D), jnp.float32),
        "w_group": nrm(ks[11], (DEPTH, D, N_GROUPS), jnp.float32) * D ** -0.5,
        "b_group": 0.01 * nrm(ks[12], (DEPTH, N_GROUPS), jnp.float32),
        "w_router": nrm(ks[13], (DEPTH, D, N_EXPERTS), jnp.float32) * D ** -0.5,
        "b_router": 0.01 * nrm(ks[14], (DEPTH, N_EXPERTS), jnp.float32),
        "w1": nrm(ks[15], (DEPTH, N_GROUPS, EXPERTS_PER_GROUP, D, D_EXPERT), jnp.float32) * D ** -0.5 * BETA,
        "w3": nrm(ks[16], (DEPTH, N_GROUPS, EXPERTS_PER_GROUP, D, D_EXPERT), jnp.float32) * D ** -0.5 * BETA,
        "w2": nrm(ks[17], (DEPTH, N_GROUPS, EXPERTS_PER_GROUP, D_EXPERT, D), jnp.float32) * D_EXPERT ** -0.5 * BETA,
        "ln2_w": 1.0 + 0.02 * nrm(ks[18], (DEPTH, D), jnp.float32),
        "ln2_b": 0.02 * nrm(ks[19], (DEPTH, D), jnp.float32),
    }


def reference(x, c, positions, w_ada, b_ada, w_in, ret_gn_w, w_pool, pool_scale, w_out,
              ln1_w, ln1_b, w_group, b_group, w_router, b_router, w1, w3, w2, ln2_w, ln2_b):
    B, S, D = x.shape
    inv_freq = ROPE_BASE ** (-jnp.arange(0, RET_HEAD_DIM, 2, dtype=jnp.float32) / RET_HEAD_DIM)
    ang = positions.astype(jnp.float32)[..., None] * inv_freq
    cos, sin = jnp.cos(ang)[:, :, None, :], jnp.sin(ang)[:, :, None, :]
    c_act = jax.nn.silu(c)
    for l in range(DEPTH):
        ada = (c_act @ w_ada[l] + b_ada[l])[:, None, :]
        shift1, scale1, gate1, shift2, scale2, gate2 = jnp.split(ada, 6, axis=-1)

        u = _layer_norm(x) * (1.0 + scale1) + shift1
        proj = u @ w_in[l]
        q, k, v, g, p = jnp.split(proj, [RET_WIDTH, 2 * RET_WIDTH, 3 * RET_WIDTH, 4 * RET_WIDTH], axis=-1)
        hs = (B, S, RET_HEADS, RET_HEAD_DIM)
        qr = _rotary(q.reshape(hs).astype(jnp.float32), cos, sin)
        kr = _rotary(k.reshape(hs).astype(jnp.float32), cos, sin) * RET_HEAD_DIM ** -0.5
        r = _retention(qr, kr, v.reshape(hs).astype(jnp.float32))
        r_mu = jnp.mean(r, axis=-1, keepdims=True)
        r_var = jnp.mean(jnp.square(r - r_mu), axis=-1, keepdims=True)
        r = ((r - r_mu) * lax.rsqrt(r_var + LN_EPS)).reshape(B, S, RET_WIDTH) * ret_gn_w[l]
        ret_out = jax.nn.silu(g) * r.astype(x.dtype)

        pooled = _multi_scale_pool(p).astype(x.dtype)
        pool_out = jnp.einsum('bsgc,gcd->bsgd', pooled, w_pool[l]).reshape(B, S, POOL_WIDTH) * pool_scale[l]

        mix = jnp.concatenate([ret_out, pool_out], axis=-1) @ w_out[l]
        x = _layer_norm(ALPHA * x + gate1 * mix, ln1_w[l], ln1_b[l])

        u2 = _layer_norm(x) * (1.0 + scale2) + shift2
        y = _hier_moe(u2.reshape(B * S, D), w_group[l], b_group[l], w_router[l], b_router[l],
                      w1[l], w3[l], w2[l]).reshape(B, S, D)
        x = _layer_norm(ALPHA * x + gate2 * y, ln2_w[l], ln2_b[l])
    return x
```

```python
import functools
import math

import jax
import jax.numpy as jnp
from jax import lax
from jax.experimental import pallas as pl
from jax.experimental.pallas import tpu as pltpu

D_MODEL = 1024
RET_WIDTH = 512
RET_HEADS = 4
HEAD_DIM = 128
POOL_WIDTH = 512
POOL_WINDOWS = (2, 4, 8, 16)
POOL_GROUP_DIM = 128
IN_COLS = 4 * RET_WIDTH + POOL_WIDTH
N_GROUPS = 4
EXPERTS_PER_GROUP = 8
N_EXPERTS = 32
D_EXPERT = 256
DEPTH = 1
ALPHA = (2.0 * DEPTH) ** 0.25
LN_EPS = 1e-5
ROPE_BASE = 10000.0

LANES = 128
POOL_HALO = 16
TS = 256
TM = 256
TF = 256
N_FILL = 2 * N_EXPERTS
VMEM_LIMIT = 56 * 1024 * 1024

_LOG_GAMMA = tuple(math.log1p(-(2.0 ** (-5.0 - h))) for h in range(RET_HEADS))


def _ln(x):
    mu = jnp.mean(x, axis=-1, keepdims=True)
    xc = x - mu
    var = jnp.mean(xc * xc, axis=-1, keepdims=True)
    return xc * lax.rsqrt(var + LN_EPS)


def _bf(x):
    return x.astype(jnp.bfloat16)


def _ada_kernel(c_ref, w_ref, b_ref, o_ref):
    c = c_ref[...]
    ca = c * jax.nn.sigmoid(c)
    o_ref[...] = jnp.dot(ca, w_ref[...], precision=lax.Precision.HIGHEST,
                         preferred_element_type=jnp.float32) + b_ref[...]


def _ada(c, w_ada, b_ada):
    B, D = c.shape
    n = w_ada.shape[1]
    return pl.pallas_call(
        _ada_kernel,
        grid=(n // D,),
        in_specs=[pl.BlockSpec((B, D), lambda j: (0, 0)),
                  pl.BlockSpec((D, D), lambda j: (0, j)),
                  pl.BlockSpec((1, D), lambda j: (0, j))],
        out_specs=pl.BlockSpec((B, D), lambda j: (0, j)),
        out_shape=jax.ShapeDtypeStruct((B, n), jnp.float32),
        compiler_params=pltpu.CompilerParams(dimension_semantics=("arbitrary",), vmem_limit_bytes=VMEM_LIMIT),
        name="ada",
    )(c, w_ada, b_ada.reshape(1, n))


def _mix_kernel(x_ref, pos_ref, ada_ref, rope_ref, win_ref, wout_ref, wpool_ref, gnw_ref, pscale_ref,
                ln1w_ref, ln1b_ref, wrt_ref, brt_ref,
                x1_ref, u2_ref, route_ref, cnt_ref,
                state_ref, halo_ref, dmat_ref, qdec_ref, kdec_ref, carry_ref, cat_ref):
    b = pl.program_id(0)
    s = pl.program_id(1)

    @pl.when((b == 0) & (s == 0))
    def _init_tables():
        ri = lax.broadcasted_iota(jnp.int32, (TS, TS), 0)
        ci = lax.broadcasted_iota(jnp.int32, (TS, TS), 1)
        rel = (ri - ci).astype(jnp.float32)
        for h in range(RET_HEADS):
            dmat_ref[h] = jnp.where(rel >= 0.0, jnp.exp(jnp.maximum(rel, 0.0) * _LOG_GAMMA[h]), 0.0)
        row = lax.broadcasted_iota(jnp.int32, (TS, RET_WIDTH), 0).astype(jnp.float32)
        lane = lax.broadcasted_iota(jnp.int32, (TS, RET_WIDTH), 1)
        lg = jnp.full((TS, RET_WIDTH), _LOG_GAMMA[0], jnp.float32)
        for h in range(1, RET_HEADS):
            lg = jnp.where(lane >= h * HEAD_DIM, _LOG_GAMMA[h], lg)
        qdec_ref[...] = jnp.exp((row + 1.0) * lg)
        kdec_ref[...] = jnp.exp((TS - 1.0 - row) * lg)
        carry_ref[...] = jnp.zeros_like(carry_ref)

    @pl.when(s == 0)
    def _init_carries():
        state_ref[...] = jnp.zeros_like(state_ref)
        halo_ref[...] = jnp.zeros_like(halo_ref)

    ada = ada_ref[...]
    shift1, scale1, gate1 = ada[0:1], ada[1:2], ada[2:3]
    shift2, scale2 = ada[3:4], ada[4:5]

    x = x_ref[...]
    u = _bf(_ln(x) * (1.0 + scale1) + shift1)

    ang = pos_ref[...].astype(jnp.float32) * rope_ref[0:1, :]
    cos_t = jnp.cos(ang)
    sin_t = jnp.sin(ang) * rope_ref[1:2, :]

    q = jnp.dot(u, win_ref[:, 0:RET_WIDTH], preferred_element_type=jnp.float32)
    k = jnp.dot(u, win_ref[:, RET_WIDTH:2 * RET_WIDTH], preferred_element_type=jnp.float32)
    v = jnp.dot(u, win_ref[:, 2 * RET_WIDTH:3 * RET_WIDTH], preferred_element_type=jnp.float32)
    g = jnp.dot(u, win_ref[:, 3 * RET_WIDTH:4 * RET_WIDTH], preferred_element_type=jnp.float32)
    p = jnp.dot(u, win_ref[:, 4 * RET_WIDTH:IN_COLS], preferred_element_type=jnp.float32)

    gnw = gnw_ref[...]
    for h in range(RET_HEADS):
        sl = slice(h * HEAD_DIM, (h + 1) * HEAD_DIM)
        qh, kh, vh = q[:, sl], k[:, sl], v[:, sl]
        qr = qh * cos_t + pltpu.roll(qh, HEAD_DIM // 2, 1) * sin_t
        kr = (kh * cos_t + pltpu.roll(kh, HEAD_DIM // 2, 1) * sin_t) * (HEAD_DIM ** -0.5)
        vb = _bf(vh)
        sc = lax.dot_general(_bf(qr), _bf(kr), (((1,), (1,)), ((), ())), preferred_element_type=jnp.float32)
        intra = jnp.dot(_bf(sc * dmat_ref[h]), vb, preferred_element_type=jnp.float32)
        st = state_ref[h]
        cross = jnp.dot(_bf(qr * qdec_ref[:, sl]), _bf(st), preferred_element_type=jnp.float32)
        kv = lax.dot_general(_bf(kr * kdec_ref[:, sl]), vb, (((0,), (0,)), ((), ())),
                             preferred_element_type=jnp.float32)
        state_ref[h] = st * math.exp(TS * _LOG_GAMMA[h]) + kv
        r = _ln(intra + cross) * gnw[:, sl]
        gh = g[:, sl]
        cat_ref[:, sl] = _bf(gh * jax.nn.sigmoid(gh) * r)

    pext = jnp.concatenate([halo_ref[...], p], axis=0)
    halo_ref[...] = p[TS - POOL_HALO:, :]
    t_abs = (s * TS + lax.broadcasted_iota(jnp.int32, (TS, 1), 0) + 1).astype(jnp.float32)
    pscale = pscale_ref[...]
    for grp, w in enumerate(POOL_WINDOWS):
        sl = slice(grp * POOL_GROUP_DIM, (grp + 1) * POOL_GROUP_DIM)
        acc = pext[:, sl]
        shift = 1
        while shift < w:
            acc = acc + pltpu.roll(acc, shift, 0)
            shift *= 2
        pooled = acc[POOL_HALO:, :] / jnp.minimum(t_abs, float(w)) - p[:, sl]
        po = jnp.dot(_bf(pooled), wpool_ref[grp], preferred_element_type=jnp.float32) * pscale[:, sl]
        cat_ref[:, RET_WIDTH + grp * POOL_GROUP_DIM:RET_WIDTH + (grp + 1) * POOL_GROUP_DIM] = _bf(po)

    mix = jnp.dot(cat_ref[...], wout_ref[...], preferred_element_type=jnp.float32)
    x1 = _ln(ALPHA * x + gate1 * mix) * ln1w_ref[...] + ln1b_ref[...]
    x1_ref[...] = x1
    u2 = _ln(x1) * (1.0 + scale2) + shift2
    u2_ref[...] = u2

    logits = jnp.dot(u2, wrt_ref[...], precision=lax.Precision.HIGHEST,
                     preferred_element_type=jnp.float32) + brt_ref[...]
    lane = lax.broadcasted_iota(jnp.int32, (TS, LANES), 1)
    neg = jnp.float32(-jnp.inf)
    gl = jnp.where(lane < N_GROUPS, logits, neg)
    gmax = jnp.max(gl, axis=-1, keepdims=True)
    gidx = jnp.min(jnp.where(gl == gmax, lane, LANES), axis=-1, keepdims=True)
    gprob = 1.0 / jnp.sum(jnp.exp(gl - gmax), axis=-1, keepdims=True)
    lo = N_GROUPS + EXPERTS_PER_GROUP * gidx
    el = jnp.where((lane >= lo) & (lane < lo + EXPERTS_PER_GROUP), logits, neg)
    m1 = jnp.max(el, axis=-1, keepdims=True)
    i1 = jnp.min(jnp.where(el == m1, lane, LANES), axis=-1, keepdims=True)
    el2 = jnp.where(lane == i1, neg, el)
    m2 = jnp.max(el2, axis=-1, keepdims=True)
    i2 = jnp.min(jnp.where(el2 == m2, lane, LANES), axis=-1, keepdims=True)
    e21 = jnp.exp(m2 - m1)
    den = 1.0 / (1.0 + e21)
    cw1 = gprob * den
    cw2 = gprob * e21 * den

    oh1 = lane == (i1 - N_GROUPS)
    oh2 = lane == (i2 - N_GROUPS)
    oh = jnp.where(oh1 | oh2, 1.0, 0.0)
    ri = lax.broadcasted_iota(jnp.int32, (TS, TS), 0)
    ci = lax.broadcasted_iota(jnp.int32, (TS, TS), 1)
    tri = _bf(jnp.where(ci < ri, 1.0, 0.0))
    before = jnp.dot(tri, _bf(oh), preferred_element_type=jnp.float32) + carry_ref[...]
    rank1 = jnp.sum(jnp.where(oh1, before, 0.0), axis=-1, keepdims=True)
    rank2 = jnp.sum(jnp.where(oh2, before, 0.0), axis=-1, keepdims=True)
    carry = carry_ref[...] + jnp.sum(oh, axis=0, keepdims=True)
    carry_ref[...] = carry
    cnt_ref[...] = jnp.broadcast_to(carry, cnt_ref.shape)

    e1f = (i1 - N_GROUPS).astype(jnp.float32)
    e2f = (i2 - N_GROUPS).astype(jnp.float32)
    route = jnp.where(lane == 0, e1f, 0.0)
    route = jnp.where(lane == 1, e2f, route)
    route = jnp.where(lane == 2, rank1, route)
    route = jnp.where(lane == 3, rank2, route)
    route = jnp.where(lane == 4, cw1, route)
    route = jnp.where(lane == 5, cw2, route)
    route_ref[...] = route


def _mix(x, positions, ada, rope, win, wout, wpool, gnw, pscale, ln1w, ln1b, wrt, brt):
    B, S, D = x.shape
    ns = S // TS
    const2 = lambda b, s: (0, 0)
    const3 = lambda b, s: (0, 0, 0)
    tile = lambda b, s: (b, s, 0)
    return pl.pallas_call(
        _mix_kernel,
        grid=(B, ns),
        in_specs=[
            pl.BlockSpec((None, TS, D), tile),
            pl.BlockSpec((None, TS, 1), tile),
            pl.BlockSpec((None, 6, D), lambda b, s: (b, 0, 0)),
            pl.BlockSpec((2, LANES), const2),
            pl.BlockSpec((D, IN_COLS), const2),
            pl.BlockSpec((D, D), const2),
            pl.BlockSpec((len(POOL_WINDOWS), POOL_GROUP_DIM, POOL_GROUP_DIM), const3),
            pl.BlockSpec((1, RET_WIDTH), const2),
            pl.BlockSpec((1, POOL_WIDTH), const2),
            pl.BlockSpec((1, D), const2),
            pl.BlockSpec((1, D), const2),
            pl.BlockSpec((D, LANES), const2),
            pl.BlockSpec((1, LANES), const2),
        ],
        out_specs=[
            pl.BlockSpec((None, TS, D), tile),
            pl.BlockSpec((None, TS, D), tile),
            pl.BlockSpec((None, TS, LANES), tile),
            pl.BlockSpec((8, LANES), const2),
        ],
        out_shape=[
            jax.ShapeDtypeStruct((B, S, D), jnp.float32),
            jax.ShapeDtypeStruct((B, S, D), jnp.float32),
            jax.ShapeDtypeStruct((B, S, LANES), jnp.float32),
            jax.ShapeDtypeStruct((8, LANES), jnp.float32),
        ],
        scratch_shapes=[
            pltpu.VMEM((RET_HEADS, HEAD_DIM, HEAD_DIM), jnp.float32),
            pltpu.VMEM((POOL_HALO, POOL_WIDTH), jnp.float32),
            pltpu.VMEM((RET_HEADS, TS, TS), jnp.float32),
            pltpu.VMEM((TS, RET_WIDTH), jnp.float32),
            pltpu.VMEM((TS, RET_WIDTH), jnp.float32),
            pltpu.VMEM((1, LANES), jnp.float32),
            pltpu.VMEM((TS, D), jnp.bfloat16),
        ],
        compiler_params=pltpu.CompilerParams(dimension_semantics=("arbitrary", "arbitrary"),
                                             vmem_limit_bytes=VMEM_LIMIT),
        name="mix",
    )(x, positions.reshape(B, S, 1), ada, rope, win, wout, wpool, gnw, pscale, ln1w, ln1b, wrt, brt)


def _row_copy(src_ref, src_row, dst_ref, dst_row, sem):
    return pltpu.make_async_copy(src_ref.at[pl.ds(src_row, 1), :], dst_ref.at[pl.ds(dst_row, 1), :], sem)


def _scatter_kernel(fill_ref, dst_ref, u2_ref, xs_ref, zero_buf, sem):
    @pl.when(pl.program_id(0) == 0)
    def _zero_padding_rows():
        zero_buf[...] = jnp.zeros_like(zero_buf)

        def fill(j):
            return pltpu.make_async_copy(zero_buf, xs_ref.at[pl.ds(pl.multiple_of(fill_ref[j], TM), TM), :], sem)

        for j in range(N_FILL):
            pl.when(fill_ref[j] >= 0)(lambda j=j: fill(j).start())
        for j in range(N_FILL):
            pl.when(fill_ref[j] >= 0)(lambda j=j: fill(j).wait())

    def issue(t, carry):
        _row_copy(u2_ref, t, xs_ref, dst_ref[0, 0, t], sem).start()
        _row_copy(u2_ref, t, xs_ref, dst_ref[0, 1, t], sem).start()
        return carry

    lax.fori_loop(0, TS, issue, 0)

    def drain(t, carry):
        _row_copy(u2_ref, t, xs_ref, dst_ref[0, 0, t], sem).wait()
        _row_copy(u2_ref, t, xs_ref, dst_ref[0, 1, t], sem).wait()
        return carry

    lax.fori_loop(0, TS, drain, 0)


def _scatter(fill_rows, dst, u2, n_rows):
    T, D = u2.shape
    grid_spec = pltpu.PrefetchScalarGridSpec(
        num_scalar_prefetch=1,
        grid=(T // TS,),
        in_specs=[pl.BlockSpec((1, 2, TS), lambda i, fr: (i, 0, 0), memory_space=pltpu.SMEM),
                  pl.BlockSpec((TS, D), lambda i, fr: (i, 0))],
        out_specs=pl.BlockSpec(memory_space=pl.ANY),
        scratch_shapes=[pltpu.VMEM((TM, D), jnp.float32), pltpu.SemaphoreType.DMA(())],
    )
    return pl.pallas_call(
        _scatter_kernel,
        grid_spec=grid_spec,
        out_shape=jax.ShapeDtypeStruct((n_rows, D), jnp.float32),
        compiler_params=pltpu.CompilerParams(dimension_semantics=("arbitrary",), vmem_limit_bytes=VMEM_LIMIT),
        name="scatter",
    )(fill_rows, dst, u2)


def _expert_kernel(te_ref, nv_ref, xs_ref, w1_ref, w3_ref, w2_ref, ys_ref):
    i = pl.program_id(0)
    nv = nv_ref[i]

    @pl.when(nv > 0)
    def _compute():
        xb = _bf(xs_ref[...])
        a = jnp.dot(xb, w1_ref[...], preferred_element_type=jnp.float32)
        c = jnp.dot(xb, w3_ref[...], preferred_element_type=jnp.float32)
        h = a * jax.nn.sigmoid(a) * c
        ys_ref[...] = jnp.dot(_bf(h), w2_ref[...], preferred_element_type=jnp.float32)

    @pl.when(nv == 0)
    def _unused():
        ys_ref[...] = jnp.zeros_like(ys_ref)


def _experts(tile_expert, tile_valid, xs, w1, w3, w2):
    n_rows, D = xs.shape
    nt = n_rows // TM
    grid_spec = pltpu.PrefetchScalarGridSpec(
        num_scalar_prefetch=2,
        grid=(nt,),
        in_specs=[pl.BlockSpec((TM, D), lambda i, te, nv: (i, 0)),
                  pl.BlockSpec((None, D, D_EXPERT), lambda i, te, nv: (te[i], 0, 0)),
                  pl.BlockSpec((None, D, D_EXPERT), lambda i, te, nv: (te[i], 0, 0)),
                  pl.BlockSpec((None, D_EXPERT, D), lambda i, te, nv: (te[i], 0, 0))],
        out_specs=pl.BlockSpec((TM, D), lambda i, te, nv: (i, 0)),
    )
    return pl.pallas_call(
        _expert_kernel,
        grid_spec=grid_spec,
        out_shape=jax.ShapeDtypeStruct((n_rows, D), jnp.float32),
        compiler_params=pltpu.CompilerParams(dimension_semantics=("arbitrary",), vmem_limit_bytes=VMEM_LIMIT),
        name="experts",
    )(tile_expert, tile_valid, xs, w1, w3, w2)


def _final_kernel(src_ref, x1_ref, cw_ref, ada_ref, lnw_ref, lnb_ref, ys_ref, o_ref, y1_buf, y2_buf, sem):
    def issue(t, carry):
        _row_copy(ys_ref, src_ref[0, 0, t], y1_buf, t, sem).start()
        _row_copy(ys_ref, src_ref[0, 1, t], y2_buf, t, sem).start()
        return carry

    lax.fori_loop(0, TF, issue, 0)

    def drain(t, carry):
        _row_copy(ys_ref, src_ref[0, 0, t], y1_buf, t, sem).wait()
        _row_copy(ys_ref, src_ref[0, 1, t], y2_buf, t, sem).wait()
        return carry

    lax.fori_loop(0, TF, drain, 0)

    cw = cw_ref[...]
    y = y1_buf[...] * cw[:, 4:5] + y2_buf[...] * cw[:, 5:6]
    gate2 = ada_ref[5:6, :]
    o_ref[...] = _ln(ALPHA * x1_ref[...] + gate2 * y) * lnw_ref[...] + lnb_ref[...]


def _final(src, x1, route, ada, lnw, lnb, ys, seq_len):
    T, D = x1.shape
    steps_per_seq = seq_len // TF
    return pl.pallas_call(
        _final_kernel,
        grid=(T // TF,),
        in_specs=[pl.BlockSpec((1, 2, TF), lambda i: (i, 0, 0), memory_space=pltpu.SMEM),
                  pl.BlockSpec((TF, D), lambda i: (i, 0)),
                  pl.BlockSpec((TF, LANES), lambda i: (i, 0)),
                  pl.BlockSpec((None, 6, D), lambda i: (i // steps_per_seq, 0, 0)),
                  pl.BlockSpec((1, D), lambda i: (0, 0)),
                  pl.BlockSpec((1, D), lambda i: (0, 0)),
                  pl.BlockSpec(memory_space=pl.ANY)],
        out_specs=pl.BlockSpec((TF, D), lambda i: (i, 0)),
        out_shape=jax.ShapeDtypeStruct((T, D), jnp.float32),
        scratch_shapes=[pltpu.VMEM((TF, D), jnp.float32), pltpu.VMEM((TF, D), jnp.float32),
                        pltpu.SemaphoreType.DMA(())],
        compiler_params=pltpu.CompilerParams(dimension_semantics=("arbitrary",), vmem_limit_bytes=VMEM_LIMIT),
        name="final",
    )(src, x1, route, ada, lnw, lnb, ys)


def kernel(x, c, positions, w_ada, b_ada, w_in, ret_gn_w, w_pool, pool_scale, w_out, ln1_w, ln1_b, w_group, b_group,
           w_router, b_router, w1, w3, w2, ln2_w, ln2_b):
    B, S, D = x.shape
    T = B * S
    assert w_ada.shape[0] == DEPTH and D == D_MODEL and S % TS == 0 and S % TF == 0

    inv_freq = ROPE_BASE ** (-jnp.arange(0, HEAD_DIM, 2, dtype=jnp.float32) / HEAD_DIM)
    half = HEAD_DIM // 2
    rope = jnp.stack([jnp.concatenate([inv_freq, inv_freq]),
                      jnp.concatenate([-jnp.ones((half,), jnp.float32), jnp.ones((half,), jnp.float32)])])

    n_rows = 2 * T + N_EXPERTS * TM
    nt = n_rows // TM

    for l in range(DEPTH):
        ada = _ada(c, w_ada[l], b_ada[l]).reshape(B, 6, D)
        pad = LANES - N_GROUPS - N_EXPERTS
        wrt = jnp.concatenate([w_group[l], w_router[l], jnp.zeros((D, pad), jnp.float32)], axis=1)
        brt = jnp.concatenate([b_group[l], b_router[l], jnp.zeros((pad,), jnp.float32)]).reshape(1, LANES)
        x1, u2, route, counts = _mix(
            x, positions, ada, rope, _bf(w_in[l]), _bf(w_out[l]), _bf(w_pool[l]),
            ret_gn_w[l].reshape(1, RET_WIDTH), pool_scale[l].reshape(1, POOL_WIDTH),
            ln1_w[l].reshape(1, D), ln1_b[l].reshape(1, D), wrt, brt)

        cnt = counts[0, :N_EXPERTS].astype(jnp.int32)
        tiles_e = (cnt + TM - 1) // TM
        tile_end = jnp.cumsum(tiles_e)
        tile_start = tile_end - tiles_e
        tile_ids = jnp.arange(nt, dtype=jnp.int32)
        tile_expert = jnp.minimum(jnp.sum(tile_ids[:, None] >= tile_end[None, :], axis=1), N_EXPERTS - 1).astype(jnp.int32)
        tile_valid = jnp.clip(cnt[tile_expert] - (tile_ids - tile_start[tile_expert]) * TM, 0, TM)
        tile_valid = jnp.where(tile_ids < tile_end[-1], tile_valid, 0).astype(jnp.int32)
        row_start = (tile_start * TM).astype(jnp.float32)
        last_tile = jnp.where(tiles_e > 0, (tile_end - 1) * TM, -1)
        spare = tile_end[-1] + jnp.arange(N_EXPERTS, dtype=jnp.int32)
        fill_rows = jnp.concatenate([last_tile, jnp.where(spare < nt, spare * TM, -1)]).astype(jnp.int32)

        route = route.reshape(T, LANES)
        eid = route[:, 0:2]
        rank = route[:, 2:4]
        base = jnp.sum(jnp.where(eid[:, :, None] == jnp.arange(N_EXPERTS, dtype=jnp.float32), row_start, 0.0), axis=-1)
        dest = (base + rank).astype(jnp.int32)

        xs = _scatter(fill_rows, dest.reshape(T // TS, TS, 2).transpose(0, 2, 1), u2.reshape(T, D), n_rows)
        ys = _experts(tile_expert, tile_valid, xs,
                      _bf(w1[l]).reshape(N_EXPERTS, D, D_EXPERT), _bf(w3[l]).reshape(N_EXPERTS, D, D_EXPERT),
                      _bf(w2[l]).reshape(N_EXPERTS, D_EXPERT, D))
        x = _final(dest.reshape(T // TF, TF, 2).transpose(0, 2, 1), x1.reshape(T, D), route, ada,
                   ln2_w[l].reshape(1, D), ln2_b[l].reshape(1, D), ys, S).reshape(B, S, D)
    return x
```

```python
import math

import jax
import jax.numpy as jnp
from jax import lax
from jax.experimental import pallas as pl
from jax.experimental.pallas import tpu as pltpu

D_MODEL = 1024
RET_WIDTH = 512
RET_HEADS = 4
HEAD_DIM = 128
POOL_WIDTH = 512
POOL_WINDOWS = (2, 4, 8, 16)
POOL_GROUP_DIM = 128
IN_COLS = 4 * RET_WIDTH + POOL_WIDTH
N_GROUPS = 4
EXPERTS_PER_GROUP = 8
N_EXPERTS = 32
D_EXPERT = 256
DEPTH = 1
ALPHA = (2.0 * DEPTH) ** 0.25
LN_EPS = 1e-5
ROPE_BASE = 10000.0

LANES = 128
SUBLANES = 8
HALF = D_MODEL // 2
POOL_HALO = 16
TS = 256
TM = 256
TR = -(-(2 * TS + N_EXPERTS * (SUBLANES - 1)) // LANES) * LANES
VMEM_LIMIT = 56 * 1024 * 1024

_LOG_GAMMA = tuple(math.log1p(-(2.0 ** (-5.0 - h))) for h in range(RET_HEADS))
_HI = lax.Precision.HIGHEST


def _ln(x):
    mu = jnp.mean(x, axis=-1, keepdims=True)
    xc = x - mu
    var = jnp.mean(xc * xc, axis=-1, keepdims=True)
    return xc * lax.rsqrt(var + LN_EPS)


def _bf(x):
    return x.astype(jnp.bfloat16)


def _pack_rows(x):
    n = x.shape[0]
    lo = x[:, :HALF].reshape(n // SUBLANES, SUBLANES, HALF)
    hi = x[:, HALF:].reshape(n // SUBLANES, SUBLANES, HALF)
    return _bf(jnp.concatenate([lo, hi], axis=1).reshape(2 * n, HALF))


def _unpack_rows(z):
    n = z.shape[0] // 2
    zf = z.astype(jnp.float32).reshape(n // SUBLANES, 2 * SUBLANES, HALF)
    return _bf(zf[:, :SUBLANES, :].reshape(n, HALF)), _bf(zf[:, SUBLANES:, :].reshape(n, HALF))


def _ada_kernel(c_ref, w_ref, b_ref, o_ref):
    c = c_ref[...]
    ca = c * jax.nn.sigmoid(c)
    o_ref[...] = jnp.dot(ca, w_ref[...], precision=_HI, preferred_element_type=jnp.float32) + b_ref[...]


def _ada(c, w_ada, b_ada):
    B, D = c.shape
    n = w_ada.shape[1]
    return pl.pallas_call(
        _ada_kernel,
        grid=(n // D,),
        in_specs=[pl.BlockSpec((B, D), lambda j: (0, 0)),
                  pl.BlockSpec((D, D), lambda j: (0, j)),
                  pl.BlockSpec((1, D), lambda j: (0, j))],
        out_specs=pl.BlockSpec((B, D), lambda j: (0, j)),
        out_shape=jax.ShapeDtypeStruct((B, n), jnp.float32),
        compiler_params=pltpu.CompilerParams(dimension_semantics=("arbitrary",), vmem_limit_bytes=VMEM_LIMIT),
        name="ada",
    )(c, w_ada, b_ada.reshape(1, n))


def _mix_kernel(x_ref, pos_ref, ada_ref, rope_ref, win_ref, wout_ref, wpool_ref, gnw_ref, pscale_ref,
                ln1w_ref, ln1b_ref, wrt_ref, brt_ref,
                x1_ref, xt_ref, route_ref, cnt_ref,
                state_ref, halo_ref, dmat_ref, qdec_ref, kdec_ref, cat_ref):
    b = pl.program_id(0)
    s = pl.program_id(1)

    @pl.when((b == 0) & (s == 0))
    def _init_tables():
        ri = lax.broadcasted_iota(jnp.int32, (TS, TS), 0)
        ci = lax.broadcasted_iota(jnp.int32, (TS, TS), 1)
        rel = (ri - ci).astype(jnp.float32)
        for h in range(RET_HEADS):
            dmat_ref[h] = jnp.where(rel >= 0.0, jnp.exp(jnp.maximum(rel, 0.0) * _LOG_GAMMA[h]), 0.0)
        row = lax.broadcasted_iota(jnp.int32, (TS, RET_WIDTH), 0).astype(jnp.float32)
        lane = lax.broadcasted_iota(jnp.int32, (TS, RET_WIDTH), 1)
        lg = jnp.full((TS, RET_WIDTH), _LOG_GAMMA[0], jnp.float32)
        for h in range(1, RET_HEADS):
            lg = jnp.where(lane >= h * HEAD_DIM, _LOG_GAMMA[h], lg)
        qdec_ref[...] = jnp.exp((row + 1.0) * lg)
        kdec_ref[...] = jnp.exp((TS - 1.0 - row) * lg)

    @pl.when(s == 0)
    def _init_carries():
        state_ref[...] = jnp.zeros_like(state_ref)
        halo_ref[...] = jnp.zeros_like(halo_ref)

    ada = ada_ref[...]
    shift1, scale1, gate1 = ada[0:1], ada[1:2], ada[2:3]
    shift2, scale2 = ada[3:4], ada[4:5]

    x = x_ref[...]
    u = _bf(_ln(x) * (1.0 + scale1) + shift1)

    ang = pos_ref[...].astype(jnp.float32) * rope_ref[0:1, :]
    cos_t = jnp.cos(ang)
    sin_t = jnp.sin(ang) * rope_ref[1:2, :]

    q = jnp.dot(u, win_ref[:, 0:RET_WIDTH], preferred_element_type=jnp.float32)
    k = jnp.dot(u, win_ref[:, RET_WIDTH:2 * RET_WIDTH], preferred_element_type=jnp.float32)
    v = jnp.dot(u, win_ref[:, 2 * RET_WIDTH:3 * RET_WIDTH], preferred_element_type=jnp.float32)
    g = jnp.dot(u, win_ref[:, 3 * RET_WIDTH:4 * RET_WIDTH], preferred_element_type=jnp.float32)
    p = jnp.dot(u, win_ref[:, 4 * RET_WIDTH:IN_COLS], preferred_element_type=jnp.float32)

    gnw = gnw_ref[...]
    for h in range(RET_HEADS):
        sl = slice(h * HEAD_DIM, (h + 1) * HEAD_DIM)
        qh, kh, vh = q[:, sl], k[:, sl], v[:, sl]
        qr = qh * cos_t + pltpu.roll(qh, HEAD_DIM // 2, 1) * sin_t
        kr = (kh * cos_t + pltpu.roll(kh, HEAD_DIM // 2, 1) * sin_t) * (HEAD_DIM ** -0.5)
        vb = _bf(vh)
        sc = lax.dot_general(_bf(qr), _bf(kr), (((1,), (1,)), ((), ())), preferred_element_type=jnp.float32)
        intra = jnp.dot(_bf(sc * dmat_ref[h]), vb, preferred_element_type=jnp.float32)
        st = state_ref[h]
        cross = jnp.dot(_bf(qr * qdec_ref[:, sl]), _bf(st), preferred_element_type=jnp.float32)
        kv = lax.dot_general(_bf(kr * kdec_ref[:, sl]), vb, (((0,), (0,)), ((), ())),
                             preferred_element_type=jnp.float32)
        state_ref[h] = st * math.exp(TS * _LOG_GAMMA[h]) + kv
        r = _ln(intra + cross) * gnw[:, sl]
        gh = g[:, sl]
        cat_ref[:, sl] = _bf(gh * jax.nn.sigmoid(gh) * r)

    pext = jnp.concatenate([halo_ref[...], p], axis=0)
    halo_ref[...] = p[TS - POOL_HALO:, :]
    t_abs = (s * TS + lax.broadcasted_iota(jnp.int32, (TS, 1), 0) + 1).astype(jnp.float32)
    pscale = pscale_ref[...]
    for grp, w in enumerate(POOL_WINDOWS):
        sl = slice(grp * POOL_GROUP_DIM, (grp + 1) * POOL_GROUP_DIM)
        acc = pext[:, sl]
        shift = 1
        while shift < w:
            acc = acc + pltpu.roll(acc, shift, 0)
            shift *= 2
        pooled = acc[POOL_HALO:, :] / jnp.minimum(t_abs, float(w)) - p[:, sl]
        po = jnp.dot(_bf(pooled), wpool_ref[grp], preferred_element_type=jnp.float32) * pscale[:, sl]
        cat_ref[:, RET_WIDTH + grp * POOL_GROUP_DIM:RET_WIDTH + (grp + 1) * POOL_GROUP_DIM] = _bf(po)

    mix = jnp.dot(cat_ref[...], wout_ref[...], preferred_element_type=jnp.float32)
    x1 = _ln(ALPHA * x + gate1 * mix) * ln1w_ref[...] + ln1b_ref[...]
    x1_ref[...] = x1
    u2 = _ln(x1) * (1.0 + scale2) + shift2

    logits = jnp.dot(u2, wrt_ref[...], precision=_HI, preferred_element_type=jnp.float32) + brt_ref[...]
    lane = lax.broadcasted_iota(jnp.int32, (TS, LANES), 1)
    neg = jnp.float32(-jnp.inf)
    gl = jnp.where(lane < N_GROUPS, logits, neg)
    gmax = jnp.max(gl, axis=-1, keepdims=True)
    gidx = jnp.min(jnp.where(gl == gmax, lane, LANES), axis=-1, keepdims=True)
    gprob = 1.0 / jnp.sum(jnp.exp(gl - gmax), axis=-1, keepdims=True)
    lo = N_GROUPS + EXPERTS_PER_GROUP * gidx
    el = jnp.where((lane >= lo) & (lane < lo + EXPERTS_PER_GROUP), logits, neg)
    m1 = jnp.max(el, axis=-1, keepdims=True)
    i1 = jnp.min(jnp.where(el == m1, lane, LANES), axis=-1, keepdims=True)
    el2 = jnp.where(lane == i1, neg, el)
    m2 = jnp.max(el2, axis=-1, keepdims=True)
    i2 = jnp.min(jnp.where(el2 == m2, lane, LANES), axis=-1, keepdims=True)
    e21 = jnp.exp(m2 - m1)
    den = 1.0 / (1.0 + e21)
    cw1 = gprob * den
    cw2 = gprob * e21 * den

    oh1 = lane == (i1 - N_GROUPS)
    oh2 = lane == (i2 - N_GROUPS)
    oh = jnp.where(oh1 | oh2, 1.0, 0.0)
    cnt = jnp.sum(oh, axis=0, keepdims=True)
    run = jnp.floor((cnt + (SUBLANES - 1.0)) * (1.0 / SUBLANES)) * SUBLANES
    li = lax.broadcasted_iota(jnp.int32, (LANES, LANES), 0)
    lj = lax.broadcasted_iota(jnp.int32, (LANES, LANES), 1)
    run_start = jnp.dot(jnp.broadcast_to(run, (SUBLANES, LANES)), jnp.where(li < lj, 1.0, 0.0),
                        precision=_HI, preferred_element_type=jnp.float32)[0:1]
    ri = lax.broadcasted_iota(jnp.int32, (TS, TS), 0)
    ci = lax.broadcasted_iota(jnp.int32, (TS, TS), 1)
    tri = _bf(jnp.where(ci < ri, 1.0, 0.0))
    before = jnp.dot(tri, _bf(oh), preferred_element_type=jnp.float32) + run_start
    pos1 = jnp.sum(jnp.where(oh1, before, 0.0), axis=-1, keepdims=True)
    pos2 = jnp.sum(jnp.where(oh2, before, 0.0), axis=-1, keepdims=True)
    eye = ri == ci
    pos1_row = jnp.sum(jnp.where(eye, pos1, 0.0), axis=0, keepdims=True)
    pos2_row = jnp.sum(jnp.where(eye, pos2, 0.0), axis=0, keepdims=True)
    rr = lax.broadcasted_iota(jnp.int32, (TR, TS), 0).astype(jnp.float32)
    perm = _bf(jnp.where((rr == pos1_row) | (rr == pos2_row), 1.0, 0.0))
    xt_ref[...] = _pack_rows(jnp.dot(perm, _bf(u2), preferred_element_type=jnp.float32))
    cnt_ref[...] = jnp.broadcast_to(cnt, cnt_ref.shape)

    route = jnp.where(lane == 0, pos1, 0.0)
    route = jnp.where(lane == 1, pos2, route)
    route = jnp.where(lane == 2, cw1, route)
    route = jnp.where(lane == 3, cw2, route)
    route_ref[...] = route


def _mix(x, positions, ada, rope, win, wout, wpool, gnw, pscale, ln1w, ln1b, wrt, brt):
    B, S, D = x.shape
    ns = S // TS
    const2 = lambda b, s: (0, 0)
    const3 = lambda b, s: (0, 0, 0)
    tile = lambda b, s: (b, s, 0)
    flat = lambda b, s: (b * ns + s, 0, 0)
    return pl.pallas_call(
        _mix_kernel,
        grid=(B, ns),
        in_specs=[
            pl.BlockSpec((None, TS, D), tile),
            pl.BlockSpec((None, TS, 1), tile),
            pl.BlockSpec((None, 6, D), lambda b, s: (b, 0, 0)),
            pl.BlockSpec((2, LANES), const2),
            pl.BlockSpec((D, IN_COLS), const2),
            pl.BlockSpec((D, D), const2),
            pl.BlockSpec((len(POOL_WINDOWS), POOL_GROUP_DIM, POOL_GROUP_DIM), const3),
            pl.BlockSpec((1, RET_WIDTH), const2),
            pl.BlockSpec((1, POOL_WIDTH), const2),
            pl.BlockSpec((1, D), const2),
            pl.BlockSpec((1, D), const2),
            pl.BlockSpec((D, LANES), const2),
            pl.BlockSpec((1, LANES), const2),
        ],
        out_specs=[
            pl.BlockSpec((None, TS, D), tile),
            pl.BlockSpec((None, 2 * TR, HALF), flat),
            pl.BlockSpec((None, TS, LANES), tile),
            pl.BlockSpec((None, SUBLANES, LANES), flat),
        ],
        out_shape=[
            jax.ShapeDtypeStruct((B, S, D), jnp.float32),
            jax.ShapeDtypeStruct((B * ns, 2 * TR, HALF), jnp.bfloat16),
            jax.ShapeDtypeStruct((B, S, LANES), jnp.float32),
            jax.ShapeDtypeStruct((B * ns, SUBLANES, LANES), jnp.float32),
        ],
        scratch_shapes=[
            pltpu.VMEM((RET_HEADS, HEAD_DIM, HEAD_DIM), jnp.float32),
            pltpu.VMEM((POOL_HALO, POOL_WIDTH), jnp.float32),
            pltpu.VMEM((RET_HEADS, TS, TS), jnp.float32),
            pltpu.VMEM((TS, RET_WIDTH), jnp.float32),
            pltpu.VMEM((TS, RET_WIDTH), jnp.float32),
            pltpu.VMEM((TS, D), jnp.bfloat16),
        ],
        compiler_params=pltpu.CompilerParams(dimension_semantics=("arbitrary", "arbitrary"),
                                             vmem_limit_bytes=VMEM_LIMIT),
        name="mix",
    )(x, positions.reshape(B, S, 1), ada, rope, win, wout, wpool, gnw, pscale, ln1w, ln1b, wrt, brt)


def _group_copy(src_ref, src_row, dst_ref, dst_row, sem):
    return pltpu.make_async_copy(src_ref.at[pl.ds(pl.multiple_of(2 * src_row, 2 * SUBLANES), 2 * SUBLANES), :],
                                 dst_ref.at[pl.ds(pl.multiple_of(2 * dst_row, 2 * SUBLANES), 2 * SUBLANES), :], sem)


def _copy_run(src_ref, src_row, dst_ref, dst_row, n_groups, sem):
    def body(u, carry):
        _group_copy(src_ref, src_row + u * SUBLANES, dst_ref, dst_row + u * SUBLANES, sem).start()
        return carry

    lax.fori_loop(0, n_groups, body, 0)


def _wait_groups(src_ref, dst_ref, n_groups, sem):
    def body(u, carry):
        _group_copy(src_ref, 0, dst_ref, 0, sem).wait()
        return carry

    lax.fori_loop(0, n_groups, body, 0)


def _expert_kernel(pe_ref, pj_ref, pv_ref, t0_ref, t1_ref, so_ref, eo_ref, rn_ref,
                   xt_ref, w1_ref, w3_ref, w2_ref, ys_ref, xbuf, sem):
    g = pl.program_id(0)
    nv = pv_ref[g]

    @pl.when(nv > 0)
    def _compute():
        e = pe_ref[g]
        lo = pj_ref[g] * TM

        @pl.when(nv < TM)
        def _zero_tail():
            xbuf[...] = jnp.zeros_like(xbuf)

        def gather(i, n_total):
            a = jnp.maximum(eo_ref[e, i], lo)
            b = jnp.minimum(eo_ref[e, i] + rn_ref[e, i], lo + TM)
            n = jnp.maximum(b - a, 0) // SUBLANES
            _copy_run(xt_ref, i * TR + so_ref[e, i] + (a - eo_ref[e, i]), xbuf, a - lo, n, sem)
            return n_total + n

        n_total = lax.fori_loop(t0_ref[g], t1_ref[g], gather, 0)
        _wait_groups(xt_ref, xbuf, n_total, sem)

        xl, xh = _unpack_rows(xbuf[...])
        a = (jnp.dot(xl, w1_ref[:HALF, :], preferred_element_type=jnp.float32)
             + jnp.dot(xh, w1_ref[HALF:, :], preferred_element_type=jnp.float32))
        c = (jnp.dot(xl, w3_ref[:HALF, :], preferred_element_type=jnp.float32)
             + jnp.dot(xh, w3_ref[HALF:, :], preferred_element_type=jnp.float32))
        h = a * jax.nn.sigmoid(a) * c
        y = jnp.dot(_bf(h), w2_ref[...], preferred_element_type=jnp.float32)
        ys_ref[...] = _pack_rows(y)

    @pl.when(nv == 0)
    def _unused():
        ys_ref[...] = jnp.zeros_like(ys_ref)


def _experts(tables, xt, w1, w3, w2, n_pages):
    D = D_MODEL
    n_pref = len(tables)
    grid_spec = pltpu.PrefetchScalarGridSpec(
        num_scalar_prefetch=n_pref,
        grid=(n_pages,),
        in_specs=[pl.BlockSpec(memory_space=pl.ANY),
                  pl.BlockSpec((None, D, D_EXPERT), lambda g, pe, *_: (pe[g], 0, 0)),
                  pl.BlockSpec((None, D, D_EXPERT), lambda g, pe, *_: (pe[g], 0, 0)),
                  pl.BlockSpec((None, D_EXPERT, D), lambda g, pe, *_: (pe[g], 0, 0))],
        out_specs=pl.BlockSpec((2 * TM, HALF), lambda g, *_: (g, 0)),
        scratch_shapes=[pltpu.VMEM((2 * TM, HALF), jnp.bfloat16), pltpu.SemaphoreType.DMA(())],
    )
    return pl.pallas_call(
        _expert_kernel,
        grid_spec=grid_spec,
        out_shape=jax.ShapeDtypeStruct((n_pages * 2 * TM, HALF), jnp.bfloat16),
        compiler_params=pltpu.CompilerParams(dimension_semantics=("arbitrary",), vmem_limit_bytes=VMEM_LIMIT),
        name="experts",
    )(*tables, xt, w1, w3, w2)


def _final_kernel(so_ref, src_ref, rn_ref, x1_ref, route_ref, ada_ref, lnw_ref, lnb_ref, ys_ref, o_ref, ybuf, sem):
    def gather(e, n_total):
        n = rn_ref[0, 0, e] // SUBLANES
        _copy_run(ys_ref, src_ref[0, 0, e], ybuf, so_ref[0, 0, e], n, sem)
        return n_total + n

    n_total = lax.fori_loop(0, N_EXPERTS, gather, 0)

    def zero_group(u, carry):
        ybuf[pl.ds(pl.multiple_of(u * 2 * SUBLANES, 2 * SUBLANES), 2 * SUBLANES), :] = jnp.zeros(
            (2 * SUBLANES, HALF), jnp.bfloat16)
        return carry

    lax.fori_loop(n_total, TR // SUBLANES, zero_group, 0)
    _wait_groups(ys_ref, ybuf, n_total, sem)

    route = route_ref[...]
    pos1, pos2, cw1, cw2 = route[:, 0:1], route[:, 1:2], route[:, 2:3], route[:, 3:4]
    col = lax.broadcasted_iota(jnp.int32, (TS, TR), 1).astype(jnp.float32)
    wmat = _bf(jnp.where(col == pos1, cw1, 0.0) + jnp.where(col == pos2, cw2, 0.0))
    yl, yh = _unpack_rows(ybuf[...])
    y = jnp.concatenate([jnp.dot(wmat, yl, preferred_element_type=jnp.float32),
                         jnp.dot(wmat, yh, preferred_element_type=jnp.float32)], axis=-1)
    gate2 = ada_ref[5:6, :]
    o_ref[...] = _ln(ALPHA * x1_ref[...] + gate2 * y) * lnw_ref[...] + lnb_ref[...]


def _final(so, src, rn, x1, route, ada, lnw, lnb, ys, seq_len):
    T, D = x1.shape
    steps_per_seq = seq_len // TS
    smem = lambda: pl.BlockSpec((1, 1, N_EXPERTS), lambda i: (i, 0, 0), memory_space=pltpu.SMEM)
    return pl.pallas_call(
        _final_kernel,
        grid=(T // TS,),
        in_specs=[smem(), smem(), smem(),
                  pl.BlockSpec((TS, D), lambda i: (i, 0)),
                  pl.BlockSpec((TS, LANES), lambda i: (i, 0)),
                  pl.BlockSpec((None, 6, D), lambda i: (i // steps_per_seq, 0, 0)),
                  pl.BlockSpec((1, D), lambda i: (0, 0)),
                  pl.BlockSpec((1, D), lambda i: (0, 0)),
                  pl.BlockSpec(memory_space=pl.ANY)],
        out_specs=pl.BlockSpec((TS, D), lambda i: (i, 0)),
        out_shape=jax.ShapeDtypeStruct((T, D), jnp.float32),
        scratch_shapes=[pltpu.VMEM((2 * TR, HALF), jnp.bfloat16), pltpu.SemaphoreType.DMA(())],
        compiler_params=pltpu.CompilerParams(dimension_semantics=("arbitrary",), vmem_limit_bytes=VMEM_LIMIT),
        name="final",
    )(so, src, rn, x1, route, ada, lnw, lnb, ys)


def _page_tables(cnt, n_pages):
    nt = cnt.shape[0]
    run = (cnt + SUBLANES - 1) // SUBLANES * SUBLANES
    so = jnp.cumsum(run, axis=1) - run
    eo = jnp.cumsum(run, axis=0) - run
    tot = jnp.sum(run, axis=0)
    pages_e = (tot + TM - 1) // TM
    page_end = jnp.cumsum(pages_e)
    page_start = page_end - pages_e
    g = jnp.arange(n_pages, dtype=jnp.int32)
    pe = jnp.minimum(jnp.sum(g[:, None] >= page_end[None, :], axis=1), N_EXPERTS - 1).astype(jnp.int32)
    used = g < page_end[-1]
    pj = jnp.where(used, g - page_start[pe], 0).astype(jnp.int32)
    pv = jnp.where(used, jnp.clip(tot[pe] - pj * TM, 0, TM), 0).astype(jnp.int32)
    lo = pj * TM
    eo_p = eo.T[pe]
    rn_p = run.T[pe]
    t0 = jnp.sum(eo_p + rn_p <= lo[:, None], axis=1).astype(jnp.int32)
    t1 = jnp.sum(eo_p < (lo + TM)[:, None], axis=1).astype(jnp.int32)
    src = page_start[None, :] * TM + eo
    i32 = lambda a: a.astype(jnp.int32)
    expert_tables = (pe, pj, pv, t0, t1, i32(so.T), i32(eo.T), i32(run.T))
    final_tables = tuple(i32(a).reshape(nt, 1, N_EXPERTS) for a in (so, src, run))
    return expert_tables, final_tables


def kernel(x, c, positions, w_ada, b_ada, w_in, ret_gn_w, w_pool, pool_scale, w_out, ln1_w, ln1_b, w_group, b_group,
           w_router, b_router, w1, w3, w2, ln2_w, ln2_b):
    B, S, D = x.shape
    T = B * S
    assert w_ada.shape[0] == DEPTH and D == D_MODEL and S % TS == 0

    inv_freq = ROPE_BASE ** (-jnp.arange(0, HEAD_DIM, 2, dtype=jnp.float32) / HEAD_DIM)
    half = HEAD_DIM // 2
    rope = jnp.stack([jnp.concatenate([inv_freq, inv_freq]),
                      jnp.concatenate([-jnp.ones((half,), jnp.float32), jnp.ones((half,), jnp.float32)])])

    nt = T // TS
    n_pages = (2 * T + nt * N_EXPERTS * (SUBLANES - 1)) // TM + N_EXPERTS

    for l in range(DEPTH):
        ada = _ada(c, w_ada[l], b_ada[l]).reshape(B, 6, D)
        pad = LANES - N_GROUPS - N_EXPERTS
        wrt = jnp.concatenate([w_group[l], w_router[l], jnp.zeros((D, pad), jnp.float32)], axis=1)
        brt = jnp.concatenate([b_group[l], b_router[l], jnp.zeros((pad,), jnp.float32)]).reshape(1, LANES)
        x1, xt, route, counts = _mix(
            x, positions, ada, rope, _bf(w_in[l]), _bf(w_out[l]), _bf(w_pool[l]),
            ret_gn_w[l].reshape(1, RET_WIDTH), pool_scale[l].reshape(1, POOL_WIDTH),
            ln1_w[l].reshape(1, D), ln1_b[l].reshape(1, D), wrt, brt)

        cnt = counts[:, 0, :N_EXPERTS].astype(jnp.int32)
        expert_tables, final_tables = _page_tables(cnt, n_pages)
        ys = _experts(expert_tables, xt.reshape(nt * 2 * TR, HALF),
                      _bf(w1[l]).reshape(N_EXPERTS, D, D_EXPERT), _bf(w3[l]).reshape(N_EXPERTS, D, D_EXPERT),
                      _bf(w2[l]).reshape(N_EXPERTS, D_EXPERT, D), n_pages)
        x = _final(*final_tables, x1.reshape(T, D), route.reshape(T, LANES), ada,
                   ln2_w[l].reshape(1, D), ln2_b[l].reshape(1, D), ys, S).reshape(B, S, D)
    return x
```

```python
import math

import jax
import jax.numpy as jnp
from jax import lax
from jax.experimental import pallas as pl
from jax.experimental.pallas import tpu as pltpu

D_MODEL = 1024
RET_WIDTH = 512
RET_HEADS = 4
HEAD_DIM = 128
POOL_WIDTH = 512
POOL_WINDOWS = (2, 4, 8, 16)
POOL_GROUP_DIM = 128
IN_COLS = 4 * RET_WIDTH + POOL_WIDTH
N_GROUPS = 4
EXPERTS_PER_GROUP = 8
N_EXPERTS = 32
D_EXPERT = 256
DEPTH = 1
ALPHA = (2.0 * DEPTH) ** 0.25
LN_EPS = 1e-5
ROPE_BASE = 10000.0

LANES = 128
SUBLANES = 8
HALF = D_MODEL // 2
POOL_HALO = 16
TS = 256
TM = 256
TR = -(-(2 * TS + N_EXPERTS * (SUBLANES - 1)) // LANES) * LANES
VMEM_LIMIT = 56 * 1024 * 1024

_LOG_GAMMA = tuple(math.log1p(-(2.0 ** (-5.0 - h))) for h in range(RET_HEADS))
_HI = lax.Precision.HIGHEST


def _ln(x):
    mu = jnp.mean(x, axis=-1, keepdims=True)
    xc = x - mu
    var = jnp.mean(xc * xc, axis=-1, keepdims=True)
    return xc * lax.rsqrt(var + LN_EPS)


def _bf(x):
    return x.astype(jnp.bfloat16)


def _pack_rows(x):
    n = x.shape[0]
    lo = x[:, :HALF].reshape(n // SUBLANES, SUBLANES, HALF)
    hi = x[:, HALF:].reshape(n // SUBLANES, SUBLANES, HALF)
    return _bf(jnp.concatenate([lo, hi], axis=1).reshape(2 * n, HALF))


def _unpack_rows(z):
    n = z.shape[0] // 2
    zf = z.astype(jnp.float32).reshape(n // SUBLANES, 2 * SUBLANES, HALF)
    return _bf(zf[:, :SUBLANES, :].reshape(n, HALF)), _bf(zf[:, SUBLANES:, :].reshape(n, HALF))


def _ada_kernel(c_ref, w_ref, b_ref, o_ref):
    c = c_ref[...]
    ca = c * jax.nn.sigmoid(c)
    o_ref[...] = jnp.dot(ca, w_ref[...], precision=_HI, preferred_element_type=jnp.float32) + b_ref[...]


def _ada(c, w_ada, b_ada):
    B, D = c.shape
    n = w_ada.shape[1]
    return pl.pallas_call(
        _ada_kernel,
        grid=(n // D,),
        in_specs=[pl.BlockSpec((B, D), lambda j: (0, 0)),
                  pl.BlockSpec((D, D), lambda j: (0, j)),
                  pl.BlockSpec((1, D), lambda j: (0, j))],
        out_specs=pl.BlockSpec((B, D), lambda j: (0, j)),
        out_shape=jax.ShapeDtypeStruct((B, n), jnp.float32),
        compiler_params=pltpu.CompilerParams(dimension_semantics=("arbitrary",), vmem_limit_bytes=VMEM_LIMIT),
        name="ada",
    )(c, w_ada, b_ada.reshape(1, n))


def _mix_kernel(x_ref, pos_ref, ada_ref, rope_ref, win_ref, wout_ref, wpool_ref, gnw_ref, pscale_ref,
                ln1w_ref, ln1b_ref, wrt_ref, brt_ref,
                x1_ref, xt_ref, route_ref, cnt_ref,
                state_ref, halo_ref, dmat_ref, qdec_ref, kdec_ref, cat_ref):
    b = pl.program_id(0)
    s = pl.program_id(1)

    @pl.when((b == 0) & (s == 0))
    def _init_tables():
        ri = lax.broadcasted_iota(jnp.int32, (TS, TS), 0)
        ci = lax.broadcasted_iota(jnp.int32, (TS, TS), 1)
        rel = (ri - ci).astype(jnp.float32)
        for h in range(RET_HEADS):
            dmat_ref[h] = jnp.where(rel >= 0.0, jnp.exp(jnp.maximum(rel, 0.0) * _LOG_GAMMA[h]), 0.0)
        row = lax.broadcasted_iota(jnp.int32, (TS, RET_WIDTH), 0).astype(jnp.float32)
        lane = lax.broadcasted_iota(jnp.int32, (TS, RET_WIDTH), 1)
        lg = jnp.full((TS, RET_WIDTH), _LOG_GAMMA[0], jnp.float32)
        for h in range(1, RET_HEADS):
            lg = jnp.where(lane >= h * HEAD_DIM, _LOG_GAMMA[h], lg)
        qdec_ref[...] = jnp.exp((row + 1.0) * lg)
        kdec_ref[...] = jnp.exp((TS - 1.0 - row) * lg)

    @pl.when(s == 0)
    def _init_carries():
        state_ref[...] = jnp.zeros_like(state_ref)
        halo_ref[...] = jnp.zeros_like(halo_ref)

    ada = ada_ref[...]
    shift1, scale1, gate1 = ada[0:1], ada[1:2], ada[2:3]
    shift2, scale2 = ada[3:4], ada[4:5]

    x = x_ref[...]
    u = _bf(_ln(x) * (1.0 + scale1) + shift1)

    ang = pos_ref[...].astype(jnp.float32) * rope_ref[0:1, :]
    cos_t = jnp.cos(ang)
    sin_t = jnp.sin(ang) * rope_ref[1:2, :]

    q = jnp.dot(u, win_ref[:, 0:RET_WIDTH], preferred_element_type=jnp.float32)
    k = jnp.dot(u, win_ref[:, RET_WIDTH:2 * RET_WIDTH], preferred_element_type=jnp.float32)
    v = jnp.dot(u, win_ref[:, 2 * RET_WIDTH:3 * RET_WIDTH], preferred_element_type=jnp.float32)
    g = jnp.dot(u, win_ref[:, 3 * RET_WIDTH:4 * RET_WIDTH], preferred_element_type=jnp.float32)
    p = jnp.dot(u, win_ref[:, 4 * RET_WIDTH:IN_COLS], preferred_element_type=jnp.float32)

    gnw = gnw_ref[...]
    for h in range(RET_HEADS):
        sl = slice(h * HEAD_DIM, (h + 1) * HEAD_DIM)
        qh, kh, vh = q[:, sl], k[:, sl], v[:, sl]
        qr = qh * cos_t + pltpu.roll(qh, HEAD_DIM // 2, 1) * sin_t
        kr = (kh * cos_t + pltpu.roll(kh, HEAD_DIM // 2, 1) * sin_t) * (HEAD_DIM ** -0.5)
        vb = _bf(vh)
        sc = lax.dot_general(_bf(qr), _bf(kr), (((1,), (1,)), ((), ())), preferred_element_type=jnp.float32)
        intra = jnp.dot(_bf(sc * dmat_ref[h]), vb, preferred_element_type=jnp.float32)
        st = state_ref[h]
        cross = jnp.dot(_bf(qr * qdec_ref[:, sl]), _bf(st), preferred_element_type=jnp.float32)
        kv = lax.dot_general(_bf(kr * kdec_ref[:, sl]), vb, (((0,), (0,)), ((), ())),
                             preferred_element_type=jnp.float32)
        state_ref[h] = st * math.exp(TS * _LOG_GAMMA[h]) + kv
        r = _ln(intra + cross) * gnw[:, sl]
        gh = g[:, sl]
        cat_ref[:, sl] = _bf(gh * jax.nn.sigmoid(gh) * r)

    pext = jnp.concatenate([halo_ref[...], p], axis=0)
    halo_ref[...] = p[TS - POOL_HALO:, :]
    t_abs = (s * TS + lax.broadcasted_iota(jnp.int32, (TS, 1), 0) + 1).astype(jnp.float32)
    pscale = pscale_ref[...]
    for grp, w in enumerate(POOL_WINDOWS):
        sl = slice(grp * POOL_GROUP_DIM, (grp + 1) * POOL_GROUP_DIM)
        acc = pext[:, sl]
        shift = 1
        while shift < w:
            acc = acc + pltpu.roll(acc, shift, 0)
            shift *= 2
        pooled = acc[POOL_HALO:, :] / jnp.minimum(t_abs, float(w)) - p[:, sl]
        po = jnp.dot(_bf(pooled), wpool_ref[grp], preferred_element_type=jnp.float32) * pscale[:, sl]
        cat_ref[:, RET_WIDTH + grp * POOL_GROUP_DIM:RET_WIDTH + (grp + 1) * POOL_GROUP_DIM] = _bf(po)

    mix = jnp.dot(cat_ref[...], wout_ref[...], preferred_element_type=jnp.float32)
    x1 = _ln(ALPHA * x + gate1 * mix) * ln1w_ref[...] + ln1b_ref[...]
    x1_ref[...] = x1
    u2 = _ln(x1) * (1.0 + scale2) + shift2

    logits = jnp.dot(u2, wrt_ref[...], precision=_HI, preferred_element_type=jnp.float32) + brt_ref[...]
    lane = lax.broadcasted_iota(jnp.int32, (TS, LANES), 1)
    neg = jnp.float32(-jnp.inf)
    gl = jnp.where(lane < N_GROUPS, logits, neg)
    gmax = jnp.max(gl, axis=-1, keepdims=True)
    gidx = jnp.min(jnp.where(gl == gmax, lane, LANES), axis=-1, keepdims=True)
    gprob = 1.0 / jnp.sum(jnp.exp(gl - gmax), axis=-1, keepdims=True)
    lo = N_GROUPS + EXPERTS_PER_GROUP * gidx
    el = jnp.where((lane >= lo) & (lane < lo + EXPERTS_PER_GROUP), logits, neg)
    m1 = jnp.max(el, axis=-1, keepdims=True)
    i1 = jnp.min(jnp.where(el == m1, lane, LANES), axis=-1, keepdims=True)
    el2 = jnp.where(lane == i1, neg, el)
    m2 = jnp.max(el2, axis=-1, keepdims=True)
    i2 = jnp.min(jnp.where(el2 == m2, lane, LANES), axis=-1, keepdims=True)
    e21 = jnp.exp(m2 - m1)
    den = 1.0 / (1.0 + e21)
    cw1 = gprob * den
    cw2 = gprob * e21 * den

    oh1 = lane == (i1 - N_GROUPS)
    oh2 = lane == (i2 - N_GROUPS)
    oh = jnp.where(oh1 | oh2, 1.0, 0.0)
    cnt = jnp.sum(oh, axis=0, keepdims=True)
    run = jnp.floor((cnt + (SUBLANES - 1.0)) * (1.0 / SUBLANES)) * SUBLANES
    li = lax.broadcasted_iota(jnp.int32, (LANES, LANES), 0)
    lj = lax.broadcasted_iota(jnp.int32, (LANES, LANES), 1)
    run_start = jnp.dot(jnp.broadcast_to(run, (SUBLANES, LANES)), jnp.where(li < lj, 1.0, 0.0),
                        precision=_HI, preferred_element_type=jnp.float32)[0:1]
    ri = lax.broadcasted_iota(jnp.int32, (TS, TS), 0)
    ci = lax.broadcasted_iota(jnp.int32, (TS, TS), 1)
    tri = _bf(jnp.where(ci < ri, 1.0, 0.0))
    before = jnp.dot(tri, _bf(oh), preferred_element_type=jnp.float32) + run_start
    pos1 = jnp.sum(jnp.where(oh1, before, 0.0), axis=-1, keepdims=True)
    pos2 = jnp.sum(jnp.where(oh2, before, 0.0), axis=-1, keepdims=True)
    eye = ri == ci
    pos1_row = jnp.sum(jnp.where(eye, pos1, 0.0), axis=0, keepdims=True)
    pos2_row = jnp.sum(jnp.where(eye, pos2, 0.0), axis=0, keepdims=True)
    rr = lax.broadcasted_iota(jnp.int32, (TR, TS), 0).astype(jnp.float32)
    perm = _bf(jnp.where((rr == pos1_row) | (rr == pos2_row), 1.0, 0.0))
    xt_ref[...] = _pack_rows(jnp.dot(perm, _bf(u2), preferred_element_type=jnp.float32))
    cnt_ref[...] = jnp.broadcast_to(cnt, cnt_ref.shape)

    route = jnp.where(lane == 0, pos1, 0.0)
    route = jnp.where(lane == 1, pos2, route)
    route = jnp.where(lane == 2, cw1, route)
    route = jnp.where(lane == 3, cw2, route)
    route_ref[...] = route


def _mix(x, positions, ada, rope, win, wout, wpool, gnw, pscale, ln1w, ln1b, wrt, brt):
    B, S, D = x.shape
    ns = S // TS
    const2 = lambda b, s: (0, 0)
    const3 = lambda b, s: (0, 0, 0)
    tile = lambda b, s: (b, s, 0)
    flat = lambda b, s: (b * ns + s, 0, 0)
    return pl.pallas_call(
        _mix_kernel,
        grid=(B, ns),
        in_specs=[
            pl.BlockSpec((None, TS, D), tile),
            pl.BlockSpec((None, TS, 1), tile),
            pl.BlockSpec((None, 6, D), lambda b, s: (b, 0, 0)),
            pl.BlockSpec((2, LANES), const2),
            pl.BlockSpec((D, IN_COLS), const2),
            pl.BlockSpec((D, D), const2),
            pl.BlockSpec((len(POOL_WINDOWS), POOL_GROUP_DIM, POOL_GROUP_DIM), const3),
            pl.BlockSpec((1, RET_WIDTH), const2),
            pl.BlockSpec((1, POOL_WIDTH), const2),
            pl.BlockSpec((1, D), const2),
            pl.BlockSpec((1, D), const2),
            pl.BlockSpec((D, LANES), const2),
            pl.BlockSpec((1, LANES), const2),
        ],
        out_specs=[
            pl.BlockSpec((None, TS, D), tile),
            pl.BlockSpec((None, 2 * TR, HALF), flat),
            pl.BlockSpec((None, TS, LANES), tile),
            pl.BlockSpec((None, SUBLANES, LANES), flat),
        ],
        out_shape=[
            jax.ShapeDtypeStruct((B, S, D), jnp.float32),
            jax.ShapeDtypeStruct((B * ns, 2 * TR, HALF), jnp.bfloat16),
            jax.ShapeDtypeStruct((B, S, LANES), jnp.float32),
            jax.ShapeDtypeStruct((B * ns, SUBLANES, LANES), jnp.float32),
        ],
        scratch_shapes=[
            pltpu.VMEM((RET_HEADS, HEAD_DIM, HEAD_DIM), jnp.float32),
            pltpu.VMEM((POOL_HALO, POOL_WIDTH), jnp.float32),
            pltpu.VMEM((RET_HEADS, TS, TS), jnp.float32),
            pltpu.VMEM((TS, RET_WIDTH), jnp.float32),
            pltpu.VMEM((TS, RET_WIDTH), jnp.float32),
            pltpu.VMEM((TS, D), jnp.bfloat16),
        ],
        compiler_params=pltpu.CompilerParams(dimension_semantics=("arbitrary", "arbitrary"),
                                             vmem_limit_bytes=VMEM_LIMIT),
        name="mix",
    )(x, positions.reshape(B, S, 1), ada, rope, win, wout, wpool, gnw, pscale, ln1w, ln1b, wrt, brt)


def _group_copy(src_ref, src_row, dst_ref, dst_row, sem):
    return pltpu.make_async_copy(src_ref.at[pl.ds(pl.multiple_of(2 * src_row, 2 * SUBLANES), 2 * SUBLANES), :],
                                 dst_ref.at[pl.ds(pl.multiple_of(2 * dst_row, 2 * SUBLANES), 2 * SUBLANES), :], sem)


def _copy_run(src_ref, src_row, dst_ref, dst_row, n_groups, sem):
    def body(u, carry):
        _group_copy(src_ref, src_row + u * SUBLANES, dst_ref, dst_row + u * SUBLANES, sem).start()
        return carry

    lax.fori_loop(0, n_groups, body, 0)


def _wait_groups(src_ref, dst_ref, n_groups, sem):
    def body(u, carry):
        _group_copy(src_ref, 0, dst_ref, 0, sem).wait()
        return carry

    lax.fori_loop(0, n_groups, body, 0)


def _expert_kernel(pe_ref, pj_ref, pv_ref, t0_ref, t1_ref, so_ref, eo_ref, rn_ref,
                   xt_ref, w1_ref, w3_ref, w2_ref, ys_ref, xbuf, sem):
    g = pl.program_id(0)
    slot = g % 2

    def fetch(page, dst_slot):
        nvp = pv_ref[page]

        @pl.when(nvp > 0)
        def _start():
            e = pe_ref[page]
            lo = pj_ref[page] * TM

            @pl.when(nvp < TM)
            def _zero_tail():
                xbuf[dst_slot] = jnp.zeros(xbuf.shape[1:], xbuf.dtype)

            def gather(i, carry):
                a = jnp.maximum(eo_ref[e, i], lo)
                b = jnp.minimum(eo_ref[e, i] + rn_ref[e, i], lo + TM)
                n = jnp.maximum(b - a, 0) // SUBLANES
                _copy_run(xt_ref, i * TR + so_ref[e, i] + (a - eo_ref[e, i]), xbuf.at[dst_slot], a - lo, n,
                          sem.at[dst_slot])
                return carry

            lax.fori_loop(t0_ref[page], t1_ref[page], gather, 0)

    @pl.when(g == 0)
    def _prime():
        fetch(0, 0)

    @pl.when(g + 1 < pl.num_programs(0))
    def _prefetch():
        fetch(g + 1, 1 - slot)

    nv = pv_ref[g]

    @pl.when(nv > 0)
    def _compute():
        _wait_groups(xt_ref, xbuf.at[slot], nv // SUBLANES, sem.at[slot])
        xl, xh = _unpack_rows(xbuf[slot])
        a = (jnp.dot(xl, w1_ref[:HALF, :], preferred_element_type=jnp.float32)
             + jnp.dot(xh, w1_ref[HALF:, :], preferred_element_type=jnp.float32))
        c = (jnp.dot(xl, w3_ref[:HALF, :], preferred_element_type=jnp.float32)
             + jnp.dot(xh, w3_ref[HALF:, :], preferred_element_type=jnp.float32))
        h = a * jax.nn.sigmoid(a) * c
        y = jnp.dot(_bf(h), w2_ref[...], preferred_element_type=jnp.float32)
        ys_ref[...] = _pack_rows(y)

    @pl.when(nv == 0)
    def _unused():
        ys_ref[...] = jnp.zeros_like(ys_ref)


def _experts(tables, xt, w1, w3, w2, n_pages):
    D = D_MODEL
    n_pref = len(tables)
    grid_spec = pltpu.PrefetchScalarGridSpec(
        num_scalar_prefetch=n_pref,
        grid=(n_pages,),
        in_specs=[pl.BlockSpec(memory_space=pl.ANY),
                  pl.BlockSpec((None, D, D_EXPERT), lambda g, pe, *_: (pe[g], 0, 0)),
                  pl.BlockSpec((None, D, D_EXPERT), lambda g, pe, *_: (pe[g], 0, 0)),
                  pl.BlockSpec((None, D_EXPERT, D), lambda g, pe, *_: (pe[g], 0, 0))],
        out_specs=pl.BlockSpec((2 * TM, HALF), lambda g, *_: (g, 0)),
        scratch_shapes=[pltpu.VMEM((2, 2 * TM, HALF), jnp.bfloat16), pltpu.SemaphoreType.DMA((2,))],
    )
    return pl.pallas_call(
        _expert_kernel,
        grid_spec=grid_spec,
        out_shape=jax.ShapeDtypeStruct((n_pages * 2 * TM, HALF), jnp.bfloat16),
        compiler_params=pltpu.CompilerParams(dimension_semantics=("arbitrary",), vmem_limit_bytes=VMEM_LIMIT),
        name="experts",
    )(*tables, xt, w1, w3, w2)


def _final_kernel(so_ref, src_ref, rn_ref, so_next, src_next, rn_next, x1_ref, route_ref, ada_ref, lnw_ref, lnb_ref,
                  ys_ref, o_ref, ybuf, sem):
    i = pl.program_id(0)
    slot = i % 2

    def fetch(so, src, rn, dst_slot):
        def gather(e, n_total):
            n = rn[0, 0, e] // SUBLANES
            _copy_run(ys_ref, src[0, 0, e], ybuf.at[dst_slot], so[0, 0, e], n, sem.at[dst_slot])
            return n_total + n

        n_total = lax.fori_loop(0, N_EXPERTS, gather, 0)

        def zero_group(u, carry):
            ybuf[dst_slot, pl.ds(pl.multiple_of(u * 2 * SUBLANES, 2 * SUBLANES), 2 * SUBLANES), :] = jnp.zeros(
                (2 * SUBLANES, HALF), jnp.bfloat16)
            return carry

        lax.fori_loop(n_total, TR // SUBLANES, zero_group, 0)

    @pl.when(i == 0)
    def _prime():
        fetch(so_ref, src_ref, rn_ref, 0)

    @pl.when(i + 1 < pl.num_programs(0))
    def _prefetch():
        fetch(so_next, src_next, rn_next, 1 - slot)

    n_groups = (so_ref[0, 0, N_EXPERTS - 1] + rn_ref[0, 0, N_EXPERTS - 1]) // SUBLANES
    _wait_groups(ys_ref, ybuf.at[slot], n_groups, sem.at[slot])

    route = route_ref[...]
    pos1, pos2, cw1, cw2 = route[:, 0:1], route[:, 1:2], route[:, 2:3], route[:, 3:4]
    col = lax.broadcasted_iota(jnp.int32, (TS, TR), 1).astype(jnp.float32)
    wmat = _bf(jnp.where(col == pos1, cw1, 0.0) + jnp.where(col == pos2, cw2, 0.0))
    yl, yh = _unpack_rows(ybuf[slot])
    y = jnp.concatenate([jnp.dot(wmat, yl, preferred_element_type=jnp.float32),
                         jnp.dot(wmat, yh, preferred_element_type=jnp.float32)], axis=-1)
    gate2 = ada_ref[5:6, :]
    o_ref[...] = _ln(ALPHA * x1_ref[...] + gate2 * y) * lnw_ref[...] + lnb_ref[...]


def _final(so, src, rn, x1, route, ada, lnw, lnb, ys, seq_len):
    T, D = x1.shape
    steps_per_seq = seq_len // TS
    n_tiles = T // TS
    smem = lambda: pl.BlockSpec((1, 1, N_EXPERTS), lambda i: (i, 0, 0), memory_space=pltpu.SMEM)
    smem_next = lambda: pl.BlockSpec((1, 1, N_EXPERTS), lambda i: (jnp.minimum(i + 1, n_tiles - 1), 0, 0),
                                     memory_space=pltpu.SMEM)
    return pl.pallas_call(
        _final_kernel,
        grid=(T // TS,),
        in_specs=[smem(), smem(), smem(), smem_next(), smem_next(), smem_next(),
                  pl.BlockSpec((TS, D), lambda i: (i, 0)),
                  pl.BlockSpec((TS, LANES), lambda i: (i, 0)),
                  pl.BlockSpec((None, 6, D), lambda i: (i // steps_per_seq, 0, 0)),
                  pl.BlockSpec((1, D), lambda i: (0, 0)),
                  pl.BlockSpec((1, D), lambda i: (0, 0)),
                  pl.BlockSpec(memory_space=pl.ANY)],
        out_specs=pl.BlockSpec((TS, D), lambda i: (i, 0)),
        out_shape=jax.ShapeDtypeStruct((T, D), jnp.float32),
        scratch_shapes=[pltpu.VMEM((2, 2 * TR, HALF), jnp.bfloat16), pltpu.SemaphoreType.DMA((2,))],
        compiler_params=pltpu.CompilerParams(dimension_semantics=("arbitrary",), vmem_limit_bytes=VMEM_LIMIT),
        name="final",
    )(so, src, rn, so, src, rn, x1, route, ada, lnw, lnb, ys)


def _page_tables(cnt, n_pages):
    nt = cnt.shape[0]
    run = (cnt + SUBLANES - 1) // SUBLANES * SUBLANES
    so = jnp.cumsum(run, axis=1) - run
    eo = jnp.cumsum(run, axis=0) - run
    tot = jnp.sum(run, axis=0)
    pages_e = (tot + TM - 1) // TM
    page_end = jnp.cumsum(pages_e)
    page_start = page_end - pages_e
    g = jnp.arange(n_pages, dtype=jnp.int32)
    pe = jnp.minimum(jnp.sum(g[:, None] >= page_end[None, :], axis=1), N_EXPERTS - 1).astype(jnp.int32)
    used = g < page_end[-1]
    pj = jnp.where(used, g - page_start[pe], 0).astype(jnp.int32)
    pv = jnp.where(used, jnp.clip(tot[pe] - pj * TM, 0, TM), 0).astype(jnp.int32)
    lo = pj * TM
    eo_p = eo.T[pe]
    rn_p = run.T[pe]
    t0 = jnp.sum(eo_p + rn_p <= lo[:, None], axis=1).astype(jnp.int32)
    t1 = jnp.sum(eo_p < (lo + TM)[:, None], axis=1).astype(jnp.int32)
    src = page_start[None, :] * TM + eo
    i32 = lambda a: a.astype(jnp.int32)
    expert_tables = (pe, pj, pv, t0, t1, i32(so.T), i32(eo.T), i32(run.T))
    final_tables = tuple(i32(a).reshape(nt, 1, N_EXPERTS) for a in (so, src, run))
    return expert_tables, final_tables


def kernel(x, c, positions, w_ada, b_ada, w_in, ret_gn_w, w_pool, pool_scale, w_out, ln1_w, ln1_b, w_group, b_group,
           w_router, b_router, w1, w3, w2, ln2_w, ln2_b):
    B, S, D = x.shape
    T = B * S
    assert w_ada.shape[0] == DEPTH and D == D_MODEL and S % TS == 0

    inv_freq = ROPE_BASE ** (-jnp.arange(0, HEAD_DIM, 2, dtype=jnp.float32) / HEAD_DIM)
    half = HEAD_DIM // 2
    rope = jnp.stack([jnp.concatenate([inv_freq, inv_freq]),
                      jnp.concatenate([-jnp.ones((half,), jnp.float32), jnp.ones((half,), jnp.float32)])])

    nt = T // TS
    n_pages = (2 * T + nt * N_EXPERTS * (SUBLANES - 1)) // TM + N_EXPERTS

    for l in range(DEPTH):
        ada = _ada(c, w_ada[l], b_ada[l]).reshape(B, 6, D)
        pad = LANES - N_GROUPS - N_EXPERTS
        wrt = jnp.concatenate([w_group[l], w_router[l], jnp.zeros((D, pad), jnp.float32)], axis=1)
        brt = jnp.concatenate([b_group[l], b_router[l], jnp.zeros((pad,), jnp.float32)]).reshape(1, LANES)
        x1, xt, route, counts = _mix(
            x, positions, ada, rope, _bf(w_in[l]), _bf(w_out[l]), _bf(w_pool[l]),
            ret_gn_w[l].reshape(1, RET_WIDTH), pool_scale[l].reshape(1, POOL_WIDTH),
            ln1_w[l].reshape(1, D), ln1_b[l].reshape(1, D), wrt, brt)

        cnt = counts[:, 0, :N_EXPERTS].astype(jnp.int32)
        expert_tables, final_tables = _page_tables(cnt, n_pages)
        ys = _experts(expert_tables, xt.reshape(nt * 2 * TR, HALF),
                      _bf(w1[l]).reshape(N_EXPERTS, D, D_EXPERT), _bf(w3[l]).reshape(N_EXPERTS, D, D_EXPERT),
                      _bf(w2[l]).reshape(N_EXPERTS, D_EXPERT, D), n_pages)
        x = _final(*final_tables, x1.reshape(T, D), route.reshape(T, LANES), ada,
                   ln2_w[l].reshape(1, D), ln2_b[l].reshape(1, D), ys, S).reshape(B, S, D)
    return x
```

```python
import math

import jax
import jax.numpy as jnp
from jax import lax
from jax.experimental import pallas as pl
from jax.experimental.pallas import tpu as pltpu

D_MODEL = 1024
RET_WIDTH = 512
RET_HEADS = 4
HEAD_DIM = 128
POOL_WIDTH = 512
POOL_WINDOWS = (2, 4, 8, 16)
POOL_GROUP_DIM = 128
IN_COLS = 4 * RET_WIDTH + POOL_WIDTH
N_GROUPS = 4
EXPERTS_PER_GROUP = 8
N_EXPERTS = 32
D_EXPERT = 256
DEPTH = 1
ALPHA = (2.0 * DEPTH) ** 0.25
LN_EPS = 1e-5
ROPE_BASE = 10000.0

LANES = 128
SUBLANES = 8
HALF = D_MODEL // 2
POOL_HALO = 16
TS = 256
TM = 256
TR = -(-(2 * TS + N_EXPERTS * (SUBLANES - 1)) // LANES) * LANES
VMEM_LIMIT = 56 * 1024 * 1024

_LOG_GAMMA = tuple(math.log1p(-(2.0 ** (-5.0 - h))) for h in range(RET_HEADS))
_HI = lax.Precision.HIGHEST


def _ln(x):
    mu = jnp.mean(x, axis=-1, keepdims=True)
    xc = x - mu
    var = jnp.mean(xc * xc, axis=-1, keepdims=True)
    return xc * lax.rsqrt(var + LN_EPS)


def _bf(x):
    return x.astype(jnp.bfloat16)


def _pack_rows(x):
    n = x.shape[0]
    lo = x[:, :HALF].reshape(n // SUBLANES, SUBLANES, HALF)
    hi = x[:, HALF:].reshape(n // SUBLANES, SUBLANES, HALF)
    return _bf(jnp.concatenate([lo, hi], axis=1).reshape(2 * n, HALF))


def _unpack_rows(z):
    n = z.shape[0] // 2
    zf = z.astype(jnp.float32).reshape(n // SUBLANES, 2 * SUBLANES, HALF)
    return _bf(zf[:, :SUBLANES, :].reshape(n, HALF)), _bf(zf[:, SUBLANES:, :].reshape(n, HALF))


def _ada_kernel(c_ref, w_ref, b_ref, o_ref):
    c = c_ref[...]
    ca = c * jax.nn.sigmoid(c)
    o_ref[...] = jnp.dot(ca, w_ref[...], precision=_HI, preferred_element_type=jnp.float32) + b_ref[...]


def _ada(c, w_ada, b_ada):
    B, D = c.shape
    n = w_ada.shape[1]
    return pl.pallas_call(
        _ada_kernel,
        grid=(n // D,),
        in_specs=[pl.BlockSpec((B, D), lambda j: (0, 0)),
                  pl.BlockSpec((D, D), lambda j: (0, j)),
                  pl.BlockSpec((1, D), lambda j: (0, j))],
        out_specs=pl.BlockSpec((B, D), lambda j: (0, j)),
        out_shape=jax.ShapeDtypeStruct((B, n), jnp.float32),
        compiler_params=pltpu.CompilerParams(dimension_semantics=("arbitrary",), vmem_limit_bytes=VMEM_LIMIT),
        name="ada",
    )(c, w_ada, b_ada.reshape(1, n))


def _mix_kernel(x_ref, pos_ref, ada_ref, rope_ref, win_ref, wout_ref, wpool_ref, gnw_ref, pscale_ref,
                ln1w_ref, ln1b_ref, wrt_ref, brt_ref,
                x1_ref, xt_ref, route_ref, cnt_ref,
                state_ref, halo_ref, dmat_ref, qdec_ref, kdec_ref, cat_ref):
    b = pl.program_id(0)
    s = pl.program_id(1)

    @pl.when((b == 0) & (s == 0))
    def _init_tables():
        ri = lax.broadcasted_iota(jnp.int32, (TS, TS), 0)
        ci = lax.broadcasted_iota(jnp.int32, (TS, TS), 1)
        rel = (ri - ci).astype(jnp.float32)
        for h in range(RET_HEADS):
            dmat_ref[h] = jnp.where(rel >= 0.0, jnp.exp(jnp.maximum(rel, 0.0) * _LOG_GAMMA[h]), 0.0)
        row = lax.broadcasted_iota(jnp.int32, (TS, RET_WIDTH), 0).astype(jnp.float32)
        lane = lax.broadcasted_iota(jnp.int32, (TS, RET_WIDTH), 1)
        lg = jnp.full((TS, RET_WIDTH), _LOG_GAMMA[0], jnp.float32)
        for h in range(1, RET_HEADS):
            lg = jnp.where(lane >= h * HEAD_DIM, _LOG_GAMMA[h], lg)
        qdec_ref[...] = jnp.exp((row + 1.0) * lg)
        kdec_ref[...] = jnp.exp((TS - 1.0 - row) * lg)

    @pl.when(s == 0)
    def _init_carries():
        state_ref[...] = jnp.zeros_like(state_ref)
        halo_ref[...] = jnp.zeros_like(halo_ref)

    ada = ada_ref[...]
    shift1, scale1, gate1 = ada[0:1], ada[1:2], ada[2:3]
    shift2, scale2 = ada[3:4], ada[4:5]

    x = x_ref[...]
    u = _bf(_ln(x) * (1.0 + scale1) + shift1)

    ang = pos_ref[...].astype(jnp.float32) * rope_ref[0:1, :]
    cos_t = jnp.cos(ang)
    sin_t = jnp.sin(ang) * rope_ref[1:2, :]

    q = jnp.dot(u, win_ref[:, 0:RET_WIDTH], preferred_element_type=jnp.float32)
    k = jnp.dot(u, win_ref[:, RET_WIDTH:2 * RET_WIDTH], preferred_element_type=jnp.float32)
    v = jnp.dot(u, win_ref[:, 2 * RET_WIDTH:3 * RET_WIDTH], preferred_element_type=jnp.float32)
    g = jnp.dot(u, win_ref[:, 3 * RET_WIDTH:4 * RET_WIDTH], preferred_element_type=jnp.float32)
    p = jnp.dot(u, win_ref[:, 4 * RET_WIDTH:IN_COLS], preferred_element_type=jnp.float32)

    gnw = gnw_ref[...]
    for h in range(RET_HEADS):
        sl = slice(h * HEAD_DIM, (h + 1) * HEAD_DIM)
        qh, kh, vh = q[:, sl], k[:, sl], v[:, sl]
        qr = qh * cos_t + pltpu.roll(qh, HEAD_DIM // 2, 1) * sin_t
        kr = (kh * cos_t + pltpu.roll(kh, HEAD_DIM // 2, 1) * sin_t) * (HEAD_DIM ** -0.5)
        vb = _bf(vh)
        sc = lax.dot_general(_bf(qr), _bf(kr), (((1,), (1,)), ((), ())), preferred_element_type=jnp.float32)
        intra = jnp.dot(_bf(sc * dmat_ref[h]), vb, preferred_element_type=jnp.float32)
        st = state_ref[h]
        cross = jnp.dot(_bf(qr * qdec_ref[:, sl]), _bf(st), preferred_element_type=jnp.float32)
        kv = lax.dot_general(_bf(kr * kdec_ref[:, sl]), vb, (((0,), (0,)), ((), ())),
                             preferred_element_type=jnp.float32)
        state_ref[h] = st * math.exp(TS * _LOG_GAMMA[h]) + kv
        r = _ln(intra + cross) * gnw[:, sl]
        gh = g[:, sl]
        cat_ref[:, sl] = _bf(gh * jax.nn.sigmoid(gh) * r)

    pext = jnp.concatenate([halo_ref[...], p], axis=0)
    halo_ref[...] = p[TS - POOL_HALO:, :]
    t_abs = (s * TS + lax.broadcasted_iota(jnp.int32, (TS, 1), 0) + 1).astype(jnp.float32)
    pscale = pscale_ref[...]
    for grp, w in enumerate(POOL_WINDOWS):
        sl = slice(grp * POOL_GROUP_DIM, (grp + 1) * POOL_GROUP_DIM)
        acc = pext[:, sl]
        shift = 1
        while shift < w:
            acc = acc + pltpu.roll(acc, shift, 0)
            shift *= 2
        pooled = acc[POOL_HALO:, :] / jnp.minimum(t_abs, float(w)) - p[:, sl]
        po = jnp.dot(_bf(pooled), wpool_ref[grp], preferred_element_type=jnp.float32) * pscale[:, sl]
        cat_ref[:, RET_WIDTH + grp * POOL_GROUP_DIM:RET_WIDTH + (grp + 1) * POOL_GROUP_DIM] = _bf(po)

    mix = jnp.dot(cat_ref[...], wout_ref[...], preferred_element_type=jnp.float32)
    x1 = _ln(ALPHA * x + gate1 * mix) * ln1w_ref[...] + ln1b_ref[...]
    x1_ref[...] = x1
    u2 = _ln(x1) * (1.0 + scale2) + shift2

    logits = jnp.dot(u2, wrt_ref[...], precision=_HI, preferred_element_type=jnp.float32) + brt_ref[...]
    lane = lax.broadcasted_iota(jnp.int32, (TS, LANES), 1)
    neg = jnp.float32(-jnp.inf)
    gl = jnp.where(lane < N_GROUPS, logits, neg)
    gmax = jnp.max(gl, axis=-1, keepdims=True)
    gidx = jnp.min(jnp.where(gl == gmax, lane, LANES), axis=-1, keepdims=True)
    gprob = 1.0 / jnp.sum(jnp.exp(gl - gmax), axis=-1, keepdims=True)
    lo = N_GROUPS + EXPERTS_PER_GROUP * gidx
    el = jnp.where((lane >= lo) & (lane < lo + EXPERTS_PER_GROUP), logits, neg)
    m1 = jnp.max(el, axis=-1, keepdims=True)
    i1 = jnp.min(jnp.where(el == m1, lane, LANES), axis=-1, keepdims=True)
    el2 = jnp.where(lane == i1, neg, el)
    m2 = jnp.max(el2, axis=-1, keepdims=True)
    i2 = jnp.min(jnp.where(el2 == m2, lane, LANES), axis=-1, keepdims=True)
    e21 = jnp.exp(m2 - m1)
    den = 1.0 / (1.0 + e21)
    cw1 = gprob * den
    cw2 = gprob * e21 * den

    oh1 = lane == (i1 - N_GROUPS)
    oh2 = lane == (i2 - N_GROUPS)
    oh = jnp.where(oh1 | oh2, 1.0, 0.0)
    cnt = jnp.sum(oh, axis=0, keepdims=True)
    run = jnp.floor((cnt + (SUBLANES - 1.0)) * (1.0 / SUBLANES)) * SUBLANES
    li = lax.broadcasted_iota(jnp.int32, (LANES, LANES), 0)
    lj = lax.broadcasted_iota(jnp.int32, (LANES, LANES), 1)
    run_start = jnp.dot(jnp.broadcast_to(run, (SUBLANES, LANES)), jnp.where(li < lj, 1.0, 0.0),
                        precision=_HI, preferred_element_type=jnp.float32)[0:1]
    ri = lax.broadcasted_iota(jnp.int32, (TS, TS), 0)
    ci = lax.broadcasted_iota(jnp.int32, (TS, TS), 1)
    tri = _bf(jnp.where(ci < ri, 1.0, 0.0))
    before = jnp.dot(tri, _bf(oh), preferred_element_type=jnp.float32) + run_start
    pos1 = jnp.sum(jnp.where(oh1, before, 0.0), axis=-1, keepdims=True)
    pos2 = jnp.sum(jnp.where(oh2, before, 0.0), axis=-1, keepdims=True)
    eye = ri == ci
    pos1_row = jnp.sum(jnp.where(eye, pos1, 0.0), axis=0, keepdims=True)
    pos2_row = jnp.sum(jnp.where(eye, pos2, 0.0), axis=0, keepdims=True)
    rr = lax.broadcasted_iota(jnp.int32, (TR, TS), 0).astype(jnp.float32)
    perm = _bf(jnp.where((rr == pos1_row) | (rr == pos2_row), 1.0, 0.0))
    xt_ref[...] = _pack_rows(jnp.dot(perm, _bf(u2), preferred_element_type=jnp.float32))
    cnt_ref[...] = jnp.broadcast_to(cnt, cnt_ref.shape)

    route = jnp.where(lane == 0, pos1, 0.0)
    route = jnp.where(lane == 1, pos2, route)
    route = jnp.where(lane == 2, cw1, route)
    route = jnp.where(lane == 3, cw2, route)
    route_ref[...] = route


def _mix(x, positions, ada, rope, win, wout, wpool, gnw, pscale, ln1w, ln1b, wrt, brt):
    B, S, D = x.shape
    ns = S // TS
    const2 = lambda b, s: (0, 0)
    const3 = lambda b, s: (0, 0, 0)
    tile = lambda b, s: (b, s, 0)
    flat = lambda b, s: (b * ns + s, 0, 0)
    return pl.pallas_call(
        _mix_kernel,
        grid=(B, ns),
        in_specs=[
            pl.BlockSpec((None, TS, D), tile),
            pl.BlockSpec((None, TS, 1), tile),
            pl.BlockSpec((None, 6, D), lambda b, s: (b, 0, 0)),
            pl.BlockSpec((2, LANES), const2),
            pl.BlockSpec((D, IN_COLS), const2),
            pl.BlockSpec((D, D), const2),
            pl.BlockSpec((len(POOL_WINDOWS), POOL_GROUP_DIM, POOL_GROUP_DIM), const3),
            pl.BlockSpec((1, RET_WIDTH), const2),
            pl.BlockSpec((1, POOL_WIDTH), const2),
            pl.BlockSpec((1, D), const2),
            pl.BlockSpec((1, D), const2),
            pl.BlockSpec((D, LANES), const2),
            pl.BlockSpec((1, LANES), const2),
        ],
        out_specs=[
            pl.BlockSpec((None, TS, D), tile),
            pl.BlockSpec((None, 2 * TR, HALF), flat),
            pl.BlockSpec((None, TS, LANES), tile),
            pl.BlockSpec((None, SUBLANES, LANES), flat),
        ],
        out_shape=[
            jax.ShapeDtypeStruct((B, S, D), jnp.float32),
            jax.ShapeDtypeStruct((B * ns, 2 * TR, HALF), jnp.bfloat16),
            jax.ShapeDtypeStruct((B, S, LANES), jnp.float32),
            jax.ShapeDtypeStruct((B * ns, SUBLANES, LANES), jnp.float32),
        ],
        scratch_shapes=[
            pltpu.VMEM((RET_HEADS, HEAD_DIM, HEAD_DIM), jnp.float32),
            pltpu.VMEM((POOL_HALO, POOL_WIDTH), jnp.float32),
            pltpu.VMEM((RET_HEADS, TS, TS), jnp.float32),
            pltpu.VMEM((TS, RET_WIDTH), jnp.float32),
            pltpu.VMEM((TS, RET_WIDTH), jnp.float32),
            pltpu.VMEM((TS, D), jnp.bfloat16),
        ],
        compiler_params=pltpu.CompilerParams(dimension_semantics=("arbitrary", "arbitrary"),
                                             vmem_limit_bytes=VMEM_LIMIT),
        name="mix",
    )(x, positions.reshape(B, S, 1), ada, rope, win, wout, wpool, gnw, pscale, ln1w, ln1b, wrt, brt)


def _fetch_groups(src_ref, row_of_group, n_groups, dst_ref, sem):
    tile_rows = 2 * SUBLANES
    for u in range(n_groups):
        src = pl.multiple_of(2 * row_of_group(u), tile_rows)
        pltpu.make_async_copy(src_ref.at[pl.ds(src, tile_rows), :],
                              dst_ref.at[pl.ds(u * tile_rows, tile_rows), :], sem).start()


def _wait_fetch(src_ref, dst_ref, sem):
    pltpu.make_async_copy(src_ref.at[pl.ds(0, dst_ref.shape[0]), :], dst_ref, sem).wait()


def _expert_kernel(pe_ref, pv_ref, src_ref, src_next, xt_ref, w1_ref, w3_ref, w2_ref, ys_ref, xbuf, sem):
    g = pl.program_id(0)
    slot = g % 2

    @pl.when(g == 0)
    def _prime():
        _fetch_groups(xt_ref, lambda u: src_ref[0, 0, u], TM // SUBLANES, xbuf.at[0], sem.at[0])

    _fetch_groups(xt_ref, lambda u: src_next[0, 0, u], TM // SUBLANES, xbuf.at[1 - slot], sem.at[1 - slot])
    _wait_fetch(xt_ref, xbuf.at[slot], sem.at[slot])

    nv = pv_ref[g]

    @pl.when(nv > 0)
    def _compute():
        xl, xh = _unpack_rows(xbuf[slot])
        a = (jnp.dot(xl, w1_ref[:HALF, :], preferred_element_type=jnp.float32)
             + jnp.dot(xh, w1_ref[HALF:, :], preferred_element_type=jnp.float32))
        c = (jnp.dot(xl, w3_ref[:HALF, :], preferred_element_type=jnp.float32)
             + jnp.dot(xh, w3_ref[HALF:, :], preferred_element_type=jnp.float32))
        h = a * jax.nn.sigmoid(a) * c
        y = jnp.dot(_bf(h), w2_ref[...], preferred_element_type=jnp.float32)
        ys_ref[...] = _pack_rows(y)

    @pl.when(nv == 0)
    def _unused():
        ys_ref[...] = jnp.zeros_like(ys_ref)

    @pl.when(g == pl.num_programs(0) - 1)
    def _drain():
        _wait_fetch(xt_ref, xbuf.at[1 - slot], sem.at[1 - slot])


def _experts(pe, pv, esrc, xt, w1, w3, w2):
    D = D_MODEL
    n_pages = pe.shape[0]
    groups = TM // SUBLANES
    grid_spec = pltpu.PrefetchScalarGridSpec(
        num_scalar_prefetch=2,
        grid=(n_pages,),
        in_specs=[pl.BlockSpec((1, 1, groups), lambda g, pe, pv: (g, 0, 0), memory_space=pltpu.SMEM),
                  pl.BlockSpec((1, 1, groups), lambda g, pe, pv: (jnp.minimum(g + 1, n_pages - 1), 0, 0),
                               memory_space=pltpu.SMEM),
                  pl.BlockSpec(memory_space=pl.ANY),
                  pl.BlockSpec((None, D, D_EXPERT), lambda g, pe, pv: (pe[g], 0, 0)),
                  pl.BlockSpec((None, D, D_EXPERT), lambda g, pe, pv: (pe[g], 0, 0)),
                  pl.BlockSpec((None, D_EXPERT, D), lambda g, pe, pv: (pe[g], 0, 0))],
        out_specs=pl.BlockSpec((2 * TM, HALF), lambda g, pe, pv: (g, 0)),
        scratch_shapes=[pltpu.VMEM((2, 2 * TM, HALF), jnp.bfloat16), pltpu.SemaphoreType.DMA((2,))],
    )
    return pl.pallas_call(
        _expert_kernel,
        grid_spec=grid_spec,
        out_shape=jax.ShapeDtypeStruct((n_pages * 2 * TM, HALF), jnp.bfloat16),
        compiler_params=pltpu.CompilerParams(dimension_semantics=("arbitrary",), vmem_limit_bytes=VMEM_LIMIT),
        name="experts",
    )(pe, pv, esrc, esrc, xt, w1, w3, w2)


def _final_kernel(src_ref, src_next, x1_ref, route_ref, ada_ref, lnw_ref, lnb_ref, ys_ref, o_ref, ybuf, sem):
    i = pl.program_id(0)
    slot = i % 2

    @pl.when(i == 0)
    def _prime():
        _fetch_groups(ys_ref, lambda u: src_ref[0, 0, u], TR // SUBLANES, ybuf.at[0], sem.at[0])

    _fetch_groups(ys_ref, lambda u: src_next[0, 0, u], TR // SUBLANES, ybuf.at[1 - slot], sem.at[1 - slot])
    _wait_fetch(ys_ref, ybuf.at[slot], sem.at[slot])

    route = route_ref[...]
    pos1, pos2, cw1, cw2 = route[:, 0:1], route[:, 1:2], route[:, 2:3], route[:, 3:4]
    col = lax.broadcasted_iota(jnp.int32, (TS, TR), 1).astype(jnp.float32)
    wmat = _bf(jnp.where(col == pos1, cw1, 0.0) + jnp.where(col == pos2, cw2, 0.0))
    yl, yh = _unpack_rows(ybuf[slot])
    y = jnp.concatenate([jnp.dot(wmat, yl, preferred_element_type=jnp.float32),
                         jnp.dot(wmat, yh, preferred_element_type=jnp.float32)], axis=-1)
    gate2 = ada_ref[5:6, :]
    o_ref[...] = _ln(ALPHA * x1_ref[...] + gate2 * y) * lnw_ref[...] + lnb_ref[...]

    @pl.when(i == pl.num_programs(0) - 1)
    def _drain():
        _wait_fetch(ys_ref, ybuf.at[1 - slot], sem.at[1 - slot])


def _final(fsrc, x1, route, ada, lnw, lnb, ys, seq_len):
    T, D = x1.shape
    steps_per_seq = seq_len // TS
    n_tiles = T // TS
    groups = TR // SUBLANES
    return pl.pallas_call(
        _final_kernel,
        grid=(n_tiles,),
        in_specs=[pl.BlockSpec((1, 1, groups), lambda i: (i, 0, 0), memory_space=pltpu.SMEM),
                  pl.BlockSpec((1, 1, groups), lambda i: (jnp.minimum(i + 1, n_tiles - 1), 0, 0),
                               memory_space=pltpu.SMEM),
                  pl.BlockSpec((TS, D), lambda i: (i, 0)),
                  pl.BlockSpec((TS, LANES), lambda i: (i, 0)),
                  pl.BlockSpec((None, 6, D), lambda i: (i // steps_per_seq, 0, 0)),
                  pl.BlockSpec((1, D), lambda i: (0, 0)),
                  pl.BlockSpec((1, D), lambda i: (0, 0)),
                  pl.BlockSpec(memory_space=pl.ANY)],
        out_specs=pl.BlockSpec((TS, D), lambda i: (i, 0)),
        out_shape=jax.ShapeDtypeStruct((T, D), jnp.float32),
        scratch_shapes=[pltpu.VMEM((2, 2 * TR, HALF), jnp.bfloat16), pltpu.SemaphoreType.DMA((2,))],
        compiler_params=pltpu.CompilerParams(dimension_semantics=("arbitrary",), vmem_limit_bytes=VMEM_LIMIT),
        name="final",
    )(fsrc, fsrc, x1, route, ada, lnw, lnb, ys)


def _gather_tables(cnt, n_pages):
    nt = cnt.shape[0]
    run = (cnt + SUBLANES - 1) // SUBLANES * SUBLANES
    so = jnp.cumsum(run, axis=1) - run
    eo = jnp.cumsum(run, axis=0) - run
    tot = jnp.sum(run, axis=0)
    pages_e = (tot + TM - 1) // TM
    page_end = jnp.cumsum(pages_e)
    page_start = page_end - pages_e
    g = jnp.arange(n_pages, dtype=jnp.int32)
    pe = jnp.minimum(jnp.sum(g[:, None] >= page_end[None, :], axis=1), N_EXPERTS - 1).astype(jnp.int32)
    used = g < page_end[-1]
    pj = jnp.where(used, g - page_start[pe], 0)
    pv = jnp.where(used, jnp.clip(tot[pe] - pj * TM, 0, TM), 0).astype(jnp.int32)

    q = (pj * TM)[:, None] + SUBLANES * jnp.arange(TM // SUBLANES, dtype=jnp.int32)[None, :]
    eo_p, rn_p, so_p = eo.T[pe], run.T[pe], so.T[pe]
    tiles = jnp.arange(nt, dtype=jnp.int32)
    t = jnp.sum(eo_p[:, None, :] + rn_p[:, None, :] <= q[:, :, None], axis=-1)
    base = tiles[None, :] * TR + so_p - eo_p
    src = jnp.sum(jnp.where(tiles[None, None, :] == t[:, :, None], base[:, None, :], 0), axis=-1) + q
    zero_xt = TR - SUBLANES
    esrc = jnp.where((q < tot[pe][:, None]) & used[:, None], src, zero_xt).astype(jnp.int32)

    r = SUBLANES * jnp.arange(TR // SUBLANES, dtype=jnp.int32)
    end = so + run
    experts = jnp.arange(N_EXPERTS, dtype=jnp.int32)
    e = jnp.sum(end[:, None, :] <= r[None, :, None], axis=-1)
    basef = (page_start * TM)[None, :] + eo - so
    srcf = jnp.sum(jnp.where(experts[None, None, :] == e[:, :, None], basef[:, None, :], 0), axis=-1) + r[None, :]
    zero_ys = (n_pages - 1) * TM
    fsrc = jnp.where(r[None, :] < end[:, -1:], srcf, zero_ys).astype(jnp.int32)
    return pe, pv, esrc.reshape(n_pages, 1, -1), fsrc.reshape(nt, 1, -1)


def kernel(x, c, positions, w_ada, b_ada, w_in, ret_gn_w, w_pool, pool_scale, w_out, ln1_w, ln1_b, w_group, b_group,
           w_router, b_router, w1, w3, w2, ln2_w, ln2_b):
    B, S, D = x.shape
    T = B * S
    assert w_ada.shape[0] == DEPTH and D == D_MODEL and S % TS == 0

    inv_freq = ROPE_BASE ** (-jnp.arange(0, HEAD_DIM, 2, dtype=jnp.float32) / HEAD_DIM)
    half = HEAD_DIM // 2
    rope = jnp.stack([jnp.concatenate([inv_freq, inv_freq]),
                      jnp.concatenate([-jnp.ones((half,), jnp.float32), jnp.ones((half,), jnp.float32)])])

    nt = T // TS
    n_pages = (2 * T + nt * N_EXPERTS * (SUBLANES - 1)) // TM + N_EXPERTS + 1

    for l in range(DEPTH):
        ada = _ada(c, w_ada[l], b_ada[l]).reshape(B, 6, D)
        pad = LANES - N_GROUPS - N_EXPERTS
        wrt = jnp.concatenate([w_group[l], w_router[l], jnp.zeros((D, pad), jnp.float32)], axis=1)
        brt = jnp.concatenate([b_group[l], b_router[l], jnp.zeros((pad,), jnp.float32)]).reshape(1, LANES)
        x1, xt, route, counts = _mix(
            x, positions, ada, rope, _bf(w_in[l]), _bf(w_out[l]), _bf(w_pool[l]),
            ret_gn_w[l].reshape(1, RET_WIDTH), pool_scale[l].reshape(1, POOL_WIDTH),
            ln1_w[l].reshape(1, D), ln1_b[l].reshape(1, D), wrt, brt)

        cnt = counts[:, 0, :N_EXPERTS].astype(jnp.int32)
        pe, pv, esrc, fsrc = _gather_tables(cnt, n_pages)
        ys = _experts(pe, pv, esrc, xt.reshape(nt * 2 * TR, HALF),
                      _bf(w1[l]).reshape(N_EXPERTS, D, D_EXPERT), _bf(w3[l]).reshape(N_EXPERTS, D, D_EXPERT),
                      _bf(w2[l]).reshape(N_EXPERTS, D_EXPERT, D))
        x = _final(fsrc, x1.reshape(T, D), route.reshape(T, LANES), ada,
                   ln2_w[l].reshape(1, D), ln2_b[l].reshape(1, D), ys, S).reshape(B, S, D)
    return x
```

```python
import math

import jax
import jax.numpy as jnp
from jax import lax
from jax.experimental import pallas as pl
from jax.experimental.pallas import tpu as pltpu

D_MODEL = 1024
RET_WIDTH = 512
RET_HEADS = 4
HEAD_DIM = 128
POOL_WIDTH = 512
POOL_WINDOWS = (2, 4, 8, 16)
POOL_GROUP_DIM = 128
IN_COLS = 4 * RET_WIDTH + POOL_WIDTH
N_GROUPS = 4
EXPERTS_PER_GROUP = 8
N_EXPERTS = 32
D_EXPERT = 256
DEPTH = 1
ALPHA = (2.0 * DEPTH) ** 0.25
LN_EPS = 1e-5
ROPE_BASE = 10000.0

LANES = 128
SUBLANES = 8
HALF = D_MODEL // 2
POOL_HALO = 16
TS = 256
TM = 256
TR = -(-(2 * TS + N_EXPERTS * (SUBLANES - 1)) // LANES) * LANES
FETCH_AHEAD = 2
FETCH_SLOTS = FETCH_AHEAD + 1
VMEM_LIMIT = 56 * 1024 * 1024

_LOG_GAMMA = tuple(math.log1p(-(2.0 ** (-5.0 - h))) for h in range(RET_HEADS))
_HI = lax.Precision.HIGHEST


def _ln(x):
    mu = jnp.mean(x, axis=-1, keepdims=True)
    xc = x - mu
    var = jnp.mean(xc * xc, axis=-1, keepdims=True)
    return xc * lax.rsqrt(var + LN_EPS)


def _bf(x):
    return x.astype(jnp.bfloat16)


def _pack_rows(x):
    n = x.shape[0]
    lo = x[:, :HALF].reshape(n // SUBLANES, SUBLANES, HALF)
    hi = x[:, HALF:].reshape(n // SUBLANES, SUBLANES, HALF)
    return _bf(jnp.concatenate([lo, hi], axis=1).reshape(2 * n, HALF))


def _unpack_rows(z):
    n = z.shape[0] // 2
    zf = z.astype(jnp.float32).reshape(n // SUBLANES, 2 * SUBLANES, HALF)
    return _bf(zf[:, :SUBLANES, :].reshape(n, HALF)), _bf(zf[:, SUBLANES:, :].reshape(n, HALF))


def _ada_kernel(c_ref, w_ref, b_ref, o_ref):
    c = c_ref[...]
    ca = c * jax.nn.sigmoid(c)
    o_ref[...] = jnp.dot(ca, w_ref[...], precision=_HI, preferred_element_type=jnp.float32) + b_ref[...]


def _ada(c, w_ada, b_ada):
    B, D = c.shape
    n = w_ada.shape[1]
    return pl.pallas_call(
        _ada_kernel,
        grid=(n // D,),
        in_specs=[pl.BlockSpec((B, D), lambda j: (0, 0)),
                  pl.BlockSpec((D, D), lambda j: (0, j)),
                  pl.BlockSpec((1, D), lambda j: (0, j))],
        out_specs=pl.BlockSpec((B, D), lambda j: (0, j)),
        out_shape=jax.ShapeDtypeStruct((B, n), jnp.float32),
        compiler_params=pltpu.CompilerParams(dimension_semantics=("arbitrary",), vmem_limit_bytes=VMEM_LIMIT),
        name="ada",
    )(c, w_ada, b_ada.reshape(1, n))


def _mix_kernel(x_ref, pos_ref, ada_ref, rope_ref, win_ref, wout_ref, wpool_ref, gnw_ref, pscale_ref,
                ln1w_ref, ln1b_ref, wrt_ref, brt_ref,
                x1_ref, xt_ref, route_ref, cnt_ref,
                state_ref, halo_ref, dmat_ref, qdec_ref, kdec_ref, cat_ref):
    b = pl.program_id(0)
    s = pl.program_id(1)

    @pl.when((b == 0) & (s == 0))
    def _init_tables():
        ri = lax.broadcasted_iota(jnp.int32, (TS, TS), 0)
        ci = lax.broadcasted_iota(jnp.int32, (TS, TS), 1)
        rel = (ri - ci).astype(jnp.float32)
        for h in range(RET_HEADS):
            dmat_ref[h] = jnp.where(rel >= 0.0, jnp.exp(jnp.maximum(rel, 0.0) * _LOG_GAMMA[h]), 0.0)
        row = lax.broadcasted_iota(jnp.int32, (TS, RET_WIDTH), 0).astype(jnp.float32)
        lane = lax.broadcasted_iota(jnp.int32, (TS, RET_WIDTH), 1)
        lg = jnp.full((TS, RET_WIDTH), _LOG_GAMMA[0], jnp.float32)
        for h in range(1, RET_HEADS):
            lg = jnp.where(lane >= h * HEAD_DIM, _LOG_GAMMA[h], lg)
        qdec_ref[...] = jnp.exp((row + 1.0) * lg)
        kdec_ref[...] = jnp.exp((TS - 1.0 - row) * lg)

    @pl.when(s == 0)
    def _init_carries():
        state_ref[...] = jnp.zeros_like(state_ref)
        halo_ref[...] = jnp.zeros_like(halo_ref)

    ada = ada_ref[...]
    shift1, scale1, gate1 = ada[0:1], ada[1:2], ada[2:3]
    shift2, scale2 = ada[3:4], ada[4:5]

    x = x_ref[...]
    u = _bf(_ln(x) * (1.0 + scale1) + shift1)

    ang = pos_ref[...].astype(jnp.float32) * rope_ref[0:1, :]
    cos_t = jnp.cos(ang)
    sin_t = jnp.sin(ang) * rope_ref[1:2, :]

    q = jnp.dot(u, win_ref[:, 0:RET_WIDTH], preferred_element_type=jnp.float32)
    k = jnp.dot(u, win_ref[:, RET_WIDTH:2 * RET_WIDTH], preferred_element_type=jnp.float32)
    v = jnp.dot(u, win_ref[:, 2 * RET_WIDTH:3 * RET_WIDTH], preferred_element_type=jnp.float32)
    g = jnp.dot(u, win_ref[:, 3 * RET_WIDTH:4 * RET_WIDTH], preferred_element_type=jnp.float32)
    p = jnp.dot(u, win_ref[:, 4 * RET_WIDTH:IN_COLS], preferred_element_type=jnp.float32)

    gnw = gnw_ref[...]
    for h in range(RET_HEADS):
        sl = slice(h * HEAD_DIM, (h + 1) * HEAD_DIM)
        qh, kh, vh = q[:, sl], k[:, sl], v[:, sl]
        qr = qh * cos_t + pltpu.roll(qh, HEAD_DIM // 2, 1) * sin_t
        kr = (kh * cos_t + pltpu.roll(kh, HEAD_DIM // 2, 1) * sin_t) * (HEAD_DIM ** -0.5)
        vb = _bf(vh)
        sc = lax.dot_general(_bf(qr), _bf(kr), (((1,), (1,)), ((), ())), preferred_element_type=jnp.float32)
        intra = jnp.dot(_bf(sc * dmat_ref[h]), vb, preferred_element_type=jnp.float32)
        st = state_ref[h]
        cross = jnp.dot(_bf(qr * qdec_ref[:, sl]), _bf(st), preferred_element_type=jnp.float32)
        kv = lax.dot_general(_bf(kr * kdec_ref[:, sl]), vb, (((0,), (0,)), ((), ())),
                             preferred_element_type=jnp.float32)
        state_ref[h] = st * math.exp(TS * _LOG_GAMMA[h]) + kv
        r = _ln(intra + cross) * gnw[:, sl]
        gh = g[:, sl]
        cat_ref[:, sl] = _bf(gh * jax.nn.sigmoid(gh) * r)

    pext = jnp.concatenate([halo_ref[...], p], axis=0)
    halo_ref[...] = p[TS - POOL_HALO:, :]
    t_abs = (s * TS + lax.broadcasted_iota(jnp.int32, (TS, 1), 0) + 1).astype(jnp.float32)
    pscale = pscale_ref[...]
    for grp, w in enumerate(POOL_WINDOWS):
        sl = slice(grp * POOL_GROUP_DIM, (grp + 1) * POOL_GROUP_DIM)
        acc = pext[:, sl]
        shift = 1
        while shift < w:
            acc = acc + pltpu.roll(acc, shift, 0)
            shift *= 2
        pooled = acc[POOL_HALO:, :] / jnp.minimum(t_abs, float(w)) - p[:, sl]
        po = jnp.dot(_bf(pooled), wpool_ref[grp], preferred_element_type=jnp.float32) * pscale[:, sl]
        cat_ref[:, RET_WIDTH + grp * POOL_GROUP_DIM:RET_WIDTH + (grp + 1) * POOL_GROUP_DIM] = _bf(po)

    mix = jnp.dot(cat_ref[...], wout_ref[...], preferred_element_type=jnp.float32)
    x1 = _ln(ALPHA * x + gate1 * mix) * ln1w_ref[...] + ln1b_ref[...]
    x1_ref[...] = x1
    u2 = _ln(x1) * (1.0 + scale2) + shift2

    logits = jnp.dot(u2, wrt_ref[...], precision=_HI, preferred_element_type=jnp.float32) + brt_ref[...]
    lane = lax.broadcasted_iota(jnp.int32, (TS, LANES), 1)
    neg = jnp.float32(-jnp.inf)
    gl = jnp.where(lane < N_GROUPS, logits, neg)
    gmax = jnp.max(gl, axis=-1, keepdims=True)
    gidx = jnp.min(jnp.where(gl == gmax, lane, LANES), axis=-1, keepdims=True)
    gprob = 1.0 / jnp.sum(jnp.exp(gl - gmax), axis=-1, keepdims=True)
    lo = N_GROUPS + EXPERTS_PER_GROUP * gidx
    el = jnp.where((lane >= lo) & (lane < lo + EXPERTS_PER_GROUP), logits, neg)
    m1 = jnp.max(el, axis=-1, keepdims=True)
    i1 = jnp.min(jnp.where(el == m1, lane, LANES), axis=-1, keepdims=True)
    el2 = jnp.where(lane == i1, neg, el)
    m2 = jnp.max(el2, axis=-1, keepdims=True)
    i2 = jnp.min(jnp.where(el2 == m2, lane, LANES), axis=-1, keepdims=True)
    e21 = jnp.exp(m2 - m1)
    den = 1.0 / (1.0 + e21)
    cw1 = gprob * den
    cw2 = gprob * e21 * den

    oh1 = lane == (i1 - N_GROUPS)
    oh2 = lane == (i2 - N_GROUPS)
    oh = jnp.where(oh1 | oh2, 1.0, 0.0)
    cnt = jnp.sum(oh, axis=0, keepdims=True)
    run = jnp.floor((cnt + (SUBLANES - 1.0)) * (1.0 / SUBLANES)) * SUBLANES
    li = lax.broadcasted_iota(jnp.int32, (LANES, LANES), 0)
    lj = lax.broadcasted_iota(jnp.int32, (LANES, LANES), 1)
    run_start = jnp.dot(jnp.broadcast_to(run, (SUBLANES, LANES)), jnp.where(li < lj, 1.0, 0.0),
                        precision=_HI, preferred_element_type=jnp.float32)[0:1]
    ri = lax.broadcasted_iota(jnp.int32, (TS, TS), 0)
    ci = lax.broadcasted_iota(jnp.int32, (TS, TS), 1)
    tri = _bf(jnp.where(ci < ri, 1.0, 0.0))
    before = jnp.dot(tri, _bf(oh), preferred_element_type=jnp.float32) + run_start
    pos1 = jnp.sum(jnp.where(oh1, before, 0.0), axis=-1, keepdims=True)
    pos2 = jnp.sum(jnp.where(oh2, before, 0.0), axis=-1, keepdims=True)
    eye = ri == ci
    pos1_row = jnp.sum(jnp.where(eye, pos1, 0.0), axis=0, keepdims=True)
    pos2_row = jnp.sum(jnp.where(eye, pos2, 0.0), axis=0, keepdims=True)
    rr = lax.broadcasted_iota(jnp.int32, (TR, TS), 0).astype(jnp.float32)
    perm = _bf(jnp.where((rr == pos1_row) | (rr == pos2_row), 1.0, 0.0))
    xt_ref[...] = _pack_rows(jnp.dot(perm, _bf(u2), preferred_element_type=jnp.float32))
    cnt_ref[...] = jnp.broadcast_to(cnt, cnt_ref.shape)

    route = jnp.where(lane == 0, pos1, 0.0)
    route = jnp.where(lane == 1, pos2, route)
    route = jnp.where(lane == 2, cw1, route)
    route = jnp.where(lane == 3, cw2, route)
    route_ref[...] = route


def _mix(x, positions, ada, rope, win, wout, wpool, gnw, pscale, ln1w, ln1b, wrt, brt):
    B, S, D = x.shape
    ns = S // TS
    const2 = lambda b, s: (0, 0)
    const3 = lambda b, s: (0, 0, 0)
    tile = lambda b, s: (b, s, 0)
    flat = lambda b, s: (b * ns + s, 0, 0)
    return pl.pallas_call(
        _mix_kernel,
        grid=(B, ns),
        in_specs=[
            pl.BlockSpec((None, TS, D), tile),
            pl.BlockSpec((None, TS, 1), tile),
            pl.BlockSpec((None, 6, D), lambda b, s: (b, 0, 0)),
            pl.BlockSpec((2, LANES), const2),
            pl.BlockSpec((D, IN_COLS), const2),
            pl.BlockSpec((D, D), const2),
            pl.BlockSpec((len(POOL_WINDOWS), POOL_GROUP_DIM, POOL_GROUP_DIM), const3),
            pl.BlockSpec((1, RET_WIDTH), const2),
            pl.BlockSpec((1, POOL_WIDTH), const2),
            pl.BlockSpec((1, D), const2),
            pl.BlockSpec((1, D), const2),
            pl.BlockSpec((D, LANES), const2),
            pl.BlockSpec((1, LANES), const2),
        ],
        out_specs=[
            pl.BlockSpec((None, TS, D), tile),
            pl.BlockSpec((None, 2 * TR, HALF), flat),
            pl.BlockSpec((None, TS, LANES), tile),
            pl.BlockSpec((None, SUBLANES, LANES), flat),
        ],
        out_shape=[
            jax.ShapeDtypeStruct((B, S, D), jnp.float32),
            jax.ShapeDtypeStruct((B * ns, 2 * TR, HALF), jnp.bfloat16),
            jax.ShapeDtypeStruct((B, S, LANES), jnp.float32),
            jax.ShapeDtypeStruct((B * ns, SUBLANES, LANES), jnp.float32),
        ],
        scratch_shapes=[
            pltpu.VMEM((RET_HEADS, HEAD_DIM, HEAD_DIM), jnp.float32),
            pltpu.VMEM((POOL_HALO, POOL_WIDTH), jnp.float32),
            pltpu.VMEM((RET_HEADS, TS, TS), jnp.float32),
            pltpu.VMEM((TS, RET_WIDTH), jnp.float32),
            pltpu.VMEM((TS, RET_WIDTH), jnp.float32),
            pltpu.VMEM((TS, D), jnp.bfloat16),
        ],
        compiler_params=pltpu.CompilerParams(dimension_semantics=("arbitrary", "arbitrary"),
                                             vmem_limit_bytes=VMEM_LIMIT),
        name="mix",
    )(x, positions.reshape(B, S, 1), ada, rope, win, wout, wpool, gnw, pscale, ln1w, ln1b, wrt, brt)


def _fetch_groups(src_ref, row_of_group, n_groups, dst_ref, sem):
    tile_rows = 2 * SUBLANES
    for u in range(n_groups):
        src = pl.multiple_of(2 * row_of_group(u), tile_rows)
        pltpu.make_async_copy(src_ref.at[pl.ds(src, tile_rows), :],
                              dst_ref.at[pl.ds(u * tile_rows, tile_rows), :], sem).start()


def _wait_fetch(src_ref, dst_ref, sem):
    pltpu.make_async_copy(src_ref.at[pl.ds(0, dst_ref.shape[0]), :], dst_ref, sem).wait()


def _expert_kernel(pe_ref, pv_ref, src_ref, xt_ref, w1_ref, w3_ref, w2_ref, ys_ref, xbuf, sem):
    g = pl.program_id(0)
    last = pl.num_programs(0) - 1
    groups = TM // SUBLANES

    def fetch(page, dst_slot):
        _fetch_groups(xt_ref, lambda u: src_ref[page * groups + u], groups, xbuf.at[dst_slot], sem.at[dst_slot])

    @pl.when(g == 0)
    def _prime():
        for k in range(FETCH_AHEAD):
            fetch(k, k)

    fetch(jnp.minimum(g + FETCH_AHEAD, last), (g + FETCH_AHEAD) % FETCH_SLOTS)
    slot = g % FETCH_SLOTS
    _wait_fetch(xt_ref, xbuf.at[slot], sem.at[slot])

    nv = pv_ref[g]

    @pl.when(nv > 0)
    def _compute():
        xl, xh = _unpack_rows(xbuf[slot])
        a = (jnp.dot(xl, w1_ref[:HALF, :], preferred_element_type=jnp.float32)
             + jnp.dot(xh, w1_ref[HALF:, :], preferred_element_type=jnp.float32))
        c = (jnp.dot(xl, w3_ref[:HALF, :], preferred_element_type=jnp.float32)
             + jnp.dot(xh, w3_ref[HALF:, :], preferred_element_type=jnp.float32))
        h = a * jax.nn.sigmoid(a) * c
        y = jnp.dot(_bf(h), w2_ref[...], preferred_element_type=jnp.float32)
        ys_ref[...] = _pack_rows(y)

    @pl.when(nv == 0)
    def _unused():
        ys_ref[...] = jnp.zeros_like(ys_ref)

    @pl.when(g == last)
    def _drain():
        for k in range(1, FETCH_SLOTS):
            other = (g + k) % FETCH_SLOTS
            _wait_fetch(xt_ref, xbuf.at[other], sem.at[other])


def _experts(pe, pv, esrc, xt, w1, w3, w2):
    D = D_MODEL
    n_pages = pe.shape[0]
    assert n_pages > FETCH_AHEAD
    grid_spec = pltpu.PrefetchScalarGridSpec(
        num_scalar_prefetch=3,
        grid=(n_pages,),
        in_specs=[pl.BlockSpec(memory_space=pl.ANY),
                  pl.BlockSpec((None, D, D_EXPERT), lambda g, pe, pv, src: (pe[g], 0, 0)),
                  pl.BlockSpec((None, D, D_EXPERT), lambda g, pe, pv, src: (pe[g], 0, 0)),
                  pl.BlockSpec((None, D_EXPERT, D), lambda g, pe, pv, src: (pe[g], 0, 0))],
        out_specs=pl.BlockSpec((2 * TM, HALF), lambda g, pe, pv, src: (g, 0)),
        scratch_shapes=[pltpu.VMEM((FETCH_SLOTS, 2 * TM, HALF), jnp.bfloat16),
                        pltpu.SemaphoreType.DMA((FETCH_SLOTS,))],
    )
    return pl.pallas_call(
        _expert_kernel,
        grid_spec=grid_spec,
        out_shape=jax.ShapeDtypeStruct((n_pages * 2 * TM, HALF), jnp.bfloat16),
        compiler_params=pltpu.CompilerParams(dimension_semantics=("arbitrary",), vmem_limit_bytes=VMEM_LIMIT),
        name="experts",
    )(pe, pv, esrc, xt, w1, w3, w2)


def _final_kernel(src_ref, x1_ref, route_ref, ada_ref, lnw_ref, lnb_ref, ys_ref, o_ref, ybuf, sem):
    i = pl.program_id(0)
    last = pl.num_programs(0) - 1
    groups = TR // SUBLANES

    def fetch(tile, dst_slot):
        _fetch_groups(ys_ref, lambda u: src_ref[tile * groups + u], groups, ybuf.at[dst_slot], sem.at[dst_slot])

    @pl.when(i == 0)
    def _prime():
        for k in range(FETCH_AHEAD):
            fetch(k, k)

    fetch(jnp.minimum(i + FETCH_AHEAD, last), (i + FETCH_AHEAD) % FETCH_SLOTS)
    slot = i % FETCH_SLOTS
    _wait_fetch(ys_ref, ybuf.at[slot], sem.at[slot])

    route = route_ref[...]
    pos1, pos2, cw1, cw2 = route[:, 0:1], route[:, 1:2], route[:, 2:3], route[:, 3:4]
    col = lax.broadcasted_iota(jnp.int32, (TS, TR), 1).astype(jnp.float32)
    wmat = _bf(jnp.where(col == pos1, cw1, 0.0) + jnp.where(col == pos2, cw2, 0.0))
    yl, yh = _unpack_rows(ybuf[slot])
    y = jnp.concatenate([jnp.dot(wmat, yl, preferred_element_type=jnp.float32),
                         jnp.dot(wmat, yh, preferred_element_type=jnp.float32)], axis=-1)
    gate2 = ada_ref[5:6, :]
    o_ref[...] = _ln(ALPHA * x1_ref[...] + gate2 * y) * lnw_ref[...] + lnb_ref[...]

    @pl.when(i == last)
    def _drain():
        for k in range(1, FETCH_SLOTS):
            other = (i + k) % FETCH_SLOTS
            _wait_fetch(ys_ref, ybuf.at[other], sem.at[other])


def _final(fsrc, x1, route, ada, lnw, lnb, ys, seq_len):
    T, D = x1.shape
    steps_per_seq = seq_len // TS
    n_tiles = T // TS
    assert n_tiles > FETCH_AHEAD
    grid_spec = pltpu.PrefetchScalarGridSpec(
        num_scalar_prefetch=1,
        grid=(n_tiles,),
        in_specs=[pl.BlockSpec((TS, D), lambda i, src: (i, 0)),
                  pl.BlockSpec((TS, LANES), lambda i, src: (i, 0)),
                  pl.BlockSpec((None, 6, D), lambda i, src: (i // steps_per_seq, 0, 0)),
                  pl.BlockSpec((1, D), lambda i, src: (0, 0)),
                  pl.BlockSpec((1, D), lambda i, src: (0, 0)),
                  pl.BlockSpec(memory_space=pl.ANY)],
        out_specs=pl.BlockSpec((TS, D), lambda i, src: (i, 0)),
        scratch_shapes=[pltpu.VMEM((FETCH_SLOTS, 2 * TR, HALF), jnp.bfloat16),
                        pltpu.SemaphoreType.DMA((FETCH_SLOTS,))],
    )
    return pl.pallas_call(
        _final_kernel,
        grid_spec=grid_spec,
        out_shape=jax.ShapeDtypeStruct((T, D), jnp.float32),
        compiler_params=pltpu.CompilerParams(dimension_semantics=("arbitrary",), vmem_limit_bytes=VMEM_LIMIT),
        name="final",
    )(fsrc, x1, route, ada, lnw, lnb, ys)


def _gather_tables(cnt, n_pages):
    nt = cnt.shape[0]
    run = (cnt + SUBLANES - 1) // SUBLANES * SUBLANES
    so = jnp.cumsum(run, axis=1) - run
    eo = jnp.cumsum(run, axis=0) - run
    tot = jnp.sum(run, axis=0)
    pages_e = (tot + TM - 1) // TM
    page_end = jnp.cumsum(pages_e)
    page_start = page_end - pages_e
    g = jnp.arange(n_pages, dtype=jnp.int32)
    pe = jnp.minimum(jnp.sum(g[:, None] >= page_end[None, :], axis=1), N_EXPERTS - 1).astype(jnp.int32)
    used = g < page_end[-1]
    pj = jnp.where(used, g - page_start[pe], 0)
    pv = jnp.where(used, jnp.clip(tot[pe] - pj * TM, 0, TM), 0).astype(jnp.int32)

    q = (pj * TM)[:, None] + SUBLANES * jnp.arange(TM // SUBLANES, dtype=jnp.int32)[None, :]
    eo_p, rn_p, so_p = eo.T[pe], run.T[pe], so.T[pe]
    tiles = jnp.arange(nt, dtype=jnp.int32)
    t = jnp.sum(eo_p[:, None, :] + rn_p[:, None, :] <= q[:, :, None], axis=-1)
    base = tiles[None, :] * TR + so_p - eo_p
    src = jnp.sum(jnp.where(tiles[None, None, :] == t[:, :, None], base[:, None, :], 0), axis=-1) + q
    zero_xt = TR - SUBLANES
    esrc = jnp.where((q < tot[pe][:, None]) & used[:, None], src, zero_xt).astype(jnp.int32)

    r = SUBLANES * jnp.arange(TR // SUBLANES, dtype=jnp.int32)
    end = so + run
    experts = jnp.arange(N_EXPERTS, dtype=jnp.int32)
    e = jnp.sum(end[:, None, :] <= r[None, :, None], axis=-1)
    basef = (page_start * TM)[None, :] + eo - so
    srcf = jnp.sum(jnp.where(experts[None, None, :] == e[:, :, None], basef[:, None, :], 0), axis=-1) + r[None, :]
    zero_ys = (n_pages - 1) * TM
    fsrc = jnp.where(r[None, :] < end[:, -1:], srcf, zero_ys).astype(jnp.int32)
    return pe, pv, esrc.reshape(-1), fsrc.reshape(-1)


def kernel(x, c, positions, w_ada, b_ada, w_in, ret_gn_w, w_pool, pool_scale, w_out, ln1_w, ln1_b, w_group, b_group,
           w_router, b_router, w1, w3, w2, ln2_w, ln2_b):
    B, S, D = x.shape
    T = B * S
    assert w_ada.shape[0] == DEPTH and D == D_MODEL and S % TS == 0

    inv_freq = ROPE_BASE ** (-jnp.arange(0, HEAD_DIM, 2, dtype=jnp.float32) / HEAD_DIM)
    half = HEAD_DIM // 2
    rope = jnp.stack([jnp.concatenate([inv_freq, inv_freq]),
                      jnp.concatenate([-jnp.ones((half,), jnp.float32), jnp.ones((half,), jnp.float32)])])

    nt = T // TS
    n_pages = (2 * T + nt * N_EXPERTS * (SUBLANES - 1)) // TM + N_EXPERTS + 1

    for l in range(DEPTH):
        ada = _ada(c, w_ada[l], b_ada[l]).reshape(B, 6, D)
        pad = LANES - N_GROUPS - N_EXPERTS
        wrt = jnp.concatenate([w_group[l], w_router[l], jnp.zeros((D, pad), jnp.float32)], axis=1)
        brt = jnp.concatenate([b_group[l], b_router[l], jnp.zeros((pad,), jnp.float32)]).reshape(1, LANES)
        x1, xt, route, counts = _mix(
            x, positions, ada, rope, _bf(w_in[l]), _bf(w_out[l]), _bf(w_pool[l]),
            ret_gn_w[l].reshape(1, RET_WIDTH), pool_scale[l].reshape(1, POOL_WIDTH),
            ln1_w[l].reshape(1, D), ln1_b[l].reshape(1, D), wrt, brt)

        cnt = counts[:, 0, :N_EXPERTS].astype(jnp.int32)
        pe, pv, esrc, fsrc = _gather_tables(cnt, n_pages)
        ys = _experts(pe, pv, esrc, xt.reshape(nt * 2 * TR, HALF),
                      _bf(w1[l]).reshape(N_EXPERTS, D, D_EXPERT), _bf(w3[l]).reshape(N_EXPERTS, D, D_EXPERT),
                      _bf(w2[l]).reshape(N_EXPERTS, D_EXPERT, D))
        x = _final(fsrc, x1.reshape(T, D), route.reshape(T, LANES), ada,
                   ln2_w[l].reshape(1, D), ln2_b[l].reshape(1, D), ys, S).reshape(B, S, D)
    return x
```

```python
import math

import jax
import jax.numpy as jnp
from jax import lax
from jax.experimental import pallas as pl
from jax.experimental.pallas import tpu as pltpu

D_MODEL = 1024
RET_WIDTH = 512
RET_HEADS = 4
HEAD_DIM = 128
POOL_WIDTH = 512
POOL_WINDOWS = (2, 4, 8, 16)
POOL_GROUP_DIM = 128
IN_COLS = 4 * RET_WIDTH + POOL_WIDTH
N_GROUPS = 4
EXPERTS_PER_GROUP = 8
N_EXPERTS = 32
D_EXPERT = 256
DEPTH = 1
ALPHA = (2.0 * DEPTH) ** 0.25
LN_EPS = 1e-5
ROPE_BASE = 10000.0

LANES = 128
SUBLANES = 8
HALF = D_MODEL // 2
POOL_HALO = 16
TS = 256
TM = 256
TR = -(-(2 * TS + N_EXPERTS * (SUBLANES - 1)) // LANES) * LANES
ROUTE_ROWS = SUBLANES + N_EXPERTS
FETCH_AHEAD = 2
FETCH_SLOTS = FETCH_AHEAD + 1
VMEM_LIMIT = 56 * 1024 * 1024

_LOG_GAMMA = tuple(math.log1p(-(2.0 ** (-5.0 - h))) for h in range(RET_HEADS))
_HI = lax.Precision.HIGHEST


def _ln(x):
    mu = jnp.mean(x, axis=-1, keepdims=True)
    xc = x - mu
    var = jnp.mean(xc * xc, axis=-1, keepdims=True)
    return xc * lax.rsqrt(var + LN_EPS)


def _bf(x):
    return x.astype(jnp.bfloat16)


def _pack_rows(x):
    n = x.shape[0]
    lo = x[:, :HALF].reshape(n // SUBLANES, SUBLANES, HALF)
    hi = x[:, HALF:].reshape(n // SUBLANES, SUBLANES, HALF)
    return _bf(jnp.concatenate([lo, hi], axis=1).reshape(2 * n, HALF))


def _unpack_rows(z):
    n = z.shape[0] // 2
    zf = z.astype(jnp.float32).reshape(n // SUBLANES, 2 * SUBLANES, HALF)
    return _bf(zf[:, :SUBLANES, :].reshape(n, HALF)), _bf(zf[:, SUBLANES:, :].reshape(n, HALF))


def _ada_kernel(c_ref, w_ref, b_ref, o_ref):
    c = c_ref[...]
    ca = c * jax.nn.sigmoid(c)
    o_ref[...] = jnp.dot(ca, w_ref[...], precision=_HI, preferred_element_type=jnp.float32) + b_ref[...]


def _ada(c, w_ada, b_ada):
    B, D = c.shape
    n = w_ada.shape[1]
    return pl.pallas_call(
        _ada_kernel,
        grid=(n // D,),
        in_specs=[pl.BlockSpec((B, D), lambda j: (0, 0)),
                  pl.BlockSpec((D, D), lambda j: (0, j)),
                  pl.BlockSpec((1, D), lambda j: (0, j))],
        out_specs=pl.BlockSpec((B, D), lambda j: (0, j)),
        out_shape=jax.ShapeDtypeStruct((B, n), jnp.float32),
        compiler_params=pltpu.CompilerParams(dimension_semantics=("arbitrary",), vmem_limit_bytes=VMEM_LIMIT),
        name="ada",
    )(c, w_ada, b_ada.reshape(1, n))


def _mix_kernel(x_ref, pos_ref, ada_ref, rope_ref, win_ref, wout_ref, wpool_ref, gnw_ref, pscale_ref,
                ln1w_ref, ln1b_ref, wrt_ref, brt_ref,
                x1_ref, xt_ref, route_ref, cnt_ref,
                state_ref, halo_ref, dmat_ref, qdec_ref, kdec_ref, cat_ref):
    b = pl.program_id(0)
    s = pl.program_id(1)

    @pl.when((b == 0) & (s == 0))
    def _init_tables():
        ri = lax.broadcasted_iota(jnp.int32, (TS, TS), 0)
        ci = lax.broadcasted_iota(jnp.int32, (TS, TS), 1)
        rel = (ri - ci).astype(jnp.float32)
        for h in range(RET_HEADS):
            dmat_ref[h] = jnp.where(rel >= 0.0, jnp.exp(jnp.maximum(rel, 0.0) * _LOG_GAMMA[h]), 0.0)
        row = lax.broadcasted_iota(jnp.int32, (TS, RET_WIDTH), 0).astype(jnp.float32)
        lane = lax.broadcasted_iota(jnp.int32, (TS, RET_WIDTH), 1)
        lg = jnp.full((TS, RET_WIDTH), _LOG_GAMMA[0], jnp.float32)
        for h in range(1, RET_HEADS):
            lg = jnp.where(lane >= h * HEAD_DIM, _LOG_GAMMA[h], lg)
        qdec_ref[...] = jnp.exp((row + 1.0) * lg)
        kdec_ref[...] = jnp.exp((TS - 1.0 - row) * lg)

    @pl.when(s == 0)
    def _init_carries():
        state_ref[...] = jnp.zeros_like(state_ref)
        halo_ref[...] = jnp.zeros_like(halo_ref)

    ada = ada_ref[...]
    shift1, scale1, gate1 = ada[0:1], ada[1:2], ada[2:3]
    shift2, scale2 = ada[3:4], ada[4:5]

    x = x_ref[...]
    u = _bf(_ln(x) * (1.0 + scale1) + shift1)

    posf = pos_ref[...].astype(jnp.float32)
    hl = lax.broadcasted_iota(jnp.int32, (TS // 2, HEAD_DIM), 1) < HEAD_DIM // 2
    ang = jnp.where(hl, posf[:TS // 2], posf[TS // 2:]) * rope_ref[0:1, :]
    cos_p, sin_p = jnp.cos(ang), jnp.sin(ang)
    cos_s, sin_s = pltpu.roll(cos_p, HEAD_DIM // 2, 1), pltpu.roll(sin_p, HEAD_DIM // 2, 1)
    cos_t = jnp.concatenate([jnp.where(hl, cos_p, cos_s), jnp.where(hl, cos_s, cos_p)], axis=0)
    sin_t = jnp.concatenate([jnp.where(hl, sin_p, sin_s), jnp.where(hl, sin_s, sin_p)], axis=0) * rope_ref[1:2, :]

    q = jnp.dot(u, win_ref[:, 0:RET_WIDTH], preferred_element_type=jnp.float32)
    k = jnp.dot(u, win_ref[:, RET_WIDTH:2 * RET_WIDTH], preferred_element_type=jnp.float32)
    v = jnp.dot(u, win_ref[:, 2 * RET_WIDTH:3 * RET_WIDTH], preferred_element_type=jnp.float32)
    g = jnp.dot(u, win_ref[:, 3 * RET_WIDTH:4 * RET_WIDTH], preferred_element_type=jnp.float32)
    p = jnp.dot(u, win_ref[:, 4 * RET_WIDTH:IN_COLS], preferred_element_type=jnp.float32)

    gnw = gnw_ref[...]
    for h in range(RET_HEADS):
        sl = slice(h * HEAD_DIM, (h + 1) * HEAD_DIM)
        qh, kh, vh = q[:, sl], k[:, sl], v[:, sl]
        qr = qh * cos_t + pltpu.roll(qh, HEAD_DIM // 2, 1) * sin_t
        kr = (kh * cos_t + pltpu.roll(kh, HEAD_DIM // 2, 1) * sin_t) * (HEAD_DIM ** -0.5)
        vb = _bf(vh)
        sc = lax.dot_general(_bf(qr), _bf(kr), (((1,), (1,)), ((), ())), preferred_element_type=jnp.float32)
        intra = jnp.dot(_bf(sc * dmat_ref[h]), vb, preferred_element_type=jnp.float32)
        st = state_ref[h]
        cross = jnp.dot(_bf(qr * qdec_ref[:, sl]), _bf(st), preferred_element_type=jnp.float32)
        kv = lax.dot_general(_bf(kr * kdec_ref[:, sl]), vb, (((0,), (0,)), ((), ())),
                             preferred_element_type=jnp.float32)
        state_ref[h] = st * math.exp(TS * _LOG_GAMMA[h]) + kv
        r = _ln(intra + cross) * gnw[:, sl]
        gh = g[:, sl]
        cat_ref[:, sl] = _bf(gh * jax.nn.sigmoid(gh) * r)

    pext = jnp.concatenate([halo_ref[...], p], axis=0)
    halo_ref[...] = p[TS - POOL_HALO:, :]
    t_abs = (s * TS + lax.broadcasted_iota(jnp.int32, (TS, 1), 0) + 1).astype(jnp.float32)
    pscale = pscale_ref[...]
    for grp, w in enumerate(POOL_WINDOWS):
        sl = slice(grp * POOL_GROUP_DIM, (grp + 1) * POOL_GROUP_DIM)
        acc = pext[:, sl]
        shift = 1
        while shift < w:
            acc = acc + pltpu.roll(acc, shift, 0)
            shift *= 2
        pooled = acc[POOL_HALO:, :] / jnp.minimum(t_abs, float(w)) - p[:, sl]
        po = jnp.dot(_bf(pooled), wpool_ref[grp], preferred_element_type=jnp.float32) * pscale[:, sl]
        cat_ref[:, RET_WIDTH + grp * POOL_GROUP_DIM:RET_WIDTH + (grp + 1) * POOL_GROUP_DIM] = _bf(po)

    mix = jnp.dot(cat_ref[...], wout_ref[...], preferred_element_type=jnp.float32)
    x1 = _ln(ALPHA * x + gate1 * mix) * ln1w_ref[...] + ln1b_ref[...]
    x1_ref[...] = x1
    u2 = _ln(x1) * (1.0 + scale2) + shift2

    w = wrt_ref[...]
    w_hi = _bf(w)
    w_lo = _bf(w - w_hi.astype(jnp.float32))
    u2_hi = _bf(u2)
    u2_lo = _bf(u2 - u2_hi.astype(jnp.float32))
    nt = (((1,), (1,)), ((), ()))
    logits = (lax.dot_general(w_hi, u2_hi, nt, preferred_element_type=jnp.float32)
              + lax.dot_general(w_hi, u2_lo, nt, preferred_element_type=jnp.float32)
              + lax.dot_general(w_lo, u2_hi, nt, preferred_element_type=jnp.float32)) + brt_ref[...]
    row8 = lax.broadcasted_iota(jnp.int32, (SUBLANES, TS), 0)
    neg = jnp.float32(-jnp.inf)
    gl = jnp.where(row8 < N_GROUPS, logits[0:SUBLANES], neg)
    gmax = jnp.max(gl, axis=0, keepdims=True)
    gidx = jnp.min(jnp.where(gl == gmax, row8, SUBLANES), axis=0, keepdims=True)
    gprob = 1.0 / jnp.sum(jnp.exp(gl - gmax), axis=0, keepdims=True)
    el = logits[SUBLANES:2 * SUBLANES]
    for grp in range(1, N_GROUPS):
        el = jnp.where(gidx == grp, logits[(grp + 1) * SUBLANES:(grp + 2) * SUBLANES], el)
    m1 = jnp.max(el, axis=0, keepdims=True)
    j1 = jnp.min(jnp.where(el == m1, row8, SUBLANES), axis=0, keepdims=True)
    el2 = jnp.where(row8 == j1, neg, el)
    m2 = jnp.max(el2, axis=0, keepdims=True)
    j2 = jnp.min(jnp.where(el2 == m2, row8, SUBLANES), axis=0, keepdims=True)
    e21 = jnp.exp(m2 - m1)
    den = 1.0 / (1.0 + e21)
    cw1 = gprob * den
    cw2 = gprob * e21 * den

    erow = lax.broadcasted_iota(jnp.int32, (N_EXPERTS, TS), 0)
    oh1 = erow == gidx * EXPERTS_PER_GROUP + j1
    oh2 = erow == gidx * EXPERTS_PER_GROUP + j2
    oh = jnp.where(oh1 | oh2, 1.0, 0.0)
    cnt = jnp.broadcast_to(jnp.sum(oh, axis=1, keepdims=True), (N_EXPERTS, LANES))
    run = jnp.floor((cnt + (SUBLANES - 1.0)) * (1.0 / SUBLANES)) * SUBLANES
    erow_l = lax.broadcasted_iota(jnp.int32, (N_EXPERTS, LANES), 0)
    run_end = run
    shift = 1
    while shift < N_EXPERTS:
        run_end = run_end + jnp.where(erow_l >= shift, pltpu.roll(run_end, shift, 0), 0.0)
        shift *= 2
    run_start = (run_end - run)[:, 0:1]
    ri = lax.broadcasted_iota(jnp.int32, (TS, TS), 0)
    ci = lax.broadcasted_iota(jnp.int32, (TS, TS), 1)
    earlier = _bf(jnp.where(ri < ci, 1.0, 0.0))
    before = jnp.dot(_bf(oh), earlier, preferred_element_type=jnp.float32) + run_start
    pos1 = jnp.sum(jnp.where(oh1, before, 0.0), axis=0, keepdims=True)
    pos2 = jnp.sum(jnp.where(oh2, before, 0.0), axis=0, keepdims=True)
    rr = lax.broadcasted_iota(jnp.int32, (TR, TS), 0).astype(jnp.float32)
    perm = _bf(jnp.where((rr == pos1) | (rr == pos2), 1.0, 0.0))
    xt_ref[...] = _pack_rows(jnp.dot(perm, u2_hi, preferred_element_type=jnp.float32))
    cnt_ref[...] = cnt

    rowl = lax.broadcasted_iota(jnp.int32, (LANES, TS), 0)
    rec = jnp.where(rowl == 0, pos1, 0.0)
    rec = jnp.where(rowl == 1, pos2, rec)
    rec = jnp.where(rowl == 2, cw1, rec)
    rec = jnp.where(rowl == 3, cw2, rec)
    route_ref[...] = rec.T


def _mix(x, positions, ada, rope, win, wout, wpool, gnw, pscale, ln1w, ln1b, wrt, brt):
    B, S, D = x.shape
    ns = S // TS
    const2 = lambda b, s: (0, 0)
    const3 = lambda b, s: (0, 0, 0)
    tile = lambda b, s: (b, s, 0)
    flat = lambda b, s: (b * ns + s, 0, 0)
    return pl.pallas_call(
        _mix_kernel,
        grid=(B, ns),
        in_specs=[
            pl.BlockSpec((None, TS, D), tile),
            pl.BlockSpec((None, TS, 1), tile),
            pl.BlockSpec((None, 6, D), lambda b, s: (b, 0, 0)),
            pl.BlockSpec((2, LANES), const2),
            pl.BlockSpec((D, IN_COLS), const2),
            pl.BlockSpec((D, D), const2),
            pl.BlockSpec((len(POOL_WINDOWS), POOL_GROUP_DIM, POOL_GROUP_DIM), const3),
            pl.BlockSpec((1, RET_WIDTH), const2),
            pl.BlockSpec((1, POOL_WIDTH), const2),
            pl.BlockSpec((1, D), const2),
            pl.BlockSpec((1, D), const2),
            pl.BlockSpec((ROUTE_ROWS, D), const2),
            pl.BlockSpec((ROUTE_ROWS, 1), const2),
        ],
        out_specs=[
            pl.BlockSpec((None, TS, D), tile),
            pl.BlockSpec((None, 2 * TR, HALF), flat),
            pl.BlockSpec((None, TS, LANES), tile),
            pl.BlockSpec((None, N_EXPERTS, LANES), flat),
        ],
        out_shape=[
            jax.ShapeDtypeStruct((B, S, D), jnp.float32),
            jax.ShapeDtypeStruct((B * ns, 2 * TR, HALF), jnp.bfloat16),
            jax.ShapeDtypeStruct((B, S, LANES), jnp.float32),
            jax.ShapeDtypeStruct((B * ns, N_EXPERTS, LANES), jnp.float32),
        ],
        scratch_shapes=[
            pltpu.VMEM((RET_HEADS, HEAD_DIM, HEAD_DIM), jnp.float32),
            pltpu.VMEM((POOL_HALO, POOL_WIDTH), jnp.float32),
            pltpu.VMEM((RET_HEADS, TS, TS), jnp.float32),
            pltpu.VMEM((TS, RET_WIDTH), jnp.float32),
            pltpu.VMEM((TS, RET_WIDTH), jnp.float32),
            pltpu.VMEM((TS, D), jnp.bfloat16),
        ],
        compiler_params=pltpu.CompilerParams(dimension_semantics=("arbitrary", "arbitrary"),
                                             vmem_limit_bytes=VMEM_LIMIT),
        name="mix",
    )(x, positions.reshape(B, S, 1), ada, rope, win, wout, wpool, gnw, pscale, ln1w, ln1b, wrt, brt)


def _fetch_groups(src_ref, row_of_group, n_groups, dst_ref, sem):
    tile_rows = 2 * SUBLANES
    for u in range(n_groups):
        src = pl.multiple_of(2 * row_of_group(u), tile_rows)
        pltpu.make_async_copy(src_ref.at[pl.ds(src, tile_rows), :],
                              dst_ref.at[pl.ds(u * tile_rows, tile_rows), :], sem).start()


def _wait_fetch(src_ref, dst_ref, sem):
    pltpu.make_async_copy(src_ref.at[pl.ds(0, dst_ref.shape[0]), :], dst_ref, sem).wait()


def _expert_kernel(pe_ref, pv_ref, src_ref, xt_ref, w1_ref, w3_ref, w2_ref, ys_ref, xbuf, sem):
    g = pl.program_id(0)
    last = pl.num_programs(0) - 1
    groups = TM // SUBLANES

    def fetch(page, dst_slot):
        _fetch_groups(xt_ref, lambda u: src_ref[page * groups + u], groups, xbuf.at[dst_slot], sem.at[dst_slot])

    @pl.when(g == 0)
    def _prime():
        for k in range(FETCH_AHEAD):
            fetch(k, k)

    fetch(jnp.minimum(g + FETCH_AHEAD, last), (g + FETCH_AHEAD) % FETCH_SLOTS)
    slot = g % FETCH_SLOTS
    _wait_fetch(xt_ref, xbuf.at[slot], sem.at[slot])

    nv = pv_ref[g]

    @pl.when(nv > 0)
    def _compute():
        xl, xh = _unpack_rows(xbuf[slot])
        a = (jnp.dot(xl, w1_ref[:HALF, :], preferred_element_type=jnp.float32)
             + jnp.dot(xh, w1_ref[HALF:, :], preferred_element_type=jnp.float32))
        c = (jnp.dot(xl, w3_ref[:HALF, :], preferred_element_type=jnp.float32)
             + jnp.dot(xh, w3_ref[HALF:, :], preferred_element_type=jnp.float32))
        h = a * jax.nn.sigmoid(a) * c
        y = jnp.dot(_bf(h), w2_ref[...], preferred_element_type=jnp.float32)
        ys_ref[...] = _pack_rows(y)

    @pl.when(nv == 0)
    def _unused():
        ys_ref[...] = jnp.zeros_like(ys_ref)

    @pl.when(g == last)
    def _drain():
        for k in range(1, FETCH_SLOTS):
            other = (g + k) % FETCH_SLOTS
            _wait_fetch(xt_ref, xbuf.at[other], sem.at[other])


def _experts(pe, pv, esrc, xt, w1, w3, w2):
    D = D_MODEL
    n_pages = pe.shape[0]
    assert n_pages > FETCH_AHEAD
    grid_spec = pltpu.PrefetchScalarGridSpec(
        num_scalar_prefetch=3,
        grid=(n_pages,),
        in_specs=[pl.BlockSpec(memory_space=pl.ANY),
                  pl.BlockSpec((None, D, D_EXPERT), lambda g, pe, pv, src: (pe[g], 0, 0)),
                  pl.BlockSpec((None, D, D_EXPERT), lambda g, pe, pv, src: (pe[g], 0, 0)),
                  pl.BlockSpec((None, D_EXPERT, D), lambda g, pe, pv, src: (pe[g], 0, 0))],
        out_specs=pl.BlockSpec((2 * TM, HALF), lambda g, pe, pv, src: (g, 0)),
        scratch_shapes=[pltpu.VMEM((FETCH_SLOTS, 2 * TM, HALF), jnp.bfloat16),
                        pltpu.SemaphoreType.DMA((FETCH_SLOTS,))],
    )
    return pl.pallas_call(
        _expert_kernel,
        grid_spec=grid_spec,
        out_shape=jax.ShapeDtypeStruct((n_pages * 2 * TM, HALF), jnp.bfloat16),
        compiler_params=pltpu.CompilerParams(dimension_semantics=("arbitrary",), vmem_limit_bytes=VMEM_LIMIT),
        name="experts",
    )(pe, pv, esrc, xt, w1, w3, w2)


def _final_kernel(src_ref, x1_ref, route_ref, ada_ref, lnw_ref, lnb_ref, ys_ref, o_ref, ybuf, sem):
    i = pl.program_id(0)
    last = pl.num_programs(0) - 1
    groups = TR // SUBLANES

    def fetch(tile, dst_slot):
        _fetch_groups(ys_ref, lambda u: src_ref[tile * groups + u], groups, ybuf.at[dst_slot], sem.at[dst_slot])

    @pl.when(i == 0)
    def _prime():
        for k in range(FETCH_AHEAD):
            fetch(k, k)

    fetch(jnp.minimum(i + FETCH_AHEAD, last), (i + FETCH_AHEAD) % FETCH_SLOTS)
    slot = i % FETCH_SLOTS
    _wait_fetch(ys_ref, ybuf.at[slot], sem.at[slot])

    route = route_ref[...]
    pos1, pos2, cw1, cw2 = route[:, 0:1], route[:, 1:2], route[:, 2:3], route[:, 3:4]
    col = lax.broadcasted_iota(jnp.int32, (TS, TR), 1).astype(jnp.float32)
    wmat = _bf(jnp.where(col == pos1, cw1, 0.0) + jnp.where(col == pos2, cw2, 0.0))
    yl, yh = _unpack_rows(ybuf[slot])
    y = jnp.concatenate([jnp.dot(wmat, yl, preferred_element_type=jnp.float32),
                         jnp.dot(wmat, yh, preferred_element_type=jnp.float32)], axis=-1)
    gate2 = ada_ref[5:6, :]
    o_ref[...] = _ln(ALPHA * x1_ref[...] + gate2 * y) * lnw_ref[...] + lnb_ref[...]

    @pl.when(i == last)
    def _drain():
        for k in range(1, FETCH_SLOTS):
            other = (i + k) % FETCH_SLOTS
            _wait_fetch(ys_ref, ybuf.at[other], sem.at[other])


def _final(fsrc, x1, route, ada, lnw, lnb, ys, seq_len):
    T, D = x1.shape
    steps_per_seq = seq_len // TS
    n_tiles = T // TS
    assert n_tiles > FETCH_AHEAD
    grid_spec = pltpu.PrefetchScalarGridSpec(
        num_scalar_prefetch=1,
        grid=(n_tiles,),
        in_specs=[pl.BlockSpec((TS, D), lambda i, src: (i, 0)),
                  pl.BlockSpec((TS, LANES), lambda i, src: (i, 0)),
                  pl.BlockSpec((None, 6, D), lambda i, src: (i // steps_per_seq, 0, 0)),
                  pl.BlockSpec((1, D), lambda i, src: (0, 0)),
                  pl.BlockSpec((1, D), lambda i, src: (0, 0)),
                  pl.BlockSpec(memory_space=pl.ANY)],
        out_specs=pl.BlockSpec((TS, D), lambda i, src: (i, 0)),
        scratch_shapes=[pltpu.VMEM((FETCH_SLOTS, 2 * TR, HALF), jnp.bfloat16),
                        pltpu.SemaphoreType.DMA((FETCH_SLOTS,))],
    )
    return pl.pallas_call(
        _final_kernel,
        grid_spec=grid_spec,
        out_shape=jax.ShapeDtypeStruct((T, D), jnp.float32),
        compiler_params=pltpu.CompilerParams(dimension_semantics=("arbitrary",), vmem_limit_bytes=VMEM_LIMIT),
        name="final",
    )(fsrc, x1, route, ada, lnw, lnb, ys)


def _gather_tables(cnt, n_pages):
    nt = cnt.shape[0]
    run = (cnt + SUBLANES - 1) // SUBLANES * SUBLANES
    so = jnp.cumsum(run, axis=1) - run
    eo = jnp.cumsum(run, axis=0) - run
    tot = jnp.sum(run, axis=0)
    pages_e = (tot + TM - 1) // TM
    page_end = jnp.cumsum(pages_e)
    page_start = page_end - pages_e
    g = jnp.arange(n_pages, dtype=jnp.int32)
    pe = jnp.minimum(jnp.sum(g[:, None] >= page_end[None, :], axis=1), N_EXPERTS - 1).astype(jnp.int32)
    used = g < page_end[-1]
    pj = jnp.where(used, g - page_start[pe], 0)
    pv = jnp.where(used, jnp.clip(tot[pe] - pj * TM, 0, TM), 0).astype(jnp.int32)

    q = (pj * TM)[:, None] + SUBLANES * jnp.arange(TM // SUBLANES, dtype=jnp.int32)[None, :]
    tiles = jnp.arange(nt, dtype=jnp.int32)
    offset = (tiles[:, None] * TR + so - eo)[:, pe]
    step = jnp.concatenate([offset[1:] - offset[:-1], jnp.zeros_like(offset[:1])], axis=0)
    run_end = (eo + run)[:, pe]
    src = q + offset[0][:, None] + jnp.sum(jnp.where(run_end[:, :, None] <= q[None], step[:, :, None], 0), axis=0)
    zero_xt = TR - SUBLANES
    esrc = jnp.where((q < tot[pe][:, None]) & used[:, None], src, zero_xt).astype(jnp.int32)

    r = SUBLANES * jnp.arange(TR // SUBLANES, dtype=jnp.int32)
    end = (so + run).T
    offset_f = ((page_start * TM)[None, :] + eo - so).T
    step_f = jnp.concatenate([offset_f[1:] - offset_f[:-1], jnp.zeros_like(offset_f[:1])], axis=0)
    srcf = (r[None, :] + offset_f[0][:, None]
            + jnp.sum(jnp.where(end[:, :, None] <= r[None, None, :], step_f[:, :, None], 0), axis=0))
    zero_ys = (n_pages - 1) * TM
    fsrc = jnp.where(r[None, :] < end[-1][:, None], srcf, zero_ys).astype(jnp.int32)
    return pe, pv, esrc.reshape(-1), fsrc.reshape(-1)


def kernel(x, c, positions, w_ada, b_ada, w_in, ret_gn_w, w_pool, pool_scale, w_out, ln1_w, ln1_b, w_group, b_group,
           w_router, b_router, w1, w3, w2, ln2_w, ln2_b):
    B, S, D = x.shape
    T = B * S
    assert w_ada.shape[0] == DEPTH and D == D_MODEL and S % TS == 0

    inv_freq = ROPE_BASE ** (-jnp.arange(0, HEAD_DIM, 2, dtype=jnp.float32) / HEAD_DIM)
    half = HEAD_DIM // 2
    rope = jnp.stack([jnp.concatenate([inv_freq, inv_freq]),
                      jnp.concatenate([-jnp.ones((half,), jnp.float32), jnp.ones((half,), jnp.float32)])])

    nt = T // TS
    n_pages = (2 * T + nt * N_EXPERTS * (SUBLANES - 1)) // TM + N_EXPERTS + 1

    for l in range(DEPTH):
        ada = _ada(c, w_ada[l], b_ada[l]).reshape(B, 6, D)
        pad = SUBLANES - N_GROUPS
        wrt = jnp.concatenate([w_group[l].T, jnp.zeros((pad, D), jnp.float32), w_router[l].T], axis=0)
        brt = jnp.concatenate([b_group[l], jnp.zeros((pad,), jnp.float32), b_router[l]]).reshape(ROUTE_ROWS, 1)
        x1, xt, route, counts = _mix(
            x, positions, ada, rope, _bf(w_in[l]), _bf(w_out[l]), _bf(w_pool[l]),
            ret_gn_w[l].reshape(1, RET_WIDTH), pool_scale[l].reshape(1, POOL_WIDTH),
            ln1_w[l].reshape(1, D), ln1_b[l].reshape(1, D), wrt, brt)

        cnt = counts[:, :, 0].astype(jnp.int32)
        pe, pv, esrc, fsrc = _gather_tables(cnt, n_pages)
        ys = _experts(pe, pv, esrc, xt.reshape(nt * 2 * TR, HALF),
                      _bf(w1[l]).reshape(N_EXPERTS, D, D_EXPERT), _bf(w3[l]).reshape(N_EXPERTS, D, D_EXPERT),
                      _bf(w2[l]).reshape(N_EXPERTS, D_EXPERT, D))
        x = _final(fsrc, x1.reshape(T, D), route.reshape(T, LANES), ada,
                   ln2_w[l].reshape(1, D), ln2_b[l].reshape(1, D), ys, S).reshape(B, S, D)
    return x
```

```python
import math

import jax
import jax.numpy as jnp
from jax import lax
from jax.experimental import pallas as pl
from jax.experimental.pallas import tpu as pltpu

D_MODEL = 1024
RET_WIDTH = 512
RET_HEADS = 4
HEAD_DIM = 128
POOL_WIDTH = 512
POOL_WINDOWS = (2, 4, 8, 16)
POOL_GROUP_DIM = 128
IN_COLS = 4 * RET_WIDTH + POOL_WIDTH
N_GROUPS = 4
EXPERTS_PER_GROUP = 8
N_EXPERTS = 32
D_EXPERT = 256
DEPTH = 1
ALPHA = (2.0 * DEPTH) ** 0.25
LN_EPS = 1e-5
ROPE_BASE = 10000.0

LANES = 128
SUBLANES = 8
HALF = D_MODEL // 2
POOL_HALO = 16
TS = 256
TM = 256
TR = -(-(2 * TS + N_EXPERTS * (SUBLANES - 1)) // LANES) * LANES
ROUTE_ROWS = SUBLANES + N_EXPERTS
FETCH_AHEAD = 2
FETCH_SLOTS = FETCH_AHEAD + 1
VMEM_LIMIT = 56 * 1024 * 1024

_LOG_GAMMA = tuple(math.log1p(-(2.0 ** (-5.0 - h))) for h in range(RET_HEADS))
_HI = lax.Precision.HIGHEST


def _ln(x):
    mu = jnp.mean(x, axis=-1, keepdims=True)
    xc = x - mu
    var = jnp.mean(xc * xc, axis=-1, keepdims=True)
    return xc * lax.rsqrt(var + LN_EPS)


def _bf(x):
    return x.astype(jnp.bfloat16)


def _pack_rows(x):
    n = x.shape[0]
    lo = x[:, :HALF].reshape(n // SUBLANES, SUBLANES, HALF)
    hi = x[:, HALF:].reshape(n // SUBLANES, SUBLANES, HALF)
    return _bf(jnp.concatenate([lo, hi], axis=1).reshape(2 * n, HALF))


def _unpack_rows(z):
    n = z.shape[0] // 2
    zf = z.astype(jnp.float32).reshape(n // SUBLANES, 2 * SUBLANES, HALF)
    return _bf(zf[:, :SUBLANES, :].reshape(n, HALF)), _bf(zf[:, SUBLANES:, :].reshape(n, HALF))


def _ada_kernel(c_ref, w_ref, b_ref, o_ref):
    c = c_ref[...]
    ca = c * jax.nn.sigmoid(c)
    o_ref[...] = jnp.dot(ca, w_ref[...], precision=_HI, preferred_element_type=jnp.float32) + b_ref[...]


def _ada(c, w_ada, b_ada):
    B, D = c.shape
    n = w_ada.shape[1]
    return pl.pallas_call(
        _ada_kernel,
        grid=(n // D,),
        in_specs=[pl.BlockSpec((B, D), lambda j: (0, 0)),
                  pl.BlockSpec((D, D), lambda j: (0, j)),
                  pl.BlockSpec((1, D), lambda j: (0, j))],
        out_specs=pl.BlockSpec((B, D), lambda j: (0, j)),
        out_shape=jax.ShapeDtypeStruct((B, n), jnp.float32),
        compiler_params=pltpu.CompilerParams(dimension_semantics=("arbitrary",), vmem_limit_bytes=VMEM_LIMIT),
        name="ada",
    )(c, w_ada, b_ada.reshape(1, n))


def _mix_kernel(x_ref, pos_ref, ada_ref, rope_ref, win_ref, wout_ref, wpool_ref, gnw_ref, pscale_ref,
                ln1w_ref, ln1b_ref, wrt_ref, brt_ref,
                x1_ref, xt_ref, route_ref, cnt_ref,
                state_ref, halo_ref, dmat_ref, qdec_ref, kdec_ref, cat_ref):
    b = pl.program_id(0)
    s = pl.program_id(1)

    @pl.when((b == 0) & (s == 0))
    def _init_tables():
        ri = lax.broadcasted_iota(jnp.int32, (TS, TS), 0)
        ci = lax.broadcasted_iota(jnp.int32, (TS, TS), 1)
        rel = (ri - ci).astype(jnp.float32)
        for h in range(RET_HEADS):
            dmat_ref[h] = jnp.where(rel >= 0.0, jnp.exp(jnp.maximum(rel, 0.0) * _LOG_GAMMA[h]), 0.0)
        row = lax.broadcasted_iota(jnp.int32, (TS, RET_WIDTH), 0).astype(jnp.float32)
        lane = lax.broadcasted_iota(jnp.int32, (TS, RET_WIDTH), 1)
        lg = jnp.full((TS, RET_WIDTH), _LOG_GAMMA[0], jnp.float32)
        for h in range(1, RET_HEADS):
            lg = jnp.where(lane >= h * HEAD_DIM, _LOG_GAMMA[h], lg)
        qdec_ref[...] = jnp.exp((row + 1.0) * lg)
        kdec_ref[...] = jnp.exp((TS - 1.0 - row) * lg)

    @pl.when(s == 0)
    def _init_carries():
        state_ref[...] = jnp.zeros_like(state_ref)
        halo_ref[...] = jnp.zeros_like(halo_ref)

    ada = ada_ref[...]
    shift1, scale1, gate1 = ada[0:1], ada[1:2], ada[2:3]
    shift2, scale2 = ada[3:4], ada[4:5]

    x = x_ref[...]
    u = _bf(_ln(x) * (1.0 + scale1) + shift1)

    posf = pos_ref[...].astype(jnp.float32)
    hl = lax.broadcasted_iota(jnp.int32, (TS // 2, HEAD_DIM), 1) < HEAD_DIM // 2
    ang = jnp.where(hl, posf[:TS // 2], posf[TS // 2:]) * rope_ref[0:1, :]
    cos_p, sin_p = jnp.cos(ang), jnp.sin(ang)
    cos_s, sin_s = pltpu.roll(cos_p, HEAD_DIM // 2, 1), pltpu.roll(sin_p, HEAD_DIM // 2, 1)
    cos_t = jnp.concatenate([jnp.where(hl, cos_p, cos_s), jnp.where(hl, cos_s, cos_p)], axis=0)
    sin_t = jnp.concatenate([jnp.where(hl, sin_p, sin_s), jnp.where(hl, sin_s, sin_p)], axis=0) * rope_ref[1:2, :]

    q = jnp.dot(u, win_ref[:, 0:RET_WIDTH], preferred_element_type=jnp.float32)
    k = jnp.dot(u, win_ref[:, RET_WIDTH:2 * RET_WIDTH], preferred_element_type=jnp.float32)
    v = jnp.dot(u, win_ref[:, 2 * RET_WIDTH:3 * RET_WIDTH], preferred_element_type=jnp.float32)
    g = jnp.dot(u, win_ref[:, 3 * RET_WIDTH:4 * RET_WIDTH], preferred_element_type=jnp.float32)
    p = jnp.dot(u, win_ref[:, 4 * RET_WIDTH:IN_COLS], preferred_element_type=jnp.float32)

    gnw = gnw_ref[...]
    for h in range(RET_HEADS):
        sl = slice(h * HEAD_DIM, (h + 1) * HEAD_DIM)
        qh, kh, vh = q[:, sl], k[:, sl], v[:, sl]
        qr = qh * cos_t + pltpu.roll(qh, HEAD_DIM // 2, 1) * sin_t
        kr = (kh * cos_t + pltpu.roll(kh, HEAD_DIM // 2, 1) * sin_t) * (HEAD_DIM ** -0.5)
        vb = _bf(vh)
        sc = lax.dot_general(_bf(qr), _bf(kr), (((1,), (1,)), ((), ())), preferred_element_type=jnp.float32)
        intra = jnp.dot(_bf(sc * dmat_ref[h]), vb, preferred_element_type=jnp.float32)
        st = state_ref[h]
        cross = jnp.dot(_bf(qr * qdec_ref[:, sl]), _bf(st), preferred_element_type=jnp.float32)
        kv = lax.dot_general(_bf(kr * kdec_ref[:, sl]), vb, (((0,), (0,)), ((), ())),
                             preferred_element_type=jnp.float32)
        state_ref[h] = st * math.exp(TS * _LOG_GAMMA[h]) + kv
        r = _ln(intra + cross) * gnw[:, sl]
        gh = g[:, sl]
        cat_ref[:, sl] = _bf(gh * jax.nn.sigmoid(gh) * r)

    pext = jnp.concatenate([halo_ref[...], p], axis=0)
    halo_ref[...] = p[TS - POOL_HALO:, :]
    t_abs = (s * TS + lax.broadcasted_iota(jnp.int32, (TS, 1), 0) + 1).astype(jnp.float32)
    pscale = pscale_ref[...]
    for grp, w in enumerate(POOL_WINDOWS):
        sl = slice(grp * POOL_GROUP_DIM, (grp + 1) * POOL_GROUP_DIM)
        acc = pext[:, sl]
        shift = 1
        while shift < w:
            acc = acc + pltpu.roll(acc, shift, 0)
            shift *= 2
        pooled = acc[POOL_HALO:, :] / jnp.minimum(t_abs, float(w)) - p[:, sl]
        po = jnp.dot(_bf(pooled), wpool_ref[grp], preferred_element_type=jnp.float32) * pscale[:, sl]
        cat_ref[:, RET_WIDTH + grp * POOL_GROUP_DIM:RET_WIDTH + (grp + 1) * POOL_GROUP_DIM] = _bf(po)

    mix = jnp.dot(cat_ref[...], wout_ref[...], preferred_element_type=jnp.float32)
    x1 = _ln(ALPHA * x + gate1 * mix) * ln1w_ref[...] + ln1b_ref[...]
    x1_ref[...] = x1
    u2 = _ln(x1) * (1.0 + scale2) + shift2

    w = wrt_ref[...]
    w_hi = _bf(w)
    w_lo = _bf(w - w_hi.astype(jnp.float32))
    u2_hi = _bf(u2)
    u2_lo = _bf(u2 - u2_hi.astype(jnp.float32))
    nt = (((1,), (1,)), ((), ()))
    logits = (lax.dot_general(w_hi, u2_hi, nt, preferred_element_type=jnp.float32)
              + lax.dot_general(w_hi, u2_lo, nt, preferred_element_type=jnp.float32)
              + lax.dot_general(w_lo, u2_hi, nt, preferred_element_type=jnp.float32)) + brt_ref[...]
    row8 = lax.broadcasted_iota(jnp.int32, (SUBLANES, TS), 0)
    neg = jnp.float32(-jnp.inf)
    gl = jnp.where(row8 < N_GROUPS, logits[0:SUBLANES], neg)
    gmax = jnp.max(gl, axis=0, keepdims=True)
    gidx = jnp.min(jnp.where(gl == gmax, row8, SUBLANES), axis=0, keepdims=True)
    gprob = 1.0 / jnp.sum(jnp.exp(gl - gmax), axis=0, keepdims=True)
    el = logits[SUBLANES:2 * SUBLANES]
    for grp in range(1, N_GROUPS):
        el = jnp.where(gidx == grp, logits[(grp + 1) * SUBLANES:(grp + 2) * SUBLANES], el)
    m1 = jnp.max(el, axis=0, keepdims=True)
    j1 = jnp.min(jnp.where(el == m1, row8, SUBLANES), axis=0, keepdims=True)
    el2 = jnp.where(row8 == j1, neg, el)
    m2 = jnp.max(el2, axis=0, keepdims=True)
    j2 = jnp.min(jnp.where(el2 == m2, row8, SUBLANES), axis=0, keepdims=True)
    e21 = jnp.exp(m2 - m1)
    den = 1.0 / (1.0 + e21)
    cw1 = gprob * den
    cw2 = gprob * e21 * den

    erow = lax.broadcasted_iota(jnp.int32, (N_EXPERTS, TS), 0)
    oh1 = erow == gidx * EXPERTS_PER_GROUP + j1
    oh2 = erow == gidx * EXPERTS_PER_GROUP + j2
    oh = jnp.where(oh1 | oh2, 1.0, 0.0)
    cnt = jnp.broadcast_to(jnp.sum(oh, axis=1, keepdims=True), (N_EXPERTS, LANES))
    run = jnp.floor((cnt + (SUBLANES - 1.0)) * (1.0 / SUBLANES)) * SUBLANES
    erow_l = lax.broadcasted_iota(jnp.int32, (N_EXPERTS, LANES), 0)
    run_end = run
    shift = 1
    while shift < N_EXPERTS:
        run_end = run_end + jnp.where(erow_l >= shift, pltpu.roll(run_end, shift, 0), 0.0)
        shift *= 2
    run_start = (run_end - run)[:, 0:1]
    ri = lax.broadcasted_iota(jnp.int32, (TS, TS), 0)
    ci = lax.broadcasted_iota(jnp.int32, (TS, TS), 1)
    earlier = _bf(jnp.where(ri < ci, 1.0, 0.0))
    before = jnp.dot(_bf(oh), earlier, preferred_element_type=jnp.float32) + run_start
    pos1 = jnp.sum(jnp.where(oh1, before, 0.0), axis=0, keepdims=True)
    pos2 = jnp.sum(jnp.where(oh2, before, 0.0), axis=0, keepdims=True)
    rr = lax.broadcasted_iota(jnp.int32, (TR, TS), 0).astype(jnp.float32)
    perm = _bf(jnp.where((rr == pos1) | (rr == pos2), 1.0, 0.0))
    xt_ref[...] = _pack_rows(jnp.dot(perm, u2_hi, preferred_element_type=jnp.float32))
    cnt_ref[...] = cnt

    rowl = lax.broadcasted_iota(jnp.int32, (LANES, TS), 0)
    rec = jnp.where(rowl == 0, pos1, 0.0)
    rec = jnp.where(rowl == 1, pos2, rec)
    rec = jnp.where(rowl == 2, cw1, rec)
    rec = jnp.where(rowl == 3, cw2, rec)
    route_ref[...] = rec.T


def _mix(x, positions, ada, rope, win, wout, wpool, gnw, pscale, ln1w, ln1b, wrt, brt):
    B, S, D = x.shape
    ns = S // TS
    const2 = lambda b, s: (0, 0)
    const3 = lambda b, s: (0, 0, 0)
    tile = lambda b, s: (b, s, 0)
    flat = lambda b, s: (b * ns + s, 0, 0)
    return pl.pallas_call(
        _mix_kernel,
        grid=(B, ns),
        in_specs=[
            pl.BlockSpec((None, TS, D), tile),
            pl.BlockSpec((None, TS, 1), tile),
            pl.BlockSpec((None, 6, D), lambda b, s: (b, 0, 0)),
            pl.BlockSpec((2, LANES), const2),
            pl.BlockSpec((D, IN_COLS), const2),
            pl.BlockSpec((D, D), const2),
            pl.BlockSpec((len(POOL_WINDOWS), POOL_GROUP_DIM, POOL_GROUP_DIM), const3),
            pl.BlockSpec((1, RET_WIDTH), const2),
            pl.BlockSpec((1, POOL_WIDTH), const2),
            pl.BlockSpec((1, D), const2),
            pl.BlockSpec((1, D), const2),
            pl.BlockSpec((ROUTE_ROWS, D), const2),
            pl.BlockSpec((ROUTE_ROWS, 1), const2),
        ],
        out_specs=[
            pl.BlockSpec((None, TS, D), tile),
            pl.BlockSpec((None, 2 * TR, HALF), flat),
            pl.BlockSpec((None, TS, LANES), tile),
            pl.BlockSpec((None, N_EXPERTS, LANES), flat),
        ],
        out_shape=[
            jax.ShapeDtypeStruct((B, S, D), jnp.float32),
            jax.ShapeDtypeStruct((B * ns, 2 * TR, HALF), jnp.bfloat16),
            jax.ShapeDtypeStruct((B, S, LANES), jnp.float32),
            jax.ShapeDtypeStruct((B * ns, N_EXPERTS, LANES), jnp.float32),
        ],
        scratch_shapes=[
            pltpu.VMEM((RET_HEADS, HEAD_DIM, HEAD_DIM), jnp.float32),
            pltpu.VMEM((POOL_HALO, POOL_WIDTH), jnp.float32),
            pltpu.VMEM((RET_HEADS, TS, TS), jnp.float32),
            pltpu.VMEM((TS, RET_WIDTH), jnp.float32),
            pltpu.VMEM((TS, RET_WIDTH), jnp.float32),
            pltpu.VMEM((TS, D), jnp.bfloat16),
        ],
        compiler_params=pltpu.CompilerParams(dimension_semantics=("arbitrary", "arbitrary"),
                                             vmem_limit_bytes=VMEM_LIMIT),
        name="mix",
    )(x, positions.reshape(B, S, 1), ada, rope, win, wout, wpool, gnw, pscale, ln1w, ln1b, wrt, brt)


def _fetch_groups(src_ref, group_of, dst_ref, sem):
    for u in range(dst_ref.shape[0]):
        pltpu.make_async_copy(src_ref.at[group_of(u)], dst_ref.at[u], sem).start()


def _wait_fetch(src_ref, dst_ref, sem):
    pltpu.make_async_copy(src_ref.at[pl.ds(0, dst_ref.shape[0])], dst_ref, sem).wait()


def _expert_kernel(pe_ref, pv_ref, src_ref, xt_ref, w1_ref, w3_ref, w2_ref, ys_ref, xbuf, sem):
    g = pl.program_id(0)
    last = pl.num_programs(0) - 1
    groups = TM // SUBLANES

    def fetch(page, dst_slot):
        @pl.when(pv_ref[page] > 0)
        def _start():
            _fetch_groups(xt_ref, lambda u: src_ref[page * groups + u], xbuf.at[dst_slot], sem.at[dst_slot])

    @pl.when(g == 0)
    def _prime():
        for k in range(FETCH_AHEAD):
            fetch(k, k)

    fetch(jnp.minimum(g + FETCH_AHEAD, last), (g + FETCH_AHEAD) % FETCH_SLOTS)
    slot = g % FETCH_SLOTS
    nv = pv_ref[g]

    @pl.when(nv > 0)
    def _compute():
        _wait_fetch(xt_ref, xbuf.at[slot], sem.at[slot])
        xl, xh = _unpack_rows(xbuf[slot].reshape(2 * TM, HALF))
        a = (jnp.dot(xl, w1_ref[:HALF, :], preferred_element_type=jnp.float32)
             + jnp.dot(xh, w1_ref[HALF:, :], preferred_element_type=jnp.float32))
        c = (jnp.dot(xl, w3_ref[:HALF, :], preferred_element_type=jnp.float32)
             + jnp.dot(xh, w3_ref[HALF:, :], preferred_element_type=jnp.float32))
        h = a * jax.nn.sigmoid(a) * c
        y = jnp.dot(_bf(h), w2_ref[...], preferred_element_type=jnp.float32)
        ys_ref[...] = _pack_rows(y)

    @pl.when(nv == 0)
    def _unused():
        ys_ref[...] = jnp.zeros_like(ys_ref)

def _experts(pe, pv, esrc, xt, w1, w3, w2):
    D = D_MODEL
    n_pages = pe.shape[0]
    assert n_pages > FETCH_AHEAD
    grid_spec = pltpu.PrefetchScalarGridSpec(
        num_scalar_prefetch=3,
        grid=(n_pages,),
        in_specs=[pl.BlockSpec(memory_space=pl.ANY),
                  pl.BlockSpec((None, D, D_EXPERT), lambda g, pe, pv, src: (pe[g], 0, 0)),
                  pl.BlockSpec((None, D, D_EXPERT), lambda g, pe, pv, src: (pe[g], 0, 0)),
                  pl.BlockSpec((None, D_EXPERT, D), lambda g, pe, pv, src: (pe[g], 0, 0))],
        out_specs=pl.BlockSpec((2 * TM, HALF), lambda g, pe, pv, src: (g, 0)),
        scratch_shapes=[pltpu.VMEM((FETCH_SLOTS, TM // SUBLANES, 2 * SUBLANES, HALF), jnp.bfloat16),
                        pltpu.SemaphoreType.DMA((FETCH_SLOTS,))],
    )
    return pl.pallas_call(
        _expert_kernel,
        grid_spec=grid_spec,
        out_shape=jax.ShapeDtypeStruct((n_pages * 2 * TM, HALF), jnp.bfloat16),
        compiler_params=pltpu.CompilerParams(dimension_semantics=("arbitrary",), vmem_limit_bytes=VMEM_LIMIT),
        name="experts",
    )(pe, pv, esrc, xt, w1, w3, w2)


def _final_kernel(src_ref, x1_ref, route_ref, ada_ref, lnw_ref, lnb_ref, ys_ref, o_ref, ybuf, sem):
    i = pl.program_id(0)
    last = pl.num_programs(0) - 1
    groups = TR // SUBLANES

    def fetch(tile, dst_slot):
        _fetch_groups(ys_ref, lambda u: src_ref[tile * groups + u], ybuf.at[dst_slot], sem.at[dst_slot])

    @pl.when(i == 0)
    def _prime():
        for k in range(FETCH_AHEAD):
            fetch(k, k)

    fetch(jnp.minimum(i + FETCH_AHEAD, last), (i + FETCH_AHEAD) % FETCH_SLOTS)
    slot = i % FETCH_SLOTS
    _wait_fetch(ys_ref, ybuf.at[slot], sem.at[slot])

    route = route_ref[...]
    pos1, pos2, cw1, cw2 = route[:, 0:1], route[:, 1:2], route[:, 2:3], route[:, 3:4]
    col = lax.broadcasted_iota(jnp.int32, (TS, TR), 1).astype(jnp.float32)
    wmat = _bf(jnp.where(col == pos1, cw1, 0.0) + jnp.where(col == pos2, cw2, 0.0))
    yl, yh = _unpack_rows(ybuf[slot].reshape(2 * TR, HALF))
    y = jnp.concatenate([jnp.dot(wmat, yl, preferred_element_type=jnp.float32),
                         jnp.dot(wmat, yh, preferred_element_type=jnp.float32)], axis=-1)
    gate2 = ada_ref[5:6, :]
    o_ref[...] = _ln(ALPHA * x1_ref[...] + gate2 * y) * lnw_ref[...] + lnb_ref[...]

    @pl.when(i == last)
    def _drain():
        for k in range(1, FETCH_SLOTS):
            other = (i + k) % FETCH_SLOTS
            _wait_fetch(ys_ref, ybuf.at[other], sem.at[other])


def _final(fsrc, x1, route, ada, lnw, lnb, ys, seq_len):
    T, D = x1.shape
    steps_per_seq = seq_len // TS
    n_tiles = T // TS
    assert n_tiles > FETCH_AHEAD
    grid_spec = pltpu.PrefetchScalarGridSpec(
        num_scalar_prefetch=1,
        grid=(n_tiles,),
        in_specs=[pl.BlockSpec((TS, D), lambda i, src: (i, 0)),
                  pl.BlockSpec((TS, LANES), lambda i, src: (i, 0)),
                  pl.BlockSpec((None, 6, D), lambda i, src: (i // steps_per_seq, 0, 0)),
                  pl.BlockSpec((1, D), lambda i, src: (0, 0)),
                  pl.BlockSpec((1, D), lambda i, src: (0, 0)),
                  pl.BlockSpec(memory_space=pl.ANY)],
        out_specs=pl.BlockSpec((TS, D), lambda i, src: (i, 0)),
        scratch_shapes=[pltpu.VMEM((FETCH_SLOTS, TR // SUBLANES, 2 * SUBLANES, HALF), jnp.bfloat16),
                        pltpu.SemaphoreType.DMA((FETCH_SLOTS,))],
    )
    return pl.pallas_call(
        _final_kernel,
        grid_spec=grid_spec,
        out_shape=jax.ShapeDtypeStruct((T, D), jnp.float32),
        compiler_params=pltpu.CompilerParams(dimension_semantics=("arbitrary",), vmem_limit_bytes=VMEM_LIMIT),
        name="final",
    )(fsrc, x1, route, ada, lnw, lnb, ys)


def _gather_tables(cnt, n_pages):
    nt = cnt.shape[0]
    run = (cnt + SUBLANES - 1) // SUBLANES * SUBLANES
    so = jnp.cumsum(run, axis=1) - run
    eo = jnp.cumsum(run, axis=0) - run
    tot = jnp.sum(run, axis=0)
    pages_e = (tot + TM - 1) // TM
    page_end = jnp.cumsum(pages_e)
    page_start = page_end - pages_e
    g = jnp.arange(n_pages, dtype=jnp.int32)
    pe = jnp.minimum(jnp.sum(g[:, None] >= page_end[None, :], axis=1), N_EXPERTS - 1).astype(jnp.int32)
    used = g < page_end[-1]
    owner = (g[None, :] >= page_start[:, None]) & (g[None, :] < page_end[:, None])
    of_page = lambda a: jnp.sum(jnp.where(owner, a[:, None], 0), axis=0)
    of_page2 = lambda a: jnp.sum(jnp.where(owner[:, None, :], a.T[:, :, None], 0), axis=0)
    pj = g - of_page(page_start)
    tot_p = of_page(tot)
    pv = jnp.where(used, jnp.clip(tot_p - pj * TM, 0, TM), 0).astype(jnp.int32)

    q = (pj * TM)[:, None] + SUBLANES * jnp.arange(TM // SUBLANES, dtype=jnp.int32)[None, :]
    tiles = jnp.arange(nt, dtype=jnp.int32)
    offset = of_page2(tiles[:, None] * TR + so - eo)
    step = jnp.concatenate([offset[1:] - offset[:-1], jnp.zeros_like(offset[:1])], axis=0)
    run_end = of_page2(eo + run)
    src = q + offset[0][:, None] + jnp.sum(jnp.where(run_end[:, :, None] <= q[None], step[:, :, None], 0), axis=0)
    zero_xt = TR - SUBLANES
    esrc = jnp.where((q < tot_p[:, None]) & used[:, None], src, zero_xt).astype(jnp.int32)

    r = SUBLANES * jnp.arange(TR // SUBLANES, dtype=jnp.int32)
    end = (so + run).T
    offset_f = ((page_start * TM)[None, :] + eo - so).T
    step_f = jnp.concatenate([offset_f[1:] - offset_f[:-1], jnp.zeros_like(offset_f[:1])], axis=0)
    srcf = (r[None, :] + offset_f[0][:, None]
            + jnp.sum(jnp.where(end[:, :, None] <= r[None, None, :], step_f[:, :, None], 0), axis=0))
    zero_ys = (n_pages - 1) * TM
    fsrc = jnp.where(r[None, :] < end[-1][:, None], srcf, zero_ys).astype(jnp.int32)
    return pe, pv, esrc.reshape(-1) // SUBLANES, fsrc.reshape(-1) // SUBLANES


def kernel(x, c, positions, w_ada, b_ada, w_in, ret_gn_w, w_pool, pool_scale, w_out, ln1_w, ln1_b, w_group, b_group,
           w_router, b_router, w1, w3, w2, ln2_w, ln2_b):
    B, S, D = x.shape
    T = B * S
    assert w_ada.shape[0] == DEPTH and D == D_MODEL and S % TS == 0

    inv_freq = ROPE_BASE ** (-jnp.arange(0, HEAD_DIM, 2, dtype=jnp.float32) / HEAD_DIM)
    half = HEAD_DIM // 2
    rope = jnp.stack([jnp.concatenate([inv_freq, inv_freq]),
                      jnp.concatenate([-jnp.ones((half,), jnp.float32), jnp.ones((half,), jnp.float32)])])

    nt = T // TS
    n_pages = (2 * T + nt * N_EXPERTS * (SUBLANES - 1)) // TM + N_EXPERTS + 1

    for l in range(DEPTH):
        ada = _ada(c, w_ada[l], b_ada[l]).reshape(B, 6, D)
        pad = SUBLANES - N_GROUPS
        wrt = jnp.concatenate([w_group[l].T, jnp.zeros((pad, D), jnp.float32), w_router[l].T], axis=0)
        brt = jnp.concatenate([b_group[l], jnp.zeros((pad,), jnp.float32), b_router[l]]).reshape(ROUTE_ROWS, 1)
        x1, xt, route, counts = _mix(
            x, positions, ada, rope, _bf(w_in[l]), _bf(w_out[l]), _bf(w_pool[l]),
            ret_gn_w[l].reshape(1, RET_WIDTH), pool_scale[l].reshape(1, POOL_WIDTH),
            ln1_w[l].reshape(1, D), ln1_b[l].reshape(1, D), wrt, brt)

        cnt = counts[:, :, 0].astype(jnp.int32)
        pe, pv, esrc, fsrc = _gather_tables(cnt, n_pages)
        ys = _experts(pe, pv, esrc, xt.reshape(nt * TR // SUBLANES, 2 * SUBLANES, HALF),
                      _bf(w1[l]).reshape(N_EXPERTS, D, D_EXPERT), _bf(w3[l]).reshape(N_EXPERTS, D, D_EXPERT),
                      _bf(w2[l]).reshape(N_EXPERTS, D_EXPERT, D))
        x = _final(fsrc, x1.reshape(T, D), route.reshape(T, LANES), ada,
                   ln2_w[l].reshape(1, D), ln2_b[l].reshape(1, D),
                   ys.reshape(n_pages * TM // SUBLANES, 2 * SUBLANES, HALF), S).reshape(B, S, D)
    return x
```

```python
import math

import jax
import jax.numpy as jnp
from jax import lax
from jax.experimental import pallas as pl
from jax.experimental.pallas import tpu as pltpu

D_MODEL = 1024
RET_WIDTH = 512
RET_HEADS = 4
HEAD_DIM = 128
POOL_WIDTH = 512
POOL_WINDOWS = (2, 4, 8, 16)
POOL_GROUP_DIM = 128
IN_COLS = 4 * RET_WIDTH + POOL_WIDTH
N_GROUPS = 4
EXPERTS_PER_GROUP = 8
N_EXPERTS = 32
D_EXPERT = 256
DEPTH = 1
ALPHA = (2.0 * DEPTH) ** 0.25
LN_EPS = 1e-5
ROPE_BASE = 10000.0

LANES = 128
SUBLANES = 8
HALF = D_MODEL // 2
POOL_HALO = 16
TS = 256
TM = 256
TR = -(-(2 * TS + N_EXPERTS * (SUBLANES - 1)) // LANES) * LANES
SEQS_PER_STEP = 2
ROUTE_ROWS = SUBLANES + N_EXPERTS
FETCH_AHEAD = 2
FETCH_SLOTS = FETCH_AHEAD + 1
VMEM_LIMIT = 56 * 1024 * 1024

_LOG_GAMMA = tuple(math.log1p(-(2.0 ** (-5.0 - h))) for h in range(RET_HEADS))
_HI = lax.Precision.HIGHEST


def _ln(x):
    mu = jnp.mean(x, axis=-1, keepdims=True)
    xc = x - mu
    var = jnp.mean(xc * xc, axis=-1, keepdims=True)
    return xc * lax.rsqrt(var + LN_EPS)


def _bf(x):
    return x.astype(jnp.bfloat16)


def _pack_rows(x):
    n = x.shape[0]
    lo = x[:, :HALF].reshape(n // SUBLANES, SUBLANES, HALF)
    hi = x[:, HALF:].reshape(n // SUBLANES, SUBLANES, HALF)
    return _bf(jnp.concatenate([lo, hi], axis=1).reshape(2 * n, HALF))


def _unpack_rows(z):
    n = z.shape[0] // 2
    zf = z.astype(jnp.float32).reshape(n // SUBLANES, 2 * SUBLANES, HALF)
    return _bf(zf[:, :SUBLANES, :].reshape(n, HALF)), _bf(zf[:, SUBLANES:, :].reshape(n, HALF))


def _ada_kernel(c_ref, w_ref, b_ref, o_ref):
    c = c_ref[...]
    ca = c * jax.nn.sigmoid(c)
    o_ref[...] = jnp.dot(ca, w_ref[...], precision=_HI, preferred_element_type=jnp.float32) + b_ref[...]


def _ada(c, w_ada, b_ada):
    B, D = c.shape
    n = w_ada.shape[1]
    return pl.pallas_call(
        _ada_kernel,
        grid=(n // D,),
        in_specs=[pl.BlockSpec((B, D), lambda j: (0, 0)),
                  pl.BlockSpec((D, D), lambda j: (0, j)),
                  pl.BlockSpec((1, D), lambda j: (0, j))],
        out_specs=pl.BlockSpec((B, D), lambda j: (0, j)),
        out_shape=jax.ShapeDtypeStruct((B, n), jnp.float32),
        compiler_params=pltpu.CompilerParams(dimension_semantics=("arbitrary",), vmem_limit_bytes=VMEM_LIMIT),
        name="ada",
    )(c, w_ada, b_ada.reshape(1, n))


def _mix_kernel(x_ref, pos_ref, ada_ref, rope_ref, win_ref, wout_ref, wpool_ref, gnw_ref, pscale_ref,
                ln1w_ref, ln1b_ref, wrt_ref, brt_ref,
                x1_ref, xt_ref, route_ref, cnt_ref,
                state_ref, halo_ref, dmat_ref, qdec_ref, kdec_ref, cat_ref):
    b = pl.program_id(0)
    s = pl.program_id(1)

    @pl.when((b == 0) & (s == 0))
    def _init_tables():
        ri = lax.broadcasted_iota(jnp.int32, (TS, TS), 0)
        ci = lax.broadcasted_iota(jnp.int32, (TS, TS), 1)
        rel = (ri - ci).astype(jnp.float32)
        for h in range(RET_HEADS):
            dmat_ref[h] = jnp.where(rel >= 0.0, jnp.exp(jnp.maximum(rel, 0.0) * _LOG_GAMMA[h]), 0.0)
        row = lax.broadcasted_iota(jnp.int32, (TS, RET_WIDTH), 0).astype(jnp.float32)
        lane = lax.broadcasted_iota(jnp.int32, (TS, RET_WIDTH), 1)
        lg = jnp.full((TS, RET_WIDTH), _LOG_GAMMA[0], jnp.float32)
        for h in range(1, RET_HEADS):
            lg = jnp.where(lane >= h * HEAD_DIM, _LOG_GAMMA[h], lg)
        qdec_ref[...] = jnp.exp((row + 1.0) * lg)
        kdec_ref[...] = jnp.exp((TS - 1.0 - row) * lg)

    @pl.when(s == 0)
    def _init_carries():
        state_ref[...] = jnp.zeros_like(state_ref)
        halo_ref[...] = jnp.zeros_like(halo_ref)

    tiles = [_mix_tile(s, x_ref.at[j], pos_ref.at[j], ada_ref.at[j], rope_ref, win_ref, wout_ref, wpool_ref, gnw_ref,
                       pscale_ref, ln1w_ref, ln1b_ref, wrt_ref, brt_ref,
                       x1_ref.at[j], xt_ref.at[j], route_ref.at[j], cnt_ref.at[j],
                       state_ref.at[j], halo_ref.at[j], dmat_ref, qdec_ref, kdec_ref, cat_ref.at[j])
             for j in range(SEQS_PER_STEP)]
    while tiles:
        tiles = [t for t in tiles if next(t, True) is None]


def _mix_tile(s, x_ref, pos_ref, ada_ref, rope_ref, win_ref, wout_ref, wpool_ref, gnw_ref, pscale_ref,
              ln1w_ref, ln1b_ref, wrt_ref, brt_ref,
              x1_ref, xt_ref, route_ref, cnt_ref,
              state_ref, halo_ref, dmat_ref, qdec_ref, kdec_ref, cat_ref):
    ada = ada_ref[...]
    shift1, scale1, gate1 = ada[0:1], ada[1:2], ada[2:3]
    shift2, scale2 = ada[3:4], ada[4:5]

    x = x_ref[...]
    u = _bf(_ln(x) * (1.0 + scale1) + shift1)
    yield

    posf = pos_ref[...].astype(jnp.float32)
    hl = lax.broadcasted_iota(jnp.int32, (TS // 2, HEAD_DIM), 1) < HEAD_DIM // 2
    ang = jnp.where(hl, posf[:TS // 2], posf[TS // 2:]) * rope_ref[0:1, :]
    cos_p, sin_p = jnp.cos(ang), jnp.sin(ang)
    cos_s, sin_s = pltpu.roll(cos_p, HEAD_DIM // 2, 1), pltpu.roll(sin_p, HEAD_DIM // 2, 1)
    cos_t = jnp.concatenate([jnp.where(hl, cos_p, cos_s), jnp.where(hl, cos_s, cos_p)], axis=0)
    sin_t = jnp.concatenate([jnp.where(hl, sin_p, sin_s), jnp.where(hl, sin_s, sin_p)], axis=0) * rope_ref[1:2, :]
    yield

    q = jnp.dot(u, win_ref[:, 0:RET_WIDTH], preferred_element_type=jnp.float32)
    k = jnp.dot(u, win_ref[:, RET_WIDTH:2 * RET_WIDTH], preferred_element_type=jnp.float32)
    v = jnp.dot(u, win_ref[:, 2 * RET_WIDTH:3 * RET_WIDTH], preferred_element_type=jnp.float32)
    g = jnp.dot(u, win_ref[:, 3 * RET_WIDTH:4 * RET_WIDTH], preferred_element_type=jnp.float32)
    p = jnp.dot(u, win_ref[:, 4 * RET_WIDTH:IN_COLS], preferred_element_type=jnp.float32)
    yield

    gnw = gnw_ref[...]
    for h in range(RET_HEADS):
        sl = slice(h * HEAD_DIM, (h + 1) * HEAD_DIM)
        qh, kh, vh = q[:, sl], k[:, sl], v[:, sl]
        qr = qh * cos_t + pltpu.roll(qh, HEAD_DIM // 2, 1) * sin_t
        kr = (kh * cos_t + pltpu.roll(kh, HEAD_DIM // 2, 1) * sin_t) * (HEAD_DIM ** -0.5)
        vb = _bf(vh)
        sc = lax.dot_general(_bf(qr), _bf(kr), (((1,), (1,)), ((), ())), preferred_element_type=jnp.float32)
        intra = jnp.dot(_bf(sc * dmat_ref[h]), vb, preferred_element_type=jnp.float32)
        st = state_ref[h]
        cross = jnp.dot(_bf(qr * qdec_ref[:, sl]), _bf(st), preferred_element_type=jnp.float32)
        kv = lax.dot_general(_bf(kr * kdec_ref[:, sl]), vb, (((0,), (0,)), ((), ())),
                             preferred_element_type=jnp.float32)
        state_ref[h] = st * math.exp(TS * _LOG_GAMMA[h]) + kv
        r = _ln(intra + cross) * gnw[:, sl]
        gh = g[:, sl]
        cat_ref[:, sl] = _bf(gh * jax.nn.sigmoid(gh) * r)
        yield

    pext = jnp.concatenate([halo_ref[...], p], axis=0)
    halo_ref[...] = p[TS - POOL_HALO:, :]
    t_abs = (s * TS + lax.broadcasted_iota(jnp.int32, (TS, 1), 0) + 1).astype(jnp.float32)
    pscale = pscale_ref[...]
    for grp, w in enumerate(POOL_WINDOWS):
        sl = slice(grp * POOL_GROUP_DIM, (grp + 1) * POOL_GROUP_DIM)
        acc = pext[:, sl]
        shift = 1
        while shift < w:
            acc = acc + pltpu.roll(acc, shift, 0)
            shift *= 2
        pooled = acc[POOL_HALO:, :] / jnp.minimum(t_abs, float(w)) - p[:, sl]
        po = jnp.dot(_bf(pooled), wpool_ref[grp], preferred_element_type=jnp.float32) * pscale[:, sl]
        cat_ref[:, RET_WIDTH + grp * POOL_GROUP_DIM:RET_WIDTH + (grp + 1) * POOL_GROUP_DIM] = _bf(po)
    yield

    mix = jnp.dot(cat_ref[...], wout_ref[...], preferred_element_type=jnp.float32)
    yield
    x1 = _ln(ALPHA * x + gate1 * mix) * ln1w_ref[...] + ln1b_ref[...]
    x1_ref[...] = x1
    u2 = _ln(x1) * (1.0 + scale2) + shift2
    yield

    w = wrt_ref[...]
    w_hi = _bf(w)
    w_lo = _bf(w - w_hi.astype(jnp.float32))
    u2_hi = _bf(u2)
    u2_lo = _bf(u2 - u2_hi.astype(jnp.float32))
    nt = (((1,), (1,)), ((), ()))
    logits = (lax.dot_general(w_hi, u2_hi, nt, preferred_element_type=jnp.float32)
              + lax.dot_general(w_hi, u2_lo, nt, preferred_element_type=jnp.float32)
              + lax.dot_general(w_lo, u2_hi, nt, preferred_element_type=jnp.float32)) + brt_ref[...]
    row8 = lax.broadcasted_iota(jnp.int32, (SUBLANES, TS), 0)
    neg = jnp.float32(-jnp.inf)
    gl = jnp.where(row8 < N_GROUPS, logits[0:SUBLANES], neg)
    gmax = jnp.max(gl, axis=0, keepdims=True)
    gidx = jnp.min(jnp.where(gl == gmax, row8, SUBLANES), axis=0, keepdims=True)
    gprob = 1.0 / jnp.sum(jnp.exp(gl - gmax), axis=0, keepdims=True)
    el = logits[SUBLANES:2 * SUBLANES]
    for grp in range(1, N_GROUPS):
        el = jnp.where(gidx == grp, logits[(grp + 1) * SUBLANES:(grp + 2) * SUBLANES], el)
    m1 = jnp.max(el, axis=0, keepdims=True)
    j1 = jnp.min(jnp.where(el == m1, row8, SUBLANES), axis=0, keepdims=True)
    el2 = jnp.where(row8 == j1, neg, el)
    m2 = jnp.max(el2, axis=0, keepdims=True)
    j2 = jnp.min(jnp.where(el2 == m2, row8, SUBLANES), axis=0, keepdims=True)
    e21 = jnp.exp(m2 - m1)
    den = 1.0 / (1.0 + e21)
    cw1 = gprob * den
    cw2 = gprob * e21 * den
    yield

    erow = lax.broadcasted_iota(jnp.int32, (N_EXPERTS, TS), 0)
    oh1 = erow == gidx * EXPERTS_PER_GROUP + j1
    oh2 = erow == gidx * EXPERTS_PER_GROUP + j2
    oh = jnp.where(oh1 | oh2, 1.0, 0.0)
    cnt = jnp.broadcast_to(jnp.sum(oh, axis=1, keepdims=True), (N_EXPERTS, LANES))
    run = jnp.floor((cnt + (SUBLANES - 1.0)) * (1.0 / SUBLANES)) * SUBLANES
    erow_l = lax.broadcasted_iota(jnp.int32, (N_EXPERTS, LANES), 0)
    run_end = run
    shift = 1
    while shift < N_EXPERTS:
        run_end = run_end + jnp.where(erow_l >= shift, pltpu.roll(run_end, shift, 0), 0.0)
        shift *= 2
    run_start = (run_end - run)[:, 0:1]
    ri = lax.broadcasted_iota(jnp.int32, (TS, TS), 0)
    ci = lax.broadcasted_iota(jnp.int32, (TS, TS), 1)
    earlier = _bf(jnp.where(ri < ci, 1.0, 0.0))
    before = jnp.dot(_bf(oh), earlier, preferred_element_type=jnp.float32) + run_start
    pos1 = jnp.sum(jnp.where(oh1, before, 0.0), axis=0, keepdims=True)
    pos2 = jnp.sum(jnp.where(oh2, before, 0.0), axis=0, keepdims=True)
    rr = lax.broadcasted_iota(jnp.int32, (TR, TS), 0).astype(jnp.float32)
    perm = _bf(jnp.where((rr == pos1) | (rr == pos2), 1.0, 0.0))
    yield
    xt_ref[...] = _pack_rows(jnp.dot(perm, u2_hi, preferred_element_type=jnp.float32))
    cnt_ref[...] = cnt

    rowl = lax.broadcasted_iota(jnp.int32, (LANES, TS), 0)
    rec = jnp.where(rowl == 0, pos1, 0.0)
    rec = jnp.where(rowl == 1, pos2, rec)
    rec = jnp.where(rowl == 2, cw1, rec)
    rec = jnp.where(rowl == 3, cw2, rec)
    route_ref[...] = rec.T


def _mix(x, positions, ada, rope, win, wout, wpool, gnw, pscale, ln1w, ln1b, wrt, brt):
    B, S, D = x.shape
    ns = S // TS
    assert B % SEQS_PER_STEP == 0
    P = SEQS_PER_STEP
    const2 = lambda b, s: (0, 0)
    const3 = lambda b, s: (0, 0, 0)
    tile = lambda b, s: (b, s, 0)
    flat = lambda b, s: (b, s, 0, 0)
    return pl.pallas_call(
        _mix_kernel,
        grid=(B // P, ns),
        in_specs=[
            pl.BlockSpec((P, TS, D), tile),
            pl.BlockSpec((P, TS, 1), tile),
            pl.BlockSpec((P, 6, D), lambda b, s: (b, 0, 0)),
            pl.BlockSpec((2, LANES), const2),
            pl.BlockSpec((D, IN_COLS), const2),
            pl.BlockSpec((D, D), const2),
            pl.BlockSpec((len(POOL_WINDOWS), POOL_GROUP_DIM, POOL_GROUP_DIM), const3),
            pl.BlockSpec((1, RET_WIDTH), const2),
            pl.BlockSpec((1, POOL_WIDTH), const2),
            pl.BlockSpec((1, D), const2),
            pl.BlockSpec((1, D), const2),
            pl.BlockSpec((ROUTE_ROWS, D), const2),
            pl.BlockSpec((ROUTE_ROWS, 1), const2),
        ],
        out_specs=[
            pl.BlockSpec((P, TS, D), tile),
            pl.BlockSpec((P, None, 2 * TR, HALF), flat),
            pl.BlockSpec((P, TS, LANES), tile),
            pl.BlockSpec((P, None, N_EXPERTS, LANES), flat),
        ],
        out_shape=[
            jax.ShapeDtypeStruct((B, S, D), jnp.float32),
            jax.ShapeDtypeStruct((B, ns, 2 * TR, HALF), jnp.bfloat16),
            jax.ShapeDtypeStruct((B, S, LANES), jnp.float32),
            jax.ShapeDtypeStruct((B, ns, N_EXPERTS, LANES), jnp.float32),
        ],
        scratch_shapes=[
            pltpu.VMEM((P, RET_HEADS, HEAD_DIM, HEAD_DIM), jnp.float32),
            pltpu.VMEM((P, POOL_HALO, POOL_WIDTH), jnp.float32),
            pltpu.VMEM((RET_HEADS, TS, TS), jnp.float32),
            pltpu.VMEM((TS, RET_WIDTH), jnp.float32),
            pltpu.VMEM((TS, RET_WIDTH), jnp.float32),
            pltpu.VMEM((P, TS, D), jnp.bfloat16),
        ],
        compiler_params=pltpu.CompilerParams(dimension_semantics=("arbitrary", "arbitrary"),
                                             vmem_limit_bytes=VMEM_LIMIT),
        name="mix",
    )(x, positions.reshape(B, S, 1), ada, rope, win, wout, wpool, gnw, pscale, ln1w, ln1b, wrt, brt)


def _fetch_groups(src_ref, group_of, dst_ref, sem):
    for u in range(dst_ref.shape[0]):
        pltpu.make_async_copy(src_ref.at[group_of(u)], dst_ref.at[u], sem).start()


def _wait_fetch(src_ref, dst_ref, sem):
    pltpu.make_async_copy(src_ref.at[pl.ds(0, dst_ref.shape[0])], dst_ref, sem).wait()


def _expert_kernel(pe_ref, pv_ref, src_ref, xt_ref, w1_ref, w3_ref, w2_ref, ys_ref, xbuf, sem):
    g = pl.program_id(0)
    last = pl.num_programs(0) - 1
    groups = TM // SUBLANES

    def fetch(page, dst_slot):
        @pl.when(pv_ref[page] > 0)
        def _start():
            _fetch_groups(xt_ref, lambda u: src_ref[page * groups + u], xbuf.at[dst_slot], sem.at[dst_slot])

    @pl.when(g == 0)
    def _prime():
        for k in range(FETCH_AHEAD):
            fetch(k, k)

    fetch(jnp.minimum(g + FETCH_AHEAD, last), (g + FETCH_AHEAD) % FETCH_SLOTS)
    slot = g % FETCH_SLOTS
    nv = pv_ref[g]

    @pl.when(nv > 0)
    def _compute():
        _wait_fetch(xt_ref, xbuf.at[slot], sem.at[slot])
        xl, xh = _unpack_rows(xbuf[slot].reshape(2 * TM, HALF))
        a = (jnp.dot(xl, w1_ref[:HALF, :], preferred_element_type=jnp.float32)
             + jnp.dot(xh, w1_ref[HALF:, :], preferred_element_type=jnp.float32))
        c = (jnp.dot(xl, w3_ref[:HALF, :], preferred_element_type=jnp.float32)
             + jnp.dot(xh, w3_ref[HALF:, :], preferred_element_type=jnp.float32))
        h = a * jax.nn.sigmoid(a) * c
        y = jnp.dot(_bf(h), w2_ref[...], preferred_element_type=jnp.float32)
        ys_ref[...] = _pack_rows(y)

    @pl.when(nv == 0)
    def _unused():
        ys_ref[...] = jnp.zeros_like(ys_ref)

def _experts(pe, pv, esrc, xt, w1, w3, w2):
    D = D_MODEL
    n_pages = pe.shape[0]
    assert n_pages > FETCH_AHEAD
    grid_spec = pltpu.PrefetchScalarGridSpec(
        num_scalar_prefetch=3,
        grid=(n_pages,),
        in_specs=[pl.BlockSpec(memory_space=pl.ANY),
                  pl.BlockSpec((None, D, D_EXPERT), lambda g, pe, pv, src: (pe[g], 0, 0)),
                  pl.BlockSpec((None, D, D_EXPERT), lambda g, pe, pv, src: (pe[g], 0, 0)),
                  pl.BlockSpec((None, D_EXPERT, D), lambda g, pe, pv, src: (pe[g], 0, 0))],
        out_specs=pl.BlockSpec((2 * TM, HALF), lambda g, pe, pv, src: (g, 0)),
        scratch_shapes=[pltpu.VMEM((FETCH_SLOTS, TM // SUBLANES, 2 * SUBLANES, HALF), jnp.bfloat16),
                        pltpu.SemaphoreType.DMA((FETCH_SLOTS,))],
    )
    return pl.pallas_call(
        _expert_kernel,
        grid_spec=grid_spec,
        out_shape=jax.ShapeDtypeStruct((n_pages * 2 * TM, HALF), jnp.bfloat16),
        compiler_params=pltpu.CompilerParams(dimension_semantics=("arbitrary",), vmem_limit_bytes=VMEM_LIMIT),
        name="experts",
    )(pe, pv, esrc, xt, w1, w3, w2)


def _final_kernel(src_ref, x1_ref, route_ref, ada_ref, lnw_ref, lnb_ref, ys_ref, o_ref, ybuf, sem):
    i = pl.program_id(0)
    last = pl.num_programs(0) - 1
    groups = TR // SUBLANES

    def fetch(tile, dst_slot):
        _fetch_groups(ys_ref, lambda u: src_ref[tile * groups + u], ybuf.at[dst_slot], sem.at[dst_slot])

    @pl.when(i == 0)
    def _prime():
        for k in range(FETCH_AHEAD):
            fetch(k, k)

    fetch(jnp.minimum(i + FETCH_AHEAD, last), (i + FETCH_AHEAD) % FETCH_SLOTS)
    slot = i % FETCH_SLOTS
    _wait_fetch(ys_ref, ybuf.at[slot], sem.at[slot])

    route = route_ref[...]
    pos1, pos2, cw1, cw2 = route[:, 0:1], route[:, 1:2], route[:, 2:3], route[:, 3:4]
    col = lax.broadcasted_iota(jnp.int32, (TS, TR), 1).astype(jnp.float32)
    wmat = _bf(jnp.where(col == pos1, cw1, 0.0) + jnp.where(col == pos2, cw2, 0.0))
    yl, yh = _unpack_rows(ybuf[slot].reshape(2 * TR, HALF))
    y = jnp.concatenate([jnp.dot(wmat, yl, preferred_element_type=jnp.float32),
                         jnp.dot(wmat, yh, preferred_element_type=jnp.float32)], axis=-1)
    gate2 = ada_ref[5:6, :]
    o_ref[...] = _ln(ALPHA * x1_ref[...] + gate2 * y) * lnw_ref[...] + lnb_ref[...]

    @pl.when(i == last)
    def _drain():
        for k in range(1, FETCH_SLOTS):
            other = (i + k) % FETCH_SLOTS
            _wait_fetch(ys_ref, ybuf.at[other], sem.at[other])


def _final(fsrc, x1, route, ada, lnw, lnb, ys, seq_len):
    T, D = x1.shape
    steps_per_seq = seq_len // TS
    n_tiles = T // TS
    assert n_tiles > FETCH_AHEAD
    grid_spec = pltpu.PrefetchScalarGridSpec(
        num_scalar_prefetch=1,
        grid=(n_tiles,),
        in_specs=[pl.BlockSpec((TS, D), lambda i, src: (i, 0)),
                  pl.BlockSpec((TS, LANES), lambda i, src: (i, 0)),
                  pl.BlockSpec((None, 6, D), lambda i, src: (i // steps_per_seq, 0, 0)),
                  pl.BlockSpec((1, D), lambda i, src: (0, 0)),
                  pl.BlockSpec((1, D), lambda i, src: (0, 0)),
                  pl.BlockSpec(memory_space=pl.ANY)],
        out_specs=pl.BlockSpec((TS, D), lambda i, src: (i, 0)),
        scratch_shapes=[pltpu.VMEM((FETCH_SLOTS, TR // SUBLANES, 2 * SUBLANES, HALF), jnp.bfloat16),
                        pltpu.SemaphoreType.DMA((FETCH_SLOTS,))],
    )
    return pl.pallas_call(
        _final_kernel,
        grid_spec=grid_spec,
        out_shape=jax.ShapeDtypeStruct((T, D), jnp.float32),
        compiler_params=pltpu.CompilerParams(dimension_semantics=("arbitrary",), vmem_limit_bytes=VMEM_LIMIT),
        name="final",
    )(fsrc, x1, route, ada, lnw, lnb, ys)


def _gather_tables(cnt, n_pages):
    nt = cnt.shape[0]
    run = (cnt + SUBLANES - 1) // SUBLANES * SUBLANES
    so = jnp.cumsum(run, axis=1) - run
    eo = jnp.cumsum(run, axis=0) - run
    tot = jnp.sum(run, axis=0)
    pages_e = (tot + TM - 1) // TM
    page_end = jnp.cumsum(pages_e)
    page_start = page_end - pages_e
    g = jnp.arange(n_pages, dtype=jnp.int32)
    pe = jnp.minimum(jnp.sum(g[:, None] >= page_end[None, :], axis=1), N_EXPERTS - 1).astype(jnp.int32)
    used = g < page_end[-1]
    owner = (g[None, :] >= page_start[:, None]) & (g[None, :] < page_end[:, None])
    of_page = lambda a: jnp.sum(jnp.where(owner, a[:, None], 0), axis=0)
    of_page2 = lambda a: jnp.sum(jnp.where(owner[:, None, :], a.T[:, :, None], 0), axis=0)
    pj = g - of_page(page_start)
    tot_p = of_page(tot)
    pv = jnp.where(used, jnp.clip(tot_p - pj * TM, 0, TM), 0).astype(jnp.int32)

    q = (pj * TM)[:, None] + SUBLANES * jnp.arange(TM // SUBLANES, dtype=jnp.int32)[None, :]
    tiles = jnp.arange(nt, dtype=jnp.int32)
    offset = of_page2(tiles[:, None] * TR + so - eo)
    step = jnp.concatenate([offset[1:] - offset[:-1], jnp.zeros_like(offset[:1])], axis=0)
    run_end = of_page2(eo + run)
    src = q + offset[0][:, None] + jnp.sum(jnp.where(run_end[:, :, None] <= q[None], step[:, :, None], 0), axis=0)
    zero_xt = TR - SUBLANES
    esrc = jnp.where((q < tot_p[:, None]) & used[:, None], src, zero_xt).astype(jnp.int32)

    r = SUBLANES * jnp.arange(TR // SUBLANES, dtype=jnp.int32)
    end = (so + run).T
    offset_f = ((page_start * TM)[None, :] + eo - so).T
    step_f = jnp.concatenate([offset_f[1:] - offset_f[:-1], jnp.zeros_like(offset_f[:1])], axis=0)
    srcf = (r[None, :] + offset_f[0][:, None]
            + jnp.sum(jnp.where(end[:, :, None] <= r[None, None, :], step_f[:, :, None], 0), axis=0))
    zero_ys = (n_pages - 1) * TM
    fsrc = jnp.where(r[None, :] < end[-1][:, None], srcf, zero_ys).astype(jnp.int32)
    return pe, pv, esrc.reshape(-1) // SUBLANES, fsrc.reshape(-1) // SUBLANES


def kernel(x, c, positions, w_ada, b_ada, w_in, ret_gn_w, w_pool, pool_scale, w_out, ln1_w, ln1_b, w_group, b_group,
           w_router, b_router, w1, w3, w2, ln2_w, ln2_b):
    B, S, D = x.shape
    T = B * S
    assert w_ada.shape[0] == DEPTH and D == D_MODEL and S % TS == 0

    inv_freq = ROPE_BASE ** (-jnp.arange(0, HEAD_DIM, 2, dtype=jnp.float32) / HEAD_DIM)
    half = HEAD_DIM // 2
    rope = jnp.stack([jnp.concatenate([inv_freq, inv_freq]),
                      jnp.concatenate([-jnp.ones((half,), jnp.float32), jnp.ones((half,), jnp.float32)])])

    nt = T // TS
    n_pages = (2 * T + nt * N_EXPERTS * (SUBLANES - 1)) // TM + N_EXPERTS + 1

    for l in range(DEPTH):
        ada = _ada(c, w_ada[l], b_ada[l]).reshape(B, 6, D)
        pad = SUBLANES - N_GROUPS
        wrt = jnp.concatenate([w_group[l].T, jnp.zeros((pad, D), jnp.float32), w_router[l].T], axis=0)
        brt = jnp.concatenate([b_group[l], jnp.zeros((pad,), jnp.float32), b_router[l]]).reshape(ROUTE_ROWS, 1)
        x1, xt, route, counts = _mix(
            x, positions, ada, rope, _bf(w_in[l]), _bf(w_out[l]), _bf(w_pool[l]),
            ret_gn_w[l].reshape(1, RET_WIDTH), pool_scale[l].reshape(1, POOL_WIDTH),
            ln1_w[l].reshape(1, D), ln1_b[l].reshape(1, D), wrt, brt)

        cnt = counts[:, :, :, 0].reshape(nt, N_EXPERTS).astype(jnp.int32)
        pe, pv, esrc, fsrc = _gather_tables(cnt, n_pages)
        ys = _experts(pe, pv, esrc, xt.reshape(nt * TR // SUBLANES, 2 * SUBLANES, HALF),
                      _bf(w1[l]).reshape(N_EXPERTS, D, D_EXPERT), _bf(w3[l]).reshape(N_EXPERTS, D, D_EXPERT),
                      _bf(w2[l]).reshape(N_EXPERTS, D_EXPERT, D))
        x = _final(fsrc, x1.reshape(T, D), route.reshape(T, LANES), ada,
                   ln2_w[l].reshape(1, D), ln2_b[l].reshape(1, D),
                   ys.reshape(n_pages * TM // SUBLANES, 2 * SUBLANES, HALF), S).reshape(B, S, D)
    return x
```

```python
import math

import jax
import jax.numpy as jnp
from jax import lax
from jax.experimental import pallas as pl
from jax.experimental.pallas import tpu as pltpu

D_MODEL = 1024
RET_WIDTH = 512
RET_HEADS = 4
HEAD_DIM = 128
POOL_WIDTH = 512
POOL_WINDOWS = (2, 4, 8, 16)
POOL_GROUP_DIM = 128
IN_COLS = 4 * RET_WIDTH + POOL_WIDTH
N_GROUPS = 4
EXPERTS_PER_GROUP = 8
N_EXPERTS = 32
D_EXPERT = 256
DEPTH = 1
ALPHA = (2.0 * DEPTH) ** 0.25
LN_EPS = 1e-5
ROPE_BASE = 10000.0

LANES = 128
SUBLANES = 8
HALF = D_MODEL // 2
POOL_HALO = 16
TS = 256
TM = 256
TR = -(-(2 * TS + N_EXPERTS * (SUBLANES - 1)) // LANES) * LANES
SEQS_PER_STEP = 2
PAGES_PER_STEP = 2
TILES_PER_STEP = 2
ROUTE_ROWS = SUBLANES + N_EXPERTS
FETCH_AHEAD = 2
FETCH_SLOTS = FETCH_AHEAD + 1
VMEM_LIMIT = 56 * 1024 * 1024

_LOG_GAMMA = tuple(math.log1p(-(2.0 ** (-5.0 - h))) for h in range(RET_HEADS))
_HI = lax.Precision.HIGHEST


def _ln(x):
    mu = jnp.mean(x, axis=-1, keepdims=True)
    xc = x - mu
    var = jnp.mean(xc * xc, axis=-1, keepdims=True)
    return xc * lax.rsqrt(var + LN_EPS)


def _bf(x):
    return x.astype(jnp.bfloat16)


def _pack_rows(x):
    n = x.shape[0]
    lo = x[:, :HALF].reshape(n // SUBLANES, SUBLANES, HALF)
    hi = x[:, HALF:].reshape(n // SUBLANES, SUBLANES, HALF)
    return _bf(jnp.concatenate([lo, hi], axis=1).reshape(2 * n, HALF))


def _unpack_rows(z):
    n = z.shape[0] // 2
    zf = z.astype(jnp.float32).reshape(n // SUBLANES, 2 * SUBLANES, HALF)
    return _bf(zf[:, :SUBLANES, :].reshape(n, HALF)), _bf(zf[:, SUBLANES:, :].reshape(n, HALF))


def _ada_kernel(c_ref, w_ref, b_ref, o_ref):
    c = c_ref[...]
    ca = c * jax.nn.sigmoid(c)
    o_ref[...] = jnp.dot(ca, w_ref[...], precision=_HI, preferred_element_type=jnp.float32) + b_ref[...]


def _ada(c, w_ada, b_ada):
    B, D = c.shape
    n = w_ada.shape[1]
    return pl.pallas_call(
        _ada_kernel,
        grid=(n // D,),
        in_specs=[pl.BlockSpec((B, D), lambda j: (0, 0)),
                  pl.BlockSpec((D, D), lambda j: (0, j)),
                  pl.BlockSpec((1, D), lambda j: (0, j))],
        out_specs=pl.BlockSpec((B, D), lambda j: (0, j)),
        out_shape=jax.ShapeDtypeStruct((B, n), jnp.float32),
        compiler_params=pltpu.CompilerParams(dimension_semantics=("arbitrary",), vmem_limit_bytes=VMEM_LIMIT),
        name="ada",
    )(c, w_ada, b_ada.reshape(1, n))


def _mix_kernel(x_ref, pos_ref, ada_ref, rope_ref, win_ref, wout_ref, wpool_ref, gnw_ref, pscale_ref,
                ln1w_ref, ln1b_ref, wrt_ref, brt_ref,
                x1_ref, xt_ref, route_ref, cnt_ref,
                state_ref, halo_ref, dmat_ref, qdec_ref, kdec_ref, cat_ref):
    b = pl.program_id(0)
    s = pl.program_id(1)

    @pl.when((b == 0) & (s == 0))
    def _init_tables():
        ri = lax.broadcasted_iota(jnp.int32, (TS, TS), 0)
        ci = lax.broadcasted_iota(jnp.int32, (TS, TS), 1)
        rel = (ri - ci).astype(jnp.float32)
        for h in range(RET_HEADS):
            dmat_ref[h] = jnp.where(rel >= 0.0, jnp.exp(jnp.maximum(rel, 0.0) * _LOG_GAMMA[h]), 0.0)
        row = lax.broadcasted_iota(jnp.int32, (TS, RET_WIDTH), 0).astype(jnp.float32)
        lane = lax.broadcasted_iota(jnp.int32, (TS, RET_WIDTH), 1)
        lg = jnp.full((TS, RET_WIDTH), _LOG_GAMMA[0], jnp.float32)
        for h in range(1, RET_HEADS):
            lg = jnp.where(lane >= h * HEAD_DIM, _LOG_GAMMA[h], lg)
        qdec_ref[...] = jnp.exp((row + 1.0) * lg)
        kdec_ref[...] = jnp.exp((TS - 1.0 - row) * lg)

    @pl.when(s == 0)
    def _init_carries():
        state_ref[...] = jnp.zeros_like(state_ref)
        halo_ref[...] = jnp.zeros_like(halo_ref)

    tiles = [_mix_tile(s, x_ref.at[j], pos_ref.at[j], ada_ref.at[j], rope_ref, win_ref, wout_ref, wpool_ref, gnw_ref,
                       pscale_ref, ln1w_ref, ln1b_ref, wrt_ref, brt_ref,
                       x1_ref.at[j], xt_ref.at[j], route_ref.at[j], cnt_ref.at[j],
                       state_ref.at[j], halo_ref.at[j], dmat_ref, qdec_ref, kdec_ref, cat_ref.at[j])
             for j in range(SEQS_PER_STEP)]
    _round_robin(tiles)


def _mix_tile(s, x_ref, pos_ref, ada_ref, rope_ref, win_ref, wout_ref, wpool_ref, gnw_ref, pscale_ref,
              ln1w_ref, ln1b_ref, wrt_ref, brt_ref,
              x1_ref, xt_ref, route_ref, cnt_ref,
              state_ref, halo_ref, dmat_ref, qdec_ref, kdec_ref, cat_ref):
    ada = ada_ref[...]
    shift1, scale1, gate1 = ada[0:1], ada[1:2], ada[2:3]
    shift2, scale2 = ada[3:4], ada[4:5]

    x = x_ref[...]
    u = _bf(_ln(x) * (1.0 + scale1) + shift1)
    yield

    posf = pos_ref[...].astype(jnp.float32)
    hl = lax.broadcasted_iota(jnp.int32, (TS // 2, HEAD_DIM), 1) < HEAD_DIM // 2
    ang = jnp.where(hl, posf[:TS // 2], posf[TS // 2:]) * rope_ref[0:1, :]
    cos_p, sin_p = jnp.cos(ang), jnp.sin(ang)
    cos_s, sin_s = pltpu.roll(cos_p, HEAD_DIM // 2, 1), pltpu.roll(sin_p, HEAD_DIM // 2, 1)
    cos_t = jnp.concatenate([jnp.where(hl, cos_p, cos_s), jnp.where(hl, cos_s, cos_p)], axis=0)
    sin_t = jnp.concatenate([jnp.where(hl, sin_p, sin_s), jnp.where(hl, sin_s, sin_p)], axis=0) * rope_ref[1:2, :]
    yield

    q = jnp.dot(u, win_ref[:, 0:RET_WIDTH], preferred_element_type=jnp.float32)
    k = jnp.dot(u, win_ref[:, RET_WIDTH:2 * RET_WIDTH], preferred_element_type=jnp.float32)
    v = jnp.dot(u, win_ref[:, 2 * RET_WIDTH:3 * RET_WIDTH], preferred_element_type=jnp.float32)
    g = jnp.dot(u, win_ref[:, 3 * RET_WIDTH:4 * RET_WIDTH], preferred_element_type=jnp.float32)
    p = jnp.dot(u, win_ref[:, 4 * RET_WIDTH:IN_COLS], preferred_element_type=jnp.float32)
    yield

    gnw = gnw_ref[...]
    for h in range(RET_HEADS):
        sl = slice(h * HEAD_DIM, (h + 1) * HEAD_DIM)
        qh, kh, vh = q[:, sl], k[:, sl], v[:, sl]
        qr = qh * cos_t + pltpu.roll(qh, HEAD_DIM // 2, 1) * sin_t
        kr = (kh * cos_t + pltpu.roll(kh, HEAD_DIM // 2, 1) * sin_t) * (HEAD_DIM ** -0.5)
        vb = _bf(vh)
        sc = lax.dot_general(_bf(qr), _bf(kr), (((1,), (1,)), ((), ())), preferred_element_type=jnp.float32)
        intra = jnp.dot(_bf(sc * dmat_ref[h]), vb, preferred_element_type=jnp.float32)
        st = state_ref[h]
        cross = jnp.dot(_bf(qr * qdec_ref[:, sl]), _bf(st), preferred_element_type=jnp.float32)
        kv = lax.dot_general(_bf(kr * kdec_ref[:, sl]), vb, (((0,), (0,)), ((), ())),
                             preferred_element_type=jnp.float32)
        state_ref[h] = st * math.exp(TS * _LOG_GAMMA[h]) + kv
        r = _ln(intra + cross) * gnw[:, sl]
        gh = g[:, sl]
        cat_ref[:, sl] = _bf(gh * jax.nn.sigmoid(gh) * r)
        yield

    pext = jnp.concatenate([halo_ref[...], p], axis=0)
    halo_ref[...] = p[TS - POOL_HALO:, :]
    t_abs = (s * TS + lax.broadcasted_iota(jnp.int32, (TS, 1), 0) + 1).astype(jnp.float32)
    pscale = pscale_ref[...]
    for grp, w in enumerate(POOL_WINDOWS):
        sl = slice(grp * POOL_GROUP_DIM, (grp + 1) * POOL_GROUP_DIM)
        acc = pext[:, sl]
        shift = 1
        while shift < w:
            acc = acc + pltpu.roll(acc, shift, 0)
            shift *= 2
        pooled = acc[POOL_HALO:, :] / jnp.minimum(t_abs, float(w)) - p[:, sl]
        po = jnp.dot(_bf(pooled), wpool_ref[grp], preferred_element_type=jnp.float32) * pscale[:, sl]
        cat_ref[:, RET_WIDTH + grp * POOL_GROUP_DIM:RET_WIDTH + (grp + 1) * POOL_GROUP_DIM] = _bf(po)
    yield

    mix = jnp.dot(cat_ref[...], wout_ref[...], preferred_element_type=jnp.float32)
    yield
    x1 = _ln(ALPHA * x + gate1 * mix) * ln1w_ref[...] + ln1b_ref[...]
    x1_ref[...] = x1
    u2 = _ln(x1) * (1.0 + scale2) + shift2
    yield

    w = wrt_ref[...]
    w_hi = _bf(w)
    w_lo = _bf(w - w_hi.astype(jnp.float32))
    u2_hi = _bf(u2)
    u2_lo = _bf(u2 - u2_hi.astype(jnp.float32))
    nt = (((1,), (1,)), ((), ()))
    logits = (lax.dot_general(w_hi, u2_hi, nt, preferred_element_type=jnp.float32)
              + lax.dot_general(w_hi, u2_lo, nt, preferred_element_type=jnp.float32)
              + lax.dot_general(w_lo, u2_hi, nt, preferred_element_type=jnp.float32)) + brt_ref[...]
    row8 = lax.broadcasted_iota(jnp.int32, (SUBLANES, TS), 0)
    neg = jnp.float32(-jnp.inf)
    gl = jnp.where(row8 < N_GROUPS, logits[0:SUBLANES], neg)
    gmax = jnp.max(gl, axis=0, keepdims=True)
    gidx = jnp.min(jnp.where(gl == gmax, row8, SUBLANES), axis=0, keepdims=True)
    gprob = 1.0 / jnp.sum(jnp.exp(gl - gmax), axis=0, keepdims=True)
    el = logits[SUBLANES:2 * SUBLANES]
    for grp in range(1, N_GROUPS):
        el = jnp.where(gidx == grp, logits[(grp + 1) * SUBLANES:(grp + 2) * SUBLANES], el)
    m1 = jnp.max(el, axis=0, keepdims=True)
    j1 = jnp.min(jnp.where(el == m1, row8, SUBLANES), axis=0, keepdims=True)
    el2 = jnp.where(row8 == j1, neg, el)
    m2 = jnp.max(el2, axis=0, keepdims=True)
    j2 = jnp.min(jnp.where(el2 == m2, row8, SUBLANES), axis=0, keepdims=True)
    e21 = jnp.exp(m2 - m1)
    den = 1.0 / (1.0 + e21)
    cw1 = gprob * den
    cw2 = gprob * e21 * den
    yield

    erow = lax.broadcasted_iota(jnp.int32, (N_EXPERTS, TS), 0)
    oh1 = erow == gidx * EXPERTS_PER_GROUP + j1
    oh2 = erow == gidx * EXPERTS_PER_GROUP + j2
    oh = jnp.where(oh1 | oh2, 1.0, 0.0)
    cnt = jnp.broadcast_to(jnp.sum(oh, axis=1, keepdims=True), (N_EXPERTS, LANES))
    run = jnp.floor((cnt + (SUBLANES - 1.0)) * (1.0 / SUBLANES)) * SUBLANES
    erow_l = lax.broadcasted_iota(jnp.int32, (N_EXPERTS, LANES), 0)
    run_end = run
    shift = 1
    while shift < N_EXPERTS:
        run_end = run_end + jnp.where(erow_l >= shift, pltpu.roll(run_end, shift, 0), 0.0)
        shift *= 2
    run_start = (run_end - run)[:, 0:1]
    ri = lax.broadcasted_iota(jnp.int32, (TS, TS), 0)
    ci = lax.broadcasted_iota(jnp.int32, (TS, TS), 1)
    earlier = _bf(jnp.where(ri < ci, 1.0, 0.0))
    before = jnp.dot(_bf(oh), earlier, preferred_element_type=jnp.float32) + run_start
    pos1 = jnp.sum(jnp.where(oh1, before, 0.0), axis=0, keepdims=True)
    pos2 = jnp.sum(jnp.where(oh2, before, 0.0), axis=0, keepdims=True)
    rr = lax.broadcasted_iota(jnp.int32, (TR, TS), 0).astype(jnp.float32)
    perm = _bf(jnp.where((rr == pos1) | (rr == pos2), 1.0, 0.0))
    yield
    xt_ref[...] = _pack_rows(jnp.dot(perm, u2_hi, preferred_element_type=jnp.float32))
    cnt_ref[...] = cnt

    rowl = lax.broadcasted_iota(jnp.int32, (LANES, TS), 0)
    rec = jnp.where(rowl == 0, pos1, 0.0)
    rec = jnp.where(rowl == 1, pos2, rec)
    rec = jnp.where(rowl == 2, cw1, rec)
    rec = jnp.where(rowl == 3, cw2, rec)
    route_ref[...] = rec.T


def _mix(x, positions, ada, rope, win, wout, wpool, gnw, pscale, ln1w, ln1b, wrt, brt):
    B, S, D = x.shape
    ns = S // TS
    assert B % SEQS_PER_STEP == 0
    P = SEQS_PER_STEP
    const2 = lambda b, s: (0, 0)
    const3 = lambda b, s: (0, 0, 0)
    tile = lambda b, s: (b, s, 0)
    flat = lambda b, s: (b, s, 0, 0)
    return pl.pallas_call(
        _mix_kernel,
        grid=(B // P, ns),
        in_specs=[
            pl.BlockSpec((P, TS, D), tile),
            pl.BlockSpec((P, TS, 1), tile),
            pl.BlockSpec((P, 6, D), lambda b, s: (b, 0, 0)),
            pl.BlockSpec((2, LANES), const2),
            pl.BlockSpec((D, IN_COLS), const2),
            pl.BlockSpec((D, D), const2),
            pl.BlockSpec((len(POOL_WINDOWS), POOL_GROUP_DIM, POOL_GROUP_DIM), const3),
            pl.BlockSpec((1, RET_WIDTH), const2),
            pl.BlockSpec((1, POOL_WIDTH), const2),
            pl.BlockSpec((1, D), const2),
            pl.BlockSpec((1, D), const2),
            pl.BlockSpec((ROUTE_ROWS, D), const2),
            pl.BlockSpec((ROUTE_ROWS, 1), const2),
        ],
        out_specs=[
            pl.BlockSpec((P, TS, D), tile),
            pl.BlockSpec((P, None, 2 * TR, HALF), flat),
            pl.BlockSpec((P, TS, LANES), tile),
            pl.BlockSpec((P, None, N_EXPERTS, LANES), flat),
        ],
        out_shape=[
            jax.ShapeDtypeStruct((B, S, D), jnp.float32),
            jax.ShapeDtypeStruct((B, ns, 2 * TR, HALF), jnp.bfloat16),
            jax.ShapeDtypeStruct((B, S, LANES), jnp.float32),
            jax.ShapeDtypeStruct((B, ns, N_EXPERTS, LANES), jnp.float32),
        ],
        scratch_shapes=[
            pltpu.VMEM((P, RET_HEADS, HEAD_DIM, HEAD_DIM), jnp.float32),
            pltpu.VMEM((P, POOL_HALO, POOL_WIDTH), jnp.float32),
            pltpu.VMEM((RET_HEADS, TS, TS), jnp.float32),
            pltpu.VMEM((TS, RET_WIDTH), jnp.float32),
            pltpu.VMEM((TS, RET_WIDTH), jnp.float32),
            pltpu.VMEM((P, TS, D), jnp.bfloat16),
        ],
        compiler_params=pltpu.CompilerParams(dimension_semantics=("arbitrary", "arbitrary"),
                                             vmem_limit_bytes=VMEM_LIMIT),
        name="mix",
    )(x, positions.reshape(B, S, 1), ada, rope, win, wout, wpool, gnw, pscale, ln1w, ln1b, wrt, brt)


def _fetch_groups(src_ref, group_of, dst_ref, sem):
    for u in range(dst_ref.shape[0]):
        pltpu.make_async_copy(src_ref.at[group_of(u)], dst_ref.at[u], sem).start()


def _wait_fetch(src_ref, dst_ref, sem):
    pltpu.make_async_copy(src_ref.at[pl.ds(0, dst_ref.shape[0])], dst_ref, sem).wait()


def _expert_page(x_ref, w1_ref, w3_ref, w2_ref, y_ref):
    xl, xh = _unpack_rows(x_ref[...].reshape(2 * TM, HALF))
    yield
    a = (jnp.dot(xl, w1_ref[:HALF, :], preferred_element_type=jnp.float32)
         + jnp.dot(xh, w1_ref[HALF:, :], preferred_element_type=jnp.float32))
    c = (jnp.dot(xl, w3_ref[:HALF, :], preferred_element_type=jnp.float32)
         + jnp.dot(xh, w3_ref[HALF:, :], preferred_element_type=jnp.float32))
    yield
    h = _bf(a * jax.nn.sigmoid(a) * c)
    yield
    y = jnp.dot(h, w2_ref[...], preferred_element_type=jnp.float32)
    yield
    y_ref[...] = _pack_rows(y)


def _round_robin(tasks):
    while tasks:
        tasks = [t for t in tasks if next(t, True) is None]


def _expert_kernel(pe_ref, pv_ref, src_ref, xt_ref, *refs):
    P = PAGES_PER_STEP
    w_refs, (ys_ref, xbuf, sem) = refs[:3 * P], refs[3 * P:]
    g = pl.program_id(0)
    last = pl.num_programs(0) - 1
    groups = P * (TM // SUBLANES)

    def fetch(step, dst_slot):
        @pl.when(pv_ref[step * P] > 0)
        def _start():
            _fetch_groups(xt_ref, lambda u: src_ref[step * groups + u], xbuf.at[dst_slot], sem.at[dst_slot])

    @pl.when(g == 0)
    def _prime():
        for k in range(FETCH_AHEAD):
            fetch(k, k)

    fetch(jnp.minimum(g + FETCH_AHEAD, last), (g + FETCH_AHEAD) % FETCH_SLOTS)
    slot = g % FETCH_SLOTS
    used = pv_ref[g * P] > 0

    @pl.when(used)
    def _compute():
        _wait_fetch(xt_ref, xbuf.at[slot], sem.at[slot])
        per = TM // SUBLANES
        _round_robin([_expert_page(xbuf.at[slot, pl.ds(k * per, per)], *w_refs[3 * k:3 * k + 3],
                                   ys_ref.at[pl.ds(k * 2 * TM, 2 * TM)]) for k in range(P)])

    @pl.when(jnp.logical_not(used))
    def _unused():
        ys_ref[...] = jnp.zeros_like(ys_ref)


def _experts(pe, pv, esrc, xt, w1, w3, w2):
    D = D_MODEL
    P = PAGES_PER_STEP
    n_steps = pe.shape[0] // P
    assert pe.shape[0] % P == 0 and n_steps > FETCH_AHEAD
    w_specs = []
    for k in range(P):
        page_expert = lambda g, pe, pv, src, k=k: (pe[g * P + k], 0, 0)
        w_specs += [pl.BlockSpec((None, D, D_EXPERT), page_expert), pl.BlockSpec((None, D, D_EXPERT), page_expert),
                    pl.BlockSpec((None, D_EXPERT, D), page_expert)]
    grid_spec = pltpu.PrefetchScalarGridSpec(
        num_scalar_prefetch=3,
        grid=(n_steps,),
        in_specs=[pl.BlockSpec(memory_space=pl.ANY)] + w_specs,
        out_specs=pl.BlockSpec((P * 2 * TM, HALF), lambda g, pe, pv, src: (g, 0)),
        scratch_shapes=[pltpu.VMEM((FETCH_SLOTS, P * (TM // SUBLANES), 2 * SUBLANES, HALF), jnp.bfloat16),
                        pltpu.SemaphoreType.DMA((FETCH_SLOTS,))],
    )
    return pl.pallas_call(
        _expert_kernel,
        grid_spec=grid_spec,
        out_shape=jax.ShapeDtypeStruct((pe.shape[0] * 2 * TM, HALF), jnp.bfloat16),
        compiler_params=pltpu.CompilerParams(dimension_semantics=("arbitrary",), vmem_limit_bytes=VMEM_LIMIT),
        name="experts",
    )(pe, pv, esrc, xt, *([w1, w3, w2] * P))


def _final_tile(y_ref, x1_ref, route_ref, ada_ref, lnw_ref, lnb_ref, o_ref):
    route = route_ref[...]
    pos1, pos2, cw1, cw2 = route[:, 0:1], route[:, 1:2], route[:, 2:3], route[:, 3:4]
    col = lax.broadcasted_iota(jnp.int32, (TS, TR), 1).astype(jnp.float32)
    wmat = _bf(jnp.where(col == pos1, cw1, 0.0) + jnp.where(col == pos2, cw2, 0.0))
    yield
    yl, yh = _unpack_rows(y_ref[...].reshape(2 * TR, HALF))
    yield
    y = jnp.concatenate([jnp.dot(wmat, yl, preferred_element_type=jnp.float32),
                         jnp.dot(wmat, yh, preferred_element_type=jnp.float32)], axis=-1)
    yield
    gate2 = ada_ref[5:6, :]
    o_ref[...] = _ln(ALPHA * x1_ref[...] + gate2 * y) * lnw_ref[...] + lnb_ref[...]


def _final_kernel(src_ref, x1_ref, route_ref, ada_ref, lnw_ref, lnb_ref, ys_ref, o_ref, ybuf, sem):
    P = TILES_PER_STEP
    i = pl.program_id(0)
    last = pl.num_programs(0) - 1
    per = TR // SUBLANES
    groups = P * per

    def fetch(step, dst_slot):
        _fetch_groups(ys_ref, lambda u: src_ref[step * groups + u], ybuf.at[dst_slot], sem.at[dst_slot])

    @pl.when(i == 0)
    def _prime():
        for k in range(FETCH_AHEAD):
            fetch(k, k)

    fetch(jnp.minimum(i + FETCH_AHEAD, last), (i + FETCH_AHEAD) % FETCH_SLOTS)
    slot = i % FETCH_SLOTS
    _wait_fetch(ys_ref, ybuf.at[slot], sem.at[slot])

    rows = lambda k: pl.ds(k * TS, TS)
    _round_robin([_final_tile(ybuf.at[slot, pl.ds(k * per, per)], x1_ref.at[rows(k)], route_ref.at[rows(k)], ada_ref,
                              lnw_ref, lnb_ref, o_ref.at[rows(k)]) for k in range(P)])

    @pl.when(i == last)
    def _drain():
        for k in range(1, FETCH_SLOTS):
            other = (i + k) % FETCH_SLOTS
            _wait_fetch(ys_ref, ybuf.at[other], sem.at[other])


def _final(fsrc, x1, route, ada, lnw, lnb, ys, seq_len):
    T, D = x1.shape
    P = TILES_PER_STEP
    rows = P * TS
    assert seq_len % rows == 0
    steps_per_seq = seq_len // rows
    n_steps = T // rows
    assert n_steps > FETCH_AHEAD
    grid_spec = pltpu.PrefetchScalarGridSpec(
        num_scalar_prefetch=1,
        grid=(n_steps,),
        in_specs=[pl.BlockSpec((rows, D), lambda i, src: (i, 0)),
                  pl.BlockSpec((rows, LANES), lambda i, src: (i, 0)),
                  pl.BlockSpec((None, 6, D), lambda i, src: (i // steps_per_seq, 0, 0)),
                  pl.BlockSpec((1, D), lambda i, src: (0, 0)),
                  pl.BlockSpec((1, D), lambda i, src: (0, 0)),
                  pl.BlockSpec(memory_space=pl.ANY)],
        out_specs=pl.BlockSpec((rows, D), lambda i, src: (i, 0)),
        scratch_shapes=[pltpu.VMEM((FETCH_SLOTS, P * (TR // SUBLANES), 2 * SUBLANES, HALF), jnp.bfloat16),
                        pltpu.SemaphoreType.DMA((FETCH_SLOTS,))],
    )
    return pl.pallas_call(
        _final_kernel,
        grid_spec=grid_spec,
        out_shape=jax.ShapeDtypeStruct((T, D), jnp.float32),
        compiler_params=pltpu.CompilerParams(dimension_semantics=("arbitrary",), vmem_limit_bytes=VMEM_LIMIT),
        name="final",
    )(fsrc, x1, route, ada, lnw, lnb, ys)


def _gather_tables(cnt, n_pages):
    nt = cnt.shape[0]
    run = (cnt + SUBLANES - 1) // SUBLANES * SUBLANES
    so = jnp.cumsum(run, axis=1) - run
    eo = jnp.cumsum(run, axis=0) - run
    tot = jnp.sum(run, axis=0)
    pages_e = (tot + TM - 1) // TM
    page_end = jnp.cumsum(pages_e)
    page_start = page_end - pages_e
    g = jnp.arange(n_pages, dtype=jnp.int32)
    pe = jnp.minimum(jnp.sum(g[:, None] >= page_end[None, :], axis=1), N_EXPERTS - 1).astype(jnp.int32)
    used = g < page_end[-1]
    owner = (g[None, :] >= page_start[:, None]) & (g[None, :] < page_end[:, None])
    of_page = lambda a: jnp.sum(jnp.where(owner, a[:, None], 0), axis=0)
    of_page2 = lambda a: jnp.sum(jnp.where(owner[:, None, :], a.T[:, :, None], 0), axis=0)
    pj = g - of_page(page_start)
    tot_p = of_page(tot)
    pv = jnp.where(used, jnp.clip(tot_p - pj * TM, 0, TM), 0).astype(jnp.int32)

    q = (pj * TM)[:, None] + SUBLANES * jnp.arange(TM // SUBLANES, dtype=jnp.int32)[None, :]
    tiles = jnp.arange(nt, dtype=jnp.int32)
    offset = of_page2(tiles[:, None] * TR + so - eo)
    step = jnp.concatenate([offset[1:] - offset[:-1], jnp.zeros_like(offset[:1])], axis=0)
    run_end = of_page2(eo + run)
    src = q + offset[0][:, None] + jnp.sum(jnp.where(run_end[:, :, None] <= q[None], step[:, :, None], 0), axis=0)
    zero_xt = TR - SUBLANES
    esrc = jnp.where((q < tot_p[:, None]) & used[:, None], src, zero_xt).astype(jnp.int32)

    r = SUBLANES * jnp.arange(TR // SUBLANES, dtype=jnp.int32)
    end = (so + run).T
    offset_f = ((page_start * TM)[None, :] + eo - so).T
    step_f = jnp.concatenate([offset_f[1:] - offset_f[:-1], jnp.zeros_like(offset_f[:1])], axis=0)
    srcf = (r[None, :] + offset_f[0][:, None]
            + jnp.sum(jnp.where(end[:, :, None] <= r[None, None, :], step_f[:, :, None], 0), axis=0))
    zero_ys = (n_pages - 1) * TM
    fsrc = jnp.where(r[None, :] < end[-1][:, None], srcf, zero_ys).astype(jnp.int32)
    return pe, pv, esrc.reshape(-1) // SUBLANES, fsrc.reshape(-1) // SUBLANES


def kernel(x, c, positions, w_ada, b_ada, w_in, ret_gn_w, w_pool, pool_scale, w_out, ln1_w, ln1_b, w_group, b_group,
           w_router, b_router, w1, w3, w2, ln2_w, ln2_b):
    B, S, D = x.shape
    T = B * S
    assert w_ada.shape[0] == DEPTH and D == D_MODEL and S % TS == 0

    inv_freq = ROPE_BASE ** (-jnp.arange(0, HEAD_DIM, 2, dtype=jnp.float32) / HEAD_DIM)
    half = HEAD_DIM // 2
    rope = jnp.stack([jnp.concatenate([inv_freq, inv_freq]),
                      jnp.concatenate([-jnp.ones((half,), jnp.float32), jnp.ones((half,), jnp.float32)])])

    nt = T // TS
    n_pages = (2 * T + nt * N_EXPERTS * (SUBLANES - 1)) // TM + N_EXPERTS + 1
    n_pages = -(-n_pages // PAGES_PER_STEP) * PAGES_PER_STEP

    for l in range(DEPTH):
        ada = _ada(c, w_ada[l], b_ada[l]).reshape(B, 6, D)
        pad = SUBLANES - N_GROUPS
        wrt = jnp.concatenate([w_group[l].T, jnp.zeros((pad, D), jnp.float32), w_router[l].T], axis=0)
        brt = jnp.concatenate([b_group[l], jnp.zeros((pad,), jnp.float32), b_router[l]]).reshape(ROUTE_ROWS, 1)
        x1, xt, route, counts = _mix(
            x, positions, ada, rope, _bf(w_in[l]), _bf(w_out[l]), _bf(w_pool[l]),
            ret_gn_w[l].reshape(1, RET_WIDTH), pool_scale[l].reshape(1, POOL_WIDTH),
            ln1_w[l].reshape(1, D), ln1_b[l].reshape(1, D), wrt, brt)

        cnt = counts[:, :, :, 0].reshape(nt, N_EXPERTS).astype(jnp.int32)
        pe, pv, esrc, fsrc = _gather_tables(cnt, n_pages)
        ys = _experts(pe, pv, esrc, xt.reshape(nt * TR // SUBLANES, 2 * SUBLANES, HALF),
                      _bf(w1[l]).reshape(N_EXPERTS, D, D_EXPERT), _bf(w3[l]).reshape(N_EXPERTS, D, D_EXPERT),
                      _bf(w2[l]).reshape(N_EXPERTS, D_EXPERT, D))
        x = _final(fsrc, x1.reshape(T, D), route.reshape(T, LANES), ada,
                   ln2_w[l].reshape(1, D), ln2_b[l].reshape(1, D),
                   ys.reshape(n_pages * TM // SUBLANES, 2 * SUBLANES, HALF), S).reshape(B, S, D)
    return x
```

```python
import math

import jax
import jax.numpy as jnp
from jax import lax
from jax.experimental import pallas as pl
from jax.experimental.pallas import tpu as pltpu

D_MODEL = 1024
RET_WIDTH = 512
RET_HEADS = 4
HEAD_DIM = 128
POOL_WIDTH = 512
POOL_WINDOWS = (2, 4, 8, 16)
POOL_GROUP_DIM = 128
IN_COLS = 4 * RET_WIDTH + POOL_WIDTH
N_GROUPS = 4
EXPERTS_PER_GROUP = 8
N_EXPERTS = 32
D_EXPERT = 256
DEPTH = 1
ALPHA = (2.0 * DEPTH) ** 0.25
LN_EPS = 1e-5
ROPE_BASE = 10000.0

LANES = 128
SUBLANES = 8
HALF = D_MODEL // 2
POOL_HALO = 16
TS = 256
TM = 512
PAGE_SPLIT_ROWS = 256
TR = -(-(2 * TS + N_EXPERTS * (SUBLANES - 1)) // LANES) * LANES
SEQS_PER_STEP = 2
TILES_PER_STEP = 2
ROUTE_ROWS = SUBLANES + N_EXPERTS
FETCH_AHEAD = 2
FETCH_SLOTS = FETCH_AHEAD + 1
VMEM_LIMIT = 56 * 1024 * 1024

_LOG_GAMMA = tuple(math.log1p(-(2.0 ** (-5.0 - h))) for h in range(RET_HEADS))
_HI = lax.Precision.HIGHEST


def _ln(x):
    mu = jnp.mean(x, axis=-1, keepdims=True)
    xc = x - mu
    var = jnp.mean(xc * xc, axis=-1, keepdims=True)
    return xc * lax.rsqrt(var + LN_EPS)


def _bf(x):
    return x.astype(jnp.bfloat16)


def _pack_rows(x):
    n = x.shape[0]
    lo = x[:, :HALF].reshape(n // SUBLANES, SUBLANES, HALF)
    hi = x[:, HALF:].reshape(n // SUBLANES, SUBLANES, HALF)
    return _bf(jnp.concatenate([lo, hi], axis=1).reshape(2 * n, HALF))


def _unpack_rows(z):
    n = z.shape[0] // 2
    zf = z.astype(jnp.float32).reshape(n // SUBLANES, 2 * SUBLANES, HALF)
    return _bf(zf[:, :SUBLANES, :].reshape(n, HALF)), _bf(zf[:, SUBLANES:, :].reshape(n, HALF))


def _ada_kernel(c_ref, w_ref, b_ref, o_ref):
    c = c_ref[...]
    ca = c * jax.nn.sigmoid(c)
    o_ref[...] = jnp.dot(ca, w_ref[...], precision=_HI, preferred_element_type=jnp.float32) + b_ref[...]


def _ada(c, w_ada, b_ada):
    B, D = c.shape
    n = w_ada.shape[1]
    return pl.pallas_call(
        _ada_kernel,
        grid=(n // D,),
        in_specs=[pl.BlockSpec((B, D), lambda j: (0, 0)),
                  pl.BlockSpec((D, D), lambda j: (0, j)),
                  pl.BlockSpec((1, D), lambda j: (0, j))],
        out_specs=pl.BlockSpec((B, D), lambda j: (0, j)),
        out_shape=jax.ShapeDtypeStruct((B, n), jnp.float32),
        compiler_params=pltpu.CompilerParams(dimension_semantics=("arbitrary",), vmem_limit_bytes=VMEM_LIMIT),
        name="ada",
    )(c, w_ada, b_ada.reshape(1, n))


def _mix_kernel(x_ref, pos_ref, ada_ref, rope_ref, win_ref, wout_ref, wpool_ref, gnw_ref, pscale_ref,
                ln1w_ref, ln1b_ref, wrt_ref, brt_ref,
                x1_ref, xt_ref, route_ref, cnt_ref,
                state_ref, halo_ref, dmat_ref, qdec_ref, kdec_ref, cat_ref):
    b = pl.program_id(0)
    s = pl.program_id(1)

    @pl.when((b == 0) & (s == 0))
    def _init_tables():
        ri = lax.broadcasted_iota(jnp.int32, (TS, TS), 0)
        ci = lax.broadcasted_iota(jnp.int32, (TS, TS), 1)
        rel = (ri - ci).astype(jnp.float32)
        for h in range(RET_HEADS):
            dmat_ref[h] = jnp.where(rel >= 0.0, jnp.exp(jnp.maximum(rel, 0.0) * _LOG_GAMMA[h]), 0.0)
        row = lax.broadcasted_iota(jnp.int32, (TS, RET_WIDTH), 0).astype(jnp.float32)
        lane = lax.broadcasted_iota(jnp.int32, (TS, RET_WIDTH), 1)
        lg = jnp.full((TS, RET_WIDTH), _LOG_GAMMA[0], jnp.float32)
        for h in range(1, RET_HEADS):
            lg = jnp.where(lane >= h * HEAD_DIM, _LOG_GAMMA[h], lg)
        qdec_ref[...] = jnp.exp((row + 1.0) * lg)
        kdec_ref[...] = jnp.exp((TS - 1.0 - row) * lg)

    @pl.when(s == 0)
    def _init_carries():
        state_ref[...] = jnp.zeros_like(state_ref)
        halo_ref[...] = jnp.zeros_like(halo_ref)

    tiles = [_mix_tile(s, x_ref.at[j], pos_ref.at[j], ada_ref.at[j], rope_ref, win_ref, wout_ref, wpool_ref, gnw_ref,
                       pscale_ref, ln1w_ref, ln1b_ref, wrt_ref, brt_ref,
                       x1_ref.at[j], xt_ref.at[j], route_ref.at[j], cnt_ref.at[j],
                       state_ref.at[j], halo_ref.at[j], dmat_ref, qdec_ref, kdec_ref, cat_ref.at[j])
             for j in range(SEQS_PER_STEP)]
    _round_robin(tiles)


def _mix_tile(s, x_ref, pos_ref, ada_ref, rope_ref, win_ref, wout_ref, wpool_ref, gnw_ref, pscale_ref,
              ln1w_ref, ln1b_ref, wrt_ref, brt_ref,
              x1_ref, xt_ref, route_ref, cnt_ref,
              state_ref, halo_ref, dmat_ref, qdec_ref, kdec_ref, cat_ref):
    ada = ada_ref[...]
    shift1, scale1, gate1 = ada[0:1], ada[1:2], ada[2:3]
    shift2, scale2 = ada[3:4], ada[4:5]

    x = x_ref[...]
    u = _bf(_ln(x) * (1.0 + scale1) + shift1)
    yield

    posf = pos_ref[...].astype(jnp.float32)
    hl = lax.broadcasted_iota(jnp.int32, (TS // 2, HEAD_DIM), 1) < HEAD_DIM // 2
    ang = jnp.where(hl, posf[:TS // 2], posf[TS // 2:]) * rope_ref[0:1, :]
    cos_p, sin_p = jnp.cos(ang), jnp.sin(ang)
    cos_s, sin_s = pltpu.roll(cos_p, HEAD_DIM // 2, 1), pltpu.roll(sin_p, HEAD_DIM // 2, 1)
    cos_t = jnp.concatenate([jnp.where(hl, cos_p, cos_s), jnp.where(hl, cos_s, cos_p)], axis=0)
    sin_t = jnp.concatenate([jnp.where(hl, sin_p, sin_s), jnp.where(hl, sin_s, sin_p)], axis=0) * rope_ref[1:2, :]
    yield

    q = jnp.dot(u, win_ref[:, 0:RET_WIDTH], preferred_element_type=jnp.float32)
    k = jnp.dot(u, win_ref[:, RET_WIDTH:2 * RET_WIDTH], preferred_element_type=jnp.float32)
    v = jnp.dot(u, win_ref[:, 2 * RET_WIDTH:3 * RET_WIDTH], preferred_element_type=jnp.float32)
    g = jnp.dot(u, win_ref[:, 3 * RET_WIDTH:4 * RET_WIDTH], preferred_element_type=jnp.float32)
    p = jnp.dot(u, win_ref[:, 4 * RET_WIDTH:IN_COLS], preferred_element_type=jnp.float32)
    yield

    gnw = gnw_ref[...]
    for h in range(RET_HEADS):
        sl = slice(h * HEAD_DIM, (h + 1) * HEAD_DIM)
        qh, kh, vh = q[:, sl], k[:, sl], v[:, sl]
        qr = qh * cos_t + pltpu.roll(qh, HEAD_DIM // 2, 1) * sin_t
        kr = (kh * cos_t + pltpu.roll(kh, HEAD_DIM // 2, 1) * sin_t) * (HEAD_DIM ** -0.5)
        vb = _bf(vh)
        sc = lax.dot_general(_bf(qr), _bf(kr), (((1,), (1,)), ((), ())), preferred_element_type=jnp.float32)
        intra = jnp.dot(_bf(sc * dmat_ref[h]), vb, preferred_element_type=jnp.float32)
        st = state_ref[h]
        cross = jnp.dot(_bf(qr * qdec_ref[:, sl]), _bf(st), preferred_element_type=jnp.float32)
        kv = lax.dot_general(_bf(kr * kdec_ref[:, sl]), vb, (((0,), (0,)), ((), ())),
                             preferred_element_type=jnp.float32)
        state_ref[h] = st * math.exp(TS * _LOG_GAMMA[h]) + kv
        r = _ln(intra + cross) * gnw[:, sl]
        gh = g[:, sl]
        cat_ref[:, sl] = _bf(gh * jax.nn.sigmoid(gh) * r)
        yield

    pext = jnp.concatenate([halo_ref[...], p], axis=0)
    halo_ref[...] = p[TS - POOL_HALO:, :]
    t_abs = (s * TS + lax.broadcasted_iota(jnp.int32, (TS, 1), 0) + 1).astype(jnp.float32)
    pscale = pscale_ref[...]
    for grp, w in enumerate(POOL_WINDOWS):
        sl = slice(grp * POOL_GROUP_DIM, (grp + 1) * POOL_GROUP_DIM)
        acc = pext[:, sl]
        shift = 1
        while shift < w:
            acc = acc + pltpu.roll(acc, shift, 0)
            shift *= 2
        pooled = acc[POOL_HALO:, :] / jnp.minimum(t_abs, float(w)) - p[:, sl]
        po = jnp.dot(_bf(pooled), wpool_ref[grp], preferred_element_type=jnp.float32) * pscale[:, sl]
        cat_ref[:, RET_WIDTH + grp * POOL_GROUP_DIM:RET_WIDTH + (grp + 1) * POOL_GROUP_DIM] = _bf(po)
    yield

    mix = jnp.dot(cat_ref[...], wout_ref[...], preferred_element_type=jnp.float32)
    yield
    x1 = _ln(ALPHA * x + gate1 * mix) * ln1w_ref[...] + ln1b_ref[...]
    x1_ref[...] = x1
    u2 = _ln(x1) * (1.0 + scale2) + shift2
    yield

    w = wrt_ref[...]
    w_hi = _bf(w)
    w_lo = _bf(w - w_hi.astype(jnp.float32))
    u2_hi = _bf(u2)
    u2_lo = _bf(u2 - u2_hi.astype(jnp.float32))
    nt = (((1,), (1,)), ((), ()))
    logits = (lax.dot_general(w_hi, u2_hi, nt, preferred_element_type=jnp.float32)
              + lax.dot_general(w_hi, u2_lo, nt, preferred_element_type=jnp.float32)
              + lax.dot_general(w_lo, u2_hi, nt, preferred_element_type=jnp.float32)) + brt_ref[...]
    row8 = lax.broadcasted_iota(jnp.int32, (SUBLANES, TS), 0)
    neg = jnp.float32(-jnp.inf)
    gl = jnp.where(row8 < N_GROUPS, logits[0:SUBLANES], neg)
    gmax = jnp.max(gl, axis=0, keepdims=True)
    gidx = jnp.min(jnp.where(gl == gmax, row8, SUBLANES), axis=0, keepdims=True)
    gprob = 1.0 / jnp.sum(jnp.exp(gl - gmax), axis=0, keepdims=True)
    el = logits[SUBLANES:2 * SUBLANES]
    for grp in range(1, N_GROUPS):
        el = jnp.where(gidx == grp, logits[(grp + 1) * SUBLANES:(grp + 2) * SUBLANES], el)
    m1 = jnp.max(el, axis=0, keepdims=True)
    j1 = jnp.min(jnp.where(el == m1, row8, SUBLANES), axis=0, keepdims=True)
    el2 = jnp.where(row8 == j1, neg, el)
    m2 = jnp.max(el2, axis=0, keepdims=True)
    j2 = jnp.min(jnp.where(el2 == m2, row8, SUBLANES), axis=0, keepdims=True)
    e21 = jnp.exp(m2 - m1)
    den = 1.0 / (1.0 + e21)
    cw1 = gprob * den
    cw2 = gprob * e21 * den
    yield

    erow = lax.broadcasted_iota(jnp.int32, (N_EXPERTS, TS), 0)
    oh1 = erow == gidx * EXPERTS_PER_GROUP + j1
    oh2 = erow == gidx * EXPERTS_PER_GROUP + j2
    oh = jnp.where(oh1 | oh2, 1.0, 0.0)
    cnt = jnp.broadcast_to(jnp.sum(oh, axis=1, keepdims=True), (N_EXPERTS, LANES))
    run = jnp.floor((cnt + (SUBLANES - 1.0)) * (1.0 / SUBLANES)) * SUBLANES
    erow_l = lax.broadcasted_iota(jnp.int32, (N_EXPERTS, LANES), 0)
    run_end = run
    shift = 1
    while shift < N_EXPERTS:
        run_end = run_end + jnp.where(erow_l >= shift, pltpu.roll(run_end, shift, 0), 0.0)
        shift *= 2
    run_start = (run_end - run)[:, 0:1]
    ri = lax.broadcasted_iota(jnp.int32, (TS, TS), 0)
    ci = lax.broadcasted_iota(jnp.int32, (TS, TS), 1)
    earlier = _bf(jnp.where(ri < ci, 1.0, 0.0))
    before = jnp.dot(_bf(oh), earlier, preferred_element_type=jnp.float32) + run_start
    pos1 = jnp.sum(jnp.where(oh1, before, 0.0), axis=0, keepdims=True)
    pos2 = jnp.sum(jnp.where(oh2, before, 0.0), axis=0, keepdims=True)
    rr = lax.broadcasted_iota(jnp.int32, (TR, TS), 0).astype(jnp.float32)
    perm = _bf(jnp.where((rr == pos1) | (rr == pos2), 1.0, 0.0))
    yield
    xt_ref[...] = _pack_rows(jnp.dot(perm, u2_hi, preferred_element_type=jnp.float32))
    cnt_ref[...] = cnt

    rowl = lax.broadcasted_iota(jnp.int32, (LANES, TS), 0)
    rec = jnp.where(rowl == 0, pos1, 0.0)
    rec = jnp.where(rowl == 1, pos2, rec)
    rec = jnp.where(rowl == 2, cw1, rec)
    rec = jnp.where(rowl == 3, cw2, rec)
    route_ref[...] = rec.T


def _mix(x, positions, ada, rope, win, wout, wpool, gnw, pscale, ln1w, ln1b, wrt, brt):
    B, S, D = x.shape
    ns = S // TS
    assert B % SEQS_PER_STEP == 0
    P = SEQS_PER_STEP
    const2 = lambda b, s: (0, 0)
    const3 = lambda b, s: (0, 0, 0)
    tile = lambda b, s: (b, s, 0)
    flat = lambda b, s: (b, s, 0, 0)
    return pl.pallas_call(
        _mix_kernel,
        grid=(B // P, ns),
        in_specs=[
            pl.BlockSpec((P, TS, D), tile),
            pl.BlockSpec((P, TS, 1), tile),
            pl.BlockSpec((P, 6, D), lambda b, s: (b, 0, 0)),
            pl.BlockSpec((2, LANES), const2),
            pl.BlockSpec((D, IN_COLS), const2),
            pl.BlockSpec((D, D), const2),
            pl.BlockSpec((len(POOL_WINDOWS), POOL_GROUP_DIM, POOL_GROUP_DIM), const3),
            pl.BlockSpec((1, RET_WIDTH), const2),
            pl.BlockSpec((1, POOL_WIDTH), const2),
            pl.BlockSpec((1, D), const2),
            pl.BlockSpec((1, D), const2),
            pl.BlockSpec((ROUTE_ROWS, D), const2),
            pl.BlockSpec((ROUTE_ROWS, 1), const2),
        ],
        out_specs=[
            pl.BlockSpec((P, TS, D), tile),
            pl.BlockSpec((P, None, 2 * TR, HALF), flat),
            pl.BlockSpec((P, TS, LANES), tile),
            pl.BlockSpec((P, None, N_EXPERTS, LANES), flat),
        ],
        out_shape=[
            jax.ShapeDtypeStruct((B, S, D), jnp.float32),
            jax.ShapeDtypeStruct((B, ns, 2 * TR, HALF), jnp.bfloat16),
            jax.ShapeDtypeStruct((B, S, LANES), jnp.float32),
            jax.ShapeDtypeStruct((B, ns, N_EXPERTS, LANES), jnp.float32),
        ],
        scratch_shapes=[
            pltpu.VMEM((P, RET_HEADS, HEAD_DIM, HEAD_DIM), jnp.float32),
            pltpu.VMEM((P, POOL_HALO, POOL_WIDTH), jnp.float32),
            pltpu.VMEM((RET_HEADS, TS, TS), jnp.float32),
            pltpu.VMEM((TS, RET_WIDTH), jnp.float32),
            pltpu.VMEM((TS, RET_WIDTH), jnp.float32),
            pltpu.VMEM((P, TS, D), jnp.bfloat16),
        ],
        compiler_params=pltpu.CompilerParams(dimension_semantics=("arbitrary", "arbitrary"),
                                             vmem_limit_bytes=VMEM_LIMIT),
        name="mix",
    )(x, positions.reshape(B, S, 1), ada, rope, win, wout, wpool, gnw, pscale, ln1w, ln1b, wrt, brt)


def _fetch_groups(src_ref, group_of, dst_ref, sem):
    for u in range(dst_ref.shape[0]):
        pltpu.make_async_copy(src_ref.at[group_of(u)], dst_ref.at[u], sem).start()


def _wait_fetch(src_ref, dst_ref, sem):
    pltpu.make_async_copy(src_ref.at[pl.ds(0, dst_ref.shape[0])], dst_ref, sem).wait()


def _expert_rows(x_ref, weight, y_ref):
    n = PAGE_SPLIT_ROWS
    xl, xh = _unpack_rows(x_ref[...].reshape(2 * n, HALF))
    yield
    w1, w3 = weight(0), weight(1)
    a = (jnp.dot(xl, w1[:HALF, :], preferred_element_type=jnp.float32)
         + jnp.dot(xh, w1[HALF:, :], preferred_element_type=jnp.float32))
    c = (jnp.dot(xl, w3[:HALF, :], preferred_element_type=jnp.float32)
         + jnp.dot(xh, w3[HALF:, :], preferred_element_type=jnp.float32))
    yield
    h = _bf(a * jax.nn.sigmoid(a) * c)
    yield
    y = jnp.dot(h, weight(2), preferred_element_type=jnp.float32)
    yield
    y_ref[...] = _pack_rows(y)


def _round_robin(tasks):
    while tasks:
        tasks = [t for t in tasks if next(t, True) is None]


def _expert_kernel(pe_ref, pv_ref, src_ref, xt_ref, w1_ref, w3_ref, w2_ref, ys_ref, xbuf, sem):
    g = pl.program_id(0)
    last = pl.num_programs(0) - 1
    groups = TM // SUBLANES

    def fetch(page, dst_slot):
        @pl.when(pv_ref[page] > 0)
        def _start():
            _fetch_groups(xt_ref, lambda u: src_ref[page * groups + u], xbuf.at[dst_slot], sem.at[dst_slot])

    @pl.when(g == 0)
    def _prime():
        for k in range(FETCH_AHEAD):
            fetch(k, k)

    fetch(jnp.minimum(g + FETCH_AHEAD, last), (g + FETCH_AHEAD) % FETCH_SLOTS)
    slot = g % FETCH_SLOTS
    used = pv_ref[g] > 0

    @pl.when(used)
    def _compute():
        _wait_fetch(xt_ref, xbuf.at[slot], sem.at[slot])
        w_refs = (w1_ref, w3_ref, w2_ref)
        cast = {}

        def weight(i):
            if i not in cast:
                cast[i] = _bf(w_refs[i][...])
            return cast[i]

        per = PAGE_SPLIT_ROWS // SUBLANES
        _round_robin([_expert_rows(xbuf.at[slot, pl.ds(k * per, per)], weight,
                                   ys_ref.at[pl.ds(k * 2 * PAGE_SPLIT_ROWS, 2 * PAGE_SPLIT_ROWS)])
                      for k in range(TM // PAGE_SPLIT_ROWS)])

    @pl.when(jnp.logical_not(used))
    def _unused():
        ys_ref[...] = jnp.zeros_like(ys_ref)


def _experts(pe, pv, esrc, xt, w1, w3, w2):
    D = D_MODEL
    n_pages = pe.shape[0]
    assert n_pages > FETCH_AHEAD and TM % PAGE_SPLIT_ROWS == 0
    page_expert = lambda g, pe, pv, src: (pe[g], 0, 0)
    grid_spec = pltpu.PrefetchScalarGridSpec(
        num_scalar_prefetch=3,
        grid=(n_pages,),
        in_specs=[pl.BlockSpec(memory_space=pl.ANY),
                  pl.BlockSpec((None, D, D_EXPERT), page_expert),
                  pl.BlockSpec((None, D, D_EXPERT), page_expert),
                  pl.BlockSpec((None, D_EXPERT, D), page_expert)],
        out_specs=pl.BlockSpec((2 * TM, HALF), lambda g, pe, pv, src: (g, 0)),
        scratch_shapes=[pltpu.VMEM((FETCH_SLOTS, TM // SUBLANES, 2 * SUBLANES, HALF), jnp.bfloat16),
                        pltpu.SemaphoreType.DMA((FETCH_SLOTS,))],
    )
    return pl.pallas_call(
        _expert_kernel,
        grid_spec=grid_spec,
        out_shape=jax.ShapeDtypeStruct((n_pages * 2 * TM, HALF), jnp.bfloat16),
        compiler_params=pltpu.CompilerParams(dimension_semantics=("arbitrary",), vmem_limit_bytes=VMEM_LIMIT),
        name="experts",
    )(pe, pv, esrc, xt, w1, w3, w2)


def _final_tile(y_ref, x1_ref, route_ref, ada_ref, lnw_ref, lnb_ref, o_ref):
    route = route_ref[...]
    pos1, pos2, cw1, cw2 = route[:, 0:1], route[:, 1:2], route[:, 2:3], route[:, 3:4]
    col = lax.broadcasted_iota(jnp.int32, (TS, TR), 1).astype(jnp.float32)
    wmat = _bf(jnp.where(col == pos1, cw1, 0.0) + jnp.where(col == pos2, cw2, 0.0))
    yield
    yl, yh = _unpack_rows(y_ref[...].reshape(2 * TR, HALF))
    yield
    y = jnp.concatenate([jnp.dot(wmat, yl, preferred_element_type=jnp.float32),
                         jnp.dot(wmat, yh, preferred_element_type=jnp.float32)], axis=-1)
    yield
    gate2 = ada_ref[5:6, :]
    o_ref[...] = _ln(ALPHA * x1_ref[...] + gate2 * y) * lnw_ref[...] + lnb_ref[...]


def _final_kernel(src_ref, x1_ref, route_ref, ada_ref, lnw_ref, lnb_ref, ys_ref, o_ref, ybuf, sem):
    P = TILES_PER_STEP
    i = pl.program_id(0)
    last = pl.num_programs(0) - 1
    per = TR // SUBLANES
    groups = P * per

    def fetch(step, dst_slot):
        _fetch_groups(ys_ref, lambda u: src_ref[step * groups + u], ybuf.at[dst_slot], sem.at[dst_slot])

    @pl.when(i == 0)
    def _prime():
        for k in range(FETCH_AHEAD):
            fetch(k, k)

    fetch(jnp.minimum(i + FETCH_AHEAD, last), (i + FETCH_AHEAD) % FETCH_SLOTS)
    slot = i % FETCH_SLOTS
    _wait_fetch(ys_ref, ybuf.at[slot], sem.at[slot])

    rows = lambda k: pl.ds(k * TS, TS)
    _round_robin([_final_tile(ybuf.at[slot, pl.ds(k * per, per)], x1_ref.at[rows(k)], route_ref.at[rows(k)], ada_ref,
                              lnw_ref, lnb_ref, o_ref.at[rows(k)]) for k in range(P)])

    @pl.when(i == last)
    def _drain():
        for k in range(1, FETCH_SLOTS):
            other = (i + k) % FETCH_SLOTS
            _wait_fetch(ys_ref, ybuf.at[other], sem.at[other])


def _final(fsrc, x1, route, ada, lnw, lnb, ys, seq_len):
    T, D = x1.shape
    P = TILES_PER_STEP
    rows = P * TS
    assert seq_len % rows == 0
    steps_per_seq = seq_len // rows
    n_steps = T // rows
    assert n_steps > FETCH_AHEAD
    grid_spec = pltpu.PrefetchScalarGridSpec(
        num_scalar_prefetch=1,
        grid=(n_steps,),
        in_specs=[pl.BlockSpec((rows, D), lambda i, src: (i, 0)),
                  pl.BlockSpec((rows, LANES), lambda i, src: (i, 0)),
                  pl.BlockSpec((None, 6, D), lambda i, src: (i // steps_per_seq, 0, 0)),
                  pl.BlockSpec((1, D), lambda i, src: (0, 0)),
                  pl.BlockSpec((1, D), lambda i, src: (0, 0)),
                  pl.BlockSpec(memory_space=pl.ANY)],
        out_specs=pl.BlockSpec((rows, D), lambda i, src: (i, 0)),
        scratch_shapes=[pltpu.VMEM((FETCH_SLOTS, P * (TR // SUBLANES), 2 * SUBLANES, HALF), jnp.bfloat16),
                        pltpu.SemaphoreType.DMA((FETCH_SLOTS,))],
    )
    return pl.pallas_call(
        _final_kernel,
        grid_spec=grid_spec,
        out_shape=jax.ShapeDtypeStruct((T, D), jnp.float32),
        compiler_params=pltpu.CompilerParams(dimension_semantics=("arbitrary",), vmem_limit_bytes=VMEM_LIMIT),
        name="final",
    )(fsrc, x1, route, ada, lnw, lnb, ys)


def _gather_tables(cnt, n_pages):
    nt = cnt.shape[0]
    run = (cnt + SUBLANES - 1) // SUBLANES * SUBLANES
    so = jnp.cumsum(run, axis=1) - run
    eo = jnp.cumsum(run, axis=0) - run
    tot = jnp.sum(run, axis=0)
    pages_e = (tot + TM - 1) // TM
    page_end = jnp.cumsum(pages_e)
    page_start = page_end - pages_e
    g = jnp.arange(n_pages, dtype=jnp.int32)
    pe = jnp.minimum(jnp.sum(g[:, None] >= page_end[None, :], axis=1), N_EXPERTS - 1).astype(jnp.int32)
    used = g < page_end[-1]
    owner = (g[None, :] >= page_start[:, None]) & (g[None, :] < page_end[:, None])
    of_page = lambda a: jnp.sum(jnp.where(owner, a[:, None], 0), axis=0)
    of_page2 = lambda a: jnp.sum(jnp.where(owner[:, None, :], a.T[:, :, None], 0), axis=0)
    pj = g - of_page(page_start)
    tot_p = of_page(tot)
    pv = jnp.where(used, jnp.clip(tot_p - pj * TM, 0, TM), 0).astype(jnp.int32)

    q = (pj * TM)[:, None] + SUBLANES * jnp.arange(TM // SUBLANES, dtype=jnp.int32)[None, :]
    tiles = jnp.arange(nt, dtype=jnp.int32)
    offset = of_page2(tiles[:, None] * TR + so - eo)
    step = jnp.concatenate([offset[1:] - offset[:-1], jnp.zeros_like(offset[:1])], axis=0)
    run_end = of_page2(eo + run)
    src = q + offset[0][:, None] + jnp.sum(jnp.where(run_end[:, :, None] <= q[None], step[:, :, None], 0), axis=0)
    zero_xt = TR - SUBLANES
    esrc = jnp.where((q < tot_p[:, None]) & used[:, None], src, zero_xt).astype(jnp.int32)

    r = SUBLANES * jnp.arange(TR // SUBLANES, dtype=jnp.int32)
    end = (so + run).T
    offset_f = ((page_start * TM)[None, :] + eo - so).T
    step_f = jnp.concatenate([offset_f[1:] - offset_f[:-1], jnp.zeros_like(offset_f[:1])], axis=0)
    srcf = (r[None, :] + offset_f[0][:, None]
            + jnp.sum(jnp.where(end[:, :, None] <= r[None, None, :], step_f[:, :, None], 0), axis=0))
    zero_ys = (n_pages - 1) * TM
    fsrc = jnp.where(r[None, :] < end[-1][:, None], srcf, zero_ys).astype(jnp.int32)
    return pe, pv, esrc.reshape(-1) // SUBLANES, fsrc.reshape(-1) // SUBLANES


def kernel(x, c, positions, w_ada, b_ada, w_in, ret_gn_w, w_pool, pool_scale, w_out, ln1_w, ln1_b, w_group, b_group,
           w_router, b_router, w1, w3, w2, ln2_w, ln2_b):
    B, S, D = x.shape
    T = B * S
    assert w_ada.shape[0] == DEPTH and D == D_MODEL and S % TS == 0

    inv_freq = ROPE_BASE ** (-jnp.arange(0, HEAD_DIM, 2, dtype=jnp.float32) / HEAD_DIM)
    half = HEAD_DIM // 2
    rope = jnp.stack([jnp.concatenate([inv_freq, inv_freq]),
                      jnp.concatenate([-jnp.ones((half,), jnp.float32), jnp.ones((half,), jnp.float32)])])

    nt = T // TS
    n_pages = (2 * T + nt * N_EXPERTS * (SUBLANES - 1)) // TM + N_EXPERTS + 1

    for l in range(DEPTH):
        ada = _ada(c, w_ada[l], b_ada[l]).reshape(B, 6, D)
        pad = SUBLANES - N_GROUPS
        wrt = jnp.concatenate([w_group[l].T, jnp.zeros((pad, D), jnp.float32), w_router[l].T], axis=0)
        brt = jnp.concatenate([b_group[l], jnp.zeros((pad,), jnp.float32), b_router[l]]).reshape(ROUTE_ROWS, 1)
        x1, xt, route, counts = _mix(
            x, positions, ada, rope, _bf(w_in[l]), _bf(w_out[l]), _bf(w_pool[l]),
            ret_gn_w[l].reshape(1, RET_WIDTH), pool_scale[l].reshape(1, POOL_WIDTH),
            ln1_w[l].reshape(1, D), ln1_b[l].reshape(1, D), wrt, brt)

        cnt = counts[:, :, :, 0].reshape(nt, N_EXPERTS).astype(jnp.int32)
        pe, pv, esrc, fsrc = _gather_tables(cnt, n_pages)
        ys = _experts(pe, pv, esrc, xt.reshape(nt * TR // SUBLANES, 2 * SUBLANES, HALF),
                      w1[l].reshape(N_EXPERTS, D, D_EXPERT), w3[l].reshape(N_EXPERTS, D, D_EXPERT),
                      w2[l].reshape(N_EXPERTS, D_EXPERT, D))
        x = _final(fsrc, x1.reshape(T, D), route.reshape(T, LANES), ada,
                   ln2_w[l].reshape(1, D), ln2_b[l].reshape(1, D),
                   ys.reshape(n_pages * TM // SUBLANES, 2 * SUBLANES, HALF), S).reshape(B, S, D)
    return x
```

```python
import math

import jax
import jax.numpy as jnp
from jax import lax
from jax.experimental import pallas as pl
from jax.experimental.pallas import tpu as pltpu

D_MODEL = 1024
RET_WIDTH = 512
RET_HEADS = 4
HEAD_DIM = 128
POOL_WIDTH = 512
POOL_WINDOWS = (2, 4, 8, 16)
POOL_GROUP_DIM = 128
IN_COLS = 4 * RET_WIDTH + POOL_WIDTH
N_GROUPS = 4
EXPERTS_PER_GROUP = 8
N_EXPERTS = 32
D_EXPERT = 256
DEPTH = 1
ALPHA = (2.0 * DEPTH) ** 0.25
LN_EPS = 1e-5
ROPE_BASE = 10000.0

LANES = 128
SUBLANES = 8
HALF = D_MODEL // 2
POOL_HALO = 16
TS = 256
TM = 512
PAGE_SPLIT_ROWS = 256
TR = -(-(2 * TS + N_EXPERTS * (SUBLANES - 1)) // LANES) * LANES
SEQS_PER_STEP = 2
TILES_PER_STEP = 2
ROUTE_ROWS = SUBLANES + N_EXPERTS
FETCH_AHEAD = 2
FETCH_SLOTS = FETCH_AHEAD + 1
VMEM_LIMIT = 56 * 1024 * 1024

_LOG_GAMMA = tuple(math.log1p(-(2.0 ** (-5.0 - h))) for h in range(RET_HEADS))
_HI = lax.Precision.HIGHEST


def _ln(x):
    mu = jnp.mean(x, axis=-1, keepdims=True)
    xc = x - mu
    var = jnp.mean(xc * xc, axis=-1, keepdims=True)
    return xc * lax.rsqrt(var + LN_EPS)


def _bf(x):
    return x.astype(jnp.bfloat16)


def _pack_rows(x):
    n = x.shape[0]
    lo = x[:, :HALF].reshape(n // SUBLANES, SUBLANES, HALF)
    hi = x[:, HALF:].reshape(n // SUBLANES, SUBLANES, HALF)
    return _bf(jnp.concatenate([lo, hi], axis=1).reshape(2 * n, HALF))


def _unpack_rows(z):
    n = z.shape[0] // 2
    zf = z.astype(jnp.float32).reshape(n // SUBLANES, 2 * SUBLANES, HALF)
    return _bf(zf[:, :SUBLANES, :].reshape(n, HALF)), _bf(zf[:, SUBLANES:, :].reshape(n, HALF))


def _ada_kernel(c_ref, w_ref, b_ref, o_ref):
    c = c_ref[...]
    ca = c * jax.nn.sigmoid(c)
    o_ref[...] = jnp.dot(ca, w_ref[...], precision=_HI, preferred_element_type=jnp.float32) + b_ref[...]


def _ada(c, w_ada, b_ada):
    B, D = c.shape
    n = w_ada.shape[1]
    return pl.pallas_call(
        _ada_kernel,
        grid=(n // D,),
        in_specs=[pl.BlockSpec((B, D), lambda j: (0, 0)),
                  pl.BlockSpec((D, D), lambda j: (0, j)),
                  pl.BlockSpec((1, D), lambda j: (0, j))],
        out_specs=pl.BlockSpec((B, D), lambda j: (0, j)),
        out_shape=jax.ShapeDtypeStruct((B, n), jnp.float32),
        compiler_params=pltpu.CompilerParams(dimension_semantics=("arbitrary",), vmem_limit_bytes=VMEM_LIMIT),
        name="ada",
    )(c, w_ada, b_ada.reshape(1, n))


def _mix_kernel(x_ref, pos_ref, ada_ref, rope_ref, win_ref, wout_ref, wpool_ref, gnw_ref, pscale_ref,
                ln1w_ref, ln1b_ref, wrt_ref, brt_ref,
                x1_ref, xt_ref, route_ref, cnt_ref,
                state_ref, halo_ref, dmat_ref, qdec_ref, kdec_ref, cat_ref):
    b = pl.program_id(0)
    s = pl.program_id(1)

    @pl.when((b == 0) & (s == 0))
    def _init_tables():
        ri = lax.broadcasted_iota(jnp.int32, (TS, TS), 0)
        ci = lax.broadcasted_iota(jnp.int32, (TS, TS), 1)
        rel = (ri - ci).astype(jnp.float32)
        for h in range(RET_HEADS):
            dmat_ref[h] = jnp.where(rel >= 0.0, jnp.exp(jnp.maximum(rel, 0.0) * _LOG_GAMMA[h]), 0.0)
        row = lax.broadcasted_iota(jnp.int32, (TS, RET_WIDTH), 0).astype(jnp.float32)
        lane = lax.broadcasted_iota(jnp.int32, (TS, RET_WIDTH), 1)
        lg = jnp.full((TS, RET_WIDTH), _LOG_GAMMA[0], jnp.float32)
        for h in range(1, RET_HEADS):
            lg = jnp.where(lane >= h * HEAD_DIM, _LOG_GAMMA[h], lg)
        qdec_ref[...] = jnp.exp((row + 1.0) * lg)
        kdec_ref[...] = jnp.exp((TS - 1.0 - row) * lg)

    @pl.when(s == 0)
    def _init_carries():
        state_ref[...] = jnp.zeros_like(state_ref)
        halo_ref[...] = jnp.zeros_like(halo_ref)

    tiles = [_mix_tile(s, x_ref.at[j], pos_ref.at[j], ada_ref.at[j], rope_ref, win_ref, wout_ref, wpool_ref, gnw_ref,
                       pscale_ref, ln1w_ref, ln1b_ref, wrt_ref, brt_ref,
                       x1_ref.at[j], xt_ref.at[j], route_ref.at[j], cnt_ref.at[j],
                       state_ref.at[j], halo_ref.at[j], dmat_ref, qdec_ref, kdec_ref, cat_ref.at[j])
             for j in range(SEQS_PER_STEP)]
    _round_robin(tiles)


def _mix_tile(s, x_ref, pos_ref, ada_ref, rope_ref, win_ref, wout_ref, wpool_ref, gnw_ref, pscale_ref,
              ln1w_ref, ln1b_ref, wrt_ref, brt_ref,
              x1_ref, xt_ref, route_ref, cnt_ref,
              state_ref, halo_ref, dmat_ref, qdec_ref, kdec_ref, cat_ref):
    ada = ada_ref[...]
    shift1, scale1, gate1 = ada[0:1], ada[1:2], ada[2:3]
    shift2, scale2 = ada[3:4], ada[4:5]

    x = x_ref[...]
    u = _bf(_ln(x) * (1.0 + scale1) + shift1)
    yield

    posf = pos_ref[...].astype(jnp.float32)
    hl = lax.broadcasted_iota(jnp.int32, (TS // 2, HEAD_DIM), 1) < HEAD_DIM // 2
    ang = jnp.where(hl, posf[:TS // 2], posf[TS // 2:]) * rope_ref[0:1, :]
    cos_p, sin_p = jnp.cos(ang), jnp.sin(ang)
    cos_s, sin_s = pltpu.roll(cos_p, HEAD_DIM // 2, 1), pltpu.roll(sin_p, HEAD_DIM // 2, 1)
    cos_t = jnp.concatenate([jnp.where(hl, cos_p, cos_s), jnp.where(hl, cos_s, cos_p)], axis=0)
    sin_t = jnp.concatenate([jnp.where(hl, sin_p, sin_s), jnp.where(hl, sin_s, sin_p)], axis=0) * rope_ref[1:2, :]
    yield

    q = jnp.dot(u, win_ref[:, 0:RET_WIDTH], preferred_element_type=jnp.float32)
    k = jnp.dot(u, win_ref[:, RET_WIDTH:2 * RET_WIDTH], preferred_element_type=jnp.float32)
    v = jnp.dot(u, win_ref[:, 2 * RET_WIDTH:3 * RET_WIDTH], preferred_element_type=jnp.float32)
    g = jnp.dot(u, win_ref[:, 3 * RET_WIDTH:4 * RET_WIDTH], preferred_element_type=jnp.float32)
    p = jnp.dot(u, win_ref[:, 4 * RET_WIDTH:IN_COLS], preferred_element_type=jnp.float32)
    yield

    gnw = gnw_ref[...]
    for h in range(RET_HEADS):
        sl = slice(h * HEAD_DIM, (h + 1) * HEAD_DIM)
        qh, kh, vh = q[:, sl], k[:, sl], v[:, sl]
        qr = qh * cos_t + pltpu.roll(qh, HEAD_DIM // 2, 1) * sin_t
        kr = (kh * cos_t + pltpu.roll(kh, HEAD_DIM // 2, 1) * sin_t) * (HEAD_DIM ** -0.5)
        vb = _bf(vh)
        sc = lax.dot_general(_bf(qr), _bf(kr), (((1,), (1,)), ((), ())), preferred_element_type=jnp.float32)
        intra = jnp.dot(_bf(sc * dmat_ref[h]), vb, preferred_element_type=jnp.float32)
        st = state_ref[h]
        cross = jnp.dot(_bf(qr * qdec_ref[:, sl]), _bf(st), preferred_element_type=jnp.float32)
        kv = lax.dot_general(_bf(kr * kdec_ref[:, sl]), vb, (((0,), (0,)), ((), ())),
                             preferred_element_type=jnp.float32)
        state_ref[h] = st * math.exp(TS * _LOG_GAMMA[h]) + kv
        r = _ln(intra + cross) * gnw[:, sl]
        gh = g[:, sl]
        cat_ref[:, sl] = _bf(gh * jax.nn.sigmoid(gh) * r)
        yield

    pext = jnp.concatenate([halo_ref[...], p], axis=0)
    halo_ref[...] = p[TS - POOL_HALO:, :]
    t_abs = (s * TS + lax.broadcasted_iota(jnp.int32, (TS, 1), 0) + 1).astype(jnp.float32)
    pscale = pscale_ref[...]
    for grp, w in enumerate(POOL_WINDOWS):
        sl = slice(grp * POOL_GROUP_DIM, (grp + 1) * POOL_GROUP_DIM)
        acc = pext[:, sl]
        shift = 1
        while shift < w:
            acc = acc + pltpu.roll(acc, shift, 0)
            shift *= 2
        pooled = acc[POOL_HALO:, :] / jnp.minimum(t_abs, float(w)) - p[:, sl]
        po = jnp.dot(_bf(pooled), wpool_ref[grp], preferred_element_type=jnp.float32) * pscale[:, sl]
        cat_ref[:, RET_WIDTH + grp * POOL_GROUP_DIM:RET_WIDTH + (grp + 1) * POOL_GROUP_DIM] = _bf(po)
    yield

    mix = jnp.dot(cat_ref[...], wout_ref[...], preferred_element_type=jnp.float32)
    yield
    x1 = _ln(ALPHA * x + gate1 * mix) * ln1w_ref[...] + ln1b_ref[...]
    x1_ref[...] = x1
    u2 = _ln(x1) * (1.0 + scale2) + shift2
    yield

    w = wrt_ref[...]
    w_hi = _bf(w)
    w_lo = _bf(w - w_hi.astype(jnp.float32))
    u2_hi = _bf(u2)
    u2_lo = _bf(u2 - u2_hi.astype(jnp.float32))
    nt = (((1,), (1,)), ((), ()))
    logits = (lax.dot_general(w_hi, u2_hi, nt, preferred_element_type=jnp.float32)
              + lax.dot_general(w_hi, u2_lo, nt, preferred_element_type=jnp.float32)
              + lax.dot_general(w_lo, u2_hi, nt, preferred_element_type=jnp.float32)) + brt_ref[...]
    row8 = lax.broadcasted_iota(jnp.int32, (SUBLANES, TS), 0)
    neg = jnp.float32(-jnp.inf)
    gl = jnp.where(row8 < N_GROUPS, logits[0:SUBLANES], neg)
    gmax = jnp.max(gl, axis=0, keepdims=True)
    gidx = jnp.min(jnp.where(gl == gmax, row8, SUBLANES), axis=0, keepdims=True)
    gprob = 1.0 / jnp.sum(jnp.exp(gl - gmax), axis=0, keepdims=True)
    el = logits[SUBLANES:2 * SUBLANES]
    for grp in range(1, N_GROUPS):
        el = jnp.where(gidx == grp, logits[(grp + 1) * SUBLANES:(grp + 2) * SUBLANES], el)
    m1 = jnp.max(el, axis=0, keepdims=True)
    j1 = jnp.min(jnp.where(el == m1, row8, SUBLANES), axis=0, keepdims=True)
    el2 = jnp.where(row8 == j1, neg, el)
    m2 = jnp.max(el2, axis=0, keepdims=True)
    j2 = jnp.min(jnp.where(el2 == m2, row8, SUBLANES), axis=0, keepdims=True)
    e21 = jnp.exp(m2 - m1)
    den = 1.0 / (1.0 + e21)
    cw1 = gprob * den
    cw2 = gprob * e21 * den
    yield

    erow = lax.broadcasted_iota(jnp.int32, (N_EXPERTS, TS), 0)
    oh1 = erow == gidx * EXPERTS_PER_GROUP + j1
    oh2 = erow == gidx * EXPERTS_PER_GROUP + j2
    oh = jnp.where(oh1 | oh2, 1.0, 0.0)
    cnt = jnp.broadcast_to(jnp.sum(oh, axis=1, keepdims=True), (N_EXPERTS, LANES))
    run = jnp.floor((cnt + (SUBLANES - 1.0)) * (1.0 / SUBLANES)) * SUBLANES
    erow_l = lax.broadcasted_iota(jnp.int32, (N_EXPERTS, LANES), 0)
    run_end = run
    shift = 1
    while shift < N_EXPERTS:
        run_end = run_end + jnp.where(erow_l >= shift, pltpu.roll(run_end, shift, 0), 0.0)
        shift *= 2
    run_start = (run_end - run)[:, 0:1]
    ri = lax.broadcasted_iota(jnp.int32, (TS, TS), 0)
    ci = lax.broadcasted_iota(jnp.int32, (TS, TS), 1)
    earlier = _bf(jnp.where(ri < ci, 1.0, 0.0))
    before = jnp.dot(_bf(oh), earlier, preferred_element_type=jnp.float32) + run_start
    pos1 = jnp.sum(jnp.where(oh1, before, 0.0), axis=0, keepdims=True)
    pos2 = jnp.sum(jnp.where(oh2, before, 0.0), axis=0, keepdims=True)
    rr = lax.broadcasted_iota(jnp.int32, (TR, TS), 0).astype(jnp.float32)
    perm = _bf(jnp.where((rr == pos1) | (rr == pos2), 1.0, 0.0))
    yield
    xt_ref[...] = _pack_rows(jnp.dot(perm, u2_hi, preferred_element_type=jnp.float32))
    cnt_ref[...] = cnt

    rowl = lax.broadcasted_iota(jnp.int32, (LANES, TS), 0)
    rec = jnp.where(rowl == 0, pos1, 0.0)
    rec = jnp.where(rowl == 1, pos2, rec)
    rec = jnp.where(rowl == 2, cw1, rec)
    rec = jnp.where(rowl == 3, cw2, rec)
    route_ref[...] = rec.T


def _mix(x, positions, ada, rope, win, wout, wpool, gnw, pscale, ln1w, ln1b, wrt, brt):
    B, S, D = x.shape
    ns = S // TS
    assert B % SEQS_PER_STEP == 0
    P = SEQS_PER_STEP
    const2 = lambda b, s: (0, 0)
    const3 = lambda b, s: (0, 0, 0)
    tile = lambda b, s: (b, s, 0)
    flat = lambda b, s: (b, s, 0, 0)
    return pl.pallas_call(
        _mix_kernel,
        grid=(B // P, ns),
        in_specs=[
            pl.BlockSpec((P, TS, D), tile),
            pl.BlockSpec((P, TS, 1), tile),
            pl.BlockSpec((P, 6, D), lambda b, s: (b, 0, 0)),
            pl.BlockSpec((2, LANES), const2),
            pl.BlockSpec((D, IN_COLS), const2),
            pl.BlockSpec((D, D), const2),
            pl.BlockSpec((len(POOL_WINDOWS), POOL_GROUP_DIM, POOL_GROUP_DIM), const3),
            pl.BlockSpec((1, RET_WIDTH), const2),
            pl.BlockSpec((1, POOL_WIDTH), const2),
            pl.BlockSpec((1, D), const2),
            pl.BlockSpec((1, D), const2),
            pl.BlockSpec((ROUTE_ROWS, D), const2),
            pl.BlockSpec((ROUTE_ROWS, 1), const2),
        ],
        out_specs=[
            pl.BlockSpec((P, TS, D), tile),
            pl.BlockSpec((P, None, 2 * TR, HALF), flat),
            pl.BlockSpec((P, TS, LANES), tile),
            pl.BlockSpec((P, None, N_EXPERTS, LANES), flat),
        ],
        out_shape=[
            jax.ShapeDtypeStruct((B, S, D), jnp.float32),
            jax.ShapeDtypeStruct((B, ns, 2 * TR, HALF), jnp.bfloat16),
            jax.ShapeDtypeStruct((B, S, LANES), jnp.float32),
            jax.ShapeDtypeStruct((B, ns, N_EXPERTS, LANES), jnp.float32),
        ],
        scratch_shapes=[
            pltpu.VMEM((P, RET_HEADS, HEAD_DIM, HEAD_DIM), jnp.float32),
            pltpu.VMEM((P, POOL_HALO, POOL_WIDTH), jnp.float32),
            pltpu.VMEM((RET_HEADS, TS, TS), jnp.float32),
            pltpu.VMEM((TS, RET_WIDTH), jnp.float32),
            pltpu.VMEM((TS, RET_WIDTH), jnp.float32),
            pltpu.VMEM((P, TS, D), jnp.bfloat16),
        ],
        compiler_params=pltpu.CompilerParams(dimension_semantics=("arbitrary", "arbitrary"),
                                             vmem_limit_bytes=VMEM_LIMIT),
        name="mix",
    )(x, positions.reshape(B, S, 1), ada, rope, win, wout, wpool, gnw, pscale, ln1w, ln1b, wrt, brt)


def _fetch_groups(src_ref, group_of, dst_ref, sem):
    for u in range(dst_ref.shape[0]):
        pltpu.make_async_copy(src_ref.at[group_of(u)], dst_ref.at[u], sem).start()


def _wait_fetch(src_ref, dst_ref, sem):
    pltpu.make_async_copy(src_ref.at[pl.ds(0, dst_ref.shape[0])], dst_ref, sem).wait()


def _expert_rows(x_ref, weight, y_ref):
    n = PAGE_SPLIT_ROWS
    xl, xh = _unpack_rows(x_ref[...].reshape(2 * n, HALF))
    yield
    w1, w3 = weight(0), weight(1)
    a = (jnp.dot(xl, w1[:HALF, :], preferred_element_type=jnp.float32)
         + jnp.dot(xh, w1[HALF:, :], preferred_element_type=jnp.float32))
    c = (jnp.dot(xl, w3[:HALF, :], preferred_element_type=jnp.float32)
         + jnp.dot(xh, w3[HALF:, :], preferred_element_type=jnp.float32))
    yield
    h = _bf(a * jax.nn.sigmoid(a) * c)
    yield
    y = jnp.dot(h, weight(2), preferred_element_type=jnp.float32)
    yield
    y_ref[...] = _pack_rows(y)


def _round_robin(tasks):
    while tasks:
        tasks = [t for t in tasks if next(t, True) is None]


def _expert_kernel(pe_ref, pv_ref, first_ref, wslot_ref, next_ref, src_ref, xt_ref, w1_hbm, w3_hbm, w2_hbm,
                   ys_ref, xbuf, w1_buf, w3_buf, w2_buf, sem, wsem):
    g = pl.program_id(0)
    last = pl.num_programs(0) - 1
    groups = TM // SUBLANES

    def fetch(page, dst_slot):
        @pl.when(pv_ref[page] > 0)
        def _start():
            _fetch_groups(xt_ref, lambda u: src_ref[page * groups + u], xbuf.at[dst_slot], sem.at[dst_slot])

    def weight_copies(expert, ws):
        return [pltpu.make_async_copy(hbm.at[expert], buf.at[ws], wsem.at[ws])
                for hbm, buf in ((w1_hbm, w1_buf), (w3_hbm, w3_buf), (w2_hbm, w2_buf))]

    @pl.when(g == 0)
    def _prime():
        for k in range(FETCH_AHEAD):
            fetch(k, k)

        @pl.when(pv_ref[0] > 0)
        def _first_weights():
            for c in weight_copies(pe_ref[0], wslot_ref[0]):
                c.start()

    fetch(jnp.minimum(g + FETCH_AHEAD, last), (g + FETCH_AHEAD) % FETCH_SLOTS)
    slot = g % FETCH_SLOTS
    used = pv_ref[g] > 0

    @pl.when(used)
    def _compute():
        ws = wslot_ref[g]

        @pl.when(first_ref[g] == 1)
        def _new_expert():
            for c in weight_copies(pe_ref[g], ws):
                c.wait()

            @pl.when(next_ref[g] >= 0)
            def _stream_next():
                for c in weight_copies(next_ref[g], 1 - ws):
                    c.start()

        _wait_fetch(xt_ref, xbuf.at[slot], sem.at[slot])
        w_bufs = (w1_buf, w3_buf, w2_buf)
        cast = {}

        def weight(i):
            if i not in cast:
                cast[i] = _bf(w_bufs[i][ws])
            return cast[i]

        per = PAGE_SPLIT_ROWS // SUBLANES
        _round_robin([_expert_rows(xbuf.at[slot, pl.ds(k * per, per)], weight,
                                   ys_ref.at[pl.ds(k * 2 * PAGE_SPLIT_ROWS, 2 * PAGE_SPLIT_ROWS)])
                      for k in range(TM // PAGE_SPLIT_ROWS)])

    @pl.when(jnp.logical_not(used))
    def _unused():
        ys_ref[...] = jnp.zeros_like(ys_ref)


def _experts(page_tables, esrc, xt, w1, w3, w2):
    D = D_MODEL
    n_pages = page_tables[0].shape[0]
    assert n_pages > FETCH_AHEAD and TM % PAGE_SPLIT_ROWS == 0
    grid_spec = pltpu.PrefetchScalarGridSpec(
        num_scalar_prefetch=len(page_tables) + 1,
        grid=(n_pages,),
        in_specs=[pl.BlockSpec(memory_space=pl.ANY)] * 4,
        out_specs=pl.BlockSpec((2 * TM, HALF), lambda g, *_: (g, 0)),
        scratch_shapes=[pltpu.VMEM((FETCH_SLOTS, TM // SUBLANES, 2 * SUBLANES, HALF), jnp.bfloat16),
                        pltpu.VMEM((2, D, D_EXPERT), jnp.float32),
                        pltpu.VMEM((2, D, D_EXPERT), jnp.float32),
                        pltpu.VMEM((2, D_EXPERT, D), jnp.float32),
                        pltpu.SemaphoreType.DMA((FETCH_SLOTS,)),
                        pltpu.SemaphoreType.DMA((2,))],
    )
    return pl.pallas_call(
        _expert_kernel,
        grid_spec=grid_spec,
        out_shape=jax.ShapeDtypeStruct((n_pages * 2 * TM, HALF), jnp.bfloat16),
        compiler_params=pltpu.CompilerParams(dimension_semantics=("arbitrary",), vmem_limit_bytes=VMEM_LIMIT),
        name="experts",
    )(*page_tables, esrc, xt, w1, w3, w2)


def _final_tile(y_ref, x1_ref, route_ref, ada_ref, lnw_ref, lnb_ref, o_ref):
    route = route_ref[...]
    pos1, pos2, cw1, cw2 = route[:, 0:1], route[:, 1:2], route[:, 2:3], route[:, 3:4]
    col = lax.broadcasted_iota(jnp.int32, (TS, TR), 1).astype(jnp.float32)
    wmat = _bf(jnp.where(col == pos1, cw1, 0.0) + jnp.where(col == pos2, cw2, 0.0))
    yield
    yl, yh = _unpack_rows(y_ref[...].reshape(2 * TR, HALF))
    yield
    y = jnp.concatenate([jnp.dot(wmat, yl, preferred_element_type=jnp.float32),
                         jnp.dot(wmat, yh, preferred_element_type=jnp.float32)], axis=-1)
    yield
    gate2 = ada_ref[5:6, :]
    o_ref[...] = _ln(ALPHA * x1_ref[...] + gate2 * y) * lnw_ref[...] + lnb_ref[...]


def _final_kernel(src_ref, x1_ref, route_ref, ada_ref, lnw_ref, lnb_ref, ys_ref, o_ref, ybuf, sem):
    P = TILES_PER_STEP
    i = pl.program_id(0)
    last = pl.num_programs(0) - 1
    per = TR // SUBLANES
    groups = P * per

    def fetch(step, dst_slot):
        _fetch_groups(ys_ref, lambda u: src_ref[step * groups + u], ybuf.at[dst_slot], sem.at[dst_slot])

    @pl.when(i == 0)
    def _prime():
        for k in range(FETCH_AHEAD):
            fetch(k, k)

    fetch(jnp.minimum(i + FETCH_AHEAD, last), (i + FETCH_AHEAD) % FETCH_SLOTS)
    slot = i % FETCH_SLOTS
    _wait_fetch(ys_ref, ybuf.at[slot], sem.at[slot])

    rows = lambda k: pl.ds(k * TS, TS)
    _round_robin([_final_tile(ybuf.at[slot, pl.ds(k * per, per)], x1_ref.at[rows(k)], route_ref.at[rows(k)], ada_ref,
                              lnw_ref, lnb_ref, o_ref.at[rows(k)]) for k in range(P)])

    @pl.when(i == last)
    def _drain():
        for k in range(1, FETCH_SLOTS):
            other = (i + k) % FETCH_SLOTS
            _wait_fetch(ys_ref, ybuf.at[other], sem.at[other])


def _final(fsrc, x1, route, ada, lnw, lnb, ys, seq_len):
    T, D = x1.shape
    P = TILES_PER_STEP
    rows = P * TS
    assert seq_len % rows == 0
    steps_per_seq = seq_len // rows
    n_steps = T // rows
    assert n_steps > FETCH_AHEAD
    grid_spec = pltpu.PrefetchScalarGridSpec(
        num_scalar_prefetch=1,
        grid=(n_steps,),
        in_specs=[pl.BlockSpec((rows, D), lambda i, src: (i, 0)),
                  pl.BlockSpec((rows, LANES), lambda i, src: (i, 0)),
                  pl.BlockSpec((None, 6, D), lambda i, src: (i // steps_per_seq, 0, 0)),
                  pl.BlockSpec((1, D), lambda i, src: (0, 0)),
                  pl.BlockSpec((1, D), lambda i, src: (0, 0)),
                  pl.BlockSpec(memory_space=pl.ANY)],
        out_specs=pl.BlockSpec((rows, D), lambda i, src: (i, 0)),
        scratch_shapes=[pltpu.VMEM((FETCH_SLOTS, P * (TR // SUBLANES), 2 * SUBLANES, HALF), jnp.bfloat16),
                        pltpu.SemaphoreType.DMA((FETCH_SLOTS,))],
    )
    return pl.pallas_call(
        _final_kernel,
        grid_spec=grid_spec,
        out_shape=jax.ShapeDtypeStruct((T, D), jnp.float32),
        compiler_params=pltpu.CompilerParams(dimension_semantics=("arbitrary",), vmem_limit_bytes=VMEM_LIMIT),
        name="final",
    )(fsrc, x1, route, ada, lnw, lnb, ys)


def _gather_tables(cnt, n_pages):
    nt = cnt.shape[0]
    run = (cnt + SUBLANES - 1) // SUBLANES * SUBLANES
    so = jnp.cumsum(run, axis=1) - run
    eo = jnp.cumsum(run, axis=0) - run
    tot = jnp.sum(run, axis=0)
    pages_e = (tot + TM - 1) // TM
    page_end = jnp.cumsum(pages_e)
    page_start = page_end - pages_e
    g = jnp.arange(n_pages, dtype=jnp.int32)
    pe = jnp.minimum(jnp.sum(g[:, None] >= page_end[None, :], axis=1), N_EXPERTS - 1).astype(jnp.int32)
    used = g < page_end[-1]
    owner = (g[None, :] >= page_start[:, None]) & (g[None, :] < page_end[:, None])
    of_page = lambda a: jnp.sum(jnp.where(owner, a[:, None], 0), axis=0)
    of_page2 = lambda a: jnp.sum(jnp.where(owner[:, None, :], a.T[:, :, None], 0), axis=0)
    pj = g - of_page(page_start)
    tot_p = of_page(tot)
    pv = jnp.where(used, jnp.clip(tot_p - pj * TM, 0, TM), 0).astype(jnp.int32)
    experts = jnp.arange(N_EXPERTS, dtype=jnp.int32)
    has_pages = pages_e > 0
    wslot_e = (jnp.cumsum(has_pages) - 1) % 2
    later = (experts[None, :] > experts[:, None]) & has_pages[None, :]
    next_e = jnp.min(jnp.where(later, experts[None, :], N_EXPERTS), axis=1)
    next_e = jnp.where(next_e < N_EXPERTS, next_e, -1)
    first = (used & (pj == 0)).astype(jnp.int32)
    page_tables = (pe, pv, first, of_page(wslot_e).astype(jnp.int32), jnp.where(used, of_page(next_e), -1).astype(jnp.int32))

    q = (pj * TM)[:, None] + SUBLANES * jnp.arange(TM // SUBLANES, dtype=jnp.int32)[None, :]
    tiles = jnp.arange(nt, dtype=jnp.int32)
    offset = of_page2(tiles[:, None] * TR + so - eo)
    step = jnp.concatenate([offset[1:] - offset[:-1], jnp.zeros_like(offset[:1])], axis=0)
    run_end = of_page2(eo + run)
    src = q + offset[0][:, None] + jnp.sum(jnp.where(run_end[:, :, None] <= q[None], step[:, :, None], 0), axis=0)
    zero_xt = TR - SUBLANES
    esrc = jnp.where((q < tot_p[:, None]) & used[:, None], src, zero_xt).astype(jnp.int32)

    r = SUBLANES * jnp.arange(TR // SUBLANES, dtype=jnp.int32)
    end = (so + run).T
    offset_f = ((page_start * TM)[None, :] + eo - so).T
    step_f = jnp.concatenate([offset_f[1:] - offset_f[:-1], jnp.zeros_like(offset_f[:1])], axis=0)
    srcf = (r[None, :] + offset_f[0][:, None]
            + jnp.sum(jnp.where(end[:, :, None] <= r[None, None, :], step_f[:, :, None], 0), axis=0))
    zero_ys = (n_pages - 1) * TM
    fsrc = jnp.where(r[None, :] < end[-1][:, None], srcf, zero_ys).astype(jnp.int32)
    return page_tables, esrc.reshape(-1) // SUBLANES, fsrc.reshape(-1) // SUBLANES


def kernel(x, c, positions, w_ada, b_ada, w_in, ret_gn_w, w_pool, pool_scale, w_out, ln1_w, ln1_b, w_group, b_group,
           w_router, b_router, w1, w3, w2, ln2_w, ln2_b):
    B, S, D = x.shape
    T = B * S
    assert w_ada.shape[0] == DEPTH and D == D_MODEL and S % TS == 0

    inv_freq = ROPE_BASE ** (-jnp.arange(0, HEAD_DIM, 2, dtype=jnp.float32) / HEAD_DIM)
    half = HEAD_DIM // 2
    rope = jnp.stack([jnp.concatenate([inv_freq, inv_freq]),
                      jnp.concatenate([-jnp.ones((half,), jnp.float32), jnp.ones((half,), jnp.float32)])])

    nt = T // TS
    n_pages = (2 * T + nt * N_EXPERTS * (SUBLANES - 1)) // TM + N_EXPERTS + 1

    for l in range(DEPTH):
        ada = _ada(c, w_ada[l], b_ada[l]).reshape(B, 6, D)
        pad = SUBLANES - N_GROUPS
        wrt = jnp.concatenate([w_group[l].T, jnp.zeros((pad, D), jnp.float32), w_router[l].T], axis=0)
        brt = jnp.concatenate([b_group[l], jnp.zeros((pad,), jnp.float32), b_router[l]]).reshape(ROUTE_ROWS, 1)
        x1, xt, route, counts = _mix(
            x, positions, ada, rope, _bf(w_in[l]), _bf(w_out[l]), _bf(w_pool[l]),
            ret_gn_w[l].reshape(1, RET_WIDTH), pool_scale[l].reshape(1, POOL_WIDTH),
            ln1_w[l].reshape(1, D), ln1_b[l].reshape(1, D), wrt, brt)

        cnt = counts[:, :, :, 0].reshape(nt, N_EXPERTS).astype(jnp.int32)
        page_tables, esrc, fsrc = _gather_tables(cnt, n_pages)
        ys = _experts(page_tables, esrc, xt.reshape(nt * TR // SUBLANES, 2 * SUBLANES, HALF),
                      w1[l].reshape(N_EXPERTS, D, D_EXPERT), w3[l].reshape(N_EXPERTS, D, D_EXPERT),
                      w2[l].reshape(N_EXPERTS, D_EXPERT, D))
        x = _final(fsrc, x1.reshape(T, D), route.reshape(T, LANES), ada,
                   ln2_w[l].reshape(1, D), ln2_b[l].reshape(1, D),
                   ys.reshape(n_pages * TM // SUBLANES, 2 * SUBLANES, HALF), S).reshape(B, S, D)
    return x
```

```python
import math

import jax
import jax.numpy as jnp
from jax import lax
from jax.experimental import pallas as pl
from jax.experimental.pallas import tpu as pltpu

D_MODEL = 1024
RET_WIDTH = 512
RET_HEADS = 4
HEAD_DIM = 128
POOL_WIDTH = 512
POOL_WINDOWS = (2, 4, 8, 16)
POOL_GROUP_DIM = 128
IN_COLS = 4 * RET_WIDTH + POOL_WIDTH
N_GROUPS = 4
EXPERTS_PER_GROUP = 8
N_EXPERTS = 32
D_EXPERT = 256
DEPTH = 1
ALPHA = (2.0 * DEPTH) ** 0.25
LN_EPS = 1e-5
ROPE_BASE = 10000.0

LANES = 128
SUBLANES = 8
HALF = D_MODEL // 2
POOL_HALO = 16
TS = 256
TM = 512
PAGE_SPLIT_ROWS = 256
TR = -(-(2 * TS + N_EXPERTS * (SUBLANES - 1)) // LANES) * LANES
SEQS_PER_STEP = 2
TILES_PER_STEP = 2
ROUTE_ROWS = SUBLANES + N_EXPERTS
FETCH_AHEAD = 2
FETCH_SLOTS = FETCH_AHEAD + 1
VMEM_LIMIT = 56 * 1024 * 1024

_LOG_GAMMA = tuple(math.log1p(-(2.0 ** (-5.0 - h))) for h in range(RET_HEADS))
_HI = lax.Precision.HIGHEST


def _ln(x):
    mu = jnp.mean(x, axis=-1, keepdims=True)
    xc = x - mu
    var = jnp.mean(xc * xc, axis=-1, keepdims=True)
    return xc * lax.rsqrt(var + LN_EPS)


def _bf(x):
    return x.astype(jnp.bfloat16)


def _pack_rows(x):
    n = x.shape[0]
    lo = x[:, :HALF].reshape(n // SUBLANES, SUBLANES, HALF)
    hi = x[:, HALF:].reshape(n // SUBLANES, SUBLANES, HALF)
    return _bf(jnp.concatenate([lo, hi], axis=1).reshape(2 * n, HALF))


def _unpack_rows(z):
    n = z.shape[0] // 2
    zf = z.astype(jnp.float32).reshape(n // SUBLANES, 2 * SUBLANES, HALF)
    return _bf(zf[:, :SUBLANES, :].reshape(n, HALF)), _bf(zf[:, SUBLANES:, :].reshape(n, HALF))


def _ada_kernel(c_ref, w_ref, b_ref, o_ref):
    c = c_ref[...]
    ca = c * jax.nn.sigmoid(c)
    o_ref[...] = jnp.dot(ca, w_ref[...], precision=_HI, preferred_element_type=jnp.float32) + b_ref[...]


def _ada(c, w_ada, b_ada):
    B, D = c.shape
    n = w_ada.shape[1]
    return pl.pallas_call(
        _ada_kernel,
        grid=(n // D,),
        in_specs=[pl.BlockSpec((B, D), lambda j: (0, 0)),
                  pl.BlockSpec((D, D), lambda j: (0, j)),
                  pl.BlockSpec((1, D), lambda j: (0, j))],
        out_specs=pl.BlockSpec((B, D), lambda j: (0, j)),
        out_shape=jax.ShapeDtypeStruct((B, n), jnp.float32),
        compiler_params=pltpu.CompilerParams(dimension_semantics=("arbitrary",), vmem_limit_bytes=VMEM_LIMIT),
        name="ada",
    )(c, w_ada, b_ada.reshape(1, n))


def _mix_kernel(x_ref, pos_ref, ada_ref, rope_ref, win_ref, wout_ref, wpool_ref, gnw_ref, pscale_ref,
                ln1w_ref, ln1b_ref, wrt_ref, brt_ref,
                x1_ref, xt_ref, route_ref, cnt_ref,
                state_ref, halo_ref, dmat_ref, qdec_ref, kdec_ref, cat_ref):
    b = pl.program_id(0)
    s = pl.program_id(1)

    @pl.when((b == 0) & (s == 0))
    def _init_tables():
        ri = lax.broadcasted_iota(jnp.int32, (TS, TS), 0)
        ci = lax.broadcasted_iota(jnp.int32, (TS, TS), 1)
        rel = (ri - ci).astype(jnp.float32)
        for h in range(RET_HEADS):
            dmat_ref[h] = jnp.where(rel >= 0.0, jnp.exp(jnp.maximum(rel, 0.0) * _LOG_GAMMA[h]), 0.0)
        row = lax.broadcasted_iota(jnp.int32, (TS, RET_WIDTH), 0).astype(jnp.float32)
        lane = lax.broadcasted_iota(jnp.int32, (TS, RET_WIDTH), 1)
        lg = jnp.full((TS, RET_WIDTH), _LOG_GAMMA[0], jnp.float32)
        for h in range(1, RET_HEADS):
            lg = jnp.where(lane >= h * HEAD_DIM, _LOG_GAMMA[h], lg)
        qdec_ref[...] = jnp.exp((row + 1.0) * lg)
        kdec_ref[...] = jnp.exp((TS - 1.0 - row) * lg)

    @pl.when(s == 0)
    def _init_carries():
        state_ref[...] = jnp.zeros_like(state_ref)
        halo_ref[...] = jnp.zeros_like(halo_ref)

    tiles = [_mix_tile(s, x_ref.at[j], pos_ref.at[j], ada_ref.at[j], rope_ref, win_ref, wout_ref, wpool_ref, gnw_ref,
                       pscale_ref, ln1w_ref, ln1b_ref, wrt_ref, brt_ref,
                       x1_ref.at[j], xt_ref.at[j], route_ref.at[j], cnt_ref.at[j],
                       state_ref.at[j], halo_ref.at[j], dmat_ref, qdec_ref, kdec_ref, cat_ref.at[j])
             for j in range(SEQS_PER_STEP)]
    _round_robin(tiles)


def _mix_tile(s, x_ref, pos_ref, ada_ref, rope_ref, win_ref, wout_ref, wpool_ref, gnw_ref, pscale_ref,
              ln1w_ref, ln1b_ref, wrt_ref, brt_ref,
              x1_ref, xt_ref, route_ref, cnt_ref,
              state_ref, halo_ref, dmat_ref, qdec_ref, kdec_ref, cat_ref):
    ada = ada_ref[...]
    shift1, scale1, gate1 = ada[0:1], ada[1:2], ada[2:3]
    shift2, scale2 = ada[3:4], ada[4:5]

    x = x_ref[...]
    u = _bf(_ln(x) * (1.0 + scale1) + shift1)
    yield

    posf = pos_ref[...].astype(jnp.float32)
    hl = lax.broadcasted_iota(jnp.int32, (TS // 2, HEAD_DIM), 1) < HEAD_DIM // 2
    ang = jnp.where(hl, posf[:TS // 2], posf[TS // 2:]) * rope_ref[0:1, :]
    cos_p, sin_p = jnp.cos(ang), jnp.sin(ang)
    cos_s, sin_s = pltpu.roll(cos_p, HEAD_DIM // 2, 1), pltpu.roll(sin_p, HEAD_DIM // 2, 1)
    cos_t = jnp.concatenate([jnp.where(hl, cos_p, cos_s), jnp.where(hl, cos_s, cos_p)], axis=0)
    sin_t = jnp.concatenate([jnp.where(hl, sin_p, sin_s), jnp.where(hl, sin_s, sin_p)], axis=0) * rope_ref[1:2, :]
    yield

    q = jnp.dot(u, win_ref[:, 0:RET_WIDTH], preferred_element_type=jnp.float32)
    k = jnp.dot(u, win_ref[:, RET_WIDTH:2 * RET_WIDTH], preferred_element_type=jnp.float32)
    v = jnp.dot(u, win_ref[:, 2 * RET_WIDTH:3 * RET_WIDTH], preferred_element_type=jnp.float32)
    g = jnp.dot(u, win_ref[:, 3 * RET_WIDTH:4 * RET_WIDTH], preferred_element_type=jnp.float32)
    p = jnp.dot(u, win_ref[:, 4 * RET_WIDTH:IN_COLS], preferred_element_type=jnp.float32)
    yield

    gnw = gnw_ref[...]
    for h in range(RET_HEADS):
        sl = slice(h * HEAD_DIM, (h + 1) * HEAD_DIM)
        qh, kh, vh = q[:, sl], k[:, sl], v[:, sl]
        qr = qh * cos_t + pltpu.roll(qh, HEAD_DIM // 2, 1) * sin_t
        kr = (kh * cos_t + pltpu.roll(kh, HEAD_DIM // 2, 1) * sin_t) * (HEAD_DIM ** -0.5)
        vb = _bf(vh)
        sc = lax.dot_general(_bf(qr), _bf(kr), (((1,), (1,)), ((), ())), preferred_element_type=jnp.float32)
        intra = jnp.dot(_bf(sc * dmat_ref[h]), vb, preferred_element_type=jnp.float32)
        st = state_ref[h]
        cross = jnp.dot(_bf(qr * qdec_ref[:, sl]), _bf(st), preferred_element_type=jnp.float32)
        kv = lax.dot_general(_bf(kr * kdec_ref[:, sl]), vb, (((0,), (0,)), ((), ())),
                             preferred_element_type=jnp.float32)
        state_ref[h] = st * math.exp(TS * _LOG_GAMMA[h]) + kv
        r = _ln(intra + cross) * gnw[:, sl]
        gh = g[:, sl]
        cat_ref[:, sl] = _bf(gh * jax.nn.sigmoid(gh) * r)
        yield

    pext = jnp.concatenate([halo_ref[...], p], axis=0)
    halo_ref[...] = p[TS - POOL_HALO:, :]
    t_abs = (s * TS + lax.broadcasted_iota(jnp.int32, (TS, 1), 0) + 1).astype(jnp.float32)
    pscale = pscale_ref[...]
    for grp, w in enumerate(POOL_WINDOWS):
        sl = slice(grp * POOL_GROUP_DIM, (grp + 1) * POOL_GROUP_DIM)
        acc = pext[:, sl]
        shift = 1
        while shift < w:
            acc = acc + pltpu.roll(acc, shift, 0)
            shift *= 2
        pooled = acc[POOL_HALO:, :] / jnp.minimum(t_abs, float(w)) - p[:, sl]
        po = jnp.dot(_bf(pooled), wpool_ref[grp], preferred_element_type=jnp.float32) * pscale[:, sl]
        cat_ref[:, RET_WIDTH + grp * POOL_GROUP_DIM:RET_WIDTH + (grp + 1) * POOL_GROUP_DIM] = _bf(po)
    yield

    mix = jnp.dot(cat_ref[...], wout_ref[...], preferred_element_type=jnp.float32)
    yield
    x1 = _ln(ALPHA * x + gate1 * mix) * ln1w_ref[...] + ln1b_ref[...]
    x1_ref[...] = x1
    u2 = _ln(x1) * (1.0 + scale2) + shift2
    yield

    w = wrt_ref[...]
    w_hi = _bf(w)
    w_lo = _bf(w - w_hi.astype(jnp.float32))
    u2_hi = _bf(u2)
    u2_lo = _bf(u2 - u2_hi.astype(jnp.float32))
    nt = (((1,), (1,)), ((), ()))
    logits = (lax.dot_general(w_hi, u2_hi, nt, preferred_element_type=jnp.float32)
              + lax.dot_general(w_hi, u2_lo, nt, preferred_element_type=jnp.float32)
              + lax.dot_general(w_lo, u2_hi, nt, preferred_element_type=jnp.float32)) + brt_ref[...]
    row8 = lax.broadcasted_iota(jnp.int32, (SUBLANES, TS), 0)
    neg = jnp.float32(-jnp.inf)
    gl = jnp.where(row8 < N_GROUPS, logits[0:SUBLANES], neg)
    gmax = jnp.max(gl, axis=0, keepdims=True)
    gidx = jnp.min(jnp.where(gl == gmax, row8, SUBLANES), axis=0, keepdims=True)
    gprob = 1.0 / jnp.sum(jnp.exp(gl - gmax), axis=0, keepdims=True)
    el = logits[SUBLANES:2 * SUBLANES]
    for grp in range(1, N_GROUPS):
        el = jnp.where(gidx == grp, logits[(grp + 1) * SUBLANES:(grp + 2) * SUBLANES], el)
    m1 = jnp.max(el, axis=0, keepdims=True)
    j1 = jnp.min(jnp.where(el == m1, row8, SUBLANES), axis=0, keepdims=True)
    el2 = jnp.where(row8 == j1, neg, el)
    m2 = jnp.max(el2, axis=0, keepdims=True)
    j2 = jnp.min(jnp.where(el2 == m2, row8, SUBLANES), axis=0, keepdims=True)
    e21 = jnp.exp(m2 - m1)
    den = 1.0 / (1.0 + e21)
    cw1 = gprob * den
    cw2 = gprob * e21 * den
    yield

    erow = lax.broadcasted_iota(jnp.int32, (N_EXPERTS, TS), 0)
    oh1 = erow == gidx * EXPERTS_PER_GROUP + j1
    oh2 = erow == gidx * EXPERTS_PER_GROUP + j2
    oh = jnp.where(oh1 | oh2, 1.0, 0.0)
    cnt = jnp.broadcast_to(jnp.sum(oh, axis=1, keepdims=True), (N_EXPERTS, LANES))
    run = jnp.floor((cnt + (SUBLANES - 1.0)) * (1.0 / SUBLANES)) * SUBLANES
    erow_l = lax.broadcasted_iota(jnp.int32, (N_EXPERTS, LANES), 0)
    run_end = run
    shift = 1
    while shift < N_EXPERTS:
        run_end = run_end + jnp.where(erow_l >= shift, pltpu.roll(run_end, shift, 0), 0.0)
        shift *= 2
    run_start = (run_end - run)[:, 0:1]
    ri = lax.broadcasted_iota(jnp.int32, (TS, TS), 0)
    ci = lax.broadcasted_iota(jnp.int32, (TS, TS), 1)
    earlier = _bf(jnp.where(ri < ci, 1.0, 0.0))
    before = jnp.dot(_bf(oh), earlier, preferred_element_type=jnp.float32) + run_start
    pos1 = jnp.sum(jnp.where(oh1, before, 0.0), axis=0, keepdims=True)
    pos2 = jnp.sum(jnp.where(oh2, before, 0.0), axis=0, keepdims=True)
    rr = lax.broadcasted_iota(jnp.int32, (TR, TS), 0).astype(jnp.float32)
    perm = _bf(jnp.where((rr == pos1) | (rr == pos2), 1.0, 0.0))
    yield
    xt_ref[...] = _pack_rows(jnp.dot(perm, u2_hi, preferred_element_type=jnp.float32))
    cnt_ref[...] = cnt

    rowl = lax.broadcasted_iota(jnp.int32, (LANES, TS), 0)
    rec = jnp.where(rowl == 0, pos1, 0.0)
    rec = jnp.where(rowl == 1, pos2, rec)
    rec = jnp.where(rowl == 2, cw1, rec)
    rec = jnp.where(rowl == 3, cw2, rec)
    route_ref[...] = rec.T


def _mix(x, positions, ada, rope, win, wout, wpool, gnw, pscale, ln1w, ln1b, wrt, brt):
    B, S, D = x.shape
    ns = S // TS
    assert B % SEQS_PER_STEP == 0
    P = SEQS_PER_STEP
    const2 = lambda b, s: (0, 0)
    const3 = lambda b, s: (0, 0, 0)
    tile = lambda b, s: (b, s, 0)
    flat = lambda b, s: (b, s, 0, 0)
    return pl.pallas_call(
        _mix_kernel,
        grid=(B // P, ns),
        in_specs=[
            pl.BlockSpec((P, TS, D), tile),
            pl.BlockSpec((P, TS, 1), tile),
            pl.BlockSpec((P, 6, D), lambda b, s: (b, 0, 0)),
            pl.BlockSpec((2, LANES), const2),
            pl.BlockSpec((D, IN_COLS), const2),
            pl.BlockSpec((D, D), const2),
            pl.BlockSpec((len(POOL_WINDOWS), POOL_GROUP_DIM, POOL_GROUP_DIM), const3),
            pl.BlockSpec((1, RET_WIDTH), const2),
            pl.BlockSpec((1, POOL_WIDTH), const2),
            pl.BlockSpec((1, D), const2),
            pl.BlockSpec((1, D), const2),
            pl.BlockSpec((ROUTE_ROWS, D), const2),
            pl.BlockSpec((ROUTE_ROWS, 1), const2),
        ],
        out_specs=[
            pl.BlockSpec((P, TS, D), tile),
            pl.BlockSpec((P, None, 2 * TR, HALF), flat),
            pl.BlockSpec((P, TS, LANES), tile),
            pl.BlockSpec((P, None, N_EXPERTS, LANES), flat),
        ],
        out_shape=[
            jax.ShapeDtypeStruct((B, S, D), jnp.float32),
            jax.ShapeDtypeStruct((B, ns, 2 * TR, HALF), jnp.bfloat16),
            jax.ShapeDtypeStruct((B, S, LANES), jnp.float32),
            jax.ShapeDtypeStruct((B, ns, N_EXPERTS, LANES), jnp.float32),
        ],
        scratch_shapes=[
            pltpu.VMEM((P, RET_HEADS, HEAD_DIM, HEAD_DIM), jnp.float32),
            pltpu.VMEM((P, POOL_HALO, POOL_WIDTH), jnp.float32),
            pltpu.VMEM((RET_HEADS, TS, TS), jnp.float32),
            pltpu.VMEM((TS, RET_WIDTH), jnp.float32),
            pltpu.VMEM((TS, RET_WIDTH), jnp.float32),
            pltpu.VMEM((P, TS, D), jnp.bfloat16),
        ],
        compiler_params=pltpu.CompilerParams(dimension_semantics=("arbitrary", "arbitrary"),
                                             vmem_limit_bytes=VMEM_LIMIT),
        name="mix",
    )(x, positions.reshape(B, S, 1), ada, rope, win, wout, wpool, gnw, pscale, ln1w, ln1b, wrt, brt)


def _fetch_groups(src_ref, group_of, dst_ref, sem):
    for u in range(dst_ref.shape[0]):
        pltpu.make_async_copy(src_ref.at[group_of(u)], dst_ref.at[u], sem).start()


def _wait_fetch(src_ref, dst_ref, sem):
    pltpu.make_async_copy(src_ref.at[pl.ds(0, dst_ref.shape[0])], dst_ref, sem).wait()


def _expert_rows(x_ref, w1_ref, w3_ref, w2_ref, y_ref):
    n = PAGE_SPLIT_ROWS
    xl, xh = _unpack_rows(x_ref[...].reshape(2 * n, HALF))
    yield
    a = (jnp.dot(xl, w1_ref[:HALF, :], preferred_element_type=jnp.float32)
         + jnp.dot(xh, w1_ref[HALF:, :], preferred_element_type=jnp.float32))
    c = (jnp.dot(xl, w3_ref[:HALF, :], preferred_element_type=jnp.float32)
         + jnp.dot(xh, w3_ref[HALF:, :], preferred_element_type=jnp.float32))
    yield
    h = _bf(a * jax.nn.sigmoid(a) * c)
    yield
    y = jnp.dot(h, w2_ref[...], preferred_element_type=jnp.float32)
    yield
    y_ref[...] = _pack_rows(y)


def _round_robin(tasks):
    while tasks:
        tasks = [t for t in tasks if next(t, True) is None]


def _expert_kernel(pe_ref, pv_ref, first_ref, wslot_ref, next_ref, src_ref, xt_ref, w1_hbm, w3_hbm, w2_hbm,
                   ys_ref, xbuf, w1_buf, w3_buf, w2_buf, w1_bf, w3_bf, w2_bf, sem, wsem):
    g = pl.program_id(0)
    last = pl.num_programs(0) - 1
    groups = TM // SUBLANES

    def fetch(page, dst_slot):
        @pl.when(pv_ref[page] > 0)
        def _start():
            _fetch_groups(xt_ref, lambda u: src_ref[page * groups + u], xbuf.at[dst_slot], sem.at[dst_slot])

    def weight_copies(expert, ws):
        return [pltpu.make_async_copy(hbm.at[expert], buf.at[ws], wsem.at[ws])
                for hbm, buf in ((w1_hbm, w1_buf), (w3_hbm, w3_buf), (w2_hbm, w2_buf))]

    @pl.when(g == 0)
    def _prime():
        for k in range(FETCH_AHEAD):
            fetch(k, k)

        @pl.when(pv_ref[0] > 0)
        def _first_weights():
            for c in weight_copies(pe_ref[0], wslot_ref[0]):
                c.start()

    fetch(jnp.minimum(g + FETCH_AHEAD, last), (g + FETCH_AHEAD) % FETCH_SLOTS)
    slot = g % FETCH_SLOTS
    used = pv_ref[g] > 0

    @pl.when(used)
    def _compute():
        ws = wslot_ref[g]

        @pl.when(first_ref[g] == 1)
        def _new_expert():
            for c in weight_copies(pe_ref[g], ws):
                c.wait()
            w1_bf[...] = _bf(w1_buf[ws])
            w3_bf[...] = _bf(w3_buf[ws])
            w2_bf[...] = _bf(w2_buf[ws])

            @pl.when(next_ref[g] >= 0)
            def _stream_next():
                for c in weight_copies(next_ref[g], 1 - ws):
                    c.start()

        _wait_fetch(xt_ref, xbuf.at[slot], sem.at[slot])
        per = PAGE_SPLIT_ROWS // SUBLANES
        _round_robin([_expert_rows(xbuf.at[slot, pl.ds(k * per, per)], w1_bf, w3_bf, w2_bf,
                                   ys_ref.at[pl.ds(k * 2 * PAGE_SPLIT_ROWS, 2 * PAGE_SPLIT_ROWS)])
                      for k in range(TM // PAGE_SPLIT_ROWS)])

    @pl.when(jnp.logical_not(used))
    def _unused():
        ys_ref[...] = jnp.zeros_like(ys_ref)


def _experts(page_tables, esrc, xt, w1, w3, w2):
    D = D_MODEL
    n_pages = page_tables[0].shape[0]
    assert n_pages > FETCH_AHEAD and TM % PAGE_SPLIT_ROWS == 0
    grid_spec = pltpu.PrefetchScalarGridSpec(
        num_scalar_prefetch=len(page_tables) + 1,
        grid=(n_pages,),
        in_specs=[pl.BlockSpec(memory_space=pl.ANY)] * 4,
        out_specs=pl.BlockSpec((2 * TM, HALF), lambda g, *_: (g, 0)),
        scratch_shapes=[pltpu.VMEM((FETCH_SLOTS, TM // SUBLANES, 2 * SUBLANES, HALF), jnp.bfloat16),
                        pltpu.VMEM((2, D, D_EXPERT), jnp.float32),
                        pltpu.VMEM((2, D, D_EXPERT), jnp.float32),
                        pltpu.VMEM((2, D_EXPERT, D), jnp.float32),
                        pltpu.VMEM((D, D_EXPERT), jnp.bfloat16),
                        pltpu.VMEM((D, D_EXPERT), jnp.bfloat16),
                        pltpu.VMEM((D_EXPERT, D), jnp.bfloat16),
                        pltpu.SemaphoreType.DMA((FETCH_SLOTS,)),
                        pltpu.SemaphoreType.DMA((2,))],
    )
    return pl.pallas_call(
        _expert_kernel,
        grid_spec=grid_spec,
        out_shape=jax.ShapeDtypeStruct((n_pages * 2 * TM, HALF), jnp.bfloat16),
        compiler_params=pltpu.CompilerParams(dimension_semantics=("arbitrary",), vmem_limit_bytes=VMEM_LIMIT),
        name="experts",
    )(*page_tables, esrc, xt, w1, w3, w2)


def _final_tile(y_ref, x1_ref, route_ref, ada_ref, lnw_ref, lnb_ref, o_ref):
    route = route_ref[...]
    pos1, pos2, cw1, cw2 = route[:, 0:1], route[:, 1:2], route[:, 2:3], route[:, 3:4]
    col = lax.broadcasted_iota(jnp.int32, (TS, TR), 1).astype(jnp.float32)
    wmat = _bf(jnp.where(col == pos1, cw1, 0.0) + jnp.where(col == pos2, cw2, 0.0))
    yield
    yl, yh = _unpack_rows(y_ref[...].reshape(2 * TR, HALF))
    yield
    y = jnp.concatenate([jnp.dot(wmat, yl, preferred_element_type=jnp.float32),
                         jnp.dot(wmat, yh, preferred_element_type=jnp.float32)], axis=-1)
    yield
    gate2 = ada_ref[5:6, :]
    o_ref[...] = _ln(ALPHA * x1_ref[...] + gate2 * y) * lnw_ref[...] + lnb_ref[...]


def _final_kernel(src_ref, x1_ref, route_ref, ada_ref, lnw_ref, lnb_ref, ys_ref, o_ref, ybuf, sem):
    P = TILES_PER_STEP
    i = pl.program_id(0)
    last = pl.num_programs(0) - 1
    per = TR // SUBLANES
    groups = P * per

    def fetch(step, dst_slot):
        _fetch_groups(ys_ref, lambda u: src_ref[step * groups + u], ybuf.at[dst_slot], sem.at[dst_slot])

    @pl.when(i == 0)
    def _prime():
        for k in range(FETCH_AHEAD):
            fetch(k, k)

    fetch(jnp.minimum(i + FETCH_AHEAD, last), (i + FETCH_AHEAD) % FETCH_SLOTS)
    slot = i % FETCH_SLOTS
    _wait_fetch(ys_ref, ybuf.at[slot], sem.at[slot])

    rows = lambda k: pl.ds(k * TS, TS)
    _round_robin([_final_tile(ybuf.at[slot, pl.ds(k * per, per)], x1_ref.at[rows(k)], route_ref.at[rows(k)], ada_ref,
                              lnw_ref, lnb_ref, o_ref.at[rows(k)]) for k in range(P)])

    @pl.when(i == last)
    def _drain():
        for k in range(1, FETCH_SLOTS):
            other = (i + k) % FETCH_SLOTS
            _wait_fetch(ys_ref, ybuf.at[other], sem.at[other])


def _final(fsrc, x1, route, ada, lnw, lnb, ys, seq_len):
    T, D = x1.shape
    P = TILES_PER_STEP
    rows = P * TS
    assert seq_len % rows == 0
    steps_per_seq = seq_len // rows
    n_steps = T // rows
    assert n_steps > FETCH_AHEAD
    grid_spec = pltpu.PrefetchScalarGridSpec(
        num_scalar_prefetch=1,
        grid=(n_steps,),
        in_specs=[pl.BlockSpec((rows, D), lambda i, src: (i, 0)),
                  pl.BlockSpec((rows, LANES), lambda i, src: (i, 0)),
                  pl.BlockSpec((None, 6, D), lambda i, src: (i // steps_per_seq, 0, 0)),
                  pl.BlockSpec((1, D), lambda i, src: (0, 0)),
                  pl.BlockSpec((1, D), lambda i, src: (0, 0)),
                  pl.BlockSpec(memory_space=pl.ANY)],
        out_specs=pl.BlockSpec((rows, D), lambda i, src: (i, 0)),
        scratch_shapes=[pltpu.VMEM((FETCH_SLOTS, P * (TR // SUBLANES), 2 * SUBLANES, HALF), jnp.bfloat16),
                        pltpu.SemaphoreType.DMA((FETCH_SLOTS,))],
    )
    return pl.pallas_call(
        _final_kernel,
        grid_spec=grid_spec,
        out_shape=jax.ShapeDtypeStruct((T, D), jnp.float32),
        compiler_params=pltpu.CompilerParams(dimension_semantics=("arbitrary",), vmem_limit_bytes=VMEM_LIMIT),
        name="final",
    )(fsrc, x1, route, ada, lnw, lnb, ys)


def _gather_tables(cnt, n_pages):
    nt = cnt.shape[0]
    run = (cnt + SUBLANES - 1) // SUBLANES * SUBLANES
    so = jnp.cumsum(run, axis=1) - run
    eo = jnp.cumsum(run, axis=0) - run
    tot = jnp.sum(run, axis=0)
    pages_e = (tot + TM - 1) // TM
    page_end = jnp.cumsum(pages_e)
    page_start = page_end - pages_e
    g = jnp.arange(n_pages, dtype=jnp.int32)
    pe = jnp.minimum(jnp.sum(g[:, None] >= page_end[None, :], axis=1), N_EXPERTS - 1).astype(jnp.int32)
    used = g < page_end[-1]
    owner = (g[None, :] >= page_start[:, None]) & (g[None, :] < page_end[:, None])
    of_page = lambda a: jnp.sum(jnp.where(owner, a[:, None], 0), axis=0)
    of_page2 = lambda a: jnp.sum(jnp.where(owner[:, None, :], a.T[:, :, None], 0), axis=0)
    pj = g - of_page(page_start)
    tot_p = of_page(tot)
    pv = jnp.where(used, jnp.clip(tot_p - pj * TM, 0, TM), 0).astype(jnp.int32)
    experts = jnp.arange(N_EXPERTS, dtype=jnp.int32)
    has_pages = pages_e > 0
    wslot_e = (jnp.cumsum(has_pages) - 1) % 2
    later = (experts[None, :] > experts[:, None]) & has_pages[None, :]
    next_e = jnp.min(jnp.where(later, experts[None, :], N_EXPERTS), axis=1)
    next_e = jnp.where(next_e < N_EXPERTS, next_e, -1)
    first = (used & (pj == 0)).astype(jnp.int32)
    page_tables = (pe, pv, first, of_page(wslot_e).astype(jnp.int32), jnp.where(used, of_page(next_e), -1).astype(jnp.int32))

    q = (pj * TM)[:, None] + SUBLANES * jnp.arange(TM // SUBLANES, dtype=jnp.int32)[None, :]
    tiles = jnp.arange(nt, dtype=jnp.int32)
    offset = of_page2(tiles[:, None] * TR + so - eo)
    step = jnp.concatenate([offset[1:] - offset[:-1], jnp.zeros_like(offset[:1])], axis=0)
    run_end = of_page2(eo + run)
    src = q + offset[0][:, None] + jnp.sum(jnp.where(run_end[:, :, None] <= q[None], step[:, :, None], 0), axis=0)
    zero_xt = TR - SUBLANES
    esrc = jnp.where((q < tot_p[:, None]) & used[:, None], src, zero_xt).astype(jnp.int32)

    r = SUBLANES * jnp.arange(TR // SUBLANES, dtype=jnp.int32)
    end = (so + run).T
    offset_f = ((page_start * TM)[None, :] + eo - so).T
    step_f = jnp.concatenate([offset_f[1:] - offset_f[:-1], jnp.zeros_like(offset_f[:1])], axis=0)
    srcf = (r[None, :] + offset_f[0][:, None]
            + jnp.sum(jnp.where(end[:, :, None] <= r[None, None, :], step_f[:, :, None], 0), axis=0))
    zero_ys = (n_pages - 1) * TM
    fsrc = jnp.where(r[None, :] < end[-1][:, None], srcf, zero_ys).astype(jnp.int32)
    return page_tables, esrc.reshape(-1) // SUBLANES, fsrc.reshape(-1) // SUBLANES


def kernel(x, c, positions, w_ada, b_ada, w_in, ret_gn_w, w_pool, pool_scale, w_out, ln1_w, ln1_b, w_group, b_group,
           w_router, b_router, w1, w3, w2, ln2_w, ln2_b):
    B, S, D = x.shape
    T = B * S
    assert w_ada.shape[0] == DEPTH and D == D_MODEL and S % TS == 0

    inv_freq = ROPE_BASE ** (-jnp.arange(0, HEAD_DIM, 2, dtype=jnp.float32) / HEAD_DIM)
    half = HEAD_DIM // 2
    rope = jnp.stack([jnp.concatenate([inv_freq, inv_freq]),
                      jnp.concatenate([-jnp.ones((half,), jnp.float32), jnp.ones((half,), jnp.float32)])])

    nt = T // TS
    n_pages = (2 * T + nt * N_EXPERTS * (SUBLANES - 1)) // TM + N_EXPERTS + 1

    for l in range(DEPTH):
        ada = _ada(c, w_ada[l], b_ada[l]).reshape(B, 6, D)
        pad = SUBLANES - N_GROUPS
        wrt = jnp.concatenate([w_group[l].T, jnp.zeros((pad, D), jnp.float32), w_router[l].T], axis=0)
        brt = jnp.concatenate([b_group[l], jnp.zeros((pad,), jnp.float32), b_router[l]]).reshape(ROUTE_ROWS, 1)
        x1, xt, route, counts = _mix(
            x, positions, ada, rope, _bf(w_in[l]), _bf(w_out[l]), _bf(w_pool[l]),
            ret_gn_w[l].reshape(1, RET_WIDTH), pool_scale[l].reshape(1, POOL_WIDTH),
            ln1_w[l].reshape(1, D), ln1_b[l].reshape(1, D), wrt, brt)

        cnt = counts[:, :, :, 0].reshape(nt, N_EXPERTS).astype(jnp.int32)
        page_tables, esrc, fsrc = _gather_tables(cnt, n_pages)
        ys = _experts(page_tables, esrc, xt.reshape(nt * TR // SUBLANES, 2 * SUBLANES, HALF),
                      w1[l].reshape(N_EXPERTS, D, D_EXPERT), w3[l].reshape(N_EXPERTS, D, D_EXPERT),
                      w2[l].reshape(N_EXPERTS, D_EXPERT, D))
        x = _final(fsrc, x1.reshape(T, D), route.reshape(T, LANES), ada,
                   ln2_w[l].reshape(1, D), ln2_b[l].reshape(1, D),
                   ys.reshape(n_pages * TM // SUBLANES, 2 * SUBLANES, HALF), S).reshape(B, S, D)
    return x
```

```python
import math

import jax
import jax.numpy as jnp
from jax import lax
from jax.experimental import pallas as pl
from jax.experimental.pallas import tpu as pltpu

D_MODEL = 1024
RET_WIDTH = 512
RET_HEADS = 4
HEAD_DIM = 128
POOL_WIDTH = 512
POOL_WINDOWS = (2, 4, 8, 16)
POOL_GROUP_DIM = 128
IN_COLS = 4 * RET_WIDTH + POOL_WIDTH
N_GROUPS = 4
EXPERTS_PER_GROUP = 8
N_EXPERTS = 32
D_EXPERT = 256
DEPTH = 1
ALPHA = (2.0 * DEPTH) ** 0.25
LN_EPS = 1e-5
ROPE_BASE = 10000.0

LANES = 128
SUBLANES = 8
HALF = D_MODEL // 2
POOL_HALO = 16
TS = 256
TM = 512
PAGE_SPLIT_ROWS = 256
TR = -(-(2 * TS + N_EXPERTS * (SUBLANES - 1)) // LANES) * LANES
SEQS_PER_STEP = 2
TILES_PER_STEP = 2
ROUTE_ROWS = SUBLANES + N_EXPERTS
FETCH_AHEAD = 2
FETCH_SLOTS = FETCH_AHEAD + 1
VMEM_LIMIT = 56 * 1024 * 1024

_LOG_GAMMA = tuple(math.log1p(-(2.0 ** (-5.0 - h))) for h in range(RET_HEADS))
_HI = lax.Precision.HIGHEST


def _ln(x):
    mu = jnp.mean(x, axis=-1, keepdims=True)
    xc = x - mu
    var = jnp.mean(xc * xc, axis=-1, keepdims=True)
    return xc * lax.rsqrt(var + LN_EPS)


def _bf(x):
    return x.astype(jnp.bfloat16)


def _pack_rows(x):
    n = x.shape[0]
    lo = x[:, :HALF].reshape(n // SUBLANES, SUBLANES, HALF)
    hi = x[:, HALF:].reshape(n // SUBLANES, SUBLANES, HALF)
    return _bf(jnp.concatenate([lo, hi], axis=1).reshape(2 * n, HALF))


def _unpack_rows(z):
    n = z.shape[0] // 2
    zf = z.astype(jnp.float32).reshape(n // SUBLANES, 2 * SUBLANES, HALF)
    return _bf(zf[:, :SUBLANES, :].reshape(n, HALF)), _bf(zf[:, SUBLANES:, :].reshape(n, HALF))


def _ada_kernel(c_ref, w_ref, b_ref, o_ref):
    c = c_ref[...]
    ca = c * jax.nn.sigmoid(c)
    o_ref[...] = jnp.dot(ca, w_ref[...], precision=_HI, preferred_element_type=jnp.float32) + b_ref[...]


def _ada(c, w_ada, b_ada):
    B, D = c.shape
    n = w_ada.shape[1]
    return pl.pallas_call(
        _ada_kernel,
        grid=(n // D,),
        in_specs=[pl.BlockSpec((B, D), lambda j: (0, 0)),
                  pl.BlockSpec((D, D), lambda j: (0, j)),
                  pl.BlockSpec((1, D), lambda j: (0, j))],
        out_specs=pl.BlockSpec((B, D), lambda j: (0, j)),
        out_shape=jax.ShapeDtypeStruct((B, n), jnp.float32),
        compiler_params=pltpu.CompilerParams(dimension_semantics=("arbitrary",), vmem_limit_bytes=VMEM_LIMIT),
        name="ada",
    )(c, w_ada, b_ada.reshape(1, n))


def _mix_kernel(x_ref, pos_ref, ada_ref, rope_ref, win_ref, wout_ref, wpool_ref, gnw_ref, pscale_ref,
                ln1w_ref, ln1b_ref, wrt_ref, brt_ref,
                x1_ref, xt_ref, route_ref, cnt_ref,
                state_ref, halo_ref, dmat_ref, qdec_ref, kdec_ref, cat_ref):
    b = pl.program_id(0)
    s = pl.program_id(1)

    @pl.when((b == 0) & (s == 0))
    def _init_tables():
        ri = lax.broadcasted_iota(jnp.int32, (TS, TS), 0)
        ci = lax.broadcasted_iota(jnp.int32, (TS, TS), 1)
        rel = (ri - ci).astype(jnp.float32)
        for h in range(RET_HEADS):
            dmat_ref[h] = jnp.where(rel >= 0.0, jnp.exp(jnp.maximum(rel, 0.0) * _LOG_GAMMA[h]), 0.0)
        row = lax.broadcasted_iota(jnp.int32, (TS, RET_WIDTH), 0).astype(jnp.float32)
        lane = lax.broadcasted_iota(jnp.int32, (TS, RET_WIDTH), 1)
        lg = jnp.full((TS, RET_WIDTH), _LOG_GAMMA[0], jnp.float32)
        for h in range(1, RET_HEADS):
            lg = jnp.where(lane >= h * HEAD_DIM, _LOG_GAMMA[h], lg)
        qdec_ref[...] = jnp.exp((row + 1.0) * lg)
        kdec_ref[...] = jnp.exp((TS - 1.0 - row) * lg)

    @pl.when(s == 0)
    def _init_carries():
        state_ref[...] = jnp.zeros_like(state_ref)
        halo_ref[...] = jnp.zeros_like(halo_ref)

    tiles = [_mix_tile(s, x_ref.at[j], pos_ref.at[j], ada_ref.at[j], rope_ref, win_ref, wout_ref, wpool_ref, gnw_ref,
                       pscale_ref, ln1w_ref, ln1b_ref, wrt_ref, brt_ref,
                       x1_ref.at[j], xt_ref.at[j], route_ref.at[j], cnt_ref.at[j],
                       state_ref.at[j], halo_ref.at[j], dmat_ref, qdec_ref, kdec_ref, cat_ref.at[j])
             for j in range(SEQS_PER_STEP)]
    _round_robin(tiles)


def _mix_tile(s, x_ref, pos_ref, ada_ref, rope_ref, win_ref, wout_ref, wpool_ref, gnw_ref, pscale_ref,
              ln1w_ref, ln1b_ref, wrt_ref, brt_ref,
              x1_ref, xt_ref, route_ref, cnt_ref,
              state_ref, halo_ref, dmat_ref, qdec_ref, kdec_ref, cat_ref):
    ada = ada_ref[...]
    shift1, scale1, gate1 = ada[0:1], ada[1:2], ada[2:3]
    shift2, scale2 = ada[3:4], ada[4:5]

    x = x_ref[...]
    u = _bf(_ln(x) * (1.0 + scale1) + shift1)
    yield

    posf = pos_ref[...].astype(jnp.float32)
    hl = lax.broadcasted_iota(jnp.int32, (TS // 2, HEAD_DIM), 1) < HEAD_DIM // 2
    ang = jnp.where(hl, posf[:TS // 2], posf[TS // 2:]) * rope_ref[0:1, :]
    cos_p, sin_p = jnp.cos(ang), jnp.sin(ang)
    cos_s, sin_s = pltpu.roll(cos_p, HEAD_DIM // 2, 1), pltpu.roll(sin_p, HEAD_DIM // 2, 1)
    cos_t = jnp.concatenate([jnp.where(hl, cos_p, cos_s), jnp.where(hl, cos_s, cos_p)], axis=0)
    sin_t = jnp.concatenate([jnp.where(hl, sin_p, sin_s), jnp.where(hl, sin_s, sin_p)], axis=0) * rope_ref[1:2, :]
    yield

    q = jnp.dot(u, win_ref[:, 0:RET_WIDTH], preferred_element_type=jnp.float32)
    k = jnp.dot(u, win_ref[:, RET_WIDTH:2 * RET_WIDTH], preferred_element_type=jnp.float32)
    v = jnp.dot(u, win_ref[:, 2 * RET_WIDTH:3 * RET_WIDTH], preferred_element_type=jnp.float32)
    g = jnp.dot(u, win_ref[:, 3 * RET_WIDTH:4 * RET_WIDTH], preferred_element_type=jnp.float32)
    p = jnp.dot(u, win_ref[:, 4 * RET_WIDTH:IN_COLS], preferred_element_type=jnp.float32)
    yield

    gnw = gnw_ref[...]
    for h in range(RET_HEADS):
        sl = slice(h * HEAD_DIM, (h + 1) * HEAD_DIM)
        qh, kh, vh = q[:, sl], k[:, sl], v[:, sl]
        qr = qh * cos_t + pltpu.roll(qh, HEAD_DIM // 2, 1) * sin_t
        kr = (kh * cos_t + pltpu.roll(kh, HEAD_DIM // 2, 1) * sin_t) * (HEAD_DIM ** -0.5)
        vb = _bf(vh)
        sc = lax.dot_general(_bf(qr), _bf(kr), (((1,), (1,)), ((), ())), preferred_element_type=jnp.float32)
        intra = jnp.dot(_bf(sc * dmat_ref[h]), vb, preferred_element_type=jnp.float32)
        st = state_ref[h]
        cross = jnp.dot(_bf(qr * qdec_ref[:, sl]), _bf(st), preferred_element_type=jnp.float32)
        kv = lax.dot_general(_bf(kr * kdec_ref[:, sl]), vb, (((0,), (0,)), ((), ())),
                             preferred_element_type=jnp.float32)
        state_ref[h] = st * math.exp(TS * _LOG_GAMMA[h]) + kv
        r = _ln(intra + cross) * gnw[:, sl]
        gh = g[:, sl]
        cat_ref[:, sl] = _bf(gh * jax.nn.sigmoid(gh) * r)
        yield

    pext = jnp.concatenate([halo_ref[...], p], axis=0)
    halo_ref[...] = p[TS - POOL_HALO:, :]
    t_abs = (s * TS + lax.broadcasted_iota(jnp.int32, (TS, 1), 0) + 1).astype(jnp.float32)
    pscale = pscale_ref[...]
    for grp, w in enumerate(POOL_WINDOWS):
        sl = slice(grp * POOL_GROUP_DIM, (grp + 1) * POOL_GROUP_DIM)
        acc = pext[:, sl]
        shift = 1
        while shift < w:
            acc = acc + pltpu.roll(acc, shift, 0)
            shift *= 2
        pooled = acc[POOL_HALO:, :] / jnp.minimum(t_abs, float(w)) - p[:, sl]
        po = jnp.dot(_bf(pooled), wpool_ref[grp], preferred_element_type=jnp.float32) * pscale[:, sl]
        cat_ref[:, RET_WIDTH + grp * POOL_GROUP_DIM:RET_WIDTH + (grp + 1) * POOL_GROUP_DIM] = _bf(po)
    yield

    mix = jnp.dot(cat_ref[...], wout_ref[...], preferred_element_type=jnp.float32)
    yield
    x1 = _ln(ALPHA * x + gate1 * mix) * ln1w_ref[...] + ln1b_ref[...]
    x1_ref[...] = x1
    u2 = _ln(x1) * (1.0 + scale2) + shift2
    yield

    w = wrt_ref[...]
    w_hi = _bf(w)
    w_lo = _bf(w - w_hi.astype(jnp.float32))
    u2_hi = _bf(u2)
    u2_lo = _bf(u2 - u2_hi.astype(jnp.float32))
    nt = (((1,), (1,)), ((), ()))
    logits = (lax.dot_general(w_hi, u2_hi, nt, preferred_element_type=jnp.float32)
              + lax.dot_general(w_hi, u2_lo, nt, preferred_element_type=jnp.float32)
              + lax.dot_general(w_lo, u2_hi, nt, preferred_element_type=jnp.float32)) + brt_ref[...]
    row8 = lax.broadcasted_iota(jnp.int32, (SUBLANES, TS), 0)
    neg = jnp.float32(-jnp.inf)
    gl = jnp.where(row8 < N_GROUPS, logits[0:SUBLANES], neg)
    gmax = jnp.max(gl, axis=0, keepdims=True)
    gidx = jnp.min(jnp.where(gl == gmax, row8, SUBLANES), axis=0, keepdims=True)
    gprob = 1.0 / jnp.sum(jnp.exp(gl - gmax), axis=0, keepdims=True)
    el = logits[SUBLANES:2 * SUBLANES]
    for grp in range(1, N_GROUPS):
        el = jnp.where(gidx == grp, logits[(grp + 1) * SUBLANES:(grp + 2) * SUBLANES], el)
    m1 = jnp.max(el, axis=0, keepdims=True)
    j1 = jnp.min(jnp.where(el == m1, row8, SUBLANES), axis=0, keepdims=True)
    el2 = jnp.where(row8 == j1, neg, el)
    m2 = jnp.max(el2, axis=0, keepdims=True)
    j2 = jnp.min(jnp.where(el2 == m2, row8, SUBLANES), axis=0, keepdims=True)
    e21 = jnp.exp(m2 - m1)
    den = 1.0 / (1.0 + e21)
    cw1 = gprob * den
    cw2 = gprob * e21 * den
    yield

    erow = lax.broadcasted_iota(jnp.int32, (N_EXPERTS, TS), 0)
    oh1 = erow == gidx * EXPERTS_PER_GROUP + j1
    oh2 = erow == gidx * EXPERTS_PER_GROUP + j2
    oh = jnp.where(oh1 | oh2, 1.0, 0.0)
    cnt = jnp.broadcast_to(jnp.sum(oh, axis=1, keepdims=True), (N_EXPERTS, LANES))
    run = jnp.floor((cnt + (SUBLANES - 1.0)) * (1.0 / SUBLANES)) * SUBLANES
    erow_l = lax.broadcasted_iota(jnp.int32, (N_EXPERTS, LANES), 0)
    run_end = run
    shift = 1
    while shift < N_EXPERTS:
        run_end = run_end + jnp.where(erow_l >= shift, pltpu.roll(run_end, shift, 0), 0.0)
        shift *= 2
    run_start = (run_end - run)[:, 0:1]
    ri = lax.broadcasted_iota(jnp.int32, (TS, TS), 0)
    ci = lax.broadcasted_iota(jnp.int32, (TS, TS), 1)
    earlier = _bf(jnp.where(ri < ci, 1.0, 0.0))
    before = jnp.dot(_bf(oh), earlier, preferred_element_type=jnp.float32) + run_start
    pos1 = jnp.sum(jnp.where(oh1, before, 0.0), axis=0, keepdims=True)
    pos2 = jnp.sum(jnp.where(oh2, before, 0.0), axis=0, keepdims=True)
    rr = lax.broadcasted_iota(jnp.int32, (TR, TS), 0).astype(jnp.float32)
    perm = _bf(jnp.where((rr == pos1) | (rr == pos2), 1.0, 0.0))
    yield
    xt_ref[...] = _pack_rows(jnp.dot(perm, u2_hi, preferred_element_type=jnp.float32))
    cnt_ref[...] = cnt

    rowl = lax.broadcasted_iota(jnp.int32, (LANES, TS), 0)
    rec = jnp.where(rowl == 0, pos1, 0.0)
    rec = jnp.where(rowl == 1, pos2, rec)
    rec = jnp.where(rowl == 2, cw1, rec)
    rec = jnp.where(rowl == 3, cw2, rec)
    route_ref[...] = rec.T


def _mix(x, positions, ada, rope, win, wout, wpool, gnw, pscale, ln1w, ln1b, wrt, brt):
    B, S, D = x.shape
    ns = S // TS
    assert B % SEQS_PER_STEP == 0
    P = SEQS_PER_STEP
    const2 = lambda b, s: (0, 0)
    const3 = lambda b, s: (0, 0, 0)
    tile = lambda b, s: (b, s, 0)
    flat = lambda b, s: (b, s, 0, 0)
    return pl.pallas_call(
        _mix_kernel,
        grid=(B // P, ns),
        in_specs=[
            pl.BlockSpec((P, TS, D), tile),
            pl.BlockSpec((P, TS, 1), tile),
            pl.BlockSpec((P, 6, D), lambda b, s: (b, 0, 0)),
            pl.BlockSpec((2, LANES), const2),
            pl.BlockSpec((D, IN_COLS), const2),
            pl.BlockSpec((D, D), const2),
            pl.BlockSpec((len(POOL_WINDOWS), POOL_GROUP_DIM, POOL_GROUP_DIM), const3),
            pl.BlockSpec((1, RET_WIDTH), const2),
            pl.BlockSpec((1, POOL_WIDTH), const2),
            pl.BlockSpec((1, D), const2),
            pl.BlockSpec((1, D), const2),
            pl.BlockSpec((ROUTE_ROWS, D), const2),
            pl.BlockSpec((ROUTE_ROWS, 1), const2),
        ],
        out_specs=[
            pl.BlockSpec((P, TS, D), tile),
            pl.BlockSpec((P, None, 2 * TR, HALF), flat),
            pl.BlockSpec((P, TS, LANES), tile),
            pl.BlockSpec((P, None, N_EXPERTS, LANES), flat),
        ],
        out_shape=[
            jax.ShapeDtypeStruct((B, S, D), jnp.float32),
            jax.ShapeDtypeStruct((B, ns, 2 * TR, HALF), jnp.bfloat16),
            jax.ShapeDtypeStruct((B, S, LANES), jnp.float32),
            jax.ShapeDtypeStruct((B, ns, N_EXPERTS, LANES), jnp.float32),
        ],
        scratch_shapes=[
            pltpu.VMEM((P, RET_HEADS, HEAD_DIM, HEAD_DIM), jnp.float32),
            pltpu.VMEM((P, POOL_HALO, POOL_WIDTH), jnp.float32),
            pltpu.VMEM((RET_HEADS, TS, TS), jnp.float32),
            pltpu.VMEM((TS, RET_WIDTH), jnp.float32),
            pltpu.VMEM((TS, RET_WIDTH), jnp.float32),
            pltpu.VMEM((P, TS, D), jnp.bfloat16),
        ],
        compiler_params=pltpu.CompilerParams(dimension_semantics=("arbitrary", "arbitrary"),
                                             vmem_limit_bytes=VMEM_LIMIT),
        name="mix",
    )(x, positions.reshape(B, S, 1), ada, rope, win, wout, wpool, gnw, pscale, ln1w, ln1b, wrt, brt)


def _fetch_groups(src_ref, group_of, dst_ref, sem):
    for u in range(dst_ref.shape[0]):
        pltpu.make_async_copy(src_ref.at[group_of(u)], dst_ref.at[u], sem).start()


def _wait_fetch(src_ref, dst_ref, sem):
    pltpu.make_async_copy(src_ref.at[pl.ds(0, dst_ref.shape[0])], dst_ref, sem).wait()


def _expert_rows(x_ref, w1_ref, w3_ref, w2_ref, y_ref):
    n = PAGE_SPLIT_ROWS
    xl, xh = _unpack_rows(x_ref[...].reshape(2 * n, HALF))
    yield
    a = (jnp.dot(xl, w1_ref[:HALF, :], preferred_element_type=jnp.float32)
         + jnp.dot(xh, w1_ref[HALF:, :], preferred_element_type=jnp.float32))
    c = (jnp.dot(xl, w3_ref[:HALF, :], preferred_element_type=jnp.float32)
         + jnp.dot(xh, w3_ref[HALF:, :], preferred_element_type=jnp.float32))
    yield
    h = _bf(a * jax.nn.sigmoid(a) * c)
    yield
    y = jnp.dot(h, w2_ref[...], preferred_element_type=jnp.float32)
    yield
    y_ref[...] = _pack_rows(y)


def _round_robin(tasks):
    while tasks:
        tasks = [t for t in tasks if next(t, True) is None]


def _expert_kernel(used_ref, pe_ref, first_ref, wslot_ref, next_ref, src_ref, xt_ref, w1_hbm, w3_hbm, w2_hbm,
                   ys_ref, xbuf, ybuf, w1_buf, w3_buf, w2_buf, w1_bf, w3_bf, w2_bf, sem, osem, wsem):
    n_used = used_ref[0]
    n_pages = ys_ref.shape[0] // (2 * TM)
    groups = TM // SUBLANES

    def fetch(page):
        @pl.when(page < n_used)
        def _start():
            slot = page % FETCH_SLOTS
            _fetch_groups(xt_ref, lambda u: src_ref[page * groups + u], xbuf.at[slot], sem.at[slot])

    def weight_copies(expert, ws):
        return [pltpu.make_async_copy(hbm.at[expert], buf.at[ws], wsem.at[ws])
                for hbm, buf in ((w1_hbm, w1_buf), (w3_hbm, w3_buf), (w2_hbm, w2_buf))]

    def page_out(page, oslot):
        return pltpu.make_async_copy(ybuf.at[oslot], ys_ref.at[pl.ds(pl.multiple_of(page * 2 * TM, 2 * TM), 2 * TM)],
                                     osem.at[oslot])

    for k in range(FETCH_AHEAD):
        fetch(k)

    @pl.when(n_used > 0)
    def _first_weights():
        for c in weight_copies(pe_ref[0], wslot_ref[0]):
            c.start()

    def page_body(g, carry):
        fetch(g + FETCH_AHEAD)
        ws = wslot_ref[g]

        @pl.when(first_ref[g] == 1)
        def _new_expert():
            for c in weight_copies(pe_ref[g], ws):
                c.wait()
            w1_bf[...] = _bf(w1_buf[ws])
            w3_bf[...] = _bf(w3_buf[ws])
            w2_bf[...] = _bf(w2_buf[ws])

            @pl.when(next_ref[g] >= 0)
            def _stream_next():
                for c in weight_copies(next_ref[g], 1 - ws):
                    c.start()

        slot = g % FETCH_SLOTS
        oslot = g % 2
        _wait_fetch(xt_ref, xbuf.at[slot], sem.at[slot])

        @pl.when(g >= 2)
        def _reuse_out_buffer():
            page_out(g - 2, oslot).wait()

        per = PAGE_SPLIT_ROWS // SUBLANES
        _round_robin([_expert_rows(xbuf.at[slot, pl.ds(k * per, per)], w1_bf, w3_bf, w2_bf,
                                   ybuf.at[oslot, pl.ds(k * 2 * PAGE_SPLIT_ROWS, 2 * PAGE_SPLIT_ROWS)])
                      for k in range(TM // PAGE_SPLIT_ROWS)])
        page_out(g, oslot).start()
        return carry

    lax.fori_loop(0, n_used, page_body, 0)

    for back in (2, 1):
        @pl.when(n_used >= back)
        def _drain(back=back):
            page_out(n_used - back, (n_used - back) % 2).wait()

    ybuf[0] = jnp.zeros(ybuf.shape[1:], ybuf.dtype)

    def zero_start(page, carry):
        page_out(page, 0).start()
        return carry

    def zero_wait(page, carry):
        page_out(page, 0).wait()
        return carry

    lax.fori_loop(n_used, n_pages, zero_start, 0)
    lax.fori_loop(n_used, n_pages, zero_wait, 0)


def _experts(page_tables, esrc, xt, w1, w3, w2):
    D = D_MODEL
    n_pages = page_tables[1].shape[0]
    assert TM % PAGE_SPLIT_ROWS == 0
    grid_spec = pltpu.PrefetchScalarGridSpec(
        num_scalar_prefetch=len(page_tables) + 1,
        grid=(1,),
        in_specs=[pl.BlockSpec(memory_space=pl.ANY)] * 4,
        out_specs=pl.BlockSpec(memory_space=pl.ANY),
        scratch_shapes=[pltpu.VMEM((FETCH_SLOTS, TM // SUBLANES, 2 * SUBLANES, HALF), jnp.bfloat16),
                        pltpu.VMEM((2, 2 * TM, HALF), jnp.bfloat16),
                        pltpu.VMEM((2, D, D_EXPERT), jnp.float32),
                        pltpu.VMEM((2, D, D_EXPERT), jnp.float32),
                        pltpu.VMEM((2, D_EXPERT, D), jnp.float32),
                        pltpu.VMEM((D, D_EXPERT), jnp.bfloat16),
                        pltpu.VMEM((D, D_EXPERT), jnp.bfloat16),
                        pltpu.VMEM((D_EXPERT, D), jnp.bfloat16),
                        pltpu.SemaphoreType.DMA((FETCH_SLOTS,)),
                        pltpu.SemaphoreType.DMA((2,)),
                        pltpu.SemaphoreType.DMA((2,))],
    )
    return pl.pallas_call(
        _expert_kernel,
        grid_spec=grid_spec,
        out_shape=jax.ShapeDtypeStruct((n_pages * 2 * TM, HALF), jnp.bfloat16),
        compiler_params=pltpu.CompilerParams(dimension_semantics=("arbitrary",), vmem_limit_bytes=VMEM_LIMIT),
        name="experts",
    )(*page_tables, esrc, xt, w1, w3, w2)


def _final_tile(y_ref, x1_ref, route_ref, ada_ref, lnw_ref, lnb_ref, o_ref):
    route = route_ref[...]
    pos1, pos2, cw1, cw2 = route[:, 0:1], route[:, 1:2], route[:, 2:3], route[:, 3:4]
    col = lax.broadcasted_iota(jnp.int32, (TS, TR), 1).astype(jnp.float32)
    wmat = _bf(jnp.where(col == pos1, cw1, 0.0) + jnp.where(col == pos2, cw2, 0.0))
    yield
    yl, yh = _unpack_rows(y_ref[...].reshape(2 * TR, HALF))
    yield
    y = jnp.concatenate([jnp.dot(wmat, yl, preferred_element_type=jnp.float32),
                         jnp.dot(wmat, yh, preferred_element_type=jnp.float32)], axis=-1)
    yield
    gate2 = ada_ref[5:6, :]
    o_ref[...] = _ln(ALPHA * x1_ref[...] + gate2 * y) * lnw_ref[...] + lnb_ref[...]


def _final_kernel(src_ref, x1_ref, route_ref, ada_ref, lnw_ref, lnb_ref, ys_ref, o_ref, ybuf, sem):
    P = TILES_PER_STEP
    i = pl.program_id(0)
    last = pl.num_programs(0) - 1
    per = TR // SUBLANES
    groups = P * per

    def fetch(step, dst_slot):
        _fetch_groups(ys_ref, lambda u: src_ref[step * groups + u], ybuf.at[dst_slot], sem.at[dst_slot])

    @pl.when(i == 0)
    def _prime():
        for k in range(FETCH_AHEAD):
            fetch(k, k)

    fetch(jnp.minimum(i + FETCH_AHEAD, last), (i + FETCH_AHEAD) % FETCH_SLOTS)
    slot = i % FETCH_SLOTS
    _wait_fetch(ys_ref, ybuf.at[slot], sem.at[slot])

    rows = lambda k: pl.ds(k * TS, TS)
    _round_robin([_final_tile(ybuf.at[slot, pl.ds(k * per, per)], x1_ref.at[rows(k)], route_ref.at[rows(k)], ada_ref,
                              lnw_ref, lnb_ref, o_ref.at[rows(k)]) for k in range(P)])

    @pl.when(i == last)
    def _drain():
        for k in range(1, FETCH_SLOTS):
            other = (i + k) % FETCH_SLOTS
            _wait_fetch(ys_ref, ybuf.at[other], sem.at[other])


def _final(fsrc, x1, route, ada, lnw, lnb, ys, seq_len):
    T, D = x1.shape
    P = TILES_PER_STEP
    rows = P * TS
    assert seq_len % rows == 0
    steps_per_seq = seq_len // rows
    n_steps = T // rows
    assert n_steps > FETCH_AHEAD
    grid_spec = pltpu.PrefetchScalarGridSpec(
        num_scalar_prefetch=1,
        grid=(n_steps,),
        in_specs=[pl.BlockSpec((rows, D), lambda i, src: (i, 0)),
                  pl.BlockSpec((rows, LANES), lambda i, src: (i, 0)),
                  pl.BlockSpec((None, 6, D), lambda i, src: (i // steps_per_seq, 0, 0)),
                  pl.BlockSpec((1, D), lambda i, src: (0, 0)),
                  pl.BlockSpec((1, D), lambda i, src: (0, 0)),
                  pl.BlockSpec(memory_space=pl.ANY)],
        out_specs=pl.BlockSpec((rows, D), lambda i, src: (i, 0)),
        scratch_shapes=[pltpu.VMEM((FETCH_SLOTS, P * (TR // SUBLANES), 2 * SUBLANES, HALF), jnp.bfloat16),
                        pltpu.SemaphoreType.DMA((FETCH_SLOTS,))],
    )
    return pl.pallas_call(
        _final_kernel,
        grid_spec=grid_spec,
        out_shape=jax.ShapeDtypeStruct((T, D), jnp.float32),
        compiler_params=pltpu.CompilerParams(dimension_semantics=("arbitrary",), vmem_limit_bytes=VMEM_LIMIT),
        name="final",
    )(fsrc, x1, route, ada, lnw, lnb, ys)


def _gather_tables(cnt, n_pages):
    nt = cnt.shape[0]
    run = (cnt + SUBLANES - 1) // SUBLANES * SUBLANES
    so = jnp.cumsum(run, axis=1) - run
    eo = jnp.cumsum(run, axis=0) - run
    tot = jnp.sum(run, axis=0)
    pages_e = (tot + TM - 1) // TM
    page_end = jnp.cumsum(pages_e)
    page_start = page_end - pages_e
    g = jnp.arange(n_pages, dtype=jnp.int32)
    pe = jnp.minimum(jnp.sum(g[:, None] >= page_end[None, :], axis=1), N_EXPERTS - 1).astype(jnp.int32)
    used = g < page_end[-1]
    owner = (g[None, :] >= page_start[:, None]) & (g[None, :] < page_end[:, None])
    of_page = lambda a: jnp.sum(jnp.where(owner, a[:, None], 0), axis=0)
    of_page2 = lambda a: jnp.sum(jnp.where(owner[:, None, :], a.T[:, :, None], 0), axis=0)
    pj = g - of_page(page_start)
    tot_p = of_page(tot)
    pv = jnp.where(used, jnp.clip(tot_p - pj * TM, 0, TM), 0).astype(jnp.int32)
    experts = jnp.arange(N_EXPERTS, dtype=jnp.int32)
    has_pages = pages_e > 0
    wslot_e = (jnp.cumsum(has_pages) - 1) % 2
    later = (experts[None, :] > experts[:, None]) & has_pages[None, :]
    next_e = jnp.min(jnp.where(later, experts[None, :], N_EXPERTS), axis=1)
    next_e = jnp.where(next_e < N_EXPERTS, next_e, -1)
    first = (used & (pj == 0)).astype(jnp.int32)
    page_tables = (page_end[-1:].astype(jnp.int32), pe, first, of_page(wslot_e).astype(jnp.int32),
                   jnp.where(used, of_page(next_e), -1).astype(jnp.int32))

    q = (pj * TM)[:, None] + SUBLANES * jnp.arange(TM // SUBLANES, dtype=jnp.int32)[None, :]
    tiles = jnp.arange(nt, dtype=jnp.int32)
    offset = of_page2(tiles[:, None] * TR + so - eo)
    step = jnp.concatenate([offset[1:] - offset[:-1], jnp.zeros_like(offset[:1])], axis=0)
    run_end = of_page2(eo + run)
    src = q + offset[0][:, None] + jnp.sum(jnp.where(run_end[:, :, None] <= q[None], step[:, :, None], 0), axis=0)
    zero_xt = TR - SUBLANES
    esrc = jnp.where((q < tot_p[:, None]) & used[:, None], src, zero_xt).astype(jnp.int32)

    r = SUBLANES * jnp.arange(TR // SUBLANES, dtype=jnp.int32)
    end = (so + run).T
    offset_f = ((page_start * TM)[None, :] + eo - so).T
    step_f = jnp.concatenate([offset_f[1:] - offset_f[:-1], jnp.zeros_like(offset_f[:1])], axis=0)
    srcf = (r[None, :] + offset_f[0][:, None]
            + jnp.sum(jnp.where(end[:, :, None] <= r[None, None, :], step_f[:, :, None], 0), axis=0))
    zero_ys = (n_pages - 1) * TM
    fsrc = jnp.where(r[None, :] < end[-1][:, None], srcf, zero_ys).astype(jnp.int32)
    return page_tables, esrc.reshape(-1) // SUBLANES, fsrc.reshape(-1) // SUBLANES


def kernel(x, c, positions, w_ada, b_ada, w_in, ret_gn_w, w_pool, pool_scale, w_out, ln1_w, ln1_b, w_group, b_group,
           w_router, b_router, w1, w3, w2, ln2_w, ln2_b):
    B, S, D = x.shape
    T = B * S
    assert w_ada.shape[0] == DEPTH and D == D_MODEL and S % TS == 0

    inv_freq = ROPE_BASE ** (-jnp.arange(0, HEAD_DIM, 2, dtype=jnp.float32) / HEAD_DIM)
    half = HEAD_DIM // 2
    rope = jnp.stack([jnp.concatenate([inv_freq, inv_freq]),
                      jnp.concatenate([-jnp.ones((half,), jnp.float32), jnp.ones((half,), jnp.float32)])])

    nt = T // TS
    n_pages = (2 * T + nt * N_EXPERTS * (SUBLANES - 1)) // TM + N_EXPERTS + 1

    for l in range(DEPTH):
        ada = _ada(c, w_ada[l], b_ada[l]).reshape(B, 6, D)
        pad = SUBLANES - N_GROUPS
        wrt = jnp.concatenate([w_group[l].T, jnp.zeros((pad, D), jnp.float32), w_router[l].T], axis=0)
        brt = jnp.concatenate([b_group[l], jnp.zeros((pad,), jnp.float32), b_router[l]]).reshape(ROUTE_ROWS, 1)
        x1, xt, route, counts = _mix(
            x, positions, ada, rope, _bf(w_in[l]), _bf(w_out[l]), _bf(w_pool[l]),
            ret_gn_w[l].reshape(1, RET_WIDTH), pool_scale[l].reshape(1, POOL_WIDTH),
            ln1_w[l].reshape(1, D), ln1_b[l].reshape(1, D), wrt, brt)

        cnt = counts[:, :, :, 0].reshape(nt, N_EXPERTS).astype(jnp.int32)
        page_tables, esrc, fsrc = _gather_tables(cnt, n_pages)
        ys = _experts(page_tables, esrc, xt.reshape(nt * TR // SUBLANES, 2 * SUBLANES, HALF),
                      w1[l].reshape(N_EXPERTS, D, D_EXPERT), w3[l].reshape(N_EXPERTS, D, D_EXPERT),
                      w2[l].reshape(N_EXPERTS, D_EXPERT, D))
        x = _final(fsrc, x1.reshape(T, D), route.reshape(T, LANES), ada,
                   ln2_w[l].reshape(1, D), ln2_b[l].reshape(1, D),
                   ys.reshape(n_pages * TM // SUBLANES, 2 * SUBLANES, HALF), S).reshape(B, S, D)
    return x
```

```python
import math

import jax
import jax.numpy as jnp
from jax import lax
from jax.experimental import pallas as pl
from jax.experimental.pallas import tpu as pltpu

D_MODEL = 1024
RET_WIDTH = 512
RET_HEADS = 4
HEAD_DIM = 128
POOL_WIDTH = 512
POOL_WINDOWS = (2, 4, 8, 16)
POOL_GROUP_DIM = 128
IN_COLS = 4 * RET_WIDTH + POOL_WIDTH
N_GROUPS = 4
EXPERTS_PER_GROUP = 8
N_EXPERTS = 32
D_EXPERT = 256
DEPTH = 1
ALPHA = (2.0 * DEPTH) ** 0.25
LN_EPS = 1e-5
ROPE_BASE = 10000.0

LANES = 128
SUBLANES = 8
HALF = D_MODEL // 2
POOL_HALO = 16
TS = 256
TM = 512
PAGE_SPLIT_ROWS = 256
TR = -(-(2 * TS + N_EXPERTS * (SUBLANES - 1)) // LANES) * LANES
SEQS_PER_STEP = 2
TILES_PER_STEP = 2
ROUTE_ROWS = SUBLANES + N_EXPERTS
FETCH_AHEAD = 2
FETCH_SLOTS = FETCH_AHEAD + 1
VMEM_LIMIT = 56 * 1024 * 1024

_LOG_GAMMA = tuple(math.log1p(-(2.0 ** (-5.0 - h))) for h in range(RET_HEADS))
_HI = lax.Precision.HIGHEST


def _ln(x):
    mu = jnp.mean(x, axis=-1, keepdims=True)
    xc = x - mu
    var = jnp.mean(xc * xc, axis=-1, keepdims=True)
    return xc * lax.rsqrt(var + LN_EPS)


def _bf(x):
    return x.astype(jnp.bfloat16)


def _pack_rows(x):
    n = x.shape[0]
    lo = x[:, :HALF].reshape(n // SUBLANES, SUBLANES, HALF)
    hi = x[:, HALF:].reshape(n // SUBLANES, SUBLANES, HALF)
    return _bf(jnp.concatenate([lo, hi], axis=1).reshape(2 * n, HALF))


def _unpack_rows(z):
    n = z.shape[0] // 2
    zf = z.astype(jnp.float32).reshape(n // SUBLANES, 2 * SUBLANES, HALF)
    return _bf(zf[:, :SUBLANES, :].reshape(n, HALF)), _bf(zf[:, SUBLANES:, :].reshape(n, HALF))


def _ada_kernel(c_ref, w_ref, b_ref, o_ref):
    c = c_ref[...]
    ca = c * jax.nn.sigmoid(c)
    o_ref[...] = jnp.dot(ca, w_ref[...], precision=_HI, preferred_element_type=jnp.float32) + b_ref[...]


def _ada(c, w_ada, b_ada):
    B, D = c.shape
    n = w_ada.shape[1]
    return pl.pallas_call(
        _ada_kernel,
        grid=(n // D,),
        in_specs=[pl.BlockSpec((B, D), lambda j: (0, 0)),
                  pl.BlockSpec((D, D), lambda j: (0, j)),
                  pl.BlockSpec((1, D), lambda j: (0, j))],
        out_specs=pl.BlockSpec((B, D), lambda j: (0, j)),
        out_shape=jax.ShapeDtypeStruct((B, n), jnp.float32),
        compiler_params=pltpu.CompilerParams(dimension_semantics=("arbitrary",), vmem_limit_bytes=VMEM_LIMIT),
        name="ada",
    )(c, w_ada, b_ada.reshape(1, n))


def _mix_kernel(x_ref, pos_ref, ada_ref, rope_ref, win_ref, wout_ref, wpool_ref, gnw_ref, pscale_ref,
                ln1w_ref, ln1b_ref, wrt_ref, brt_ref,
                x1_ref, xt_ref, route_ref, cnt_ref,
                state_ref, halo_ref, dmat_ref, qdec_ref, kdec_ref, cat_ref):
    b = pl.program_id(0)
    s = pl.program_id(1)

    @pl.when((b == 0) & (s == 0))
    def _init_tables():
        ri = lax.broadcasted_iota(jnp.int32, (TS, TS), 0)
        ci = lax.broadcasted_iota(jnp.int32, (TS, TS), 1)
        rel = (ri - ci).astype(jnp.float32)
        for h in range(RET_HEADS):
            dmat_ref[h] = jnp.where(rel >= 0.0, jnp.exp(jnp.maximum(rel, 0.0) * _LOG_GAMMA[h]), 0.0)
        row = lax.broadcasted_iota(jnp.int32, (TS, RET_WIDTH), 0).astype(jnp.float32)
        lane = lax.broadcasted_iota(jnp.int32, (TS, RET_WIDTH), 1)
        lg = jnp.full((TS, RET_WIDTH), _LOG_GAMMA[0], jnp.float32)
        for h in range(1, RET_HEADS):
            lg = jnp.where(lane >= h * HEAD_DIM, _LOG_GAMMA[h], lg)
        qdec_ref[...] = jnp.exp((row + 1.0) * lg)
        kdec_ref[...] = jnp.exp((TS - 1.0 - row) * lg)

    @pl.when(s == 0)
    def _init_carries():
        state_ref[...] = jnp.zeros_like(state_ref)
        halo_ref[...] = jnp.zeros_like(halo_ref)

    tiles = [_mix_tile(s, x_ref.at[j], pos_ref.at[j], ada_ref.at[j], rope_ref, win_ref, wout_ref, wpool_ref, gnw_ref,
                       pscale_ref, ln1w_ref, ln1b_ref, wrt_ref, brt_ref,
                       x1_ref.at[j], xt_ref.at[j], route_ref.at[j], cnt_ref.at[j],
                       state_ref.at[j], halo_ref.at[j], dmat_ref, qdec_ref, kdec_ref, cat_ref.at[j])
             for j in range(SEQS_PER_STEP)]
    _round_robin(tiles)


def _mix_tile(s, x_ref, pos_ref, ada_ref, rope_ref, win_ref, wout_ref, wpool_ref, gnw_ref, pscale_ref,
              ln1w_ref, ln1b_ref, wrt_ref, brt_ref,
              x1_ref, xt_ref, route_ref, cnt_ref,
              state_ref, halo_ref, dmat_ref, qdec_ref, kdec_ref, cat_ref):
    ada = ada_ref[...]
    shift1, scale1, gate1 = ada[0:1], ada[1:2], ada[2:3]
    shift2, scale2 = ada[3:4], ada[4:5]

    x = x_ref[...]
    u = _bf(_ln(x) * (1.0 + scale1) + shift1)
    yield

    posf = pos_ref[...].astype(jnp.float32)
    hl = lax.broadcasted_iota(jnp.int32, (TS // 2, HEAD_DIM), 1) < HEAD_DIM // 2
    ang = jnp.where(hl, posf[:TS // 2], posf[TS // 2:]) * rope_ref[0:1, :]
    cos_p, sin_p = jnp.cos(ang), jnp.sin(ang)
    cos_s, sin_s = pltpu.roll(cos_p, HEAD_DIM // 2, 1), pltpu.roll(sin_p, HEAD_DIM // 2, 1)
    cos_t = jnp.concatenate([jnp.where(hl, cos_p, cos_s), jnp.where(hl, cos_s, cos_p)], axis=0)
    sin_t = jnp.concatenate([jnp.where(hl, sin_p, sin_s), jnp.where(hl, sin_s, sin_p)], axis=0) * rope_ref[1:2, :]
    yield

    q = jnp.dot(u, win_ref[:, 0:RET_WIDTH], preferred_element_type=jnp.float32)
    k = jnp.dot(u, win_ref[:, RET_WIDTH:2 * RET_WIDTH], preferred_element_type=jnp.float32)
    v = jnp.dot(u, win_ref[:, 2 * RET_WIDTH:3 * RET_WIDTH], preferred_element_type=jnp.float32)
    g = jnp.dot(u, win_ref[:, 3 * RET_WIDTH:4 * RET_WIDTH], preferred_element_type=jnp.float32)
    p = jnp.dot(u, win_ref[:, 4 * RET_WIDTH:IN_COLS], preferred_element_type=jnp.float32)
    yield

    gnw = gnw_ref[...]
    for h in range(RET_HEADS):
        sl = slice(h * HEAD_DIM, (h + 1) * HEAD_DIM)
        qh, kh, vh = q[:, sl], k[:, sl], v[:, sl]
        qr = qh * cos_t + pltpu.roll(qh, HEAD_DIM // 2, 1) * sin_t
        kr = (kh * cos_t + pltpu.roll(kh, HEAD_DIM // 2, 1) * sin_t) * (HEAD_DIM ** -0.5)
        vb = _bf(vh)
        sc = lax.dot_general(_bf(qr), _bf(kr), (((1,), (1,)), ((), ())), preferred_element_type=jnp.float32)
        intra = jnp.dot(_bf(sc * dmat_ref[h]), vb, preferred_element_type=jnp.float32)
        st = state_ref[h]
        cross = jnp.dot(_bf(qr * qdec_ref[:, sl]), _bf(st), preferred_element_type=jnp.float32)
        kv = lax.dot_general(_bf(kr * kdec_ref[:, sl]), vb, (((0,), (0,)), ((), ())),
                             preferred_element_type=jnp.float32)
        state_ref[h] = st * math.exp(TS * _LOG_GAMMA[h]) + kv
        r = _ln(intra + cross) * gnw[:, sl]
        gh = g[:, sl]
        cat_ref[:, sl] = _bf(gh * jax.nn.sigmoid(gh) * r)
        yield

    pext = jnp.concatenate([halo_ref[...], p], axis=0)
    halo_ref[...] = p[TS - POOL_HALO:, :]
    t_abs = (s * TS + lax.broadcasted_iota(jnp.int32, (TS, 1), 0) + 1).astype(jnp.float32)
    pscale = pscale_ref[...]
    for grp, w in enumerate(POOL_WINDOWS):
        sl = slice(grp * POOL_GROUP_DIM, (grp + 1) * POOL_GROUP_DIM)
        acc = pext[:, sl]
        shift = 1
        while shift < w:
            acc = acc + pltpu.roll(acc, shift, 0)
            shift *= 2
        pooled = acc[POOL_HALO:, :] / jnp.minimum(t_abs, float(w)) - p[:, sl]
        po = jnp.dot(_bf(pooled), wpool_ref[grp], preferred_element_type=jnp.float32) * pscale[:, sl]
        cat_ref[:, RET_WIDTH + grp * POOL_GROUP_DIM:RET_WIDTH + (grp + 1) * POOL_GROUP_DIM] = _bf(po)
    yield

    mix = jnp.dot(cat_ref[...], wout_ref[...], preferred_element_type=jnp.float32)
    yield
    x1 = _ln(ALPHA * x + gate1 * mix) * ln1w_ref[...] + ln1b_ref[...]
    x1_ref[...] = x1
    u2 = _ln(x1) * (1.0 + scale2) + shift2
    yield

    w = wrt_ref[...]
    w_hi = _bf(w)
    w_lo = _bf(w - w_hi.astype(jnp.float32))
    u2_hi = _bf(u2)
    u2_lo = _bf(u2 - u2_hi.astype(jnp.float32))
    nt = (((1,), (1,)), ((), ()))
    logits = (lax.dot_general(w_hi, u2_hi, nt, preferred_element_type=jnp.float32)
              + lax.dot_general(w_hi, u2_lo, nt, preferred_element_type=jnp.float32)
              + lax.dot_general(w_lo, u2_hi, nt, preferred_element_type=jnp.float32)) + brt_ref[...]
    row8 = lax.broadcasted_iota(jnp.int32, (SUBLANES, TS), 0)
    neg = jnp.float32(-jnp.inf)
    gl = jnp.where(row8 < N_GROUPS, logits[0:SUBLANES], neg)
    gmax = jnp.max(gl, axis=0, keepdims=True)
    gidx = jnp.min(jnp.where(gl == gmax, row8, SUBLANES), axis=0, keepdims=True)
    gprob = 1.0 / jnp.sum(jnp.exp(gl - gmax), axis=0, keepdims=True)
    el = logits[SUBLANES:2 * SUBLANES]
    for grp in range(1, N_GROUPS):
        el = jnp.where(gidx == grp, logits[(grp + 1) * SUBLANES:(grp + 2) * SUBLANES], el)
    m1 = jnp.max(el, axis=0, keepdims=True)
    j1 = jnp.min(jnp.where(el == m1, row8, SUBLANES), axis=0, keepdims=True)
    el2 = jnp.where(row8 == j1, neg, el)
    m2 = jnp.max(el2, axis=0, keepdims=True)
    j2 = jnp.min(jnp.where(el2 == m2, row8, SUBLANES), axis=0, keepdims=True)
    e21 = jnp.exp(m2 - m1)
    den = 1.0 / (1.0 + e21)
    cw1 = gprob * den
    cw2 = gprob * e21 * den
    yield

    erow = lax.broadcasted_iota(jnp.int32, (N_EXPERTS, TS), 0)
    oh1 = erow == gidx * EXPERTS_PER_GROUP + j1
    oh2 = erow == gidx * EXPERTS_PER_GROUP + j2
    oh = jnp.where(oh1 | oh2, 1.0, 0.0)
    cnt = jnp.broadcast_to(jnp.sum(oh, axis=1, keepdims=True), (N_EXPERTS, LANES))
    run = jnp.floor((cnt + (SUBLANES - 1.0)) * (1.0 / SUBLANES)) * SUBLANES
    erow_l = lax.broadcasted_iota(jnp.int32, (N_EXPERTS, LANES), 0)
    run_end = run
    shift = 1
    while shift < N_EXPERTS:
        run_end = run_end + jnp.where(erow_l >= shift, pltpu.roll(run_end, shift, 0), 0.0)
        shift *= 2
    run_start = (run_end - run)[:, 0:1]
    ri = lax.broadcasted_iota(jnp.int32, (TS, TS), 0)
    ci = lax.broadcasted_iota(jnp.int32, (TS, TS), 1)
    earlier = _bf(jnp.where(ri < ci, 1.0, 0.0))
    before = jnp.dot(_bf(oh), earlier, preferred_element_type=jnp.float32) + run_start
    pos1 = jnp.sum(jnp.where(oh1, before, 0.0), axis=0, keepdims=True)
    pos2 = jnp.sum(jnp.where(oh2, before, 0.0), axis=0, keepdims=True)
    rr = lax.broadcasted_iota(jnp.int32, (TR, TS), 0).astype(jnp.float32)
    perm = _bf(jnp.where((rr == pos1) | (rr == pos2), 1.0, 0.0))
    yield
    xt_ref[...] = _pack_rows(jnp.dot(perm, u2_hi, preferred_element_type=jnp.float32))
    cnt_ref[...] = cnt

    rowl = lax.broadcasted_iota(jnp.int32, (LANES, TS), 0)
    rec = jnp.where(rowl == 0, pos1, 0.0)
    rec = jnp.where(rowl == 1, pos2, rec)
    rec = jnp.where(rowl == 2, cw1, rec)
    rec = jnp.where(rowl == 3, cw2, rec)
    route_ref[...] = rec.T


def _mix(x, positions, ada, rope, win, wout, wpool, gnw, pscale, ln1w, ln1b, wrt, brt):
    B, S, D = x.shape
    ns = S // TS
    assert B % SEQS_PER_STEP == 0
    P = SEQS_PER_STEP
    const2 = lambda b, s: (0, 0)
    const3 = lambda b, s: (0, 0, 0)
    tile = lambda b, s: (b, s, 0)
    flat = lambda b, s: (b, s, 0, 0)
    return pl.pallas_call(
        _mix_kernel,
        grid=(B // P, ns),
        in_specs=[
            pl.BlockSpec((P, TS, D), tile),
            pl.BlockSpec((P, TS, 1), tile),
            pl.BlockSpec((P, 6, D), lambda b, s: (b, 0, 0)),
            pl.BlockSpec((2, LANES), const2),
            pl.BlockSpec((D, IN_COLS), const2),
            pl.BlockSpec((D, D), const2),
            pl.BlockSpec((len(POOL_WINDOWS), POOL_GROUP_DIM, POOL_GROUP_DIM), const3),
            pl.BlockSpec((1, RET_WIDTH), const2),
            pl.BlockSpec((1, POOL_WIDTH), const2),
            pl.BlockSpec((1, D), const2),
            pl.BlockSpec((1, D), const2),
            pl.BlockSpec((ROUTE_ROWS, D), const2),
            pl.BlockSpec((ROUTE_ROWS, 1), const2),
        ],
        out_specs=[
            pl.BlockSpec((P, TS, D), tile),
            pl.BlockSpec((P, None, 2 * TR, HALF), flat),
            pl.BlockSpec((P, TS, LANES), tile),
            pl.BlockSpec((P, None, N_EXPERTS, LANES), flat),
        ],
        out_shape=[
            jax.ShapeDtypeStruct((B, S, D), jnp.float32),
            jax.ShapeDtypeStruct((B, ns, 2 * TR, HALF), jnp.bfloat16),
            jax.ShapeDtypeStruct((B, S, LANES), jnp.float32),
            jax.ShapeDtypeStruct((B, ns, N_EXPERTS, LANES), jnp.float32),
        ],
        scratch_shapes=[
            pltpu.VMEM((P, RET_HEADS, HEAD_DIM, HEAD_DIM), jnp.float32),
            pltpu.VMEM((P, POOL_HALO, POOL_WIDTH), jnp.float32),
            pltpu.VMEM((RET_HEADS, TS, TS), jnp.float32),
            pltpu.VMEM((TS, RET_WIDTH), jnp.float32),
            pltpu.VMEM((TS, RET_WIDTH), jnp.float32),
            pltpu.VMEM((P, TS, D), jnp.bfloat16),
        ],
        compiler_params=pltpu.CompilerParams(dimension_semantics=("arbitrary", "arbitrary"),
                                             vmem_limit_bytes=VMEM_LIMIT),
        name="mix",
    )(x, positions.reshape(B, S, 1), ada, rope, win, wout, wpool, gnw, pscale, ln1w, ln1b, wrt, brt)


def _fetch_groups(src_ref, group_of, dst_ref, sem):
    for u in range(dst_ref.shape[0]):
        pltpu.make_async_copy(src_ref.at[group_of(u)], dst_ref.at[u], sem).start()


def _wait_fetch(src_ref, dst_ref, sem):
    pltpu.make_async_copy(src_ref.at[pl.ds(0, dst_ref.shape[0])], dst_ref, sem).wait()


def _expert_rows(x_ref, w1_ref, w3_ref, w2_ref, y_ref):
    n = PAGE_SPLIT_ROWS
    xl, xh = _unpack_rows(x_ref[...].reshape(2 * n, HALF))
    yield
    a = (jnp.dot(xl, w1_ref[:HALF, :], preferred_element_type=jnp.float32)
         + jnp.dot(xh, w1_ref[HALF:, :], preferred_element_type=jnp.float32))
    c = (jnp.dot(xl, w3_ref[:HALF, :], preferred_element_type=jnp.float32)
         + jnp.dot(xh, w3_ref[HALF:, :], preferred_element_type=jnp.float32))
    yield
    h = _bf(a * jax.nn.sigmoid(a) * c)
    yield
    y = jnp.dot(h, w2_ref[...], preferred_element_type=jnp.float32)
    yield
    y_ref[...] = _pack_rows(y)


def _round_robin(tasks):
    while tasks:
        tasks = [t for t in tasks if next(t, True) is None]


def _expert_kernel(used_ref, pe_ref, pv_ref, first_ref, wslot_ref, next_ref, src_ref, xt_ref, w1_hbm, w3_hbm, w2_hbm,
                   ys_ref, xbuf, ybuf, w1_buf, w3_buf, w2_buf, w1_bf, w3_bf, w2_bf, sem, osem, wsem):
    n_used = used_ref[0]
    n_pages = ys_ref.shape[0] // (2 * TM)
    groups = TM // SUBLANES

    def fetch(page):
        @pl.when(page < n_used)
        def _start():
            slot = page % FETCH_SLOTS
            _fetch_groups(xt_ref, lambda u: src_ref[page * groups + u], xbuf.at[slot], sem.at[slot])

    def weight_copies(expert, ws):
        return [pltpu.make_async_copy(hbm.at[expert], buf.at[ws], wsem.at[ws])
                for hbm, buf in ((w1_hbm, w1_buf), (w3_hbm, w3_buf), (w2_hbm, w2_buf))]

    def page_out(page, oslot):
        return pltpu.make_async_copy(ybuf.at[oslot], ys_ref.at[pl.ds(pl.multiple_of(page * 2 * TM, 2 * TM), 2 * TM)],
                                     osem.at[oslot])

    for k in range(FETCH_AHEAD):
        fetch(k)

    @pl.when(n_used > 0)
    def _first_weights():
        for c in weight_copies(pe_ref[0], wslot_ref[0]):
            c.start()

    def page_body(g, carry):
        fetch(g + FETCH_AHEAD)
        ws = wslot_ref[g]

        @pl.when(first_ref[g] == 1)
        def _new_expert():
            for c in weight_copies(pe_ref[g], ws):
                c.wait()
            w1_bf[...] = _bf(w1_buf[ws])
            w3_bf[...] = _bf(w3_buf[ws])
            w2_bf[...] = _bf(w2_buf[ws])

            @pl.when(next_ref[g] >= 0)
            def _stream_next():
                for c in weight_copies(next_ref[g], 1 - ws):
                    c.start()

        slot = g % FETCH_SLOTS
        oslot = g % 2
        _wait_fetch(xt_ref, xbuf.at[slot], sem.at[slot])

        @pl.when(g >= 2)
        def _reuse_out_buffer():
            page_out(g - 2, oslot).wait()

        per = PAGE_SPLIT_ROWS // SUBLANES
        parts = TM // PAGE_SPLIT_ROWS
        part_rows = lambda k: pl.ds(k * 2 * PAGE_SPLIT_ROWS, 2 * PAGE_SPLIT_ROWS)
        part = lambda k: _expert_rows(xbuf.at[slot, pl.ds(k * per, per)], w1_bf, w3_bf, w2_bf,
                                      ybuf.at[oslot, part_rows(k)])
        n_parts = (pv_ref[g] + PAGE_SPLIT_ROWS - 1) // PAGE_SPLIT_ROWS
        for n in range(1, parts + 1):
            @pl.when(n_parts == n)
            def _parts(n=n):
                _round_robin([part(k) for k in range(n)])
                for k in range(n, parts):
                    ybuf[oslot, part_rows(k)] = jnp.zeros((2 * PAGE_SPLIT_ROWS, HALF), ybuf.dtype)
        page_out(g, oslot).start()
        return carry

    lax.fori_loop(0, n_used, page_body, 0)

    for back in (2, 1):
        @pl.when(n_used >= back)
        def _drain(back=back):
            page_out(n_used - back, (n_used - back) % 2).wait()

    ybuf[0] = jnp.zeros(ybuf.shape[1:], ybuf.dtype)

    def zero_start(page, carry):
        page_out(page, 0).start()
        return carry

    def zero_wait(page, carry):
        page_out(page, 0).wait()
        return carry

    lax.fori_loop(n_used, n_pages, zero_start, 0)
    lax.fori_loop(n_used, n_pages, zero_wait, 0)


def _experts(page_tables, esrc, xt, w1, w3, w2):
    D = D_MODEL
    n_pages = page_tables[1].shape[0]
    assert TM % PAGE_SPLIT_ROWS == 0
    grid_spec = pltpu.PrefetchScalarGridSpec(
        num_scalar_prefetch=len(page_tables) + 1,
        grid=(1,),
        in_specs=[pl.BlockSpec(memory_space=pl.ANY)] * 4,
        out_specs=pl.BlockSpec(memory_space=pl.ANY),
        scratch_shapes=[pltpu.VMEM((FETCH_SLOTS, TM // SUBLANES, 2 * SUBLANES, HALF), jnp.bfloat16),
                        pltpu.VMEM((2, 2 * TM, HALF), jnp.bfloat16),
                        pltpu.VMEM((2, D, D_EXPERT), jnp.float32),
                        pltpu.VMEM((2, D, D_EXPERT), jnp.float32),
                        pltpu.VMEM((2, D_EXPERT, D), jnp.float32),
                        pltpu.VMEM((D, D_EXPERT), jnp.bfloat16),
                        pltpu.VMEM((D, D_EXPERT), jnp.bfloat16),
                        pltpu.VMEM((D_EXPERT, D), jnp.bfloat16),
                        pltpu.SemaphoreType.DMA((FETCH_SLOTS,)),
                        pltpu.SemaphoreType.DMA((2,)),
                        pltpu.SemaphoreType.DMA((2,))],
    )
    return pl.pallas_call(
        _expert_kernel,
        grid_spec=grid_spec,
        out_shape=jax.ShapeDtypeStruct((n_pages * 2 * TM, HALF), jnp.bfloat16),
        compiler_params=pltpu.CompilerParams(dimension_semantics=("arbitrary",), vmem_limit_bytes=VMEM_LIMIT),
        name="experts",
    )(*page_tables, esrc, xt, w1, w3, w2)


def _final_tile(y_ref, x1_ref, route_ref, ada_ref, lnw_ref, lnb_ref, o_ref):
    route = route_ref[...]
    pos1, pos2, cw1, cw2 = route[:, 0:1], route[:, 1:2], route[:, 2:3], route[:, 3:4]
    col = lax.broadcasted_iota(jnp.int32, (TS, TR), 1).astype(jnp.float32)
    wmat = _bf(jnp.where(col == pos1, cw1, 0.0) + jnp.where(col == pos2, cw2, 0.0))
    yield
    yl, yh = _unpack_rows(y_ref[...].reshape(2 * TR, HALF))
    yield
    y = jnp.concatenate([jnp.dot(wmat, yl, preferred_element_type=jnp.float32),
                         jnp.dot(wmat, yh, preferred_element_type=jnp.float32)], axis=-1)
    yield
    gate2 = ada_ref[5:6, :]
    o_ref[...] = _ln(ALPHA * x1_ref[...] + gate2 * y) * lnw_ref[...] + lnb_ref[...]


def _final_kernel(src_ref, x1_ref, route_ref, ada_ref, lnw_ref, lnb_ref, ys_ref, o_ref, ybuf, sem):
    P = TILES_PER_STEP
    i = pl.program_id(0)
    last = pl.num_programs(0) - 1
    per = TR // SUBLANES
    groups = P * per

    def fetch(step, dst_slot):
        _fetch_groups(ys_ref, lambda u: src_ref[step * groups + u], ybuf.at[dst_slot], sem.at[dst_slot])

    @pl.when(i == 0)
    def _prime():
        for k in range(FETCH_AHEAD):
            fetch(k, k)

    fetch(jnp.minimum(i + FETCH_AHEAD, last), (i + FETCH_AHEAD) % FETCH_SLOTS)
    slot = i % FETCH_SLOTS
    _wait_fetch(ys_ref, ybuf.at[slot], sem.at[slot])

    rows = lambda k: pl.ds(k * TS, TS)
    _round_robin([_final_tile(ybuf.at[slot, pl.ds(k * per, per)], x1_ref.at[rows(k)], route_ref.at[rows(k)], ada_ref,
                              lnw_ref, lnb_ref, o_ref.at[rows(k)]) for k in range(P)])

    @pl.when(i == last)
    def _drain():
        for k in range(1, FETCH_SLOTS):
            other = (i + k) % FETCH_SLOTS
            _wait_fetch(ys_ref, ybuf.at[other], sem.at[other])


def _final(fsrc, x1, route, ada, lnw, lnb, ys, seq_len):
    T, D = x1.shape
    P = TILES_PER_STEP
    rows = P * TS
    assert seq_len % rows == 0
    steps_per_seq = seq_len // rows
    n_steps = T // rows
    assert n_steps > FETCH_AHEAD
    grid_spec = pltpu.PrefetchScalarGridSpec(
        num_scalar_prefetch=1,
        grid=(n_steps,),
        in_specs=[pl.BlockSpec((rows, D), lambda i, src: (i, 0)),
                  pl.BlockSpec((rows, LANES), lambda i, src: (i, 0)),
                  pl.BlockSpec((None, 6, D), lambda i, src: (i // steps_per_seq, 0, 0)),
                  pl.BlockSpec((1, D), lambda i, src: (0, 0)),
                  pl.BlockSpec((1, D), lambda i, src: (0, 0)),
                  pl.BlockSpec(memory_space=pl.ANY)],
        out_specs=pl.BlockSpec((rows, D), lambda i, src: (i, 0)),
        scratch_shapes=[pltpu.VMEM((FETCH_SLOTS, P * (TR // SUBLANES), 2 * SUBLANES, HALF), jnp.bfloat16),
                        pltpu.SemaphoreType.DMA((FETCH_SLOTS,))],
    )
    return pl.pallas_call(
        _final_kernel,
        grid_spec=grid_spec,
        out_shape=jax.ShapeDtypeStruct((T, D), jnp.float32),
        compiler_params=pltpu.CompilerParams(dimension_semantics=("arbitrary",), vmem_limit_bytes=VMEM_LIMIT),
        name="final",
    )(fsrc, x1, route, ada, lnw, lnb, ys)


def _gather_tables(cnt, n_pages):
    nt = cnt.shape[0]
    run = (cnt + SUBLANES - 1) // SUBLANES * SUBLANES
    so = jnp.cumsum(run, axis=1) - run
    eo = jnp.cumsum(run, axis=0) - run
    tot = jnp.sum(run, axis=0)
    pages_e = (tot + TM - 1) // TM
    page_end = jnp.cumsum(pages_e)
    page_start = page_end - pages_e
    g = jnp.arange(n_pages, dtype=jnp.int32)
    pe = jnp.minimum(jnp.sum(g[:, None] >= page_end[None, :], axis=1), N_EXPERTS - 1).astype(jnp.int32)
    used = g < page_end[-1]
    owner = (g[None, :] >= page_start[:, None]) & (g[None, :] < page_end[:, None])
    of_page = lambda a: jnp.sum(jnp.where(owner, a[:, None], 0), axis=0)
    of_page2 = lambda a: jnp.sum(jnp.where(owner[:, None, :], a.T[:, :, None], 0), axis=0)
    pj = g - of_page(page_start)
    tot_p = of_page(tot)
    pv = jnp.where(used, jnp.clip(tot_p - pj * TM, 0, TM), 0).astype(jnp.int32)
    experts = jnp.arange(N_EXPERTS, dtype=jnp.int32)
    has_pages = pages_e > 0
    wslot_e = (jnp.cumsum(has_pages) - 1) % 2
    later = (experts[None, :] > experts[:, None]) & has_pages[None, :]
    next_e = jnp.min(jnp.where(later, experts[None, :], N_EXPERTS), axis=1)
    next_e = jnp.where(next_e < N_EXPERTS, next_e, -1)
    first = (used & (pj == 0)).astype(jnp.int32)
    page_tables = (page_end[-1:].astype(jnp.int32), pe, pv, first, of_page(wslot_e).astype(jnp.int32),
                   jnp.where(used, of_page(next_e), -1).astype(jnp.int32))

    q = (pj * TM)[:, None] + SUBLANES * jnp.arange(TM // SUBLANES, dtype=jnp.int32)[None, :]
    tiles = jnp.arange(nt, dtype=jnp.int32)
    offset = of_page2(tiles[:, None] * TR + so - eo)
    step = jnp.concatenate([offset[1:] - offset[:-1], jnp.zeros_like(offset[:1])], axis=0)
    run_end = of_page2(eo + run)
    src = q + offset[0][:, None] + jnp.sum(jnp.where(run_end[:, :, None] <= q[None], step[:, :, None], 0), axis=0)
    zero_xt = TR - SUBLANES
    esrc = jnp.where((q < tot_p[:, None]) & used[:, None], src, zero_xt).astype(jnp.int32)

    r = SUBLANES * jnp.arange(TR // SUBLANES, dtype=jnp.int32)
    end = (so + run).T
    offset_f = ((page_start * TM)[None, :] + eo - so).T
    step_f = jnp.concatenate([offset_f[1:] - offset_f[:-1], jnp.zeros_like(offset_f[:1])], axis=0)
    srcf = (r[None, :] + offset_f[0][:, None]
            + jnp.sum(jnp.where(end[:, :, None] <= r[None, None, :], step_f[:, :, None], 0), axis=0))
    zero_ys = (n_pages - 1) * TM
    fsrc = jnp.where(r[None, :] < end[-1][:, None], srcf, zero_ys).astype(jnp.int32)
    return page_tables, esrc.reshape(-1) // SUBLANES, fsrc.reshape(-1) // SUBLANES


def kernel(x, c, positions, w_ada, b_ada, w_in, ret_gn_w, w_pool, pool_scale, w_out, ln1_w, ln1_b, w_group, b_group,
           w_router, b_router, w1, w3, w2, ln2_w, ln2_b):
    B, S, D = x.shape
    T = B * S
    assert w_ada.shape[0] == DEPTH and D == D_MODEL and S % TS == 0

    inv_freq = ROPE_BASE ** (-jnp.arange(0, HEAD_DIM, 2, dtype=jnp.float32) / HEAD_DIM)
    half = HEAD_DIM // 2
    rope = jnp.stack([jnp.concatenate([inv_freq, inv_freq]),
                      jnp.concatenate([-jnp.ones((half,), jnp.float32), jnp.ones((half,), jnp.float32)])])

    nt = T // TS
    n_pages = (2 * T + nt * N_EXPERTS * (SUBLANES - 1)) // TM + N_EXPERTS + 1

    for l in range(DEPTH):
        ada = _ada(c, w_ada[l], b_ada[l]).reshape(B, 6, D)
        pad = SUBLANES - N_GROUPS
        wrt = jnp.concatenate([w_group[l].T, jnp.zeros((pad, D), jnp.float32), w_router[l].T], axis=0)
        brt = jnp.concatenate([b_group[l], jnp.zeros((pad,), jnp.float32), b_router[l]]).reshape(ROUTE_ROWS, 1)
        x1, xt, route, counts = _mix(
            x, positions, ada, rope, _bf(w_in[l]), _bf(w_out[l]), _bf(w_pool[l]),
            ret_gn_w[l].reshape(1, RET_WIDTH), pool_scale[l].reshape(1, POOL_WIDTH),
            ln1_w[l].reshape(1, D), ln1_b[l].reshape(1, D), wrt, brt)

        cnt = counts[:, :, :, 0].reshape(nt, N_EXPERTS).astype(jnp.int32)
        page_tables, esrc, fsrc = _gather_tables(cnt, n_pages)
        ys = _experts(page_tables, esrc, xt.reshape(nt * TR // SUBLANES, 2 * SUBLANES, HALF),
                      w1[l].reshape(N_EXPERTS, D, D_EXPERT), w3[l].reshape(N_EXPERTS, D, D_EXPERT),
                      w2[l].reshape(N_EXPERTS, D_EXPERT, D))
        x = _final(fsrc, x1.reshape(T, D), route.reshape(T, LANES), ada,
                   ln2_w[l].reshape(1, D), ln2_b[l].reshape(1, D),
                   ys.reshape(n_pages * TM // SUBLANES, 2 * SUBLANES, HALF), S).reshape(B, S, D)
    return x
```

```python
import math

import jax
import jax.numpy as jnp
from jax import lax
from jax.experimental import pallas as pl
from jax.experimental.pallas import tpu as pltpu

D_MODEL = 1024
RET_WIDTH = 512
RET_HEADS = 4
HEAD_DIM = 128
POOL_WIDTH = 512
POOL_WINDOWS = (2, 4, 8, 16)
POOL_GROUP_DIM = 128
IN_COLS = 4 * RET_WIDTH + POOL_WIDTH
N_GROUPS = 4
EXPERTS_PER_GROUP = 8
N_EXPERTS = 32
D_EXPERT = 256
DEPTH = 1
ALPHA = (2.0 * DEPTH) ** 0.25
LN_EPS = 1e-5
ROPE_BASE = 10000.0

LANES = 128
SUBLANES = 8
HALF = D_MODEL // 2
POOL_HALO = 16
TS = 256
TM = 512
PAGE_SPLIT_ROWS = 256
TR = -(-(2 * TS + N_EXPERTS * (SUBLANES - 1)) // LANES) * LANES
SEQS_PER_STEP = 2
TILES_PER_STEP = 2
ROUTE_ROWS = SUBLANES + N_EXPERTS
FETCH_AHEAD = 2
FETCH_SLOTS = FETCH_AHEAD + 1
FETCH_CHUNK = 64
VMEM_LIMIT = 56 * 1024 * 1024

_LOG_GAMMA = tuple(math.log1p(-(2.0 ** (-5.0 - h))) for h in range(RET_HEADS))
_HI = lax.Precision.HIGHEST


def _ln(x):
    mu = jnp.mean(x, axis=-1, keepdims=True)
    xc = x - mu
    var = jnp.mean(xc * xc, axis=-1, keepdims=True)
    return xc * lax.rsqrt(var + LN_EPS)


def _bf(x):
    return x.astype(jnp.bfloat16)


def _pack_rows(x):
    n = x.shape[0]
    lo = x[:, :HALF].reshape(n // SUBLANES, SUBLANES, HALF)
    hi = x[:, HALF:].reshape(n // SUBLANES, SUBLANES, HALF)
    return _bf(jnp.concatenate([lo, hi], axis=1).reshape(2 * n, HALF))


def _unpack_rows(z):
    n = z.shape[0] // 2
    zf = z.astype(jnp.float32).reshape(n // SUBLANES, 2 * SUBLANES, HALF)
    return _bf(zf[:, :SUBLANES, :].reshape(n, HALF)), _bf(zf[:, SUBLANES:, :].reshape(n, HALF))


def _ada_kernel(c_ref, w_ref, b_ref, o_ref):
    c = c_ref[...]
    ca = c * jax.nn.sigmoid(c)
    o_ref[...] = jnp.dot(ca, w_ref[...], precision=_HI, preferred_element_type=jnp.float32) + b_ref[...]


def _ada(c, w_ada, b_ada):
    B, D = c.shape
    n = w_ada.shape[1]
    return pl.pallas_call(
        _ada_kernel,
        grid=(n // D,),
        in_specs=[pl.BlockSpec((B, D), lambda j: (0, 0)),
                  pl.BlockSpec((D, D), lambda j: (0, j)),
                  pl.BlockSpec((1, D), lambda j: (0, j))],
        out_specs=pl.BlockSpec((B, D), lambda j: (0, j)),
        out_shape=jax.ShapeDtypeStruct((B, n), jnp.float32),
        compiler_params=pltpu.CompilerParams(dimension_semantics=("arbitrary",), vmem_limit_bytes=VMEM_LIMIT),
        name="ada",
    )(c, w_ada, b_ada.reshape(1, n))


def _mix_kernel(x_ref, pos_ref, ada_ref, rope_ref, win_ref, wout_ref, wpool_ref, gnw_ref, pscale_ref,
                ln1w_ref, ln1b_ref, wrt_ref, brt_ref,
                x1_ref, xt_ref, route_ref, cnt_ref,
                state_ref, halo_ref, dmat_ref, qdec_ref, kdec_ref, cat_ref):
    b = pl.program_id(0)
    s = pl.program_id(1)

    @pl.when((b == 0) & (s == 0))
    def _init_tables():
        ri = lax.broadcasted_iota(jnp.int32, (TS, TS), 0)
        ci = lax.broadcasted_iota(jnp.int32, (TS, TS), 1)
        rel = (ri - ci).astype(jnp.float32)
        for h in range(RET_HEADS):
            dmat_ref[h] = jnp.where(rel >= 0.0, jnp.exp(jnp.maximum(rel, 0.0) * _LOG_GAMMA[h]), 0.0)
        row = lax.broadcasted_iota(jnp.int32, (TS, RET_WIDTH), 0).astype(jnp.float32)
        lane = lax.broadcasted_iota(jnp.int32, (TS, RET_WIDTH), 1)
        lg = jnp.full((TS, RET_WIDTH), _LOG_GAMMA[0], jnp.float32)
        for h in range(1, RET_HEADS):
            lg = jnp.where(lane >= h * HEAD_DIM, _LOG_GAMMA[h], lg)
        qdec_ref[...] = jnp.exp((row + 1.0) * lg)
        kdec_ref[...] = jnp.exp((TS - 1.0 - row) * lg)

    @pl.when(s == 0)
    def _init_carries():
        state_ref[...] = jnp.zeros_like(state_ref)
        halo_ref[...] = jnp.zeros_like(halo_ref)

    tiles = [_mix_tile(s, x_ref.at[j], pos_ref.at[j], ada_ref.at[j], rope_ref, win_ref, wout_ref, wpool_ref, gnw_ref,
                       pscale_ref, ln1w_ref, ln1b_ref, wrt_ref, brt_ref,
                       x1_ref.at[j], xt_ref.at[j], route_ref.at[j], cnt_ref.at[j],
                       state_ref.at[j], halo_ref.at[j], dmat_ref, qdec_ref, kdec_ref, cat_ref.at[j])
             for j in range(SEQS_PER_STEP)]
    _round_robin(tiles)


def _mix_tile(s, x_ref, pos_ref, ada_ref, rope_ref, win_ref, wout_ref, wpool_ref, gnw_ref, pscale_ref,
              ln1w_ref, ln1b_ref, wrt_ref, brt_ref,
              x1_ref, xt_ref, route_ref, cnt_ref,
              state_ref, halo_ref, dmat_ref, qdec_ref, kdec_ref, cat_ref):
    ada = ada_ref[...]
    shift1, scale1, gate1 = ada[0:1], ada[1:2], ada[2:3]
    shift2, scale2 = ada[3:4], ada[4:5]

    x = x_ref[...]
    u = _bf(_ln(x) * (1.0 + scale1) + shift1)
    yield

    posf = pos_ref[...].astype(jnp.float32)
    hl = lax.broadcasted_iota(jnp.int32, (TS // 2, HEAD_DIM), 1) < HEAD_DIM // 2
    ang = jnp.where(hl, posf[:TS // 2], posf[TS // 2:]) * rope_ref[0:1, :]
    cos_p, sin_p = jnp.cos(ang), jnp.sin(ang)
    cos_s, sin_s = pltpu.roll(cos_p, HEAD_DIM // 2, 1), pltpu.roll(sin_p, HEAD_DIM // 2, 1)
    cos_t = jnp.concatenate([jnp.where(hl, cos_p, cos_s), jnp.where(hl, cos_s, cos_p)], axis=0)
    sin_t = jnp.concatenate([jnp.where(hl, sin_p, sin_s), jnp.where(hl, sin_s, sin_p)], axis=0) * rope_ref[1:2, :]
    yield

    q = jnp.dot(u, win_ref[:, 0:RET_WIDTH], preferred_element_type=jnp.float32)
    k = jnp.dot(u, win_ref[:, RET_WIDTH:2 * RET_WIDTH], preferred_element_type=jnp.float32)
    v = jnp.dot(u, win_ref[:, 2 * RET_WIDTH:3 * RET_WIDTH], preferred_element_type=jnp.float32)
    g = jnp.dot(u, win_ref[:, 3 * RET_WIDTH:4 * RET_WIDTH], preferred_element_type=jnp.float32)
    p = jnp.dot(u, win_ref[:, 4 * RET_WIDTH:IN_COLS], preferred_element_type=jnp.float32)
    yield

    gnw = gnw_ref[...]
    for h in range(RET_HEADS):
        sl = slice(h * HEAD_DIM, (h + 1) * HEAD_DIM)
        qh, kh, vh = q[:, sl], k[:, sl], v[:, sl]
        qr = qh * cos_t + pltpu.roll(qh, HEAD_DIM // 2, 1) * sin_t
        kr = (kh * cos_t + pltpu.roll(kh, HEAD_DIM // 2, 1) * sin_t) * (HEAD_DIM ** -0.5)
        vb = _bf(vh)
        sc = lax.dot_general(_bf(qr), _bf(kr), (((1,), (1,)), ((), ())), preferred_element_type=jnp.float32)
        intra = jnp.dot(_bf(sc * dmat_ref[h]), vb, preferred_element_type=jnp.float32)
        st = state_ref[h]
        cross = jnp.dot(_bf(qr * qdec_ref[:, sl]), _bf(st), preferred_element_type=jnp.float32)
        kv = lax.dot_general(_bf(kr * kdec_ref[:, sl]), vb, (((0,), (0,)), ((), ())),
                             preferred_element_type=jnp.float32)
        state_ref[h] = st * math.exp(TS * _LOG_GAMMA[h]) + kv
        r = _ln(intra + cross) * gnw[:, sl]
        gh = g[:, sl]
        cat_ref[:, sl] = _bf(gh * jax.nn.sigmoid(gh) * r)
        yield

    pext = jnp.concatenate([halo_ref[...], p], axis=0)
    halo_ref[...] = p[TS - POOL_HALO:, :]
    t_abs = (s * TS + lax.broadcasted_iota(jnp.int32, (TS, 1), 0) + 1).astype(jnp.float32)
    pscale = pscale_ref[...]
    for grp, w in enumerate(POOL_WINDOWS):
        sl = slice(grp * POOL_GROUP_DIM, (grp + 1) * POOL_GROUP_DIM)
        acc = pext[:, sl]
        shift = 1
        while shift < w:
            acc = acc + pltpu.roll(acc, shift, 0)
            shift *= 2
        pooled = acc[POOL_HALO:, :] / jnp.minimum(t_abs, float(w)) - p[:, sl]
        po = jnp.dot(_bf(pooled), wpool_ref[grp], preferred_element_type=jnp.float32) * pscale[:, sl]
        cat_ref[:, RET_WIDTH + grp * POOL_GROUP_DIM:RET_WIDTH + (grp + 1) * POOL_GROUP_DIM] = _bf(po)
    yield

    mix = jnp.dot(cat_ref[...], wout_ref[...], preferred_element_type=jnp.float32)
    yield
    x1 = _ln(ALPHA * x + gate1 * mix) * ln1w_ref[...] + ln1b_ref[...]
    x1_ref[...] = x1
    u2 = _ln(x1) * (1.0 + scale2) + shift2
    yield

    w = wrt_ref[...]
    w_hi = _bf(w)
    w_lo = _bf(w - w_hi.astype(jnp.float32))
    u2_hi = _bf(u2)
    u2_lo = _bf(u2 - u2_hi.astype(jnp.float32))
    nt = (((1,), (1,)), ((), ()))
    logits = (lax.dot_general(w_hi, u2_hi, nt, preferred_element_type=jnp.float32)
              + lax.dot_general(w_hi, u2_lo, nt, preferred_element_type=jnp.float32)
              + lax.dot_general(w_lo, u2_hi, nt, preferred_element_type=jnp.float32)) + brt_ref[...]
    row8 = lax.broadcasted_iota(jnp.int32, (SUBLANES, TS), 0)
    neg = jnp.float32(-jnp.inf)
    gl = jnp.where(row8 < N_GROUPS, logits[0:SUBLANES], neg)
    gmax = jnp.max(gl, axis=0, keepdims=True)
    gidx = jnp.min(jnp.where(gl == gmax, row8, SUBLANES), axis=0, keepdims=True)
    gprob = 1.0 / jnp.sum(jnp.exp(gl - gmax), axis=0, keepdims=True)
    el = logits[SUBLANES:2 * SUBLANES]
    for grp in range(1, N_GROUPS):
        el = jnp.where(gidx == grp, logits[(grp + 1) * SUBLANES:(grp + 2) * SUBLANES], el)
    m1 = jnp.max(el, axis=0, keepdims=True)
    j1 = jnp.min(jnp.where(el == m1, row8, SUBLANES), axis=0, keepdims=True)
    el2 = jnp.where(row8 == j1, neg, el)
    m2 = jnp.max(el2, axis=0, keepdims=True)
    j2 = jnp.min(jnp.where(el2 == m2, row8, SUBLANES), axis=0, keepdims=True)
    e21 = jnp.exp(m2 - m1)
    den = 1.0 / (1.0 + e21)
    cw1 = gprob * den
    cw2 = gprob * e21 * den
    yield

    erow = lax.broadcasted_iota(jnp.int32, (N_EXPERTS, TS), 0)
    oh1 = erow == gidx * EXPERTS_PER_GROUP + j1
    oh2 = erow == gidx * EXPERTS_PER_GROUP + j2
    oh = jnp.where(oh1 | oh2, 1.0, 0.0)
    cnt = jnp.broadcast_to(jnp.sum(oh, axis=1, keepdims=True), (N_EXPERTS, LANES))
    run = jnp.floor((cnt + (SUBLANES - 1.0)) * (1.0 / SUBLANES)) * SUBLANES
    erow_l = lax.broadcasted_iota(jnp.int32, (N_EXPERTS, LANES), 0)
    run_end = run
    shift = 1
    while shift < N_EXPERTS:
        run_end = run_end + jnp.where(erow_l >= shift, pltpu.roll(run_end, shift, 0), 0.0)
        shift *= 2
    run_start = (run_end - run)[:, 0:1]
    ri = lax.broadcasted_iota(jnp.int32, (TS, TS), 0)
    ci = lax.broadcasted_iota(jnp.int32, (TS, TS), 1)
    earlier = _bf(jnp.where(ri < ci, 1.0, 0.0))
    before = jnp.dot(_bf(oh), earlier, preferred_element_type=jnp.float32) + run_start
    pos1 = jnp.sum(jnp.where(oh1, before, 0.0), axis=0, keepdims=True)
    pos2 = jnp.sum(jnp.where(oh2, before, 0.0), axis=0, keepdims=True)
    rr = lax.broadcasted_iota(jnp.int32, (TR, TS), 0).astype(jnp.float32)
    perm = _bf(jnp.where((rr == pos1) | (rr == pos2), 1.0, 0.0))
    yield
    xt_ref[...] = _pack_rows(jnp.dot(perm, u2_hi, preferred_element_type=jnp.float32))
    cnt_ref[...] = cnt

    rowl = lax.broadcasted_iota(jnp.int32, (LANES, TS), 0)
    rec = jnp.where(rowl == 0, pos1, 0.0)
    rec = jnp.where(rowl == 1, pos2, rec)
    rec = jnp.where(rowl == 2, cw1, rec)
    rec = jnp.where(rowl == 3, cw2, rec)
    route_ref[...] = rec.T


def _mix(x, positions, ada, rope, win, wout, wpool, gnw, pscale, ln1w, ln1b, wrt, brt):
    B, S, D = x.shape
    ns = S // TS
    assert B % SEQS_PER_STEP == 0
    P = SEQS_PER_STEP
    const2 = lambda b, s: (0, 0)
    const3 = lambda b, s: (0, 0, 0)
    tile = lambda b, s: (b, s, 0)
    flat = lambda b, s: (b, s, 0, 0)
    return pl.pallas_call(
        _mix_kernel,
        grid=(B // P, ns),
        in_specs=[
            pl.BlockSpec((P, TS, D), tile),
            pl.BlockSpec((P, TS, 1), tile),
            pl.BlockSpec((P, 6, D), lambda b, s: (b, 0, 0)),
            pl.BlockSpec((2, LANES), const2),
            pl.BlockSpec((D, IN_COLS), const2),
            pl.BlockSpec((D, D), const2),
            pl.BlockSpec((len(POOL_WINDOWS), POOL_GROUP_DIM, POOL_GROUP_DIM), const3),
            pl.BlockSpec((1, RET_WIDTH), const2),
            pl.BlockSpec((1, POOL_WIDTH), const2),
            pl.BlockSpec((1, D), const2),
            pl.BlockSpec((1, D), const2),
            pl.BlockSpec((ROUTE_ROWS, D), const2),
            pl.BlockSpec((ROUTE_ROWS, 1), const2),
        ],
        out_specs=[
            pl.BlockSpec((P, TS, D), tile),
            pl.BlockSpec((P, None, 2 * TR, HALF), flat),
            pl.BlockSpec((P, TS, LANES), tile),
            pl.BlockSpec((P, None, N_EXPERTS, LANES), flat),
        ],
        out_shape=[
            jax.ShapeDtypeStruct((B, S, D), jnp.float32),
            jax.ShapeDtypeStruct((B, ns, 2 * TR, HALF), jnp.bfloat16),
            jax.ShapeDtypeStruct((B, S, LANES), jnp.float32),
            jax.ShapeDtypeStruct((B, ns, N_EXPERTS, LANES), jnp.float32),
        ],
        scratch_shapes=[
            pltpu.VMEM((P, RET_HEADS, HEAD_DIM, HEAD_DIM), jnp.float32),
            pltpu.VMEM((P, POOL_HALO, POOL_WIDTH), jnp.float32),
            pltpu.VMEM((RET_HEADS, TS, TS), jnp.float32),
            pltpu.VMEM((TS, RET_WIDTH), jnp.float32),
            pltpu.VMEM((TS, RET_WIDTH), jnp.float32),
            pltpu.VMEM((P, TS, D), jnp.bfloat16),
        ],
        compiler_params=pltpu.CompilerParams(dimension_semantics=("arbitrary", "arbitrary"),
                                             vmem_limit_bytes=VMEM_LIMIT),
        name="mix",
    )(x, positions.reshape(B, S, 1), ada, rope, win, wout, wpool, gnw, pscale, ln1w, ln1b, wrt, brt)


def _fetch_groups(src_ref, group_of, dst_ref, sem):
    for u in range(dst_ref.shape[0]):
        pltpu.make_async_copy(src_ref.at[group_of(u)], dst_ref.at[u], sem).start()


def _fetch_groups_task(src_ref, group_of, dst_ref, sem, chunk):
    for u in range(dst_ref.shape[0]):
        pltpu.make_async_copy(src_ref.at[group_of(u)], dst_ref.at[u], sem).start()
        if u % chunk == chunk - 1:
            yield


def _wait_fetch(src_ref, dst_ref, sem):
    pltpu.make_async_copy(src_ref.at[pl.ds(0, dst_ref.shape[0])], dst_ref, sem).wait()


def _expert_rows(x_ref, w1_ref, w3_ref, w2_ref, y_ref):
    n = PAGE_SPLIT_ROWS
    xl, xh = _unpack_rows(x_ref[...].reshape(2 * n, HALF))
    yield
    a = (jnp.dot(xl, w1_ref[:HALF, :], preferred_element_type=jnp.float32)
         + jnp.dot(xh, w1_ref[HALF:, :], preferred_element_type=jnp.float32))
    c = (jnp.dot(xl, w3_ref[:HALF, :], preferred_element_type=jnp.float32)
         + jnp.dot(xh, w3_ref[HALF:, :], preferred_element_type=jnp.float32))
    yield
    h = _bf(a * jax.nn.sigmoid(a) * c)
    yield
    y = jnp.dot(h, w2_ref[...], preferred_element_type=jnp.float32)
    yield
    y_ref[...] = _pack_rows(y)


def _round_robin(tasks):
    while tasks:
        tasks = [t for t in tasks if next(t, True) is None]


def _expert_kernel(used_ref, pe_ref, pv_ref, first_ref, wslot_ref, next_ref, src_ref, xt_ref, w1_hbm, w3_hbm, w2_hbm,
                   ys_ref, xbuf, ybuf, w1_buf, w3_buf, w2_buf, w1_bf, w3_bf, w2_bf, sem, osem, wsem):
    n_used = used_ref[0]
    n_pages = ys_ref.shape[0] // (2 * TM)
    groups = TM // SUBLANES

    def fetch_args(page):
        slot = page % FETCH_SLOTS
        table_page = jnp.minimum(page, n_pages - 1)
        return xt_ref, lambda u: src_ref[table_page * groups + u], xbuf.at[slot], sem.at[slot]

    def weight_copies(expert, ws):
        return [pltpu.make_async_copy(hbm.at[expert], buf.at[ws], wsem.at[ws])
                for hbm, buf in ((w1_hbm, w1_buf), (w3_hbm, w3_buf), (w2_hbm, w2_buf))]

    def page_out(page, oslot):
        return pltpu.make_async_copy(ybuf.at[oslot], ys_ref.at[pl.ds(pl.multiple_of(page * 2 * TM, 2 * TM), 2 * TM)],
                                     osem.at[oslot])

    for k in range(FETCH_AHEAD):
        _fetch_groups(*fetch_args(k))

    @pl.when(n_used > 0)
    def _first_weights():
        for c in weight_copies(pe_ref[0], wslot_ref[0]):
            c.start()

    def page_body(g, carry):
        ws = wslot_ref[g]

        @pl.when(first_ref[g] == 1)
        def _new_expert():
            for c in weight_copies(pe_ref[g], ws):
                c.wait()
            w1_bf[...] = _bf(w1_buf[ws])
            w3_bf[...] = _bf(w3_buf[ws])
            w2_bf[...] = _bf(w2_buf[ws])

            @pl.when(next_ref[g] >= 0)
            def _stream_next():
                for c in weight_copies(next_ref[g], 1 - ws):
                    c.start()

        slot = g % FETCH_SLOTS
        oslot = g % 2
        _wait_fetch(xt_ref, xbuf.at[slot], sem.at[slot])

        @pl.when(g >= 2)
        def _reuse_out_buffer():
            page_out(g - 2, oslot).wait()

        per = PAGE_SPLIT_ROWS // SUBLANES
        parts = TM // PAGE_SPLIT_ROWS
        part_rows = lambda k: pl.ds(k * 2 * PAGE_SPLIT_ROWS, 2 * PAGE_SPLIT_ROWS)
        part = lambda k: _expert_rows(xbuf.at[slot, pl.ds(k * per, per)], w1_bf, w3_bf, w2_bf,
                                      ybuf.at[oslot, part_rows(k)])
        n_parts = (pv_ref[g] + PAGE_SPLIT_ROWS - 1) // PAGE_SPLIT_ROWS
        for n in range(1, parts + 1):
            @pl.when(n_parts == n)
            def _parts(n=n):
                _round_robin([part(k) for k in range(n)]
                             + [_fetch_groups_task(*fetch_args(g + FETCH_AHEAD), FETCH_CHUNK)])
                for k in range(n, parts):
                    ybuf[oslot, part_rows(k)] = jnp.zeros((2 * PAGE_SPLIT_ROWS, HALF), ybuf.dtype)
        page_out(g, oslot).start()
        return carry

    lax.fori_loop(0, n_used, page_body, 0)

    for ahead in range(FETCH_AHEAD):
        slot = (n_used + ahead) % FETCH_SLOTS
        _wait_fetch(xt_ref, xbuf.at[slot], sem.at[slot])
    for back in (2, 1):
        @pl.when(n_used >= back)
        def _drain(back=back):
            page_out(n_used - back, (n_used - back) % 2).wait()

    ybuf[0] = jnp.zeros(ybuf.shape[1:], ybuf.dtype)

    def zero_start(page, carry):
        page_out(page, 0).start()
        return carry

    def zero_wait(page, carry):
        page_out(page, 0).wait()
        return carry

    lax.fori_loop(n_used, n_pages, zero_start, 0)
    lax.fori_loop(n_used, n_pages, zero_wait, 0)


def _experts(page_tables, esrc, xt, w1, w3, w2):
    D = D_MODEL
    n_pages = page_tables[1].shape[0]
    assert TM % PAGE_SPLIT_ROWS == 0
    grid_spec = pltpu.PrefetchScalarGridSpec(
        num_scalar_prefetch=len(page_tables) + 1,
        grid=(1,),
        in_specs=[pl.BlockSpec(memory_space=pl.ANY)] * 4,
        out_specs=pl.BlockSpec(memory_space=pl.ANY),
        scratch_shapes=[pltpu.VMEM((FETCH_SLOTS, TM // SUBLANES, 2 * SUBLANES, HALF), jnp.bfloat16),
                        pltpu.VMEM((2, 2 * TM, HALF), jnp.bfloat16),
                        pltpu.VMEM((2, D, D_EXPERT), jnp.float32),
                        pltpu.VMEM((2, D, D_EXPERT), jnp.float32),
                        pltpu.VMEM((2, D_EXPERT, D), jnp.float32),
                        pltpu.VMEM((D, D_EXPERT), jnp.bfloat16),
                        pltpu.VMEM((D, D_EXPERT), jnp.bfloat16),
                        pltpu.VMEM((D_EXPERT, D), jnp.bfloat16),
                        pltpu.SemaphoreType.DMA((FETCH_SLOTS,)),
                        pltpu.SemaphoreType.DMA((2,)),
                        pltpu.SemaphoreType.DMA((2,))],
    )
    return pl.pallas_call(
        _expert_kernel,
        grid_spec=grid_spec,
        out_shape=jax.ShapeDtypeStruct((n_pages * 2 * TM, HALF), jnp.bfloat16),
        compiler_params=pltpu.CompilerParams(dimension_semantics=("arbitrary",), vmem_limit_bytes=VMEM_LIMIT),
        name="experts",
    )(*page_tables, esrc, xt, w1, w3, w2)


def _final_tile(y_ref, x1_ref, route_ref, ada_ref, lnw_ref, lnb_ref, o_ref):
    route = route_ref[...]
    pos1, pos2, cw1, cw2 = route[:, 0:1], route[:, 1:2], route[:, 2:3], route[:, 3:4]
    col = lax.broadcasted_iota(jnp.int32, (TS, TR), 1).astype(jnp.float32)
    wmat = _bf(jnp.where(col == pos1, cw1, 0.0) + jnp.where(col == pos2, cw2, 0.0))
    yield
    yl, yh = _unpack_rows(y_ref[...].reshape(2 * TR, HALF))
    yield
    y = jnp.concatenate([jnp.dot(wmat, yl, preferred_element_type=jnp.float32),
                         jnp.dot(wmat, yh, preferred_element_type=jnp.float32)], axis=-1)
    yield
    gate2 = ada_ref[5:6, :]
    o_ref[...] = _ln(ALPHA * x1_ref[...] + gate2 * y) * lnw_ref[...] + lnb_ref[...]


def _final_kernel(src_ref, x1_ref, route_ref, ada_ref, lnw_ref, lnb_ref, ys_ref, o_ref, ybuf, sem):
    P = TILES_PER_STEP
    i = pl.program_id(0)
    last = pl.num_programs(0) - 1
    per = TR // SUBLANES
    groups = P * per

    def fetch(step, dst_slot):
        _fetch_groups(ys_ref, lambda u: src_ref[step * groups + u], ybuf.at[dst_slot], sem.at[dst_slot])

    @pl.when(i == 0)
    def _prime():
        for k in range(FETCH_AHEAD):
            fetch(k, k)

    slot = i % FETCH_SLOTS
    _wait_fetch(ys_ref, ybuf.at[slot], sem.at[slot])

    ahead = jnp.minimum(i + FETCH_AHEAD, last)
    ahead_slot = (i + FETCH_AHEAD) % FETCH_SLOTS
    rows = lambda k: pl.ds(k * TS, TS)
    _round_robin([_final_tile(ybuf.at[slot, pl.ds(k * per, per)], x1_ref.at[rows(k)], route_ref.at[rows(k)], ada_ref,
                              lnw_ref, lnb_ref, o_ref.at[rows(k)]) for k in range(P)]
                 + [_fetch_groups_task(ys_ref, lambda u: src_ref[ahead * groups + u], ybuf.at[ahead_slot],
                                       sem.at[ahead_slot], FETCH_CHUNK)])

    @pl.when(i == last)
    def _drain():
        for k in range(1, FETCH_SLOTS):
            other = (i + k) % FETCH_SLOTS
            _wait_fetch(ys_ref, ybuf.at[other], sem.at[other])


def _final(fsrc, x1, route, ada, lnw, lnb, ys, seq_len):
    T, D = x1.shape
    P = TILES_PER_STEP
    rows = P * TS
    assert seq_len % rows == 0
    steps_per_seq = seq_len // rows
    n_steps = T // rows
    assert n_steps > FETCH_AHEAD
    grid_spec = pltpu.PrefetchScalarGridSpec(
        num_scalar_prefetch=1,
        grid=(n_steps,),
        in_specs=[pl.BlockSpec((rows, D), lambda i, src: (i, 0)),
                  pl.BlockSpec((rows, LANES), lambda i, src: (i, 0)),
                  pl.BlockSpec((None, 6, D), lambda i, src: (i // steps_per_seq, 0, 0)),
                  pl.BlockSpec((1, D), lambda i, src: (0, 0)),
                  pl.BlockSpec((1, D), lambda i, src: (0, 0)),
                  pl.BlockSpec(memory_space=pl.ANY)],
        out_specs=pl.BlockSpec((rows, D), lambda i, src: (i, 0)),
        scratch_shapes=[pltpu.VMEM((FETCH_SLOTS, P * (TR // SUBLANES), 2 * SUBLANES, HALF), jnp.bfloat16),
                        pltpu.SemaphoreType.DMA((FETCH_SLOTS,))],
    )
    return pl.pallas_call(
        _final_kernel,
        grid_spec=grid_spec,
        out_shape=jax.ShapeDtypeStruct((T, D), jnp.float32),
        compiler_params=pltpu.CompilerParams(dimension_semantics=("arbitrary",), vmem_limit_bytes=VMEM_LIMIT),
        name="final",
    )(fsrc, x1, route, ada, lnw, lnb, ys)


def _gather_tables(cnt, n_pages):
    nt = cnt.shape[0]
    run = (cnt + SUBLANES - 1) // SUBLANES * SUBLANES
    so = jnp.cumsum(run, axis=1) - run
    eo = jnp.cumsum(run, axis=0) - run
    tot = jnp.sum(run, axis=0)
    pages_e = (tot + TM - 1) // TM
    page_end = jnp.cumsum(pages_e)
    page_start = page_end - pages_e
    g = jnp.arange(n_pages, dtype=jnp.int32)
    pe = jnp.minimum(jnp.sum(g[:, None] >= page_end[None, :], axis=1), N_EXPERTS - 1).astype(jnp.int32)
    used = g < page_end[-1]
    owner = (g[None, :] >= page_start[:, None]) & (g[None, :] < page_end[:, None])
    of_page = lambda a: jnp.sum(jnp.where(owner, a[:, None], 0), axis=0)
    of_page2 = lambda a: jnp.sum(jnp.where(owner[:, None, :], a.T[:, :, None], 0), axis=0)
    pj = g - of_page(page_start)
    tot_p = of_page(tot)
    pv = jnp.where(used, jnp.clip(tot_p - pj * TM, 0, TM), 0).astype(jnp.int32)
    experts = jnp.arange(N_EXPERTS, dtype=jnp.int32)
    has_pages = pages_e > 0
    wslot_e = (jnp.cumsum(has_pages) - 1) % 2
    later = (experts[None, :] > experts[:, None]) & has_pages[None, :]
    next_e = jnp.min(jnp.where(later, experts[None, :], N_EXPERTS), axis=1)
    next_e = jnp.where(next_e < N_EXPERTS, next_e, -1)
    first = (used & (pj == 0)).astype(jnp.int32)
    page_tables = (page_end[-1:].astype(jnp.int32), pe, pv, first, of_page(wslot_e).astype(jnp.int32),
                   jnp.where(used, of_page(next_e), -1).astype(jnp.int32))

    q = (pj * TM)[:, None] + SUBLANES * jnp.arange(TM // SUBLANES, dtype=jnp.int32)[None, :]
    tiles = jnp.arange(nt, dtype=jnp.int32)
    offset = of_page2(tiles[:, None] * TR + so - eo)
    step = jnp.concatenate([offset[1:] - offset[:-1], jnp.zeros_like(offset[:1])], axis=0)
    run_end = of_page2(eo + run)
    src = q + offset[0][:, None] + jnp.sum(jnp.where(run_end[:, :, None] <= q[None], step[:, :, None], 0), axis=0)
    zero_xt = TR - SUBLANES
    esrc = jnp.where((q < tot_p[:, None]) & used[:, None], src, zero_xt).astype(jnp.int32)

    r = SUBLANES * jnp.arange(TR // SUBLANES, dtype=jnp.int32)
    end = (so + run).T
    offset_f = ((page_start * TM)[None, :] + eo - so).T
    step_f = jnp.concatenate([offset_f[1:] - offset_f[:-1], jnp.zeros_like(offset_f[:1])], axis=0)
    srcf = (r[None, :] + offset_f[0][:, None]
            + jnp.sum(jnp.where(end[:, :, None] <= r[None, None, :], step_f[:, :, None], 0), axis=0))
    zero_ys = (n_pages - 1) * TM
    fsrc = jnp.where(r[None, :] < end[-1][:, None], srcf, zero_ys).astype(jnp.int32)
    return page_tables, esrc.reshape(-1) // SUBLANES, fsrc.reshape(-1) // SUBLANES


def kernel(x, c, positions, w_ada, b_ada, w_in, ret_gn_w, w_pool, pool_scale, w_out, ln1_w, ln1_b, w_group, b_group,
           w_router, b_router, w1, w3, w2, ln2_w, ln2_b):
    B, S, D = x.shape
    T = B * S
    assert w_ada.shape[0] == DEPTH and D == D_MODEL and S % TS == 0

    inv_freq = ROPE_BASE ** (-jnp.arange(0, HEAD_DIM, 2, dtype=jnp.float32) / HEAD_DIM)
    half = HEAD_DIM // 2
    rope = jnp.stack([jnp.concatenate([inv_freq, inv_freq]),
                      jnp.concatenate([-jnp.ones((half,), jnp.float32), jnp.ones((half,), jnp.float32)])])

    nt = T // TS
    n_pages = (2 * T + nt * N_EXPERTS * (SUBLANES - 1)) // TM + N_EXPERTS + 1

    for l in range(DEPTH):
        ada = _ada(c, w_ada[l], b_ada[l]).reshape(B, 6, D)
        pad = SUBLANES - N_GROUPS
        wrt = jnp.concatenate([w_group[l].T, jnp.zeros((pad, D), jnp.float32), w_router[l].T], axis=0)
        brt = jnp.concatenate([b_group[l], jnp.zeros((pad,), jnp.float32), b_router[l]]).reshape(ROUTE_ROWS, 1)
        x1, xt, route, counts = _mix(
            x, positions, ada, rope, _bf(w_in[l]), _bf(w_out[l]), _bf(w_pool[l]),
            ret_gn_w[l].reshape(1, RET_WIDTH), pool_scale[l].reshape(1, POOL_WIDTH),
            ln1_w[l].reshape(1, D), ln1_b[l].reshape(1, D), wrt, brt)

        cnt = counts[:, :, :, 0].reshape(nt, N_EXPERTS).astype(jnp.int32)
        page_tables, esrc, fsrc = _gather_tables(cnt, n_pages)
        ys = _experts(page_tables, esrc, xt.reshape(nt * TR // SUBLANES, 2 * SUBLANES, HALF),
                      w1[l].reshape(N_EXPERTS, D, D_EXPERT), w3[l].reshape(N_EXPERTS, D, D_EXPERT),
                      w2[l].reshape(N_EXPERTS, D_EXPERT, D))
        x = _final(fsrc, x1.reshape(T, D), route.reshape(T, LANES), ada,
                   ln2_w[l].reshape(1, D), ln2_b[l].reshape(1, D),
                   ys.reshape(n_pages * TM // SUBLANES, 2 * SUBLANES, HALF), S).reshape(B, S, D)
    return x
```

```python
import math

import jax
import jax.numpy as jnp
from jax import lax
from jax.experimental import pallas as pl
from jax.experimental.pallas import tpu as pltpu

D_MODEL = 1024
RET_WIDTH = 512
RET_HEADS = 4
HEAD_DIM = 128
POOL_WIDTH = 512
POOL_WINDOWS = (2, 4, 8, 16)
POOL_GROUP_DIM = 128
IN_COLS = 4 * RET_WIDTH + POOL_WIDTH
N_GROUPS = 4
EXPERTS_PER_GROUP = 8
N_EXPERTS = 32
D_EXPERT = 256
DEPTH = 1
ALPHA = (2.0 * DEPTH) ** 0.25
LN_EPS = 1e-5
ROPE_BASE = 10000.0

LANES = 128
SUBLANES = 8
HALF = D_MODEL // 2
POOL_HALO = 16
TS = 256
TM = 512
PAGE_SPLIT_ROWS = 256
TR = -(-(2 * TS + N_EXPERTS * (SUBLANES - 1)) // LANES) * LANES
SEQS_PER_STEP = 2
TILES_PER_STEP = 2
ROUTE_ROWS = SUBLANES + N_EXPERTS
FETCH_AHEAD = 2
FETCH_SLOTS = FETCH_AHEAD + 1
FETCH_CHUNK = 64
VMEM_LIMIT = 56 * 1024 * 1024

_LOG_GAMMA = tuple(math.log1p(-(2.0 ** (-5.0 - h))) for h in range(RET_HEADS))
_HI = lax.Precision.HIGHEST


def _ln(x):
    mu = jnp.mean(x, axis=-1, keepdims=True)
    xc = x - mu
    var = jnp.mean(xc * xc, axis=-1, keepdims=True)
    return xc * lax.rsqrt(var + LN_EPS)


def _bf(x):
    return x.astype(jnp.bfloat16)


def _pack_rows(x):
    n = x.shape[0]
    lo = x[:, :HALF].reshape(n // SUBLANES, SUBLANES, HALF)
    hi = x[:, HALF:].reshape(n // SUBLANES, SUBLANES, HALF)
    return _bf(jnp.concatenate([lo, hi], axis=1).reshape(2 * n, HALF))


def _unpack_rows(z):
    n = z.shape[0] // 2
    zf = z.astype(jnp.float32).reshape(n // SUBLANES, 2 * SUBLANES, HALF)
    return _bf(zf[:, :SUBLANES, :].reshape(n, HALF)), _bf(zf[:, SUBLANES:, :].reshape(n, HALF))


def _ada_kernel(c_ref, w_ref, b_ref, o_ref):
    c = c_ref[...]
    ca = c * jax.nn.sigmoid(c)
    o_ref[...] = jnp.dot(ca, w_ref[...], precision=_HI, preferred_element_type=jnp.float32) + b_ref[...]


def _ada(c, w_ada, b_ada):
    B, D = c.shape
    n = w_ada.shape[1]
    return pl.pallas_call(
        _ada_kernel,
        grid=(n // D,),
        in_specs=[pl.BlockSpec((B, D), lambda j: (0, 0)),
                  pl.BlockSpec((D, D), lambda j: (0, j)),
                  pl.BlockSpec((1, D), lambda j: (0, j))],
        out_specs=pl.BlockSpec((B, D), lambda j: (0, j)),
        out_shape=jax.ShapeDtypeStruct((B, n), jnp.float32),
        compiler_params=pltpu.CompilerParams(dimension_semantics=("arbitrary",), vmem_limit_bytes=VMEM_LIMIT),
        name="ada",
    )(c, w_ada, b_ada.reshape(1, n))


def _mix_kernel(x_ref, pos_ref, ada_ref, rope_ref, win_ref, wout_ref, wpool_ref, gnw_ref, pscale_ref,
                ln1w_ref, ln1b_ref, wrt_ref, brt_ref,
                x1_ref, xt_ref, route_ref, cnt_ref,
                state_ref, halo_ref, dmat_ref, qdec_ref, kdec_ref, cat_ref):
    b = pl.program_id(0)
    s = pl.program_id(1)

    @pl.when((b == 0) & (s == 0))
    def _init_tables():
        ri = lax.broadcasted_iota(jnp.int32, (TS, TS), 0)
        ci = lax.broadcasted_iota(jnp.int32, (TS, TS), 1)
        rel = (ri - ci).astype(jnp.float32)
        for h in range(RET_HEADS):
            dmat_ref[h] = jnp.where(rel >= 0.0, jnp.exp(jnp.maximum(rel, 0.0) * _LOG_GAMMA[h]), 0.0)
        row = lax.broadcasted_iota(jnp.int32, (TS, RET_WIDTH), 0).astype(jnp.float32)
        lane = lax.broadcasted_iota(jnp.int32, (TS, RET_WIDTH), 1)
        lg = jnp.full((TS, RET_WIDTH), _LOG_GAMMA[0], jnp.float32)
        for h in range(1, RET_HEADS):
            lg = jnp.where(lane >= h * HEAD_DIM, _LOG_GAMMA[h], lg)
        qdec_ref[...] = jnp.exp((row + 1.0) * lg)
        kdec_ref[...] = jnp.exp((TS - 1.0 - row) * lg)

    @pl.when(s == 0)
    def _init_carries():
        state_ref[...] = jnp.zeros_like(state_ref)
        halo_ref[...] = jnp.zeros_like(halo_ref)

    tiles = [_mix_tile(s, x_ref.at[j], pos_ref.at[j], ada_ref.at[j], rope_ref, win_ref, wout_ref, wpool_ref, gnw_ref,
                       pscale_ref, ln1w_ref, ln1b_ref, wrt_ref, brt_ref,
                       x1_ref.at[j], xt_ref.at[j], route_ref.at[j], cnt_ref.at[j],
                       state_ref.at[j], halo_ref.at[j], dmat_ref, qdec_ref, kdec_ref, cat_ref.at[j])
             for j in range(SEQS_PER_STEP)]
    _round_robin(tiles)


def _mix_tile(s, x_ref, pos_ref, ada_ref, rope_ref, win_ref, wout_ref, wpool_ref, gnw_ref, pscale_ref,
              ln1w_ref, ln1b_ref, wrt_ref, brt_ref,
              x1_ref, xt_ref, route_ref, cnt_ref,
              state_ref, halo_ref, dmat_ref, qdec_ref, kdec_ref, cat_ref):
    ada = ada_ref[...]
    shift1, scale1, gate1 = ada[0:1], ada[1:2], ada[2:3]
    shift2, scale2 = ada[3:4], ada[4:5]

    x = x_ref[...]
    u = _bf(_ln(x) * (1.0 + scale1) + shift1)
    yield

    posf = pos_ref[...].astype(jnp.float32)
    hl = lax.broadcasted_iota(jnp.int32, (TS // 2, HEAD_DIM), 1) < HEAD_DIM // 2
    ang = jnp.where(hl, posf[:TS // 2], posf[TS // 2:]) * rope_ref[0:1, :]
    cos_p, sin_p = jnp.cos(ang), jnp.sin(ang)
    cos_s, sin_s = pltpu.roll(cos_p, HEAD_DIM // 2, 1), pltpu.roll(sin_p, HEAD_DIM // 2, 1)
    cos_t = jnp.concatenate([jnp.where(hl, cos_p, cos_s), jnp.where(hl, cos_s, cos_p)], axis=0)
    sin_t = jnp.concatenate([jnp.where(hl, sin_p, sin_s), jnp.where(hl, sin_s, sin_p)], axis=0) * rope_ref[1:2, :]
    yield

    q = jnp.dot(u, win_ref[:, 0:RET_WIDTH], preferred_element_type=jnp.float32)
    k = jnp.dot(u, win_ref[:, RET_WIDTH:2 * RET_WIDTH], preferred_element_type=jnp.float32)
    v = jnp.dot(u, win_ref[:, 2 * RET_WIDTH:3 * RET_WIDTH], preferred_element_type=jnp.float32)
    g = jnp.dot(u, win_ref[:, 3 * RET_WIDTH:4 * RET_WIDTH], preferred_element_type=jnp.float32)
    p = jnp.dot(u, win_ref[:, 4 * RET_WIDTH:IN_COLS], preferred_element_type=jnp.float32)
    yield

    gnw = gnw_ref[...]
    for h in range(RET_HEADS):
        sl = slice(h * HEAD_DIM, (h + 1) * HEAD_DIM)
        qh, kh, vh = q[:, sl], k[:, sl], v[:, sl]
        qr = qh * cos_t + pltpu.roll(qh, HEAD_DIM // 2, 1) * sin_t
        kr = (kh * cos_t + pltpu.roll(kh, HEAD_DIM // 2, 1) * sin_t) * (HEAD_DIM ** -0.5)
        vb = _bf(vh)
        sc = lax.dot_general(_bf(qr), _bf(kr), (((1,), (1,)), ((), ())), preferred_element_type=jnp.float32)
        intra = jnp.dot(_bf(sc * dmat_ref[h]), vb, preferred_element_type=jnp.float32)
        st = state_ref[h]
        cross = jnp.dot(_bf(qr * qdec_ref[:, sl]), _bf(st), preferred_element_type=jnp.float32)
        kv = lax.dot_general(_bf(kr * kdec_ref[:, sl]), vb, (((0,), (0,)), ((), ())),
                             preferred_element_type=jnp.float32)
        state_ref[h] = st * math.exp(TS * _LOG_GAMMA[h]) + kv
        r = _ln(intra + cross) * gnw[:, sl]
        gh = g[:, sl]
        cat_ref[:, sl] = _bf(gh * jax.nn.sigmoid(gh) * r)
        yield

    pext = jnp.concatenate([halo_ref[...], p], axis=0)
    halo_ref[...] = p[TS - POOL_HALO:, :]
    t_abs = (s * TS + lax.broadcasted_iota(jnp.int32, (TS, 1), 0) + 1).astype(jnp.float32)
    pscale = pscale_ref[...]
    for grp, w in enumerate(POOL_WINDOWS):
        sl = slice(grp * POOL_GROUP_DIM, (grp + 1) * POOL_GROUP_DIM)
        acc = pext[:, sl]
        shift = 1
        while shift < w:
            acc = acc + pltpu.roll(acc, shift, 0)
            shift *= 2
        pooled = acc[POOL_HALO:, :] / jnp.minimum(t_abs, float(w)) - p[:, sl]
        po = jnp.dot(_bf(pooled), wpool_ref[grp], preferred_element_type=jnp.float32) * pscale[:, sl]
        cat_ref[:, RET_WIDTH + grp * POOL_GROUP_DIM:RET_WIDTH + (grp + 1) * POOL_GROUP_DIM] = _bf(po)
    yield

    mix = jnp.dot(cat_ref[...], wout_ref[...], preferred_element_type=jnp.float32)
    yield
    x1 = _ln(ALPHA * x + gate1 * mix) * ln1w_ref[...] + ln1b_ref[...]
    x1_ref[...] = x1
    u2 = _ln(x1) * (1.0 + scale2) + shift2
    yield

    w = wrt_ref[...]
    w_hi = _bf(w)
    w_lo = _bf(w - w_hi.astype(jnp.float32))
    u2_hi = _bf(u2)
    u2_lo = _bf(u2 - u2_hi.astype(jnp.float32))
    nt = (((1,), (1,)), ((), ()))
    logits = (lax.dot_general(w_hi, u2_hi, nt, preferred_element_type=jnp.float32)
              + lax.dot_general(w_hi, u2_lo, nt, preferred_element_type=jnp.float32)
              + lax.dot_general(w_lo, u2_hi, nt, preferred_element_type=jnp.float32)) + brt_ref[...]
    row8 = lax.broadcasted_iota(jnp.int32, (SUBLANES, TS), 0)
    neg = jnp.float32(-jnp.inf)
    gl = jnp.where(row8 < N_GROUPS, logits[0:SUBLANES], neg)
    gmax = jnp.max(gl, axis=0, keepdims=True)
    gidx = jnp.min(jnp.where(gl == gmax, row8, SUBLANES), axis=0, keepdims=True)
    gprob = 1.0 / jnp.sum(jnp.exp(gl - gmax), axis=0, keepdims=True)
    el = logits[SUBLANES:2 * SUBLANES]
    for grp in range(1, N_GROUPS):
        el = jnp.where(gidx == grp, logits[(grp + 1) * SUBLANES:(grp + 2) * SUBLANES], el)
    m1 = jnp.max(el, axis=0, keepdims=True)
    j1 = jnp.min(jnp.where(el == m1, row8, SUBLANES), axis=0, keepdims=True)
    el2 = jnp.where(row8 == j1, neg, el)
    m2 = jnp.max(el2, axis=0, keepdims=True)
    j2 = jnp.min(jnp.where(el2 == m2, row8, SUBLANES), axis=0, keepdims=True)
    e21 = jnp.exp(m2 - m1)
    den = 1.0 / (1.0 + e21)
    cw1 = gprob * den
    cw2 = gprob * e21 * den
    yield

    erow = lax.broadcasted_iota(jnp.int32, (N_EXPERTS, TS), 0)
    oh1 = erow == gidx * EXPERTS_PER_GROUP + j1
    oh2 = erow == gidx * EXPERTS_PER_GROUP + j2
    oh = jnp.where(oh1 | oh2, 1.0, 0.0)
    cnt = jnp.broadcast_to(jnp.sum(oh, axis=1, keepdims=True), (N_EXPERTS, LANES))
    run = jnp.floor((cnt + (SUBLANES - 1.0)) * (1.0 / SUBLANES)) * SUBLANES
    erow_l = lax.broadcasted_iota(jnp.int32, (N_EXPERTS, LANES), 0)
    run_end = run
    shift = 1
    while shift < N_EXPERTS:
        run_end = run_end + jnp.where(erow_l >= shift, pltpu.roll(run_end, shift, 0), 0.0)
        shift *= 2
    run_start = (run_end - run)[:, 0:1]
    ri = lax.broadcasted_iota(jnp.int32, (TS, TS), 0)
    ci = lax.broadcasted_iota(jnp.int32, (TS, TS), 1)
    earlier = _bf(jnp.where(ri < ci, 1.0, 0.0))
    before = jnp.dot(_bf(oh), earlier, preferred_element_type=jnp.float32) + run_start
    pos1 = jnp.sum(jnp.where(oh1, before, 0.0), axis=0, keepdims=True)
    pos2 = jnp.sum(jnp.where(oh2, before, 0.0), axis=0, keepdims=True)
    rr = lax.broadcasted_iota(jnp.int32, (TR, TS), 0).astype(jnp.float32)
    perm = _bf(jnp.where((rr == pos1) | (rr == pos2), 1.0, 0.0))
    yield
    xt_ref[...] = _pack_rows(jnp.dot(perm, u2_hi, preferred_element_type=jnp.float32))
    cnt_ref[...] = cnt

    rowl = lax.broadcasted_iota(jnp.int32, (LANES, TS), 0)
    rec = jnp.where(rowl == 0, pos1, 0.0)
    rec = jnp.where(rowl == 1, pos2, rec)
    rec = jnp.where(rowl == 2, cw1, rec)
    rec = jnp.where(rowl == 3, cw2, rec)
    route_ref[...] = rec.T


def _mix(x, positions, ada, rope, win, wout, wpool, gnw, pscale, ln1w, ln1b, wrt, brt):
    B, S, D = x.shape
    ns = S // TS
    assert B % SEQS_PER_STEP == 0
    P = SEQS_PER_STEP
    const2 = lambda b, s: (0, 0)
    const3 = lambda b, s: (0, 0, 0)
    tile = lambda b, s: (b, s, 0)
    flat = lambda b, s: (b, s, 0, 0)
    return pl.pallas_call(
        _mix_kernel,
        grid=(B // P, ns),
        in_specs=[
            pl.BlockSpec((P, TS, D), tile),
            pl.BlockSpec((P, TS, 1), tile),
            pl.BlockSpec((P, 6, D), lambda b, s: (b, 0, 0)),
            pl.BlockSpec((2, LANES), const2),
            pl.BlockSpec((D, IN_COLS), const2),
            pl.BlockSpec((D, D), const2),
            pl.BlockSpec((len(POOL_WINDOWS), POOL_GROUP_DIM, POOL_GROUP_DIM), const3),
            pl.BlockSpec((1, RET_WIDTH), const2),
            pl.BlockSpec((1, POOL_WIDTH), const2),
            pl.BlockSpec((1, D), const2),
            pl.BlockSpec((1, D), const2),
            pl.BlockSpec((ROUTE_ROWS, D), const2),
            pl.BlockSpec((ROUTE_ROWS, 1), const2),
        ],
        out_specs=[
            pl.BlockSpec((P, TS, D), tile),
            pl.BlockSpec((P, None, 2 * TR, HALF), flat),
            pl.BlockSpec((P, TS, LANES), tile),
            pl.BlockSpec((P, None, N_EXPERTS, LANES), flat),
        ],
        out_shape=[
            jax.ShapeDtypeStruct((B, S, D), jnp.float32),
            jax.ShapeDtypeStruct((B, ns, 2 * TR, HALF), jnp.bfloat16),
            jax.ShapeDtypeStruct((B, S, LANES), jnp.float32),
            jax.ShapeDtypeStruct((B, ns, N_EXPERTS, LANES), jnp.float32),
        ],
        scratch_shapes=[
            pltpu.VMEM((P, RET_HEADS, HEAD_DIM, HEAD_DIM), jnp.float32),
            pltpu.VMEM((P, POOL_HALO, POOL_WIDTH), jnp.float32),
            pltpu.VMEM((RET_HEADS, TS, TS), jnp.float32),
            pltpu.VMEM((TS, RET_WIDTH), jnp.float32),
            pltpu.VMEM((TS, RET_WIDTH), jnp.float32),
            pltpu.VMEM((P, TS, D), jnp.bfloat16),
        ],
        compiler_params=pltpu.CompilerParams(dimension_semantics=("arbitrary", "arbitrary"),
                                             vmem_limit_bytes=VMEM_LIMIT),
        name="mix",
    )(x, positions.reshape(B, S, 1), ada, rope, win, wout, wpool, gnw, pscale, ln1w, ln1b, wrt, brt)


def _fetch_groups(src_ref, group_of, dst_ref, sem):
    for u in range(dst_ref.shape[0]):
        pltpu.make_async_copy(src_ref.at[group_of(u)], dst_ref.at[u], sem).start(priority=u % 2)


def _fetch_groups_task(src_ref, group_of, dst_ref, sem, chunk):
    for u in range(dst_ref.shape[0]):
        pltpu.make_async_copy(src_ref.at[group_of(u)], dst_ref.at[u], sem).start(priority=u % 2)
        if u % chunk == chunk - 1:
            yield


def _wait_fetch(src_ref, dst_ref, sem):
    pltpu.make_async_copy(src_ref.at[pl.ds(0, dst_ref.shape[0])], dst_ref, sem).wait()


def _expert_rows(x_ref, w1_ref, w3_ref, w2_ref, y_ref):
    n = PAGE_SPLIT_ROWS
    xl, xh = _unpack_rows(x_ref[...].reshape(2 * n, HALF))
    yield
    a = (jnp.dot(xl, w1_ref[:HALF, :], preferred_element_type=jnp.float32)
         + jnp.dot(xh, w1_ref[HALF:, :], preferred_element_type=jnp.float32))
    c = (jnp.dot(xl, w3_ref[:HALF, :], preferred_element_type=jnp.float32)
         + jnp.dot(xh, w3_ref[HALF:, :], preferred_element_type=jnp.float32))
    yield
    h = _bf(a * jax.nn.sigmoid(a) * c)
    yield
    y = jnp.dot(h, w2_ref[...], preferred_element_type=jnp.float32)
    yield
    y_ref[...] = _pack_rows(y)


def _round_robin(tasks):
    while tasks:
        tasks = [t for t in tasks if next(t, True) is None]


def _expert_kernel(used_ref, pe_ref, pv_ref, first_ref, wslot_ref, next_ref, src_ref, xt_ref, w1_hbm, w3_hbm, w2_hbm,
                   ys_ref, xbuf, ybuf, w1_buf, w3_buf, w2_buf, w1_bf, w3_bf, w2_bf, sem, osem, wsem):
    n_used = used_ref[0]
    n_pages = ys_ref.shape[0] // (2 * TM)
    groups = TM // SUBLANES

    def fetch_args(page):
        slot = page % FETCH_SLOTS
        table_page = jnp.minimum(page, n_pages - 1)
        return xt_ref, lambda u: src_ref[table_page * groups + u], xbuf.at[slot], sem.at[slot]

    def weight_copies(expert, ws):
        return [pltpu.make_async_copy(hbm.at[expert], buf.at[ws], wsem.at[ws])
                for hbm, buf in ((w1_hbm, w1_buf), (w3_hbm, w3_buf), (w2_hbm, w2_buf))]

    def page_out(page, oslot):
        return pltpu.make_async_copy(ybuf.at[oslot], ys_ref.at[pl.ds(pl.multiple_of(page * 2 * TM, 2 * TM), 2 * TM)],
                                     osem.at[oslot])

    for k in range(FETCH_AHEAD):
        _fetch_groups(*fetch_args(k))

    @pl.when(n_used > 0)
    def _first_weights():
        for c in weight_copies(pe_ref[0], wslot_ref[0]):
            c.start()

    def page_body(g, carry):
        ws = wslot_ref[g]

        @pl.when(first_ref[g] == 1)
        def _new_expert():
            for c in weight_copies(pe_ref[g], ws):
                c.wait()
            w1_bf[...] = _bf(w1_buf[ws])
            w3_bf[...] = _bf(w3_buf[ws])
            w2_bf[...] = _bf(w2_buf[ws])

            @pl.when(next_ref[g] >= 0)
            def _stream_next():
                for c in weight_copies(next_ref[g], 1 - ws):
                    c.start()

        slot = g % FETCH_SLOTS
        oslot = g % 2
        _wait_fetch(xt_ref, xbuf.at[slot], sem.at[slot])

        @pl.when(g >= 2)
        def _reuse_out_buffer():
            page_out(g - 2, oslot).wait()

        per = PAGE_SPLIT_ROWS // SUBLANES
        parts = TM // PAGE_SPLIT_ROWS
        part_rows = lambda k: pl.ds(k * 2 * PAGE_SPLIT_ROWS, 2 * PAGE_SPLIT_ROWS)
        part = lambda k: _expert_rows(xbuf.at[slot, pl.ds(k * per, per)], w1_bf, w3_bf, w2_bf,
                                      ybuf.at[oslot, part_rows(k)])
        n_parts = (pv_ref[g] + PAGE_SPLIT_ROWS - 1) // PAGE_SPLIT_ROWS
        for n in range(1, parts + 1):
            @pl.when(n_parts == n)
            def _parts(n=n):
                _round_robin([part(k) for k in range(n)]
                             + [_fetch_groups_task(*fetch_args(g + FETCH_AHEAD), FETCH_CHUNK)])
                for k in range(n, parts):
                    ybuf[oslot, part_rows(k)] = jnp.zeros((2 * PAGE_SPLIT_ROWS, HALF), ybuf.dtype)
        page_out(g, oslot).start()
        return carry

    lax.fori_loop(0, n_used, page_body, 0)

    for ahead in range(FETCH_AHEAD):
        slot = (n_used + ahead) % FETCH_SLOTS
        _wait_fetch(xt_ref, xbuf.at[slot], sem.at[slot])
    for back in (2, 1):
        @pl.when(n_used >= back)
        def _drain(back=back):
            page_out(n_used - back, (n_used - back) % 2).wait()

    ybuf[0] = jnp.zeros(ybuf.shape[1:], ybuf.dtype)

    def zero_start(page, carry):
        page_out(page, 0).start()
        return carry

    def zero_wait(page, carry):
        page_out(page, 0).wait()
        return carry

    lax.fori_loop(n_used, n_pages, zero_start, 0)
    lax.fori_loop(n_used, n_pages, zero_wait, 0)


def _experts(page_tables, esrc, xt, w1, w3, w2):
    D = D_MODEL
    n_pages = page_tables[1].shape[0]
    assert TM % PAGE_SPLIT_ROWS == 0
    grid_spec = pltpu.PrefetchScalarGridSpec(
        num_scalar_prefetch=len(page_tables) + 1,
        grid=(1,),
        in_specs=[pl.BlockSpec(memory_space=pl.ANY)] * 4,
        out_specs=pl.BlockSpec(memory_space=pl.ANY),
        scratch_shapes=[pltpu.VMEM((FETCH_SLOTS, TM // SUBLANES, 2 * SUBLANES, HALF), jnp.bfloat16),
                        pltpu.VMEM((2, 2 * TM, HALF), jnp.bfloat16),
                        pltpu.VMEM((2, D, D_EXPERT), jnp.float32),
                        pltpu.VMEM((2, D, D_EXPERT), jnp.float32),
                        pltpu.VMEM((2, D_EXPERT, D), jnp.float32),
                        pltpu.VMEM((D, D_EXPERT), jnp.bfloat16),
                        pltpu.VMEM((D, D_EXPERT), jnp.bfloat16),
                        pltpu.VMEM((D_EXPERT, D), jnp.bfloat16),
                        pltpu.SemaphoreType.DMA((FETCH_SLOTS,)),
                        pltpu.SemaphoreType.DMA((2,)),
                        pltpu.SemaphoreType.DMA((2,))],
    )
    return pl.pallas_call(
        _expert_kernel,
        grid_spec=grid_spec,
        out_shape=jax.ShapeDtypeStruct((n_pages * 2 * TM, HALF), jnp.bfloat16),
        compiler_params=pltpu.CompilerParams(dimension_semantics=("arbitrary",), vmem_limit_bytes=VMEM_LIMIT),
        name="experts",
    )(*page_tables, esrc, xt, w1, w3, w2)


def _final_tile(y_ref, x1_ref, route_ref, ada_ref, lnw_ref, lnb_ref, o_ref):
    route = route_ref[...]
    pos1, pos2, cw1, cw2 = route[:, 0:1], route[:, 1:2], route[:, 2:3], route[:, 3:4]
    col = lax.broadcasted_iota(jnp.int32, (TS, TR), 1).astype(jnp.float32)
    wmat = _bf(jnp.where(col == pos1, cw1, 0.0) + jnp.where(col == pos2, cw2, 0.0))
    yield
    yl, yh = _unpack_rows(y_ref[...].reshape(2 * TR, HALF))
    yield
    y = jnp.concatenate([jnp.dot(wmat, yl, preferred_element_type=jnp.float32),
                         jnp.dot(wmat, yh, preferred_element_type=jnp.float32)], axis=-1)
    yield
    gate2 = ada_ref[5:6, :]
    o_ref[...] = _ln(ALPHA * x1_ref[...] + gate2 * y) * lnw_ref[...] + lnb_ref[...]


def _final_kernel(src_ref, x1_ref, route_ref, ada_ref, lnw_ref, lnb_ref, ys_ref, o_ref, ybuf, sem):
    P = TILES_PER_STEP
    i = pl.program_id(0)
    last = pl.num_programs(0) - 1
    per = TR // SUBLANES
    groups = P * per

    def fetch(step, dst_slot):
        _fetch_groups(ys_ref, lambda u: src_ref[step * groups + u], ybuf.at[dst_slot], sem.at[dst_slot])

    @pl.when(i == 0)
    def _prime():
        for k in range(FETCH_AHEAD):
            fetch(k, k)

    slot = i % FETCH_SLOTS
    _wait_fetch(ys_ref, ybuf.at[slot], sem.at[slot])

    ahead = jnp.minimum(i + FETCH_AHEAD, last)
    ahead_slot = (i + FETCH_AHEAD) % FETCH_SLOTS
    rows = lambda k: pl.ds(k * TS, TS)
    _round_robin([_final_tile(ybuf.at[slot, pl.ds(k * per, per)], x1_ref.at[rows(k)], route_ref.at[rows(k)], ada_ref,
                              lnw_ref, lnb_ref, o_ref.at[rows(k)]) for k in range(P)]
                 + [_fetch_groups_task(ys_ref, lambda u: src_ref[ahead * groups + u], ybuf.at[ahead_slot],
                                       sem.at[ahead_slot], FETCH_CHUNK)])

    @pl.when(i == last)
    def _drain():
        for k in range(1, FETCH_SLOTS):
            other = (i + k) % FETCH_SLOTS
            _wait_fetch(ys_ref, ybuf.at[other], sem.at[other])


def _final(fsrc, x1, route, ada, lnw, lnb, ys, seq_len):
    T, D = x1.shape
    P = TILES_PER_STEP
    rows = P * TS
    assert seq_len % rows == 0
    steps_per_seq = seq_len // rows
    n_steps = T // rows
    assert n_steps > FETCH_AHEAD
    grid_spec = pltpu.PrefetchScalarGridSpec(
        num_scalar_prefetch=1,
        grid=(n_steps,),
        in_specs=[pl.BlockSpec((rows, D), lambda i, src: (i, 0)),
                  pl.BlockSpec((rows, LANES), lambda i, src: (i, 0)),
                  pl.BlockSpec((None, 6, D), lambda i, src: (i // steps_per_seq, 0, 0)),
                  pl.BlockSpec((1, D), lambda i, src: (0, 0)),
                  pl.BlockSpec((1, D), lambda i, src: (0, 0)),
                  pl.BlockSpec(memory_space=pl.ANY)],
        out_specs=pl.BlockSpec((rows, D), lambda i, src: (i, 0)),
        scratch_shapes=[pltpu.VMEM((FETCH_SLOTS, P * (TR // SUBLANES), 2 * SUBLANES, HALF), jnp.bfloat16),
                        pltpu.SemaphoreType.DMA((FETCH_SLOTS,))],
    )
    return pl.pallas_call(
        _final_kernel,
        grid_spec=grid_spec,
        out_shape=jax.ShapeDtypeStruct((T, D), jnp.float32),
        compiler_params=pltpu.CompilerParams(dimension_semantics=("arbitrary",), vmem_limit_bytes=VMEM_LIMIT),
        name="final",
    )(fsrc, x1, route, ada, lnw, lnb, ys)


def _gather_tables(cnt, n_pages):
    nt = cnt.shape[0]
    run = (cnt + SUBLANES - 1) // SUBLANES * SUBLANES
    so = jnp.cumsum(run, axis=1) - run
    eo = jnp.cumsum(run, axis=0) - run
    tot = jnp.sum(run, axis=0)
    pages_e = (tot + TM - 1) // TM
    page_end = jnp.cumsum(pages_e)
    page_start = page_end - pages_e
    g = jnp.arange(n_pages, dtype=jnp.int32)
    pe = jnp.minimum(jnp.sum(g[:, None] >= page_end[None, :], axis=1), N_EXPERTS - 1).astype(jnp.int32)
    used = g < page_end[-1]
    owner = (g[None, :] >= page_start[:, None]) & (g[None, :] < page_end[:, None])
    of_page = lambda a: jnp.sum(jnp.where(owner, a[:, None], 0), axis=0)
    of_page2 = lambda a: jnp.sum(jnp.where(owner[:, None, :], a.T[:, :, None], 0), axis=0)
    pj = g - of_page(page_start)
    tot_p = of_page(tot)
    pv = jnp.where(used, jnp.clip(tot_p - pj * TM, 0, TM), 0).astype(jnp.int32)
    experts = jnp.arange(N_EXPERTS, dtype=jnp.int32)
    has_pages = pages_e > 0
    wslot_e = (jnp.cumsum(has_pages) - 1) % 2
    later = (experts[None, :] > experts[:, None]) & has_pages[None, :]
    next_e = jnp.min(jnp.where(later, experts[None, :], N_EXPERTS), axis=1)
    next_e = jnp.where(next_e < N_EXPERTS, next_e, -1)
    first = (used & (pj == 0)).astype(jnp.int32)
    page_tables = (page_end[-1:].astype(jnp.int32), pe, pv, first, of_page(wslot_e).astype(jnp.int32),
                   jnp.where(used, of_page(next_e), -1).astype(jnp.int32))

    q = (pj * TM)[:, None] + SUBLANES * jnp.arange(TM // SUBLANES, dtype=jnp.int32)[None, :]
    tiles = jnp.arange(nt, dtype=jnp.int32)
    offset = of_page2(tiles[:, None] * TR + so - eo)
    step = jnp.concatenate([offset[1:] - offset[:-1], jnp.zeros_like(offset[:1])], axis=0)
    run_end = of_page2(eo + run)
    src = q + offset[0][:, None] + jnp.sum(jnp.where(run_end[:, :, None] <= q[None], step[:, :, None], 0), axis=0)
    zero_xt = TR - SUBLANES
    esrc = jnp.where((q < tot_p[:, None]) & used[:, None], src, zero_xt).astype(jnp.int32)

    r = SUBLANES * jnp.arange(TR // SUBLANES, dtype=jnp.int32)
    end = (so + run).T
    offset_f = ((page_start * TM)[None, :] + eo - so).T
    step_f = jnp.concatenate([offset_f[1:] - offset_f[:-1], jnp.zeros_like(offset_f[:1])], axis=0)
    srcf = (r[None, :] + offset_f[0][:, None]
            + jnp.sum(jnp.where(end[:, :, None] <= r[None, None, :], step_f[:, :, None], 0), axis=0))
    zero_ys = (n_pages - 1) * TM
    fsrc = jnp.where(r[None, :] < end[-1][:, None], srcf, zero_ys).astype(jnp.int32)
    return page_tables, esrc.reshape(-1) // SUBLANES, fsrc.reshape(-1) // SUBLANES


def kernel(x, c, positions, w_ada, b_ada, w_in, ret_gn_w, w_pool, pool_scale, w_out, ln1_w, ln1_b, w_group, b_group,
           w_router, b_router, w1, w3, w2, ln2_w, ln2_b):
    B, S, D = x.shape
    T = B * S
    assert w_ada.shape[0] == DEPTH and D == D_MODEL and S % TS == 0

    inv_freq = ROPE_BASE ** (-jnp.arange(0, HEAD_DIM, 2, dtype=jnp.float32) / HEAD_DIM)
    half = HEAD_DIM // 2
    rope = jnp.stack([jnp.concatenate([inv_freq, inv_freq]),
                      jnp.concatenate([-jnp.ones((half,), jnp.float32), jnp.ones((half,), jnp.float32)])])

    nt = T // TS
    n_pages = (2 * T + nt * N_EXPERTS * (SUBLANES - 1)) // TM + N_EXPERTS + 1

    for l in range(DEPTH):
        ada = _ada(c, w_ada[l], b_ada[l]).reshape(B, 6, D)
        pad = SUBLANES - N_GROUPS
        wrt = jnp.concatenate([w_group[l].T, jnp.zeros((pad, D), jnp.float32), w_router[l].T], axis=0)
        brt = jnp.concatenate([b_group[l], jnp.zeros((pad,), jnp.float32), b_router[l]]).reshape(ROUTE_ROWS, 1)
        x1, xt, route, counts = _mix(
            x, positions, ada, rope, _bf(w_in[l]), _bf(w_out[l]), _bf(w_pool[l]),
            ret_gn_w[l].reshape(1, RET_WIDTH), pool_scale[l].reshape(1, POOL_WIDTH),
            ln1_w[l].reshape(1, D), ln1_b[l].reshape(1, D), wrt, brt)

        cnt = counts[:, :, :, 0].reshape(nt, N_EXPERTS).astype(jnp.int32)
        page_tables, esrc, fsrc = _gather_tables(cnt, n_pages)
        ys = _experts(page_tables, esrc, xt.reshape(nt * TR // SUBLANES, 2 * SUBLANES, HALF),
                      w1[l].reshape(N_EXPERTS, D, D_EXPERT), w3[l].reshape(N_EXPERTS, D, D_EXPERT),
                      w2[l].reshape(N_EXPERTS, D_EXPERT, D))
        x = _final(fsrc, x1.reshape(T, D), route.reshape(T, LANES), ada,
                   ln2_w[l].reshape(1, D), ln2_b[l].reshape(1, D),
                   ys.reshape(n_pages * TM // SUBLANES, 2 * SUBLANES, HALF), S).reshape(B, S, D)
    return x
```

```python
import math

import jax
import jax.numpy as jnp
from jax import lax
from jax.experimental import pallas as pl
from jax.experimental.pallas import tpu as pltpu

D_MODEL = 1024
RET_WIDTH = 512
RET_HEADS = 4
HEAD_DIM = 128
POOL_WIDTH = 512
POOL_WINDOWS = (2, 4, 8, 16)
POOL_GROUP_DIM = 128
IN_COLS = 4 * RET_WIDTH + POOL_WIDTH
N_GROUPS = 4
EXPERTS_PER_GROUP = 8
N_EXPERTS = 32
D_EXPERT = 256
DEPTH = 1
ALPHA = (2.0 * DEPTH) ** 0.25
LN_EPS = 1e-5
ROPE_BASE = 10000.0

LANES = 128
SUBLANES = 8
HALF = D_MODEL // 2
POOL_HALO = 16
TS = 256
TM = 512
PAGE_SPLIT_ROWS = 256
TR = -(-(2 * TS + N_EXPERTS * (SUBLANES - 1)) // LANES) * LANES
SEQS_PER_STEP = 2
TILES_PER_STEP = 2
ROUTE_ROWS = SUBLANES + N_EXPERTS
FETCH_AHEAD = 2
FETCH_SLOTS = FETCH_AHEAD + 1
FETCH_CHUNK = 64
VMEM_LIMIT = 56 * 1024 * 1024

_LOG_GAMMA = tuple(math.log1p(-(2.0 ** (-5.0 - h))) for h in range(RET_HEADS))
_HI = lax.Precision.HIGHEST


def _ln(x):
    mu = jnp.mean(x, axis=-1, keepdims=True)
    xc = x - mu
    var = jnp.mean(xc * xc, axis=-1, keepdims=True)
    return xc * lax.rsqrt(var + LN_EPS)


def _bf(x):
    return x.astype(jnp.bfloat16)


def _pack_rows(x):
    n = x.shape[0]
    lo = x[:, :HALF].reshape(n // SUBLANES, SUBLANES, HALF)
    hi = x[:, HALF:].reshape(n // SUBLANES, SUBLANES, HALF)
    return _bf(jnp.concatenate([lo, hi], axis=1).reshape(2 * n, HALF))


def _unpack_rows(z):
    n = z.shape[0] // 2
    zf = z.astype(jnp.float32).reshape(n // SUBLANES, 2 * SUBLANES, HALF)
    return _bf(zf[:, :SUBLANES, :].reshape(n, HALF)), _bf(zf[:, SUBLANES:, :].reshape(n, HALF))


def _ada_kernel(c_ref, w_ref, b_ref, o_ref):
    k = pl.program_id(0)

    @pl.when(k == 0)
    def _init():
        o_ref[...] = jnp.broadcast_to(b_ref[...], o_ref.shape)

    c = c_ref[...]
    ca = c * jax.nn.sigmoid(c)
    o_ref[...] += jnp.dot(ca, w_ref[...], precision=_HI, preferred_element_type=jnp.float32)


def _ada(c, w_ada, b_ada):
    B, D = c.shape
    n = w_ada.shape[1]
    kb = LANES
    c_blocks = c.reshape(B, D // kb, kb).transpose(1, 0, 2)
    return pl.pallas_call(
        _ada_kernel,
        grid=(D // kb,),
        in_specs=[pl.BlockSpec((None, B, kb), lambda k: (k, 0, 0)),
                  pl.BlockSpec((kb, n), lambda k: (k, 0)),
                  pl.BlockSpec((1, n), lambda k: (0, 0))],
        out_specs=pl.BlockSpec((B, n), lambda k: (0, 0)),
        out_shape=jax.ShapeDtypeStruct((B, n), jnp.float32),
        compiler_params=pltpu.CompilerParams(dimension_semantics=("arbitrary",), vmem_limit_bytes=VMEM_LIMIT),
        name="ada",
    )(c_blocks, w_ada, b_ada.reshape(1, n))


def _mix_kernel(x_ref, pos_ref, ada_ref, rope_ref, win_ref, wout_ref, wpool_ref, gnw_ref, pscale_ref,
                ln1w_ref, ln1b_ref, wrt_ref, brt_ref,
                x1_ref, xt_ref, route_ref, cnt_ref,
                state_ref, halo_ref, dmat_ref, qdec_ref, kdec_ref, cat_ref):
    b = pl.program_id(0)
    s = pl.program_id(1)

    @pl.when((b == 0) & (s == 0))
    def _init_tables():
        ri = lax.broadcasted_iota(jnp.int32, (TS, TS), 0)
        ci = lax.broadcasted_iota(jnp.int32, (TS, TS), 1)
        rel = (ri - ci).astype(jnp.float32)
        for h in range(RET_HEADS):
            dmat_ref[h] = jnp.where(rel >= 0.0, jnp.exp(jnp.maximum(rel, 0.0) * _LOG_GAMMA[h]), 0.0)
        row = lax.broadcasted_iota(jnp.int32, (TS, RET_WIDTH), 0).astype(jnp.float32)
        lane = lax.broadcasted_iota(jnp.int32, (TS, RET_WIDTH), 1)
        lg = jnp.full((TS, RET_WIDTH), _LOG_GAMMA[0], jnp.float32)
        for h in range(1, RET_HEADS):
            lg = jnp.where(lane >= h * HEAD_DIM, _LOG_GAMMA[h], lg)
        qdec_ref[...] = jnp.exp((row + 1.0) * lg)
        kdec_ref[...] = jnp.exp((TS - 1.0 - row) * lg)

    @pl.when(s == 0)
    def _init_carries():
        state_ref[...] = jnp.zeros_like(state_ref)
        halo_ref[...] = jnp.zeros_like(halo_ref)

    tiles = [_mix_tile(s, x_ref.at[j], pos_ref.at[j], ada_ref.at[j], rope_ref, win_ref, wout_ref, wpool_ref, gnw_ref,
                       pscale_ref, ln1w_ref, ln1b_ref, wrt_ref, brt_ref,
                       x1_ref.at[j], xt_ref.at[j], route_ref.at[j], cnt_ref.at[j],
                       state_ref.at[j], halo_ref.at[j], dmat_ref, qdec_ref, kdec_ref, cat_ref.at[j])
             for j in range(SEQS_PER_STEP)]
    _round_robin(tiles)


def _mix_tile(s, x_ref, pos_ref, ada_ref, rope_ref, win_ref, wout_ref, wpool_ref, gnw_ref, pscale_ref,
              ln1w_ref, ln1b_ref, wrt_ref, brt_ref,
              x1_ref, xt_ref, route_ref, cnt_ref,
              state_ref, halo_ref, dmat_ref, qdec_ref, kdec_ref, cat_ref):
    ada = ada_ref[...]
    shift1, scale1, gate1 = ada[0:1], ada[1:2], ada[2:3]
    shift2, scale2 = ada[3:4], ada[4:5]

    x = x_ref[...]
    u = _bf(_ln(x) * (1.0 + scale1) + shift1)
    yield

    posf = pos_ref[...].astype(jnp.float32)
    hl = lax.broadcasted_iota(jnp.int32, (TS // 2, HEAD_DIM), 1) < HEAD_DIM // 2
    ang = jnp.where(hl, posf[:TS // 2], posf[TS // 2:]) * rope_ref[0:1, :]
    cos_p, sin_p = jnp.cos(ang), jnp.sin(ang)
    cos_s, sin_s = pltpu.roll(cos_p, HEAD_DIM // 2, 1), pltpu.roll(sin_p, HEAD_DIM // 2, 1)
    cos_t = jnp.concatenate([jnp.where(hl, cos_p, cos_s), jnp.where(hl, cos_s, cos_p)], axis=0)
    sin_t = jnp.concatenate([jnp.where(hl, sin_p, sin_s), jnp.where(hl, sin_s, sin_p)], axis=0) * rope_ref[1:2, :]
    yield

    q = jnp.dot(u, win_ref[:, 0:RET_WIDTH], preferred_element_type=jnp.float32)
    k = jnp.dot(u, win_ref[:, RET_WIDTH:2 * RET_WIDTH], preferred_element_type=jnp.float32)
    v = jnp.dot(u, win_ref[:, 2 * RET_WIDTH:3 * RET_WIDTH], preferred_element_type=jnp.float32)
    g = jnp.dot(u, win_ref[:, 3 * RET_WIDTH:4 * RET_WIDTH], preferred_element_type=jnp.float32)
    p = jnp.dot(u, win_ref[:, 4 * RET_WIDTH:IN_COLS], preferred_element_type=jnp.float32)
    yield

    gnw = gnw_ref[...]
    for h in range(RET_HEADS):
        sl = slice(h * HEAD_DIM, (h + 1) * HEAD_DIM)
        qh, kh, vh = q[:, sl], k[:, sl], v[:, sl]
        qr = qh * cos_t + pltpu.roll(qh, HEAD_DIM // 2, 1) * sin_t
        kr = (kh * cos_t + pltpu.roll(kh, HEAD_DIM // 2, 1) * sin_t) * (HEAD_DIM ** -0.5)
        vb = _bf(vh)
        sc = lax.dot_general(_bf(qr), _bf(kr), (((1,), (1,)), ((), ())), preferred_element_type=jnp.float32)
        intra = jnp.dot(_bf(sc * dmat_ref[h]), vb, preferred_element_type=jnp.float32)
        st = state_ref[h]
        cross = jnp.dot(_bf(qr * qdec_ref[:, sl]), _bf(st), preferred_element_type=jnp.float32)
        kv = lax.dot_general(_bf(kr * kdec_ref[:, sl]), vb, (((0,), (0,)), ((), ())),
                             preferred_element_type=jnp.float32)
        state_ref[h] = st * math.exp(TS * _LOG_GAMMA[h]) + kv
        r = _ln(intra + cross) * gnw[:, sl]
        gh = g[:, sl]
        cat_ref[:, sl] = _bf(gh * jax.nn.sigmoid(gh) * r)
        yield

    pext = jnp.concatenate([halo_ref[...], p], axis=0)
    halo_ref[...] = p[TS - POOL_HALO:, :]
    t_abs = (s * TS + lax.broadcasted_iota(jnp.int32, (TS, 1), 0) + 1).astype(jnp.float32)
    pscale = pscale_ref[...]
    for grp, w in enumerate(POOL_WINDOWS):
        sl = slice(grp * POOL_GROUP_DIM, (grp + 1) * POOL_GROUP_DIM)
        acc = pext[:, sl]
        shift = 1
        while shift < w:
            acc = acc + pltpu.roll(acc, shift, 0)
            shift *= 2
        pooled = acc[POOL_HALO:, :] / jnp.minimum(t_abs, float(w)) - p[:, sl]
        po = jnp.dot(_bf(pooled), wpool_ref[grp], preferred_element_type=jnp.float32) * pscale[:, sl]
        cat_ref[:, RET_WIDTH + grp * POOL_GROUP_DIM:RET_WIDTH + (grp + 1) * POOL_GROUP_DIM] = _bf(po)
    yield

    mix = jnp.dot(cat_ref[...], wout_ref[...], preferred_element_type=jnp.float32)
    yield
    x1 = _ln(ALPHA * x + gate1 * mix) * ln1w_ref[...] + ln1b_ref[...]
    x1_ref[...] = x1
    u2 = _ln(x1) * (1.0 + scale2) + shift2
    yield

    w = wrt_ref[...]
    w_hi = _bf(w)
    w_lo = _bf(w - w_hi.astype(jnp.float32))
    u2_hi = _bf(u2)
    u2_lo = _bf(u2 - u2_hi.astype(jnp.float32))
    nt = (((1,), (1,)), ((), ()))
    logits = (lax.dot_general(w_hi, u2_hi, nt, preferred_element_type=jnp.float32)
              + lax.dot_general(w_hi, u2_lo, nt, preferred_element_type=jnp.float32)
              + lax.dot_general(w_lo, u2_hi, nt, preferred_element_type=jnp.float32)) + brt_ref[...]
    row8 = lax.broadcasted_iota(jnp.int32, (SUBLANES, TS), 0)
    neg = jnp.float32(-jnp.inf)
    gl = jnp.where(row8 < N_GROUPS, logits[0:SUBLANES], neg)
    gmax = jnp.max(gl, axis=0, keepdims=True)
    gidx = jnp.min(jnp.where(gl == gmax, row8, SUBLANES), axis=0, keepdims=True)
    gprob = 1.0 / jnp.sum(jnp.exp(gl - gmax), axis=0, keepdims=True)
    el = logits[SUBLANES:2 * SUBLANES]
    for grp in range(1, N_GROUPS):
        el = jnp.where(gidx == grp, logits[(grp + 1) * SUBLANES:(grp + 2) * SUBLANES], el)
    m1 = jnp.max(el, axis=0, keepdims=True)
    j1 = jnp.min(jnp.where(el == m1, row8, SUBLANES), axis=0, keepdims=True)
    el2 = jnp.where(row8 == j1, neg, el)
    m2 = jnp.max(el2, axis=0, keepdims=True)
    j2 = jnp.min(jnp.where(el2 == m2, row8, SUBLANES), axis=0, keepdims=True)
    e21 = jnp.exp(m2 - m1)
    den = 1.0 / (1.0 + e21)
    cw1 = gprob * den
    cw2 = gprob * e21 * den
    yield

    erow = lax.broadcasted_iota(jnp.int32, (N_EXPERTS, TS), 0)
    oh1 = erow == gidx * EXPERTS_PER_GROUP + j1
    oh2 = erow == gidx * EXPERTS_PER_GROUP + j2
    oh = jnp.where(oh1 | oh2, 1.0, 0.0)
    cnt = jnp.broadcast_to(jnp.sum(oh, axis=1, keepdims=True), (N_EXPERTS, LANES))
    run = jnp.floor((cnt + (SUBLANES - 1.0)) * (1.0 / SUBLANES)) * SUBLANES
    erow_l = lax.broadcasted_iota(jnp.int32, (N_EXPERTS, LANES), 0)
    run_end = run
    shift = 1
    while shift < N_EXPERTS:
        run_end = run_end + jnp.where(erow_l >= shift, pltpu.roll(run_end, shift, 0), 0.0)
        shift *= 2
    run_start = (run_end - run)[:, 0:1]
    ri = lax.broadcasted_iota(jnp.int32, (TS, TS), 0)
    ci = lax.broadcasted_iota(jnp.int32, (TS, TS), 1)
    earlier = _bf(jnp.where(ri < ci, 1.0, 0.0))
    before = jnp.dot(_bf(oh), earlier, preferred_element_type=jnp.float32) + run_start
    pos1 = jnp.sum(jnp.where(oh1, before, 0.0), axis=0, keepdims=True)
    pos2 = jnp.sum(jnp.where(oh2, before, 0.0), axis=0, keepdims=True)
    rr = lax.broadcasted_iota(jnp.int32, (TR, TS), 0).astype(jnp.float32)
    perm = _bf(jnp.where((rr == pos1) | (rr == pos2), 1.0, 0.0))
    yield
    xt_ref[...] = _pack_rows(jnp.dot(perm, u2_hi, preferred_element_type=jnp.float32))
    cnt_ref[...] = cnt

    rowl = lax.broadcasted_iota(jnp.int32, (LANES, TS), 0)
    rec = jnp.where(rowl == 0, pos1, 0.0)
    rec = jnp.where(rowl == 1, pos2, rec)
    rec = jnp.where(rowl == 2, cw1, rec)
    rec = jnp.where(rowl == 3, cw2, rec)
    route_ref[...] = rec.T


def _mix(x, positions, ada, rope, win, wout, wpool, gnw, pscale, ln1w, ln1b, wrt, brt):
    B, S, D = x.shape
    ns = S // TS
    assert B % SEQS_PER_STEP == 0
    P = SEQS_PER_STEP
    const2 = lambda b, s: (0, 0)
    const3 = lambda b, s: (0, 0, 0)
    tile = lambda b, s: (b, s, 0)
    flat = lambda b, s: (b, s, 0, 0)
    return pl.pallas_call(
        _mix_kernel,
        grid=(B // P, ns),
        in_specs=[
            pl.BlockSpec((P, TS, D), tile),
            pl.BlockSpec((P, TS, 1), tile),
            pl.BlockSpec((P, 6, D), lambda b, s: (b, 0, 0)),
            pl.BlockSpec((2, LANES), const2),
            pl.BlockSpec((D, IN_COLS), const2),
            pl.BlockSpec((D, D), const2),
            pl.BlockSpec((len(POOL_WINDOWS), POOL_GROUP_DIM, POOL_GROUP_DIM), const3),
            pl.BlockSpec((1, RET_WIDTH), const2),
            pl.BlockSpec((1, POOL_WIDTH), const2),
            pl.BlockSpec((1, D), const2),
            pl.BlockSpec((1, D), const2),
            pl.BlockSpec((ROUTE_ROWS, D), const2),
            pl.BlockSpec((ROUTE_ROWS, 1), const2),
        ],
        out_specs=[
            pl.BlockSpec((P, TS, D), tile),
            pl.BlockSpec((P, None, 2 * TR, HALF), flat),
            pl.BlockSpec((P, TS, LANES), tile),
            pl.BlockSpec((P, None, N_EXPERTS, LANES), flat),
        ],
        out_shape=[
            jax.ShapeDtypeStruct((B, S, D), jnp.float32),
            jax.ShapeDtypeStruct((B, ns, 2 * TR, HALF), jnp.bfloat16),
            jax.ShapeDtypeStruct((B, S, LANES), jnp.float32),
            jax.ShapeDtypeStruct((B, ns, N_EXPERTS, LANES), jnp.float32),
        ],
        scratch_shapes=[
            pltpu.VMEM((P, RET_HEADS, HEAD_DIM, HEAD_DIM), jnp.float32),
            pltpu.VMEM((P, POOL_HALO, POOL_WIDTH), jnp.float32),
            pltpu.VMEM((RET_HEADS, TS, TS), jnp.float32),
            pltpu.VMEM((TS, RET_WIDTH), jnp.float32),
            pltpu.VMEM((TS, RET_WIDTH), jnp.float32),
            pltpu.VMEM((P, TS, D), jnp.bfloat16),
        ],
        compiler_params=pltpu.CompilerParams(dimension_semantics=("arbitrary", "arbitrary"),
                                             vmem_limit_bytes=VMEM_LIMIT),
        name="mix",
    )(x, positions.reshape(B, S, 1), ada, rope, win, wout, wpool, gnw, pscale, ln1w, ln1b, wrt, brt)


def _fetch_groups(src_ref, group_of, dst_ref, sem):
    for u in range(dst_ref.shape[0]):
        pltpu.make_async_copy(src_ref.at[group_of(u)], dst_ref.at[u], sem).start()


def _fetch_groups_task(src_ref, group_of, dst_ref, sem, chunk):
    for u in range(dst_ref.shape[0]):
        pltpu.make_async_copy(src_ref.at[group_of(u)], dst_ref.at[u], sem).start()
        if u % chunk == chunk - 1:
            yield


def _wait_fetch(src_ref, dst_ref, sem):
    pltpu.make_async_copy(src_ref.at[pl.ds(0, dst_ref.shape[0])], dst_ref, sem).wait()


def _expert_rows(x_ref, w1_ref, w3_ref, w2_ref, y_ref):
    n = PAGE_SPLIT_ROWS
    xl, xh = _unpack_rows(x_ref[...].reshape(2 * n, HALF))
    yield
    a = (jnp.dot(xl, w1_ref[:HALF, :], preferred_element_type=jnp.float32)
         + jnp.dot(xh, w1_ref[HALF:, :], preferred_element_type=jnp.float32))
    c = (jnp.dot(xl, w3_ref[:HALF, :], preferred_element_type=jnp.float32)
         + jnp.dot(xh, w3_ref[HALF:, :], preferred_element_type=jnp.float32))
    yield
    h = _bf(a * jax.nn.sigmoid(a) * c)
    yield
    y = jnp.dot(h, w2_ref[...], preferred_element_type=jnp.float32)
    yield
    y_ref[...] = _pack_rows(y)


def _round_robin(tasks):
    while tasks:
        tasks = [t for t in tasks if next(t, True) is None]


def _expert_kernel(used_ref, pe_ref, pv_ref, first_ref, wslot_ref, next_ref, src_ref, xt_ref, w1_hbm, w3_hbm, w2_hbm,
                   ys_ref, xbuf, ybuf, zbuf, w1_buf, w3_buf, w2_buf, w1_bf, w3_bf, w2_bf, sem, osem, wsem, zsem):
    n_used = used_ref[0]
    n_pages = ys_ref.shape[0] // (2 * TM)
    groups = TM // SUBLANES

    def fetch(page):
        @pl.when(page < n_used)
        def _start():
            slot = page % FETCH_SLOTS
            _fetch_groups(xt_ref, lambda u: src_ref[page * groups + u], xbuf.at[slot], sem.at[slot])

    def weight_copies(expert, ws):
        return [pltpu.make_async_copy(hbm.at[expert], buf.at[ws], wsem.at[ws])
                for hbm, buf in ((w1_hbm, w1_buf), (w3_hbm, w3_buf), (w2_hbm, w2_buf))]

    def page_out(page, oslot):
        return pltpu.make_async_copy(ybuf.at[oslot], ys_ref.at[pl.ds(pl.multiple_of(page * 2 * TM, 2 * TM), 2 * TM)],
                                     osem.at[oslot])

    for k in range(FETCH_AHEAD):
        fetch(k)

    zbuf[...] = jnp.zeros_like(zbuf)

    def zero_page(page):
        return pltpu.make_async_copy(zbuf, ys_ref.at[pl.ds(pl.multiple_of(page * 2 * TM, 2 * TM), 2 * TM)], zsem)

    def zero_start(page, carry):
        zero_page(page).start()
        return carry

    lax.fori_loop(n_used, n_pages, zero_start, 0)

    @pl.when(n_used > 0)
    def _first_weights():
        for c in weight_copies(pe_ref[0], wslot_ref[0]):
            c.start()

    def page_body(g, carry):
        fetch(g + FETCH_AHEAD)
        ws = wslot_ref[g]

        @pl.when(first_ref[g] == 1)
        def _new_expert():
            for c in weight_copies(pe_ref[g], ws):
                c.wait()
            w1_bf[...] = _bf(w1_buf[ws])
            w3_bf[...] = _bf(w3_buf[ws])
            w2_bf[...] = _bf(w2_buf[ws])

            @pl.when(next_ref[g] >= 0)
            def _stream_next():
                for c in weight_copies(next_ref[g], 1 - ws):
                    c.start()

        slot = g % FETCH_SLOTS
        oslot = g % 2
        _wait_fetch(xt_ref, xbuf.at[slot], sem.at[slot])

        @pl.when(g >= 2)
        def _reuse_out_buffer():
            page_out(g - 2, oslot).wait()

        per = PAGE_SPLIT_ROWS // SUBLANES
        parts = TM // PAGE_SPLIT_ROWS
        part_rows = lambda k: pl.ds(k * 2 * PAGE_SPLIT_ROWS, 2 * PAGE_SPLIT_ROWS)
        part = lambda k: _expert_rows(xbuf.at[slot, pl.ds(k * per, per)], w1_bf, w3_bf, w2_bf,
                                      ybuf.at[oslot, part_rows(k)])
        n_parts = (pv_ref[g] + PAGE_SPLIT_ROWS - 1) // PAGE_SPLIT_ROWS
        for n in range(1, parts + 1):
            @pl.when(n_parts == n)
            def _parts(n=n):
                _round_robin([part(k) for k in range(n)])
                for k in range(n, parts):
                    ybuf[oslot, part_rows(k)] = jnp.zeros((2 * PAGE_SPLIT_ROWS, HALF), ybuf.dtype)
        page_out(g, oslot).start()
        return carry

    lax.fori_loop(0, n_used, page_body, 0)

    for back in (2, 1):
        @pl.when(n_used >= back)
        def _drain(back=back):
            page_out(n_used - back, (n_used - back) % 2).wait()

    def zero_wait(page, carry):
        zero_page(page).wait()
        return carry

    lax.fori_loop(n_used, n_pages, zero_wait, 0)


def _experts(page_tables, esrc, xt, w1, w3, w2):
    D = D_MODEL
    n_pages = page_tables[1].shape[0]
    assert TM % PAGE_SPLIT_ROWS == 0
    grid_spec = pltpu.PrefetchScalarGridSpec(
        num_scalar_prefetch=len(page_tables) + 1,
        grid=(1,),
        in_specs=[pl.BlockSpec(memory_space=pl.ANY)] * 4,
        out_specs=pl.BlockSpec(memory_space=pl.ANY),
        scratch_shapes=[pltpu.VMEM((FETCH_SLOTS, TM // SUBLANES, 2 * SUBLANES, HALF), jnp.bfloat16),
                        pltpu.VMEM((2, 2 * TM, HALF), jnp.bfloat16),
                        pltpu.VMEM((2 * TM, HALF), jnp.bfloat16),
                        pltpu.VMEM((2, D, D_EXPERT), jnp.float32),
                        pltpu.VMEM((2, D, D_EXPERT), jnp.float32),
                        pltpu.VMEM((2, D_EXPERT, D), jnp.float32),
                        pltpu.VMEM((D, D_EXPERT), jnp.bfloat16),
                        pltpu.VMEM((D, D_EXPERT), jnp.bfloat16),
                        pltpu.VMEM((D_EXPERT, D), jnp.bfloat16),
                        pltpu.SemaphoreType.DMA((FETCH_SLOTS,)),
                        pltpu.SemaphoreType.DMA((2,)),
                        pltpu.SemaphoreType.DMA((2,)),
                        pltpu.SemaphoreType.DMA(())],
    )
    return pl.pallas_call(
        _expert_kernel,
        grid_spec=grid_spec,
        out_shape=jax.ShapeDtypeStruct((n_pages * 2 * TM, HALF), jnp.bfloat16),
        compiler_params=pltpu.CompilerParams(dimension_semantics=("arbitrary",), vmem_limit_bytes=VMEM_LIMIT),
        name="experts",
    )(*page_tables, esrc, xt, w1, w3, w2)


def _final_tile(y_ref, x1_ref, route_ref, ada_ref, lnw_ref, lnb_ref, o_ref):
    route = route_ref[...]
    pos1, pos2, cw1, cw2 = route[:, 0:1], route[:, 1:2], route[:, 2:3], route[:, 3:4]
    col = lax.broadcasted_iota(jnp.int32, (TS, TR), 1).astype(jnp.float32)
    wmat = _bf(jnp.where(col == pos1, cw1, 0.0) + jnp.where(col == pos2, cw2, 0.0))
    yield
    yl, yh = _unpack_rows(y_ref[...].reshape(2 * TR, HALF))
    yield
    y = jnp.concatenate([jnp.dot(wmat, yl, preferred_element_type=jnp.float32),
                         jnp.dot(wmat, yh, preferred_element_type=jnp.float32)], axis=-1)
    yield
    gate2 = ada_ref[5:6, :]
    o_ref[...] = _ln(ALPHA * x1_ref[...] + gate2 * y) * lnw_ref[...] + lnb_ref[...]


def _final_kernel(src_ref, x1_ref, route_ref, ada_ref, lnw_ref, lnb_ref, ys_ref, o_ref, ybuf, sem):
    P = TILES_PER_STEP
    i = pl.program_id(0)
    last = pl.num_programs(0) - 1
    per = TR // SUBLANES
    groups = P * per

    def fetch(step, dst_slot):
        _fetch_groups(ys_ref, lambda u: src_ref[step * groups + u], ybuf.at[dst_slot], sem.at[dst_slot])

    @pl.when(i == 0)
    def _prime():
        for k in range(FETCH_AHEAD):
            fetch(k, k)

    slot = i % FETCH_SLOTS
    _wait_fetch(ys_ref, ybuf.at[slot], sem.at[slot])

    ahead = jnp.minimum(i + FETCH_AHEAD, last)
    ahead_slot = (i + FETCH_AHEAD) % FETCH_SLOTS
    rows = lambda k: pl.ds(k * TS, TS)
    _round_robin([_final_tile(ybuf.at[slot, pl.ds(k * per, per)], x1_ref.at[rows(k)], route_ref.at[rows(k)], ada_ref,
                              lnw_ref, lnb_ref, o_ref.at[rows(k)]) for k in range(P)]
                 + [_fetch_groups_task(ys_ref, lambda u: src_ref[ahead * groups + u], ybuf.at[ahead_slot],
                                       sem.at[ahead_slot], FETCH_CHUNK)])

    @pl.when(i == last)
    def _drain():
        for k in range(1, FETCH_SLOTS):
            other = (i + k) % FETCH_SLOTS
            _wait_fetch(ys_ref, ybuf.at[other], sem.at[other])


def _final(fsrc, x1, route, ada, lnw, lnb, ys, seq_len):
    T, D = x1.shape
    P = TILES_PER_STEP
    rows = P * TS
    assert seq_len % rows == 0
    steps_per_seq = seq_len // rows
    n_steps = T // rows
    assert n_steps > FETCH_AHEAD
    grid_spec = pltpu.PrefetchScalarGridSpec(
        num_scalar_prefetch=1,
        grid=(n_steps,),
        in_specs=[pl.BlockSpec((rows, D), lambda i, src: (i, 0)),
                  pl.BlockSpec((rows, LANES), lambda i, src: (i, 0)),
                  pl.BlockSpec((None, 6, D), lambda i, src: (i // steps_per_seq, 0, 0)),
                  pl.BlockSpec((1, D), lambda i, src: (0, 0)),
                  pl.BlockSpec((1, D), lambda i, src: (0, 0)),
                  pl.BlockSpec(memory_space=pl.ANY)],
        out_specs=pl.BlockSpec((rows, D), lambda i, src: (i, 0)),
        scratch_shapes=[pltpu.VMEM((FETCH_SLOTS, P * (TR // SUBLANES), 2 * SUBLANES, HALF), jnp.bfloat16),
                        pltpu.SemaphoreType.DMA((FETCH_SLOTS,))],
    )
    return pl.pallas_call(
        _final_kernel,
        grid_spec=grid_spec,
        out_shape=jax.ShapeDtypeStruct((T, D), jnp.float32),
        compiler_params=pltpu.CompilerParams(dimension_semantics=("arbitrary",), vmem_limit_bytes=VMEM_LIMIT),
        name="final",
    )(fsrc, x1, route, ada, lnw, lnb, ys)


def _gather_tables(cnt, n_pages):
    nt = cnt.shape[0]
    run = (cnt + SUBLANES - 1) // SUBLANES * SUBLANES
    so = jnp.cumsum(run, axis=1) - run
    eo = jnp.cumsum(run, axis=0) - run
    tot = jnp.sum(run, axis=0)
    pages_e = (tot + TM - 1) // TM
    page_end = jnp.cumsum(pages_e)
    page_start = page_end - pages_e
    g = jnp.arange(n_pages, dtype=jnp.int32)
    pe = jnp.minimum(jnp.sum(g[:, None] >= page_end[None, :], axis=1), N_EXPERTS - 1).astype(jnp.int32)
    used = g < page_end[-1]
    owner = (g[None, :] >= page_start[:, None]) & (g[None, :] < page_end[:, None])
    of_page = lambda a: jnp.sum(jnp.where(owner, a[:, None], 0), axis=0)
    of_page2 = lambda a: jnp.sum(jnp.where(owner[:, None, :], a.T[:, :, None], 0), axis=0)
    pj = g - of_page(page_start)
    tot_p = of_page(tot)
    pv = jnp.where(used, jnp.clip(tot_p - pj * TM, 0, TM), 0).astype(jnp.int32)
    experts = jnp.arange(N_EXPERTS, dtype=jnp.int32)
    has_pages = pages_e > 0
    wslot_e = (jnp.cumsum(has_pages) - 1) % 2
    later = (experts[None, :] > experts[:, None]) & has_pages[None, :]
    next_e = jnp.min(jnp.where(later, experts[None, :], N_EXPERTS), axis=1)
    next_e = jnp.where(next_e < N_EXPERTS, next_e, -1)
    first = (used & (pj == 0)).astype(jnp.int32)
    page_tables = (page_end[-1:].astype(jnp.int32), pe, pv, first, of_page(wslot_e).astype(jnp.int32),
                   jnp.where(used, of_page(next_e), -1).astype(jnp.int32))

    q = (pj * TM)[:, None] + SUBLANES * jnp.arange(TM // SUBLANES, dtype=jnp.int32)[None, :]
    tiles = jnp.arange(nt, dtype=jnp.int32)
    offset = of_page2(tiles[:, None] * TR + so - eo)
    step = jnp.concatenate([offset[1:] - offset[:-1], jnp.zeros_like(offset[:1])], axis=0)
    run_end = of_page2(eo + run)
    src = q + offset[0][:, None] + jnp.sum(jnp.where(run_end[:, :, None] <= q[None], step[:, :, None], 0), axis=0)
    zero_xt = TR - SUBLANES
    esrc = jnp.where((q < tot_p[:, None]) & used[:, None], src, zero_xt).astype(jnp.int32)

    r = SUBLANES * jnp.arange(TR // SUBLANES, dtype=jnp.int32)
    end = (so + run).T
    offset_f = ((page_start * TM)[None, :] + eo - so).T
    step_f = jnp.concatenate([offset_f[1:] - offset_f[:-1], jnp.zeros_like(offset_f[:1])], axis=0)
    srcf = (r[None, :] + offset_f[0][:, None]
            + jnp.sum(jnp.where(end[:, :, None] <= r[None, None, :], step_f[:, :, None], 0), axis=0))
    zero_ys = (n_pages - 1) * TM
    fsrc = jnp.where(r[None, :] < end[-1][:, None], srcf, zero_ys).astype(jnp.int32)
    return page_tables, esrc.reshape(-1) // SUBLANES, fsrc.reshape(-1) // SUBLANES


def kernel(x, c, positions, w_ada, b_ada, w_in, ret_gn_w, w_pool, pool_scale, w_out, ln1_w, ln1_b, w_group, b_group,
           w_router, b_router, w1, w3, w2, ln2_w, ln2_b):
    B, S, D = x.shape
    T = B * S
    assert w_ada.shape[0] == DEPTH and D == D_MODEL and S % TS == 0

    inv_freq = ROPE_BASE ** (-jnp.arange(0, HEAD_DIM, 2, dtype=jnp.float32) / HEAD_DIM)
    half = HEAD_DIM // 2
    rope = jnp.stack([jnp.concatenate([inv_freq, inv_freq]),
                      jnp.concatenate([-jnp.ones((half,), jnp.float32), jnp.ones((half,), jnp.float32)])])

    nt = T // TS
    n_pages = (2 * T + nt * N_EXPERTS * (SUBLANES - 1)) // TM + N_EXPERTS + 1

    for l in range(DEPTH):
        ada = _ada(c, w_ada[l], b_ada[l]).reshape(B, 6, D)
        pad = SUBLANES - N_GROUPS
        wrt = jnp.concatenate([w_group[l].T, jnp.zeros((pad, D), jnp.float32), w_router[l].T], axis=0)
        brt = jnp.concatenate([b_group[l], jnp.zeros((pad,), jnp.float32), b_router[l]]).reshape(ROUTE_ROWS, 1)
        x1, xt, route, counts = _mix(
            x, positions, ada, rope, _bf(w_in[l]), _bf(w_out[l]), _bf(w_pool[l]),
            ret_gn_w[l].reshape(1, RET_WIDTH), pool_scale[l].reshape(1, POOL_WIDTH),
            ln1_w[l].reshape(1, D), ln1_b[l].reshape(1, D), wrt, brt)

        cnt = counts[:, :, :, 0].reshape(nt, N_EXPERTS).astype(jnp.int32)
        page_tables, esrc, fsrc = _gather_tables(cnt, n_pages)
        ys = _experts(page_tables, esrc, xt.reshape(nt * TR // SUBLANES, 2 * SUBLANES, HALF),
                      w1[l].reshape(N_EXPERTS, D, D_EXPERT), w3[l].reshape(N_EXPERTS, D, D_EXPERT),
                      w2[l].reshape(N_EXPERTS, D_EXPERT, D))
        x = _final(fsrc, x1.reshape(T, D), route.reshape(T, LANES), ada,
                   ln2_w[l].reshape(1, D), ln2_b[l].reshape(1, D),
                   ys.reshape(n_pages * TM // SUBLANES, 2 * SUBLANES, HALF), S).reshape(B, S, D)
    return x
```

```python
import math

import jax
import jax.numpy as jnp
from jax import lax
from jax.experimental import pallas as pl
from jax.experimental.pallas import tpu as pltpu

D_MODEL = 1024
RET_WIDTH = 512
RET_HEADS = 4
HEAD_DIM = 128
POOL_WIDTH = 512
POOL_WINDOWS = (2, 4, 8, 16)
POOL_GROUP_DIM = 128
IN_COLS = 4 * RET_WIDTH + POOL_WIDTH
N_GROUPS = 4
EXPERTS_PER_GROUP = 8
N_EXPERTS = 32
D_EXPERT = 256
DEPTH = 1
ALPHA = (2.0 * DEPTH) ** 0.25
LN_EPS = 1e-5
ROPE_BASE = 10000.0

LANES = 128
SUBLANES = 8
HALF = D_MODEL // 2
POOL_HALO = 16
TS = 256
TM = 512
PAGE_SPLIT_ROWS = 256
TR = -(-(2 * TS + N_EXPERTS * (SUBLANES - 1)) // LANES) * LANES
SEQS_PER_STEP = 4
TILES_PER_STEP = 2
ROUTE_ROWS = SUBLANES + N_EXPERTS
FETCH_AHEAD = 2
FETCH_SLOTS = FETCH_AHEAD + 1
FETCH_CHUNK = 64
VMEM_LIMIT = 56 * 1024 * 1024

_LOG_GAMMA = tuple(math.log1p(-(2.0 ** (-5.0 - h))) for h in range(RET_HEADS))
_HI = lax.Precision.HIGHEST


def _ln(x):
    mu = jnp.mean(x, axis=-1, keepdims=True)
    xc = x - mu
    var = jnp.mean(xc * xc, axis=-1, keepdims=True)
    return xc * lax.rsqrt(var + LN_EPS)


def _bf(x):
    return x.astype(jnp.bfloat16)


def _pack_rows(x):
    n = x.shape[0]
    lo = x[:, :HALF].reshape(n // SUBLANES, SUBLANES, HALF)
    hi = x[:, HALF:].reshape(n // SUBLANES, SUBLANES, HALF)
    return _bf(jnp.concatenate([lo, hi], axis=1).reshape(2 * n, HALF))


def _unpack_rows(z):
    n = z.shape[0] // 2
    zf = z.astype(jnp.float32).reshape(n // SUBLANES, 2 * SUBLANES, HALF)
    return _bf(zf[:, :SUBLANES, :].reshape(n, HALF)), _bf(zf[:, SUBLANES:, :].reshape(n, HALF))


def _ada_kernel(c_ref, w_ref, b_ref, o_ref):
    k = pl.program_id(0)

    @pl.when(k == 0)
    def _init():
        o_ref[...] = jnp.broadcast_to(b_ref[...], o_ref.shape)

    c = c_ref[...]
    ca = c * jax.nn.sigmoid(c)
    o_ref[...] += jnp.dot(ca, w_ref[...], precision=_HI, preferred_element_type=jnp.float32)


def _ada(c, w_ada, b_ada):
    B, D = c.shape
    n = w_ada.shape[1]
    kb = LANES
    c_blocks = c.reshape(B, D // kb, kb).transpose(1, 0, 2)
    return pl.pallas_call(
        _ada_kernel,
        grid=(D // kb,),
        in_specs=[pl.BlockSpec((None, B, kb), lambda k: (k, 0, 0)),
                  pl.BlockSpec((kb, n), lambda k: (k, 0)),
                  pl.BlockSpec((1, n), lambda k: (0, 0))],
        out_specs=pl.BlockSpec((B, n), lambda k: (0, 0)),
        out_shape=jax.ShapeDtypeStruct((B, n), jnp.float32),
        compiler_params=pltpu.CompilerParams(dimension_semantics=("arbitrary",), vmem_limit_bytes=VMEM_LIMIT),
        name="ada",
    )(c_blocks, w_ada, b_ada.reshape(1, n))


def _mix_kernel(x_ref, pos_ref, ada_ref, rope_ref, win_ref, wout_ref, wpool_ref, gnw_ref, pscale_ref,
                ln1w_ref, ln1b_ref, wrt_ref, brt_ref,
                x1_ref, xt_ref, route_ref, cnt_ref,
                state_ref, halo_ref, dmat_ref, qdec_ref, kdec_ref, cat_ref):
    b = pl.program_id(0)
    s = pl.program_id(1)

    @pl.when((b == 0) & (s == 0))
    def _init_tables():
        ri = lax.broadcasted_iota(jnp.int32, (TS, TS), 0)
        ci = lax.broadcasted_iota(jnp.int32, (TS, TS), 1)
        rel = (ri - ci).astype(jnp.float32)
        for h in range(RET_HEADS):
            dmat_ref[h] = jnp.where(rel >= 0.0, jnp.exp(jnp.maximum(rel, 0.0) * _LOG_GAMMA[h]), 0.0)
        row = lax.broadcasted_iota(jnp.int32, (TS, RET_WIDTH), 0).astype(jnp.float32)
        lane = lax.broadcasted_iota(jnp.int32, (TS, RET_WIDTH), 1)
        lg = jnp.full((TS, RET_WIDTH), _LOG_GAMMA[0], jnp.float32)
        for h in range(1, RET_HEADS):
            lg = jnp.where(lane >= h * HEAD_DIM, _LOG_GAMMA[h], lg)
        qdec_ref[...] = jnp.exp((row + 1.0) * lg)
        kdec_ref[...] = jnp.exp((TS - 1.0 - row) * lg)

    @pl.when(s == 0)
    def _init_carries():
        state_ref[...] = jnp.zeros_like(state_ref)
        halo_ref[...] = jnp.zeros_like(halo_ref)

    tiles = [_mix_tile(s, x_ref.at[j], pos_ref.at[j], ada_ref.at[j], rope_ref, win_ref, wout_ref, wpool_ref, gnw_ref,
                       pscale_ref, ln1w_ref, ln1b_ref, wrt_ref, brt_ref,
                       x1_ref.at[j], xt_ref.at[j], route_ref.at[j], cnt_ref.at[j],
                       state_ref.at[j], halo_ref.at[j], dmat_ref, qdec_ref, kdec_ref, cat_ref.at[j])
             for j in range(SEQS_PER_STEP)]
    _round_robin(tiles)


def _mix_tile(s, x_ref, pos_ref, ada_ref, rope_ref, win_ref, wout_ref, wpool_ref, gnw_ref, pscale_ref,
              ln1w_ref, ln1b_ref, wrt_ref, brt_ref,
              x1_ref, xt_ref, route_ref, cnt_ref,
              state_ref, halo_ref, dmat_ref, qdec_ref, kdec_ref, cat_ref):
    ada = ada_ref[...]
    shift1, scale1, gate1 = ada[0:1], ada[1:2], ada[2:3]
    shift2, scale2 = ada[3:4], ada[4:5]

    x = x_ref[...]
    u = _bf(_ln(x) * (1.0 + scale1) + shift1)
    yield

    posf = pos_ref[...].astype(jnp.float32)
    hl = lax.broadcasted_iota(jnp.int32, (TS // 2, HEAD_DIM), 1) < HEAD_DIM // 2
    ang = jnp.where(hl, posf[:TS // 2], posf[TS // 2:]) * rope_ref[0:1, :]
    cos_p, sin_p = jnp.cos(ang), jnp.sin(ang)
    cos_s, sin_s = pltpu.roll(cos_p, HEAD_DIM // 2, 1), pltpu.roll(sin_p, HEAD_DIM // 2, 1)
    cos_t = jnp.concatenate([jnp.where(hl, cos_p, cos_s), jnp.where(hl, cos_s, cos_p)], axis=0)
    sin_t = jnp.concatenate([jnp.where(hl, sin_p, sin_s), jnp.where(hl, sin_s, sin_p)], axis=0) * rope_ref[1:2, :]
    yield

    q = jnp.dot(u, win_ref[:, 0:RET_WIDTH], preferred_element_type=jnp.float32)
    k = jnp.dot(u, win_ref[:, RET_WIDTH:2 * RET_WIDTH], preferred_element_type=jnp.float32)
    v = jnp.dot(u, win_ref[:, 2 * RET_WIDTH:3 * RET_WIDTH], preferred_element_type=jnp.float32)
    g = jnp.dot(u, win_ref[:, 3 * RET_WIDTH:4 * RET_WIDTH], preferred_element_type=jnp.float32)
    p = jnp.dot(u, win_ref[:, 4 * RET_WIDTH:IN_COLS], preferred_element_type=jnp.float32)
    yield

    gnw = gnw_ref[...]
    for h in range(RET_HEADS):
        sl = slice(h * HEAD_DIM, (h + 1) * HEAD_DIM)
        qh, kh, vh = q[:, sl], k[:, sl], v[:, sl]
        qr = qh * cos_t + pltpu.roll(qh, HEAD_DIM // 2, 1) * sin_t
        kr = (kh * cos_t + pltpu.roll(kh, HEAD_DIM // 2, 1) * sin_t) * (HEAD_DIM ** -0.5)
        vb = _bf(vh)
        sc = lax.dot_general(_bf(qr), _bf(kr), (((1,), (1,)), ((), ())), preferred_element_type=jnp.float32)
        intra = jnp.dot(_bf(sc * dmat_ref[h]), vb, preferred_element_type=jnp.float32)
        st = state_ref[h]
        cross = jnp.dot(_bf(qr * qdec_ref[:, sl]), _bf(st), preferred_element_type=jnp.float32)
        kv = lax.dot_general(_bf(kr * kdec_ref[:, sl]), vb, (((0,), (0,)), ((), ())),
                             preferred_element_type=jnp.float32)
        state_ref[h] = st * math.exp(TS * _LOG_GAMMA[h]) + kv
        r = _ln(intra + cross) * gnw[:, sl]
        gh = g[:, sl]
        cat_ref[:, sl] = _bf(gh * jax.nn.sigmoid(gh) * r)
        yield

    pext = jnp.concatenate([halo_ref[...], p], axis=0)
    halo_ref[...] = p[TS - POOL_HALO:, :]
    t_abs = (s * TS + lax.broadcasted_iota(jnp.int32, (TS, 1), 0) + 1).astype(jnp.float32)
    pscale = pscale_ref[...]
    for grp, w in enumerate(POOL_WINDOWS):
        sl = slice(grp * POOL_GROUP_DIM, (grp + 1) * POOL_GROUP_DIM)
        acc = pext[:, sl]
        shift = 1
        while shift < w:
            acc = acc + pltpu.roll(acc, shift, 0)
            shift *= 2
        pooled = acc[POOL_HALO:, :] / jnp.minimum(t_abs, float(w)) - p[:, sl]
        po = jnp.dot(_bf(pooled), wpool_ref[grp], preferred_element_type=jnp.float32) * pscale[:, sl]
        cat_ref[:, RET_WIDTH + grp * POOL_GROUP_DIM:RET_WIDTH + (grp + 1) * POOL_GROUP_DIM] = _bf(po)
    yield

    mix = jnp.dot(cat_ref[...], wout_ref[...], preferred_element_type=jnp.float32)
    yield
    x1 = _ln(ALPHA * x + gate1 * mix) * ln1w_ref[...] + ln1b_ref[...]
    x1_ref[...] = x1
    u2 = _ln(x1) * (1.0 + scale2) + shift2
    yield

    w = wrt_ref[...]
    w_hi = _bf(w)
    w_lo = _bf(w - w_hi.astype(jnp.float32))
    u2_hi = _bf(u2)
    u2_lo = _bf(u2 - u2_hi.astype(jnp.float32))
    nt = (((1,), (1,)), ((), ()))
    logits = (lax.dot_general(w_hi, u2_hi, nt, preferred_element_type=jnp.float32)
              + lax.dot_general(w_hi, u2_lo, nt, preferred_element_type=jnp.float32)
              + lax.dot_general(w_lo, u2_hi, nt, preferred_element_type=jnp.float32)) + brt_ref[...]
    row8 = lax.broadcasted_iota(jnp.int32, (SUBLANES, TS), 0)
    neg = jnp.float32(-jnp.inf)
    gl = jnp.where(row8 < N_GROUPS, logits[0:SUBLANES], neg)
    gmax = jnp.max(gl, axis=0, keepdims=True)
    gidx = jnp.min(jnp.where(gl == gmax, row8, SUBLANES), axis=0, keepdims=True)
    gprob = 1.0 / jnp.sum(jnp.exp(gl - gmax), axis=0, keepdims=True)
    el = logits[SUBLANES:2 * SUBLANES]
    for grp in range(1, N_GROUPS):
        el = jnp.where(gidx == grp, logits[(grp + 1) * SUBLANES:(grp + 2) * SUBLANES], el)
    m1 = jnp.max(el, axis=0, keepdims=True)
    j1 = jnp.min(jnp.where(el == m1, row8, SUBLANES), axis=0, keepdims=True)
    el2 = jnp.where(row8 == j1, neg, el)
    m2 = jnp.max(el2, axis=0, keepdims=True)
    j2 = jnp.min(jnp.where(el2 == m2, row8, SUBLANES), axis=0, keepdims=True)
    e21 = jnp.exp(m2 - m1)
    den = 1.0 / (1.0 + e21)
    cw1 = gprob * den
    cw2 = gprob * e21 * den
    yield

    erow = lax.broadcasted_iota(jnp.int32, (N_EXPERTS, TS), 0)
    oh1 = erow == gidx * EXPERTS_PER_GROUP + j1
    oh2 = erow == gidx * EXPERTS_PER_GROUP + j2
    oh = jnp.where(oh1 | oh2, 1.0, 0.0)
    cnt = jnp.broadcast_to(jnp.sum(oh, axis=1, keepdims=True), (N_EXPERTS, LANES))
    run = jnp.floor((cnt + (SUBLANES - 1.0)) * (1.0 / SUBLANES)) * SUBLANES
    erow_l = lax.broadcasted_iota(jnp.int32, (N_EXPERTS, LANES), 0)
    run_end = run
    shift = 1
    while shift < N_EXPERTS:
        run_end = run_end + jnp.where(erow_l >= shift, pltpu.roll(run_end, shift, 0), 0.0)
        shift *= 2
    run_start = (run_end - run)[:, 0:1]
    ri = lax.broadcasted_iota(jnp.int32, (TS, TS), 0)
    ci = lax.broadcasted_iota(jnp.int32, (TS, TS), 1)
    earlier = _bf(jnp.where(ri < ci, 1.0, 0.0))
    before = jnp.dot(_bf(oh), earlier, preferred_element_type=jnp.float32) + run_start
    pos1 = jnp.sum(jnp.where(oh1, before, 0.0), axis=0, keepdims=True)
    pos2 = jnp.sum(jnp.where(oh2, before, 0.0), axis=0, keepdims=True)
    rr = lax.broadcasted_iota(jnp.int32, (TR, TS), 0).astype(jnp.float32)
    perm = _bf(jnp.where((rr == pos1) | (rr == pos2), 1.0, 0.0))
    yield
    xt_ref[...] = _pack_rows(jnp.dot(perm, u2_hi, preferred_element_type=jnp.float32))
    cnt_ref[...] = cnt

    rowl = lax.broadcasted_iota(jnp.int32, (LANES, TS), 0)
    rec = jnp.where(rowl == 0, pos1, 0.0)
    rec = jnp.where(rowl == 1, pos2, rec)
    rec = jnp.where(rowl == 2, cw1, rec)
    rec = jnp.where(rowl == 3, cw2, rec)
    route_ref[...] = rec.T


def _mix(x, positions, ada, rope, win, wout, wpool, gnw, pscale, ln1w, ln1b, wrt, brt):
    B, S, D = x.shape
    ns = S // TS
    assert B % SEQS_PER_STEP == 0
    P = SEQS_PER_STEP
    const2 = lambda b, s: (0, 0)
    const3 = lambda b, s: (0, 0, 0)
    tile = lambda b, s: (b, s, 0)
    flat = lambda b, s: (b, s, 0, 0)
    return pl.pallas_call(
        _mix_kernel,
        grid=(B // P, ns),
        in_specs=[
            pl.BlockSpec((P, TS, D), tile),
            pl.BlockSpec((P, TS, 1), tile),
            pl.BlockSpec((P, 6, D), lambda b, s: (b, 0, 0)),
            pl.BlockSpec((2, LANES), const2),
            pl.BlockSpec((D, IN_COLS), const2),
            pl.BlockSpec((D, D), const2),
            pl.BlockSpec((len(POOL_WINDOWS), POOL_GROUP_DIM, POOL_GROUP_DIM), const3),
            pl.BlockSpec((1, RET_WIDTH), const2),
            pl.BlockSpec((1, POOL_WIDTH), const2),
            pl.BlockSpec((1, D), const2),
            pl.BlockSpec((1, D), const2),
            pl.BlockSpec((ROUTE_ROWS, D), const2),
            pl.BlockSpec((ROUTE_ROWS, 1), const2),
        ],
        out_specs=[
            pl.BlockSpec((P, TS, D), tile),
            pl.BlockSpec((P, None, 2 * TR, HALF), flat),
            pl.BlockSpec((P, TS, LANES), tile),
            pl.BlockSpec((P, None, N_EXPERTS, LANES), flat),
        ],
        out_shape=[
            jax.ShapeDtypeStruct((B, S, D), jnp.float32),
            jax.ShapeDtypeStruct((B, ns, 2 * TR, HALF), jnp.bfloat16),
            jax.ShapeDtypeStruct((B, S, LANES), jnp.float32),
            jax.ShapeDtypeStruct((B, ns, N_EXPERTS, LANES), jnp.float32),
        ],
        scratch_shapes=[
            pltpu.VMEM((P, RET_HEADS, HEAD_DIM, HEAD_DIM), jnp.float32),
            pltpu.VMEM((P, POOL_HALO, POOL_WIDTH), jnp.float32),
            pltpu.VMEM((RET_HEADS, TS, TS), jnp.float32),
            pltpu.VMEM((TS, RET_WIDTH), jnp.float32),
            pltpu.VMEM((TS, RET_WIDTH), jnp.float32),
            pltpu.VMEM((P, TS, D), jnp.bfloat16),
        ],
        compiler_params=pltpu.CompilerParams(dimension_semantics=("arbitrary", "arbitrary"),
                                             vmem_limit_bytes=VMEM_LIMIT),
        name="mix",
    )(x, positions.reshape(B, S, 1), ada, rope, win, wout, wpool, gnw, pscale, ln1w, ln1b, wrt, brt)


def _fetch_groups(src_ref, group_of, dst_ref, sem):
    for u in range(dst_ref.shape[0]):
        pltpu.make_async_copy(src_ref.at[group_of(u)], dst_ref.at[u], sem).start()


def _fetch_groups_task(src_ref, group_of, dst_ref, sem, chunk):
    for u in range(dst_ref.shape[0]):
        pltpu.make_async_copy(src_ref.at[group_of(u)], dst_ref.at[u], sem).start()
        if u % chunk == chunk - 1:
            yield


def _wait_fetch(src_ref, dst_ref, sem):
    pltpu.make_async_copy(src_ref.at[pl.ds(0, dst_ref.shape[0])], dst_ref, sem).wait()


def _expert_rows(x_ref, w1_ref, w3_ref, w2_ref, y_ref):
    n = PAGE_SPLIT_ROWS
    xl, xh = _unpack_rows(x_ref[...].reshape(2 * n, HALF))
    yield
    a = (jnp.dot(xl, w1_ref[:HALF, :], preferred_element_type=jnp.float32)
         + jnp.dot(xh, w1_ref[HALF:, :], preferred_element_type=jnp.float32))
    c = (jnp.dot(xl, w3_ref[:HALF, :], preferred_element_type=jnp.float32)
         + jnp.dot(xh, w3_ref[HALF:, :], preferred_element_type=jnp.float32))
    yield
    h = _bf(a * jax.nn.sigmoid(a) * c)
    yield
    y = jnp.dot(h, w2_ref[...], preferred_element_type=jnp.float32)
    yield
    y_ref[...] = _pack_rows(y)


def _round_robin(tasks):
    while tasks:
        tasks = [t for t in tasks if next(t, True) is None]


def _expert_kernel(used_ref, pe_ref, pv_ref, first_ref, wslot_ref, next_ref, src_ref, xt_ref, w1_hbm, w3_hbm, w2_hbm,
                   ys_ref, xbuf, ybuf, zbuf, w1_buf, w3_buf, w2_buf, w1_bf, w3_bf, w2_bf, sem, osem, wsem, zsem):
    n_used = used_ref[0]
    n_pages = ys_ref.shape[0] // (2 * TM)
    groups = TM // SUBLANES

    def fetch(page):
        @pl.when(page < n_used)
        def _start():
            slot = page % FETCH_SLOTS
            _fetch_groups(xt_ref, lambda u: src_ref[page * groups + u], xbuf.at[slot], sem.at[slot])

    def weight_copies(expert, ws):
        return [pltpu.make_async_copy(hbm.at[expert], buf.at[ws], wsem.at[ws])
                for hbm, buf in ((w1_hbm, w1_buf), (w3_hbm, w3_buf), (w2_hbm, w2_buf))]

    def page_out(page, oslot):
        return pltpu.make_async_copy(ybuf.at[oslot], ys_ref.at[pl.ds(pl.multiple_of(page * 2 * TM, 2 * TM), 2 * TM)],
                                     osem.at[oslot])

    for k in range(FETCH_AHEAD):
        fetch(k)

    zbuf[...] = jnp.zeros_like(zbuf)

    def zero_page(page):
        return pltpu.make_async_copy(zbuf, ys_ref.at[pl.ds(pl.multiple_of(page * 2 * TM, 2 * TM), 2 * TM)], zsem)

    def zero_start(page, carry):
        zero_page(page).start()
        return carry

    lax.fori_loop(n_used, n_pages, zero_start, 0)

    @pl.when(n_used > 0)
    def _first_weights():
        for c in weight_copies(pe_ref[0], wslot_ref[0]):
            c.start()

    def page_body(g, carry):
        fetch(g + FETCH_AHEAD)
        ws = wslot_ref[g]

        @pl.when(first_ref[g] == 1)
        def _new_expert():
            for c in weight_copies(pe_ref[g], ws):
                c.wait()
            w1_bf[...] = _bf(w1_buf[ws])
            w3_bf[...] = _bf(w3_buf[ws])
            w2_bf[...] = _bf(w2_buf[ws])

            @pl.when(next_ref[g] >= 0)
            def _stream_next():
                for c in weight_copies(next_ref[g], 1 - ws):
                    c.start()

        slot = g % FETCH_SLOTS
        oslot = g % 2
        _wait_fetch(xt_ref, xbuf.at[slot], sem.at[slot])

        @pl.when(g >= 2)
        def _reuse_out_buffer():
            page_out(g - 2, oslot).wait()

        per = PAGE_SPLIT_ROWS // SUBLANES
        parts = TM // PAGE_SPLIT_ROWS
        part_rows = lambda k: pl.ds(k * 2 * PAGE_SPLIT_ROWS, 2 * PAGE_SPLIT_ROWS)
        part = lambda k: _expert_rows(xbuf.at[slot, pl.ds(k * per, per)], w1_bf, w3_bf, w2_bf,
                                      ybuf.at[oslot, part_rows(k)])
        n_parts = (pv_ref[g] + PAGE_SPLIT_ROWS - 1) // PAGE_SPLIT_ROWS
        for n in range(1, parts + 1):
            @pl.when(n_parts == n)
            def _parts(n=n):
                _round_robin([part(k) for k in range(n)])
                for k in range(n, parts):
                    ybuf[oslot, part_rows(k)] = jnp.zeros((2 * PAGE_SPLIT_ROWS, HALF), ybuf.dtype)
        page_out(g, oslot).start()
        return carry

    lax.fori_loop(0, n_used, page_body, 0)

    for back in (2, 1):
        @pl.when(n_used >= back)
        def _drain(back=back):
            page_out(n_used - back, (n_used - back) % 2).wait()

    def zero_wait(page, carry):
        zero_page(page).wait()
        return carry

    lax.fori_loop(n_used, n_pages, zero_wait, 0)


def _experts(page_tables, esrc, xt, w1, w3, w2):
    D = D_MODEL
    n_pages = page_tables[1].shape[0]
    assert TM % PAGE_SPLIT_ROWS == 0
    grid_spec = pltpu.PrefetchScalarGridSpec(
        num_scalar_prefetch=len(page_tables) + 1,
        grid=(1,),
        in_specs=[pl.BlockSpec(memory_space=pl.ANY)] * 4,
        out_specs=pl.BlockSpec(memory_space=pl.ANY),
        scratch_shapes=[pltpu.VMEM((FETCH_SLOTS, TM // SUBLANES, 2 * SUBLANES, HALF), jnp.bfloat16),
                        pltpu.VMEM((2, 2 * TM, HALF), jnp.bfloat16),
                        pltpu.VMEM((2 * TM, HALF), jnp.bfloat16),
                        pltpu.VMEM((2, D, D_EXPERT), jnp.float32),
                        pltpu.VMEM((2, D, D_EXPERT), jnp.float32),
                        pltpu.VMEM((2, D_EXPERT, D), jnp.float32),
                        pltpu.VMEM((D, D_EXPERT), jnp.bfloat16),
                        pltpu.VMEM((D, D_EXPERT), jnp.bfloat16),
                        pltpu.VMEM((D_EXPERT, D), jnp.bfloat16),
                        pltpu.SemaphoreType.DMA((FETCH_SLOTS,)),
                        pltpu.SemaphoreType.DMA((2,)),
                        pltpu.SemaphoreType.DMA((2,)),
                        pltpu.SemaphoreType.DMA(())],
    )
    return pl.pallas_call(
        _expert_kernel,
        grid_spec=grid_spec,
        out_shape=jax.ShapeDtypeStruct((n_pages * 2 * TM, HALF), jnp.bfloat16),
        compiler_params=pltpu.CompilerParams(dimension_semantics=("arbitrary",), vmem_limit_bytes=VMEM_LIMIT),
        name="experts",
    )(*page_tables, esrc, xt, w1, w3, w2)


def _final_tile(y_ref, x1_ref, route_ref, ada_ref, lnw_ref, lnb_ref, o_ref):
    route = route_ref[...]
    pos1, pos2, cw1, cw2 = route[:, 0:1], route[:, 1:2], route[:, 2:3], route[:, 3:4]
    col = lax.broadcasted_iota(jnp.int32, (TS, TR), 1).astype(jnp.float32)
    wmat = _bf(jnp.where(col == pos1, cw1, 0.0) + jnp.where(col == pos2, cw2, 0.0))
    yield
    yl, yh = _unpack_rows(y_ref[...].reshape(2 * TR, HALF))
    yield
    y = jnp.concatenate([jnp.dot(wmat, yl, preferred_element_type=jnp.float32),
                         jnp.dot(wmat, yh, preferred_element_type=jnp.float32)], axis=-1)
    yield
    gate2 = ada_ref[5:6, :]
    o_ref[...] = _ln(ALPHA * x1_ref[...] + gate2 * y) * lnw_ref[...] + lnb_ref[...]


def _final_kernel(src_ref, x1_ref, route_ref, ada_ref, lnw_ref, lnb_ref, ys_ref, o_ref, ybuf, sem):
    P = TILES_PER_STEP
    i = pl.program_id(0)
    last = pl.num_programs(0) - 1
    per = TR // SUBLANES
    groups = P * per

    def fetch(step, dst_slot):
        _fetch_groups(ys_ref, lambda u: src_ref[step * groups + u], ybuf.at[dst_slot], sem.at[dst_slot])

    @pl.when(i == 0)
    def _prime():
        for k in range(FETCH_AHEAD):
            fetch(k, k)

    slot = i % FETCH_SLOTS
    _wait_fetch(ys_ref, ybuf.at[slot], sem.at[slot])

    ahead = jnp.minimum(i + FETCH_AHEAD, last)
    ahead_slot = (i + FETCH_AHEAD) % FETCH_SLOTS
    rows = lambda k: pl.ds(k * TS, TS)
    _round_robin([_final_tile(ybuf.at[slot, pl.ds(k * per, per)], x1_ref.at[rows(k)], route_ref.at[rows(k)], ada_ref,
                              lnw_ref, lnb_ref, o_ref.at[rows(k)]) for k in range(P)]
                 + [_fetch_groups_task(ys_ref, lambda u: src_ref[ahead * groups + u], ybuf.at[ahead_slot],
                                       sem.at[ahead_slot], FETCH_CHUNK)])

    @pl.when(i == last)
    def _drain():
        for k in range(1, FETCH_SLOTS):
            other = (i + k) % FETCH_SLOTS
            _wait_fetch(ys_ref, ybuf.at[other], sem.at[other])


def _final(fsrc, x1, route, ada, lnw, lnb, ys, seq_len):
    T, D = x1.shape
    P = TILES_PER_STEP
    rows = P * TS
    assert seq_len % rows == 0
    steps_per_seq = seq_len // rows
    n_steps = T // rows
    assert n_steps > FETCH_AHEAD
    grid_spec = pltpu.PrefetchScalarGridSpec(
        num_scalar_prefetch=1,
        grid=(n_steps,),
        in_specs=[pl.BlockSpec((rows, D), lambda i, src: (i, 0)),
                  pl.BlockSpec((rows, LANES), lambda i, src: (i, 0)),
                  pl.BlockSpec((None, 6, D), lambda i, src: (i // steps_per_seq, 0, 0)),
                  pl.BlockSpec((1, D), lambda i, src: (0, 0)),
                  pl.BlockSpec((1, D), lambda i, src: (0, 0)),
                  pl.BlockSpec(memory_space=pl.ANY)],
        out_specs=pl.BlockSpec((rows, D), lambda i, src: (i, 0)),
        scratch_shapes=[pltpu.VMEM((FETCH_SLOTS, P * (TR // SUBLANES), 2 * SUBLANES, HALF), jnp.bfloat16),
                        pltpu.SemaphoreType.DMA((FETCH_SLOTS,))],
    )
    return pl.pallas_call(
        _final_kernel,
        grid_spec=grid_spec,
        out_shape=jax.ShapeDtypeStruct((T, D), jnp.float32),
        compiler_params=pltpu.CompilerParams(dimension_semantics=("arbitrary",), vmem_limit_bytes=VMEM_LIMIT),
        name="final",
    )(fsrc, x1, route, ada, lnw, lnb, ys)


def _gather_tables(cnt, n_pages):
    nt = cnt.shape[0]
    run = (cnt + SUBLANES - 1) // SUBLANES * SUBLANES
    so = jnp.cumsum(run, axis=1) - run
    eo = jnp.cumsum(run, axis=0) - run
    tot = jnp.sum(run, axis=0)
    pages_e = (tot + TM - 1) // TM
    page_end = jnp.cumsum(pages_e)
    page_start = page_end - pages_e
    g = jnp.arange(n_pages, dtype=jnp.int32)
    pe = jnp.minimum(jnp.sum(g[:, None] >= page_end[None, :], axis=1), N_EXPERTS - 1).astype(jnp.int32)
    used = g < page_end[-1]
    owner = (g[None, :] >= page_start[:, None]) & (g[None, :] < page_end[:, None])
    of_page = lambda a: jnp.sum(jnp.where(owner, a[:, None], 0), axis=0)
    of_page2 = lambda a: jnp.sum(jnp.where(owner[:, None, :], a.T[:, :, None], 0), axis=0)
    pj = g - of_page(page_start)
    tot_p = of_page(tot)
    pv = jnp.where(used, jnp.clip(tot_p - pj * TM, 0, TM), 0).astype(jnp.int32)
    experts = jnp.arange(N_EXPERTS, dtype=jnp.int32)
    has_pages = pages_e > 0
    wslot_e = (jnp.cumsum(has_pages) - 1) % 2
    later = (experts[None, :] > experts[:, None]) & has_pages[None, :]
    next_e = jnp.min(jnp.where(later, experts[None, :], N_EXPERTS), axis=1)
    next_e = jnp.where(next_e < N_EXPERTS, next_e, -1)
    first = (used & (pj == 0)).astype(jnp.int32)
    page_tables = (page_end[-1:].astype(jnp.int32), pe, pv, first, of_page(wslot_e).astype(jnp.int32),
                   jnp.where(used, of_page(next_e), -1).astype(jnp.int32))

    q = (pj * TM)[:, None] + SUBLANES * jnp.arange(TM // SUBLANES, dtype=jnp.int32)[None, :]
    tiles = jnp.arange(nt, dtype=jnp.int32)
    offset = of_page2(tiles[:, None] * TR + so - eo)
    step = jnp.concatenate([offset[1:] - offset[:-1], jnp.zeros_like(offset[:1])], axis=0)
    run_end = of_page2(eo + run)
    src = q + offset[0][:, None] + jnp.sum(jnp.where(run_end[:, :, None] <= q[None], step[:, :, None], 0), axis=0)
    zero_xt = TR - SUBLANES
    esrc = jnp.where((q < tot_p[:, None]) & used[:, None], src, zero_xt).astype(jnp.int32)

    r = SUBLANES * jnp.arange(TR // SUBLANES, dtype=jnp.int32)
    end = (so + run).T
    offset_f = ((page_start * TM)[None, :] + eo - so).T
    step_f = jnp.concatenate([offset_f[1:] - offset_f[:-1], jnp.zeros_like(offset_f[:1])], axis=0)
    srcf = (r[None, :] + offset_f[0][:, None]
            + jnp.sum(jnp.where(end[:, :, None] <= r[None, None, :], step_f[:, :, None], 0), axis=0))
    zero_ys = (n_pages - 1) * TM
    fsrc = jnp.where(r[None, :] < end[-1][:, None], srcf, zero_ys).astype(jnp.int32)
    return page_tables, esrc.reshape(-1) // SUBLANES, fsrc.reshape(-1) // SUBLANES


def kernel(x, c, positions, w_ada, b_ada, w_in, ret_gn_w, w_pool, pool_scale, w_out, ln1_w, ln1_b, w_group, b_group,
           w_router, b_router, w1, w3, w2, ln2_w, ln2_b):
    B, S, D = x.shape
    T = B * S
    assert w_ada.shape[0] == DEPTH and D == D_MODEL and S % TS == 0

    inv_freq = ROPE_BASE ** (-jnp.arange(0, HEAD_DIM, 2, dtype=jnp.float32) / HEAD_DIM)
    half = HEAD_DIM // 2
    rope = jnp.stack([jnp.concatenate([inv_freq, inv_freq]),
                      jnp.concatenate([-jnp.ones((half,), jnp.float32), jnp.ones((half,), jnp.float32)])])

    nt = T // TS
    n_pages = (2 * T + nt * N_EXPERTS * (SUBLANES - 1)) // TM + N_EXPERTS + 1

    for l in range(DEPTH):
        ada = _ada(c, w_ada[l], b_ada[l]).reshape(B, 6, D)
        pad = SUBLANES - N_GROUPS
        wrt = jnp.concatenate([w_group[l].T, jnp.zeros((pad, D), jnp.float32), w_router[l].T], axis=0)
        brt = jnp.concatenate([b_group[l], jnp.zeros((pad,), jnp.float32), b_router[l]]).reshape(ROUTE_ROWS, 1)
        x1, xt, route, counts = _mix(
            x, positions, ada, rope, _bf(w_in[l]), _bf(w_out[l]), _bf(w_pool[l]),
            ret_gn_w[l].reshape(1, RET_WIDTH), pool_scale[l].reshape(1, POOL_WIDTH),
            ln1_w[l].reshape(1, D), ln1_b[l].reshape(1, D), wrt, brt)

        cnt = counts[:, :, :, 0].reshape(nt, N_EXPERTS).astype(jnp.int32)
        page_tables, esrc, fsrc = _gather_tables(cnt, n_pages)
        ys = _experts(page_tables, esrc, xt.reshape(nt * TR // SUBLANES, 2 * SUBLANES, HALF),
                      w1[l].reshape(N_EXPERTS, D, D_EXPERT), w3[l].reshape(N_EXPERTS, D, D_EXPERT),
                      w2[l].reshape(N_EXPERTS, D_EXPERT, D))
        x = _final(fsrc, x1.reshape(T, D), route.reshape(T, LANES), ada,
                   ln2_w[l].reshape(1, D), ln2_b[l].reshape(1, D),
                   ys.reshape(n_pages * TM // SUBLANES, 2 * SUBLANES, HALF), S).reshape(B, S, D)
    return x
```

```python
import math

import jax
import jax.numpy as jnp
from jax import lax
from jax.experimental import pallas as pl
from jax.experimental.pallas import tpu as pltpu

D_MODEL = 1024
RET_WIDTH = 512
RET_HEADS = 4
HEAD_DIM = 128
POOL_WIDTH = 512
POOL_WINDOWS = (2, 4, 8, 16)
POOL_GROUP_DIM = 128
IN_COLS = 4 * RET_WIDTH + POOL_WIDTH
N_GROUPS = 4
EXPERTS_PER_GROUP = 8
N_EXPERTS = 32
D_EXPERT = 256
DEPTH = 1
ALPHA = (2.0 * DEPTH) ** 0.25
LN_EPS = 1e-5
ROPE_BASE = 10000.0

LANES = 128
SUBLANES = 8
ROW_GROUP = 16
HALF = D_MODEL // 2
POOL_HALO = 16
TS = 256
TM = 512
PAGE_SPLIT_ROWS = 256
TR = -(-(2 * TS + N_EXPERTS * (ROW_GROUP - 1)) // LANES) * LANES
SEQS_PER_STEP = 4
TILES_PER_STEP = 2
ROUTE_ROWS = SUBLANES + N_EXPERTS
FETCH_AHEAD = 2
FETCH_SLOTS = FETCH_AHEAD + 1
FETCH_CHUNK = 64
VMEM_LIMIT = 60 * 1024 * 1024

_LOG_GAMMA = tuple(math.log1p(-(2.0 ** (-5.0 - h))) for h in range(RET_HEADS))
_HI = lax.Precision.HIGHEST


def _ln(x):
    mu = jnp.mean(x, axis=-1, keepdims=True)
    xc = x - mu
    var = jnp.mean(xc * xc, axis=-1, keepdims=True)
    return xc * lax.rsqrt(var + LN_EPS)


def _bf(x):
    return x.astype(jnp.bfloat16)


def _pack_rows(x):
    n = x.shape[0]
    lo = x[:, :HALF].reshape(n // ROW_GROUP, ROW_GROUP, HALF)
    hi = x[:, HALF:].reshape(n // ROW_GROUP, ROW_GROUP, HALF)
    return _bf(jnp.concatenate([lo, hi], axis=1).reshape(2 * n, HALF))


def _unpack_rows(z):
    n = z.shape[0] // 2
    zf = z.astype(jnp.float32).reshape(n // ROW_GROUP, 2 * ROW_GROUP, HALF)
    return _bf(zf[:, :ROW_GROUP, :].reshape(n, HALF)), _bf(zf[:, ROW_GROUP:, :].reshape(n, HALF))


def _ada_kernel(c_ref, w_ref, b_ref, o_ref):
    k = pl.program_id(0)

    @pl.when(k == 0)
    def _init():
        o_ref[...] = jnp.broadcast_to(b_ref[...], o_ref.shape)

    c = c_ref[...]
    ca = c * jax.nn.sigmoid(c)
    o_ref[...] += jnp.dot(ca, w_ref[...], precision=_HI, preferred_element_type=jnp.float32)


def _ada(c, w_ada, b_ada):
    B, D = c.shape
    n = w_ada.shape[1]
    kb = LANES
    c_blocks = c.reshape(B, D // kb, kb).transpose(1, 0, 2)
    return pl.pallas_call(
        _ada_kernel,
        grid=(D // kb,),
        in_specs=[pl.BlockSpec((None, B, kb), lambda k: (k, 0, 0)),
                  pl.BlockSpec((kb, n), lambda k: (k, 0)),
                  pl.BlockSpec((1, n), lambda k: (0, 0))],
        out_specs=pl.BlockSpec((B, n), lambda k: (0, 0)),
        out_shape=jax.ShapeDtypeStruct((B, n), jnp.float32),
        compiler_params=pltpu.CompilerParams(dimension_semantics=("arbitrary",), vmem_limit_bytes=VMEM_LIMIT),
        name="ada",
    )(c_blocks, w_ada, b_ada.reshape(1, n))


def _mix_kernel(x_ref, pos_ref, ada_ref, rope_ref, win_ref, wout_ref, wpool_ref, gnw_ref, pscale_ref,
                ln1w_ref, ln1b_ref, wrt_ref, brt_ref,
                x1_ref, xt_ref, route_ref, cnt_ref,
                state_ref, halo_ref, dmat_ref, qdec_ref, kdec_ref, cat_ref):
    b = pl.program_id(0)
    s = pl.program_id(1)

    @pl.when((b == 0) & (s == 0))
    def _init_tables():
        ri = lax.broadcasted_iota(jnp.int32, (TS, TS), 0)
        ci = lax.broadcasted_iota(jnp.int32, (TS, TS), 1)
        rel = (ri - ci).astype(jnp.float32)
        for h in range(RET_HEADS):
            dmat_ref[h] = jnp.where(rel >= 0.0, jnp.exp(jnp.maximum(rel, 0.0) * _LOG_GAMMA[h]), 0.0)
        row = lax.broadcasted_iota(jnp.int32, (TS, RET_WIDTH), 0).astype(jnp.float32)
        lane = lax.broadcasted_iota(jnp.int32, (TS, RET_WIDTH), 1)
        lg = jnp.full((TS, RET_WIDTH), _LOG_GAMMA[0], jnp.float32)
        for h in range(1, RET_HEADS):
            lg = jnp.where(lane >= h * HEAD_DIM, _LOG_GAMMA[h], lg)
        qdec_ref[...] = jnp.exp((row + 1.0) * lg)
        kdec_ref[...] = jnp.exp((TS - 1.0 - row) * lg)

    @pl.when(s == 0)
    def _init_carries():
        state_ref[...] = jnp.zeros_like(state_ref)
        halo_ref[...] = jnp.zeros_like(halo_ref)

    tiles = [_mix_tile(s, x_ref.at[j], pos_ref.at[j], ada_ref.at[j], rope_ref, win_ref, wout_ref, wpool_ref, gnw_ref,
                       pscale_ref, ln1w_ref, ln1b_ref, wrt_ref, brt_ref,
                       x1_ref.at[j], xt_ref.at[j], route_ref.at[j], cnt_ref.at[j],
                       state_ref.at[j], halo_ref.at[j], dmat_ref, qdec_ref, kdec_ref, cat_ref.at[j])
             for j in range(SEQS_PER_STEP)]
    _round_robin(tiles)


def _mix_tile(s, x_ref, pos_ref, ada_ref, rope_ref, win_ref, wout_ref, wpool_ref, gnw_ref, pscale_ref,
              ln1w_ref, ln1b_ref, wrt_ref, brt_ref,
              x1_ref, xt_ref, route_ref, cnt_ref,
              state_ref, halo_ref, dmat_ref, qdec_ref, kdec_ref, cat_ref):
    ada = ada_ref[...]
    shift1, scale1, gate1 = ada[0:1], ada[1:2], ada[2:3]
    shift2, scale2 = ada[3:4], ada[4:5]

    x = x_ref[...]
    u = _bf(_ln(x) * (1.0 + scale1) + shift1)
    yield

    posf = jnp.broadcast_to(pos_ref[...].astype(jnp.float32), (LANES, TS)).T
    hl = lax.broadcasted_iota(jnp.int32, (TS // 2, HEAD_DIM), 1) < HEAD_DIM // 2
    ang = jnp.where(hl, posf[:TS // 2], posf[TS // 2:]) * rope_ref[0:1, :]
    cos_p, sin_p = jnp.cos(ang), jnp.sin(ang)
    cos_s, sin_s = pltpu.roll(cos_p, HEAD_DIM // 2, 1), pltpu.roll(sin_p, HEAD_DIM // 2, 1)
    cos_t = jnp.concatenate([jnp.where(hl, cos_p, cos_s), jnp.where(hl, cos_s, cos_p)], axis=0)
    sin_t = jnp.concatenate([jnp.where(hl, sin_p, sin_s), jnp.where(hl, sin_s, sin_p)], axis=0) * rope_ref[1:2, :]
    yield

    q = jnp.dot(u, win_ref[:, 0:RET_WIDTH], preferred_element_type=jnp.float32)
    k = jnp.dot(u, win_ref[:, RET_WIDTH:2 * RET_WIDTH], preferred_element_type=jnp.float32)
    v = jnp.dot(u, win_ref[:, 2 * RET_WIDTH:3 * RET_WIDTH], preferred_element_type=jnp.float32)
    g = jnp.dot(u, win_ref[:, 3 * RET_WIDTH:4 * RET_WIDTH], preferred_element_type=jnp.float32)
    p = jnp.dot(u, win_ref[:, 4 * RET_WIDTH:IN_COLS], preferred_element_type=jnp.float32)
    yield

    gnw = gnw_ref[...]
    for h in range(RET_HEADS):
        sl = slice(h * HEAD_DIM, (h + 1) * HEAD_DIM)
        qh, kh, vh = q[:, sl], k[:, sl], v[:, sl]
        qr = qh * cos_t + pltpu.roll(qh, HEAD_DIM // 2, 1) * sin_t
        kr = (kh * cos_t + pltpu.roll(kh, HEAD_DIM // 2, 1) * sin_t) * (HEAD_DIM ** -0.5)
        vb = _bf(vh)
        sc = lax.dot_general(_bf(qr), _bf(kr), (((1,), (1,)), ((), ())), preferred_element_type=jnp.float32)
        intra = jnp.dot(_bf(sc * dmat_ref[h]), vb, preferred_element_type=jnp.float32)
        st = state_ref[h]
        cross = jnp.dot(_bf(qr * qdec_ref[:, sl]), _bf(st), preferred_element_type=jnp.float32)
        kv = lax.dot_general(_bf(kr * kdec_ref[:, sl]), vb, (((0,), (0,)), ((), ())),
                             preferred_element_type=jnp.float32)
        state_ref[h] = st * math.exp(TS * _LOG_GAMMA[h]) + kv
        r = _ln(intra + cross) * gnw[:, sl]
        gh = g[:, sl]
        cat_ref[:, sl] = _bf(gh * jax.nn.sigmoid(gh) * r)
        yield

    pext = jnp.concatenate([halo_ref[...], p], axis=0)
    halo_ref[...] = p[TS - POOL_HALO:, :]
    t_abs = (s * TS + lax.broadcasted_iota(jnp.int32, (TS, 1), 0) + 1).astype(jnp.float32)
    pscale = pscale_ref[...]
    for grp, w in enumerate(POOL_WINDOWS):
        sl = slice(grp * POOL_GROUP_DIM, (grp + 1) * POOL_GROUP_DIM)
        acc = pext[:, sl]
        shift = 1
        while shift < w:
            acc = acc + pltpu.roll(acc, shift, 0)
            shift *= 2
        pooled = acc[POOL_HALO:, :] / jnp.minimum(t_abs, float(w)) - p[:, sl]
        po = jnp.dot(_bf(pooled), wpool_ref[grp], preferred_element_type=jnp.float32) * pscale[:, sl]
        cat_ref[:, RET_WIDTH + grp * POOL_GROUP_DIM:RET_WIDTH + (grp + 1) * POOL_GROUP_DIM] = _bf(po)
    yield

    mix = jnp.dot(cat_ref[...], wout_ref[...], preferred_element_type=jnp.float32)
    yield
    x1 = _ln(ALPHA * x + gate1 * mix) * ln1w_ref[...] + ln1b_ref[...]
    x1_ref[...] = x1
    u2 = _ln(x1) * (1.0 + scale2) + shift2
    yield

    w = wrt_ref[...]
    w_hi = _bf(w)
    w_lo = _bf(w - w_hi.astype(jnp.float32))
    u2_hi = _bf(u2)
    u2_lo = _bf(u2 - u2_hi.astype(jnp.float32))
    nt = (((1,), (1,)), ((), ()))
    logits = (lax.dot_general(w_hi, u2_hi, nt, preferred_element_type=jnp.float32)
              + lax.dot_general(w_hi, u2_lo, nt, preferred_element_type=jnp.float32)
              + lax.dot_general(w_lo, u2_hi, nt, preferred_element_type=jnp.float32)) + brt_ref[...]
    row8 = lax.broadcasted_iota(jnp.int32, (SUBLANES, TS), 0)
    neg = jnp.float32(-jnp.inf)
    gl = jnp.where(row8 < N_GROUPS, logits[0:SUBLANES], neg)
    gmax = jnp.max(gl, axis=0, keepdims=True)
    gidx = jnp.min(jnp.where(gl == gmax, row8, SUBLANES), axis=0, keepdims=True)
    gprob = 1.0 / jnp.sum(jnp.exp(gl - gmax), axis=0, keepdims=True)
    el = logits[SUBLANES:2 * SUBLANES]
    for grp in range(1, N_GROUPS):
        el = jnp.where(gidx == grp, logits[(grp + 1) * SUBLANES:(grp + 2) * SUBLANES], el)
    m1 = jnp.max(el, axis=0, keepdims=True)
    j1 = jnp.min(jnp.where(el == m1, row8, SUBLANES), axis=0, keepdims=True)
    el2 = jnp.where(row8 == j1, neg, el)
    m2 = jnp.max(el2, axis=0, keepdims=True)
    j2 = jnp.min(jnp.where(el2 == m2, row8, SUBLANES), axis=0, keepdims=True)
    e21 = jnp.exp(m2 - m1)
    den = 1.0 / (1.0 + e21)
    cw1 = gprob * den
    cw2 = gprob * e21 * den
    yield

    erow = lax.broadcasted_iota(jnp.int32, (N_EXPERTS, TS), 0)
    oh1 = erow == gidx * EXPERTS_PER_GROUP + j1
    oh2 = erow == gidx * EXPERTS_PER_GROUP + j2
    oh = jnp.where(oh1 | oh2, 1.0, 0.0)
    cnt = jnp.broadcast_to(jnp.sum(oh, axis=1, keepdims=True), (N_EXPERTS, LANES))
    run = jnp.floor((cnt + (ROW_GROUP - 1.0)) * (1.0 / ROW_GROUP)) * ROW_GROUP
    erow_l = lax.broadcasted_iota(jnp.int32, (N_EXPERTS, LANES), 0)
    run_end = run
    shift = 1
    while shift < N_EXPERTS:
        run_end = run_end + jnp.where(erow_l >= shift, pltpu.roll(run_end, shift, 0), 0.0)
        shift *= 2
    run_start = (run_end - run)[:, 0:1]
    ri = lax.broadcasted_iota(jnp.int32, (TS, TS), 0)
    ci = lax.broadcasted_iota(jnp.int32, (TS, TS), 1)
    earlier = _bf(jnp.where(ri < ci, 1.0, 0.0))
    before = jnp.dot(_bf(oh), earlier, preferred_element_type=jnp.float32) + run_start
    pos1 = jnp.sum(jnp.where(oh1, before, 0.0), axis=0, keepdims=True)
    pos2 = jnp.sum(jnp.where(oh2, before, 0.0), axis=0, keepdims=True)
    rr = lax.broadcasted_iota(jnp.int32, (TR, TS), 0).astype(jnp.float32)
    perm = _bf(jnp.where((rr == pos1) | (rr == pos2), 1.0, 0.0))
    yield
    xt_ref[...] = _pack_rows(jnp.dot(perm, u2_hi, preferred_element_type=jnp.float32))
    cnt_ref[...] = cnt

    rowl = lax.broadcasted_iota(jnp.int32, (LANES, TS), 0)
    rec = jnp.where(rowl == 0, pos1, 0.0)
    rec = jnp.where(rowl == 1, pos2, rec)
    rec = jnp.where(rowl == 2, cw1, rec)
    rec = jnp.where(rowl == 3, cw2, rec)
    route_ref[...] = rec.T


def _mix(x, positions, ada, rope, win, wout, wpool, gnw, pscale, ln1w, ln1b, wrt, brt):
    B, S, D = x.shape
    ns = S // TS
    assert B % SEQS_PER_STEP == 0
    P = SEQS_PER_STEP
    const2 = lambda b, s: (0, 0)
    const3 = lambda b, s: (0, 0, 0)
    tile = lambda b, s: (b, s, 0)
    flat = lambda b, s: (b, s, 0, 0)
    return pl.pallas_call(
        _mix_kernel,
        grid=(B // P, ns),
        in_specs=[
            pl.BlockSpec((P, TS, D), tile),
            pl.BlockSpec((P, None, 1, TS), flat),
            pl.BlockSpec((P, 6, D), lambda b, s: (b, 0, 0)),
            pl.BlockSpec((2, LANES), const2),
            pl.BlockSpec((D, IN_COLS), const2, pipeline_mode=pl.Buffered(1)),
            pl.BlockSpec((D, D), const2, pipeline_mode=pl.Buffered(1)),
            pl.BlockSpec((len(POOL_WINDOWS), POOL_GROUP_DIM, POOL_GROUP_DIM), const3),
            pl.BlockSpec((1, RET_WIDTH), const2),
            pl.BlockSpec((1, POOL_WIDTH), const2),
            pl.BlockSpec((1, D), const2),
            pl.BlockSpec((1, D), const2),
            pl.BlockSpec((ROUTE_ROWS, D), const2),
            pl.BlockSpec((ROUTE_ROWS, 1), const2),
        ],
        out_specs=[
            pl.BlockSpec((P, TS, D), tile),
            pl.BlockSpec((P, None, 2 * TR, HALF), flat),
            pl.BlockSpec((P, TS, LANES), tile),
            pl.BlockSpec((P, None, N_EXPERTS, LANES), flat),
        ],
        out_shape=[
            jax.ShapeDtypeStruct((B, S, D), jnp.float32),
            jax.ShapeDtypeStruct((B, ns, 2 * TR, HALF), jnp.bfloat16),
            jax.ShapeDtypeStruct((B, S, LANES), jnp.float32),
            jax.ShapeDtypeStruct((B, ns, N_EXPERTS, LANES), jnp.float32),
        ],
        scratch_shapes=[
            pltpu.VMEM((P, RET_HEADS, HEAD_DIM, HEAD_DIM), jnp.float32),
            pltpu.VMEM((P, POOL_HALO, POOL_WIDTH), jnp.float32),
            pltpu.VMEM((RET_HEADS, TS, TS), jnp.float32),
            pltpu.VMEM((TS, RET_WIDTH), jnp.float32),
            pltpu.VMEM((TS, RET_WIDTH), jnp.float32),
            pltpu.VMEM((P, TS, D), jnp.bfloat16),
        ],
        compiler_params=pltpu.CompilerParams(dimension_semantics=("arbitrary", "arbitrary"),
                                             vmem_limit_bytes=VMEM_LIMIT),
        name="mix",
    )(x, positions.reshape(B, ns, 1, TS), ada, rope, win, wout, wpool, gnw, pscale, ln1w, ln1b, wrt, brt)


def _fetch_groups(src_ref, group_of, dst_ref, sem):
    for u in range(dst_ref.shape[0]):
        pltpu.make_async_copy(src_ref.at[group_of(u)], dst_ref.at[u], sem).start()


def _fetch_groups_task(src_ref, group_of, dst_ref, sem, chunk):
    for u in range(dst_ref.shape[0]):
        pltpu.make_async_copy(src_ref.at[group_of(u)], dst_ref.at[u], sem).start()
        if u % chunk == chunk - 1:
            yield


def _wait_fetch(src_ref, dst_ref, sem):
    pltpu.make_async_copy(src_ref.at[pl.ds(0, dst_ref.shape[0])], dst_ref, sem).wait()


def _expert_rows(x_ref, w1_ref, w3_ref, w2_ref, y_ref):
    n = PAGE_SPLIT_ROWS
    xl, xh = _unpack_rows(x_ref[...].reshape(2 * n, HALF))
    yield
    a = (jnp.dot(xl, w1_ref[:HALF, :], preferred_element_type=jnp.float32)
         + jnp.dot(xh, w1_ref[HALF:, :], preferred_element_type=jnp.float32))
    c = (jnp.dot(xl, w3_ref[:HALF, :], preferred_element_type=jnp.float32)
         + jnp.dot(xh, w3_ref[HALF:, :], preferred_element_type=jnp.float32))
    yield
    h = _bf(a * jax.nn.sigmoid(a) * c)
    yield
    y = jnp.dot(h, w2_ref[...], preferred_element_type=jnp.float32)
    yield
    y_ref[...] = _pack_rows(y)


def _round_robin(tasks):
    while tasks:
        tasks = [t for t in tasks if next(t, True) is None]


def _expert_kernel(used_ref, pe_ref, pv_ref, first_ref, wslot_ref, next_ref, src_ref, xt_ref, w1_hbm, w3_hbm, w2_hbm,
                   ys_ref, xbuf, ybuf, zbuf, w1_buf, w3_buf, w2_buf, w1_bf, w3_bf, w2_bf, sem, osem, wsem, zsem):
    n_used = used_ref[0]
    n_pages = ys_ref.shape[0] // (2 * TM)
    groups = TM // ROW_GROUP

    def fetch(page):
        @pl.when(page < n_used)
        def _start():
            slot = page % FETCH_SLOTS
            _fetch_groups(xt_ref, lambda u: src_ref[page * groups + u], xbuf.at[slot], sem.at[slot])

    def weight_copies(expert, ws):
        return [pltpu.make_async_copy(hbm.at[expert], buf.at[ws], wsem.at[ws])
                for hbm, buf in ((w1_hbm, w1_buf), (w3_hbm, w3_buf), (w2_hbm, w2_buf))]

    def page_out(page, oslot):
        return pltpu.make_async_copy(ybuf.at[oslot], ys_ref.at[pl.ds(pl.multiple_of(page * 2 * TM, 2 * TM), 2 * TM)],
                                     osem.at[oslot])

    for k in range(FETCH_AHEAD):
        fetch(k)

    zbuf[...] = jnp.zeros_like(zbuf)

    def zero_page(page):
        return pltpu.make_async_copy(zbuf, ys_ref.at[pl.ds(pl.multiple_of(page * 2 * TM, 2 * TM), 2 * TM)], zsem)

    def zero_start(page, carry):
        zero_page(page).start()
        return carry

    lax.fori_loop(n_used, n_pages, zero_start, 0)

    @pl.when(n_used > 0)
    def _first_weights():
        for c in weight_copies(pe_ref[0], wslot_ref[0]):
            c.start()

    def page_body(g, carry):
        fetch(g + FETCH_AHEAD)
        ws = wslot_ref[g]

        @pl.when(first_ref[g] == 1)
        def _new_expert():
            for c in weight_copies(pe_ref[g], ws):
                c.wait()
            w1_bf[...] = _bf(w1_buf[ws])
            w3_bf[...] = _bf(w3_buf[ws])
            w2_bf[...] = _bf(w2_buf[ws])

            @pl.when(next_ref[g] >= 0)
            def _stream_next():
                for c in weight_copies(next_ref[g], 1 - ws):
                    c.start()

        slot = g % FETCH_SLOTS
        oslot = g % 2
        _wait_fetch(xt_ref, xbuf.at[slot], sem.at[slot])

        @pl.when(g >= 2)
        def _reuse_out_buffer():
            page_out(g - 2, oslot).wait()

        per = PAGE_SPLIT_ROWS // ROW_GROUP
        parts = TM // PAGE_SPLIT_ROWS
        part_rows = lambda k: pl.ds(k * 2 * PAGE_SPLIT_ROWS, 2 * PAGE_SPLIT_ROWS)
        part = lambda k: _expert_rows(xbuf.at[slot, pl.ds(k * per, per)], w1_bf, w3_bf, w2_bf,
                                      ybuf.at[oslot, part_rows(k)])
        n_parts = (pv_ref[g] + PAGE_SPLIT_ROWS - 1) // PAGE_SPLIT_ROWS
        for n in range(1, parts + 1):
            @pl.when(n_parts == n)
            def _parts(n=n):
                _round_robin([part(k) for k in range(n)])
                for k in range(n, parts):
                    ybuf[oslot, part_rows(k)] = jnp.zeros((2 * PAGE_SPLIT_ROWS, HALF), ybuf.dtype)
        page_out(g, oslot).start()
        return carry

    lax.fori_loop(0, n_used, page_body, 0)

    for back in (2, 1):
        @pl.when(n_used >= back)
        def _drain(back=back):
            page_out(n_used - back, (n_used - back) % 2).wait()

    def zero_wait(page, carry):
        zero_page(page).wait()
        return carry

    lax.fori_loop(n_used, n_pages, zero_wait, 0)


def _experts(page_tables, esrc, xt, w1, w3, w2):
    D = D_MODEL
    n_pages = page_tables[1].shape[0]
    assert TM % PAGE_SPLIT_ROWS == 0
    grid_spec = pltpu.PrefetchScalarGridSpec(
        num_scalar_prefetch=len(page_tables) + 1,
        grid=(1,),
        in_specs=[pl.BlockSpec(memory_space=pl.ANY)] * 4,
        out_specs=pl.BlockSpec(memory_space=pl.ANY),
        scratch_shapes=[pltpu.VMEM((FETCH_SLOTS, TM // ROW_GROUP, 2 * ROW_GROUP, HALF), jnp.bfloat16),
                        pltpu.VMEM((2, 2 * TM, HALF), jnp.bfloat16),
                        pltpu.VMEM((2 * TM, HALF), jnp.bfloat16),
                        pltpu.VMEM((2, D, D_EXPERT), jnp.float32),
                        pltpu.VMEM((2, D, D_EXPERT), jnp.float32),
                        pltpu.VMEM((2, D_EXPERT, D), jnp.float32),
                        pltpu.VMEM((D, D_EXPERT), jnp.bfloat16),
                        pltpu.VMEM((D, D_EXPERT), jnp.bfloat16),
                        pltpu.VMEM((D_EXPERT, D), jnp.bfloat16),
                        pltpu.SemaphoreType.DMA((FETCH_SLOTS,)),
                        pltpu.SemaphoreType.DMA((2,)),
                        pltpu.SemaphoreType.DMA((2,)),
                        pltpu.SemaphoreType.DMA(())],
    )
    return pl.pallas_call(
        _expert_kernel,
        grid_spec=grid_spec,
        out_shape=jax.ShapeDtypeStruct((n_pages * 2 * TM, HALF), jnp.bfloat16),
        compiler_params=pltpu.CompilerParams(dimension_semantics=("arbitrary",), vmem_limit_bytes=VMEM_LIMIT),
        name="experts",
    )(*page_tables, esrc, xt, w1, w3, w2)


def _final_tile(y_ref, x1_ref, route_ref, ada_ref, lnw_ref, lnb_ref, o_ref):
    route = route_ref[...]
    pos1, pos2, cw1, cw2 = route[:, 0:1], route[:, 1:2], route[:, 2:3], route[:, 3:4]
    col = lax.broadcasted_iota(jnp.int32, (TS, TR), 1).astype(jnp.float32)
    wmat = _bf(jnp.where(col == pos1, cw1, 0.0) + jnp.where(col == pos2, cw2, 0.0))
    yield
    yl, yh = _unpack_rows(y_ref[...].reshape(2 * TR, HALF))
    yield
    y = jnp.concatenate([jnp.dot(wmat, yl, preferred_element_type=jnp.float32),
                         jnp.dot(wmat, yh, preferred_element_type=jnp.float32)], axis=-1)
    yield
    gate2 = ada_ref[5:6, :]
    o_ref[...] = _ln(ALPHA * x1_ref[...] + gate2 * y) * lnw_ref[...] + lnb_ref[...]


def _final_kernel(src_ref, x1_ref, route_ref, ada_ref, lnw_ref, lnb_ref, ys_ref, o_ref, ybuf, sem):
    P = TILES_PER_STEP
    i = pl.program_id(0)
    last = pl.num_programs(0) - 1
    per = TR // ROW_GROUP
    groups = P * per

    def fetch(step, dst_slot):
        _fetch_groups(ys_ref, lambda u: src_ref[step * groups + u], ybuf.at[dst_slot], sem.at[dst_slot])

    @pl.when(i == 0)
    def _prime():
        for k in range(FETCH_AHEAD):
            fetch(k, k)

    slot = i % FETCH_SLOTS
    _wait_fetch(ys_ref, ybuf.at[slot], sem.at[slot])

    ahead = jnp.minimum(i + FETCH_AHEAD, last)
    ahead_slot = (i + FETCH_AHEAD) % FETCH_SLOTS
    rows = lambda k: pl.ds(k * TS, TS)
    _round_robin([_final_tile(ybuf.at[slot, pl.ds(k * per, per)], x1_ref.at[rows(k)], route_ref.at[rows(k)], ada_ref,
                              lnw_ref, lnb_ref, o_ref.at[rows(k)]) for k in range(P)]
                 + [_fetch_groups_task(ys_ref, lambda u: src_ref[ahead * groups + u], ybuf.at[ahead_slot],
                                       sem.at[ahead_slot], FETCH_CHUNK)])

    @pl.when(i == last)
    def _drain():
        for k in range(1, FETCH_SLOTS):
            other = (i + k) % FETCH_SLOTS
            _wait_fetch(ys_ref, ybuf.at[other], sem.at[other])


def _final(fsrc, x1, route, ada, lnw, lnb, ys, seq_len):
    T, D = x1.shape
    P = TILES_PER_STEP
    rows = P * TS
    assert seq_len % rows == 0
    steps_per_seq = seq_len // rows
    n_steps = T // rows
    assert n_steps > FETCH_AHEAD
    grid_spec = pltpu.PrefetchScalarGridSpec(
        num_scalar_prefetch=1,
        grid=(n_steps,),
        in_specs=[pl.BlockSpec((rows, D), lambda i, src: (i, 0)),
                  pl.BlockSpec((rows, LANES), lambda i, src: (i, 0)),
                  pl.BlockSpec((None, 6, D), lambda i, src: (i // steps_per_seq, 0, 0)),
                  pl.BlockSpec((1, D), lambda i, src: (0, 0)),
                  pl.BlockSpec((1, D), lambda i, src: (0, 0)),
                  pl.BlockSpec(memory_space=pl.ANY)],
        out_specs=pl.BlockSpec((rows, D), lambda i, src: (i, 0)),
        scratch_shapes=[pltpu.VMEM((FETCH_SLOTS, P * (TR // ROW_GROUP), 2 * ROW_GROUP, HALF), jnp.bfloat16),
                        pltpu.SemaphoreType.DMA((FETCH_SLOTS,))],
    )
    return pl.pallas_call(
        _final_kernel,
        grid_spec=grid_spec,
        out_shape=jax.ShapeDtypeStruct((T, D), jnp.float32),
        compiler_params=pltpu.CompilerParams(dimension_semantics=("arbitrary",), vmem_limit_bytes=VMEM_LIMIT),
        name="final",
    )(fsrc, x1, route, ada, lnw, lnb, ys)


def _gather_tables(cnt, n_pages):
    nt = cnt.shape[0]
    run = (cnt + ROW_GROUP - 1) // ROW_GROUP * ROW_GROUP
    so = jnp.cumsum(run, axis=1) - run
    eo = jnp.cumsum(run, axis=0) - run
    tot = jnp.sum(run, axis=0)
    pages_e = (tot + TM - 1) // TM
    page_end = jnp.cumsum(pages_e)
    page_start = page_end - pages_e
    g = jnp.arange(n_pages, dtype=jnp.int32)
    pe = jnp.minimum(jnp.sum(g[:, None] >= page_end[None, :], axis=1), N_EXPERTS - 1).astype(jnp.int32)
    used = g < page_end[-1]
    owner = (g[None, :] >= page_start[:, None]) & (g[None, :] < page_end[:, None])
    of_page = lambda a: jnp.sum(jnp.where(owner, a[:, None], 0), axis=0)
    of_page2 = lambda a: jnp.sum(jnp.where(owner[:, None, :], a.T[:, :, None], 0), axis=0)
    pj = g - of_page(page_start)
    tot_p = of_page(tot)
    pv = jnp.where(used, jnp.clip(tot_p - pj * TM, 0, TM), 0).astype(jnp.int32)
    experts = jnp.arange(N_EXPERTS, dtype=jnp.int32)
    has_pages = pages_e > 0
    wslot_e = (jnp.cumsum(has_pages) - 1) % 2
    later = (experts[None, :] > experts[:, None]) & has_pages[None, :]
    next_e = jnp.min(jnp.where(later, experts[None, :], N_EXPERTS), axis=1)
    next_e = jnp.where(next_e < N_EXPERTS, next_e, -1)
    first = (used & (pj == 0)).astype(jnp.int32)
    page_tables = (page_end[-1:].astype(jnp.int32), pe, pv, first, of_page(wslot_e).astype(jnp.int32),
                   jnp.where(used, of_page(next_e), -1).astype(jnp.int32))

    q = (pj * TM)[:, None] + ROW_GROUP * jnp.arange(TM // ROW_GROUP, dtype=jnp.int32)[None, :]
    tiles = jnp.arange(nt, dtype=jnp.int32)
    offset = of_page2(tiles[:, None] * TR + so - eo)
    step = jnp.concatenate([offset[1:] - offset[:-1], jnp.zeros_like(offset[:1])], axis=0)
    run_end = of_page2(eo + run)
    src = q + offset[0][:, None] + jnp.sum(jnp.where(run_end[:, :, None] <= q[None], step[:, :, None], 0), axis=0)
    zero_xt = TR - ROW_GROUP
    esrc = jnp.where((q < tot_p[:, None]) & used[:, None], src, zero_xt).astype(jnp.int32)

    r = ROW_GROUP * jnp.arange(TR // ROW_GROUP, dtype=jnp.int32)
    end = (so + run).T
    offset_f = ((page_start * TM)[None, :] + eo - so).T
    step_f = jnp.concatenate([offset_f[1:] - offset_f[:-1], jnp.zeros_like(offset_f[:1])], axis=0)
    srcf = (r[None, :] + offset_f[0][:, None]
            + jnp.sum(jnp.where(end[:, :, None] <= r[None, None, :], step_f[:, :, None], 0), axis=0))
    zero_ys = (n_pages - 1) * TM
    fsrc = jnp.where(r[None, :] < end[-1][:, None], srcf, zero_ys).astype(jnp.int32)
    return page_tables, esrc.reshape(-1) // ROW_GROUP, fsrc.reshape(-1) // ROW_GROUP


def kernel(x, c, positions, w_ada, b_ada, w_in, ret_gn_w, w_pool, pool_scale, w_out, ln1_w, ln1_b, w_group, b_group,
           w_router, b_router, w1, w3, w2, ln2_w, ln2_b):
    B, S, D = x.shape
    T = B * S
    assert w_ada.shape[0] == DEPTH and D == D_MODEL and S % TS == 0

    inv_freq = ROPE_BASE ** (-jnp.arange(0, HEAD_DIM, 2, dtype=jnp.float32) / HEAD_DIM)
    half = HEAD_DIM // 2
    rope = jnp.stack([jnp.concatenate([inv_freq, inv_freq]),
                      jnp.concatenate([-jnp.ones((half,), jnp.float32), jnp.ones((half,), jnp.float32)])])

    nt = T // TS
    n_pages = (2 * T + nt * N_EXPERTS * (ROW_GROUP - 1)) // TM + N_EXPERTS + 1

    for l in range(DEPTH):
        ada = _ada(c, w_ada[l], b_ada[l]).reshape(B, 6, D)
        pad = SUBLANES - N_GROUPS
        wrt = jnp.concatenate([w_group[l].T, jnp.zeros((pad, D), jnp.float32), w_router[l].T], axis=0)
        brt = jnp.concatenate([b_group[l], jnp.zeros((pad,), jnp.float32), b_router[l]]).reshape(ROUTE_ROWS, 1)
        x1, xt, route, counts = _mix(
            x, positions, ada, rope, _bf(w_in[l]), _bf(w_out[l]), _bf(w_pool[l]),
            ret_gn_w[l].reshape(1, RET_WIDTH), pool_scale[l].reshape(1, POOL_WIDTH),
            ln1_w[l].reshape(1, D), ln1_b[l].reshape(1, D), wrt, brt)

        cnt = counts[:, :, :, 0].reshape(nt, N_EXPERTS).astype(jnp.int32)
        page_tables, esrc, fsrc = _gather_tables(cnt, n_pages)
        ys = _experts(page_tables, esrc, xt.reshape(nt * TR // ROW_GROUP, 2 * ROW_GROUP, HALF),
                      w1[l].reshape(N_EXPERTS, D, D_EXPERT), w3[l].reshape(N_EXPERTS, D, D_EXPERT),
                      w2[l].reshape(N_EXPERTS, D_EXPERT, D))
        x = _final(fsrc, x1.reshape(T, D), route.reshape(T, LANES), ada,
                   ln2_w[l].reshape(1, D), ln2_b[l].reshape(1, D),
                   ys.reshape(n_pages * TM // ROW_GROUP, 2 * ROW_GROUP, HALF), S).reshape(B, S, D)
    return x
```

```python
import math

import jax
import jax.numpy as jnp
from jax import lax
from jax.experimental import pallas as pl
from jax.experimental.pallas import tpu as pltpu

D_MODEL = 1024
RET_WIDTH = 512
RET_HEADS = 4
HEAD_DIM = 128
POOL_WIDTH = 512
POOL_WINDOWS = (2, 4, 8, 16)
POOL_GROUP_DIM = 128
IN_COLS = 4 * RET_WIDTH + POOL_WIDTH
N_GROUPS = 4
EXPERTS_PER_GROUP = 8
N_EXPERTS = 32
D_EXPERT = 256
DEPTH = 1
ALPHA = (2.0 * DEPTH) ** 0.25
LN_EPS = 1e-5
ROPE_BASE = 10000.0

LANES = 128
SUBLANES = 8
ROW_GROUP = 8
PAGE_CHUNK = 2 * ROW_GROUP
HALF = D_MODEL // 2
POOL_HALO = 16
TS = 256
TM = 512
PAGE_SPLIT_ROWS = 256
TR = -(-(2 * TS + N_EXPERTS * (ROW_GROUP - 1)) // LANES) * LANES
SEQS_PER_STEP = 4
TILES_PER_STEP = 2
ROUTE_ROWS = SUBLANES + N_EXPERTS
FETCH_AHEAD = 2
FETCH_SLOTS = FETCH_AHEAD + 1
FETCH_CHUNK = 64
VMEM_LIMIT = 60 * 1024 * 1024

_LOG_GAMMA = tuple(math.log1p(-(2.0 ** (-5.0 - h))) for h in range(RET_HEADS))
_HI = lax.Precision.HIGHEST


def _ln(x):
    mu = jnp.mean(x, axis=-1, keepdims=True)
    xc = x - mu
    var = jnp.mean(xc * xc, axis=-1, keepdims=True)
    return xc * lax.rsqrt(var + LN_EPS)


def _bf(x):
    return x.astype(jnp.bfloat16)


def _pack_rows(x):
    n = x.shape[0]
    lo = x[:, :HALF].reshape(n // ROW_GROUP, ROW_GROUP, HALF)
    hi = x[:, HALF:].reshape(n // ROW_GROUP, ROW_GROUP, HALF)
    return _bf(jnp.concatenate([lo, hi], axis=1).reshape(2 * n, HALF))


def _unpack_rows(z):
    n = z.shape[0] // 2
    zf = z.astype(jnp.float32).reshape(n // ROW_GROUP, 2 * ROW_GROUP, HALF)
    return _bf(zf[:, :ROW_GROUP, :].reshape(n, HALF)), _bf(zf[:, ROW_GROUP:, :].reshape(n, HALF))


def _ada_kernel(c_ref, w_ref, b_ref, o_ref):
    k = pl.program_id(0)

    @pl.when(k == 0)
    def _init():
        o_ref[...] = jnp.broadcast_to(b_ref[...], o_ref.shape)

    c = c_ref[...]
    ca = c * jax.nn.sigmoid(c)
    o_ref[...] += jnp.dot(ca, w_ref[...], precision=_HI, preferred_element_type=jnp.float32)


def _ada(c, w_ada, b_ada):
    B, D = c.shape
    n = w_ada.shape[1]
    kb = LANES
    c_blocks = c.reshape(B, D // kb, kb).transpose(1, 0, 2)
    return pl.pallas_call(
        _ada_kernel,
        grid=(D // kb,),
        in_specs=[pl.BlockSpec((None, B, kb), lambda k: (k, 0, 0)),
                  pl.BlockSpec((kb, n), lambda k: (k, 0)),
                  pl.BlockSpec((1, n), lambda k: (0, 0))],
        out_specs=pl.BlockSpec((B, n), lambda k: (0, 0)),
        out_shape=jax.ShapeDtypeStruct((B, n), jnp.float32),
        compiler_params=pltpu.CompilerParams(dimension_semantics=("arbitrary",), vmem_limit_bytes=VMEM_LIMIT),
        name="ada",
    )(c_blocks, w_ada, b_ada.reshape(1, n))


def _mix_kernel(x_ref, pos_ref, ada_ref, rope_ref, win_ref, wout_ref, wpool_ref, gnw_ref, pscale_ref,
                ln1w_ref, ln1b_ref, wrt_ref, brt_ref,
                x1_ref, xt_ref, route_ref, cnt_ref,
                state_ref, halo_ref, dmat_ref, qdec_ref, kdec_ref, cat_ref):
    b = pl.program_id(0)
    s = pl.program_id(1)

    @pl.when((b == 0) & (s == 0))
    def _init_tables():
        ri = lax.broadcasted_iota(jnp.int32, (TS, TS), 0)
        ci = lax.broadcasted_iota(jnp.int32, (TS, TS), 1)
        rel = (ri - ci).astype(jnp.float32)
        for h in range(RET_HEADS):
            dmat_ref[h] = jnp.where(rel >= 0.0, jnp.exp(jnp.maximum(rel, 0.0) * _LOG_GAMMA[h]), 0.0)
        row = lax.broadcasted_iota(jnp.int32, (TS, RET_WIDTH), 0).astype(jnp.float32)
        lane = lax.broadcasted_iota(jnp.int32, (TS, RET_WIDTH), 1)
        lg = jnp.full((TS, RET_WIDTH), _LOG_GAMMA[0], jnp.float32)
        for h in range(1, RET_HEADS):
            lg = jnp.where(lane >= h * HEAD_DIM, _LOG_GAMMA[h], lg)
        qdec_ref[...] = jnp.exp((row + 1.0) * lg)
        kdec_ref[...] = jnp.exp((TS - 1.0 - row) * lg)

    @pl.when(s == 0)
    def _init_carries():
        state_ref[...] = jnp.zeros_like(state_ref)
        halo_ref[...] = jnp.zeros_like(halo_ref)

    tiles = [_mix_tile(s, x_ref.at[j], pos_ref.at[j], ada_ref.at[j], rope_ref, win_ref, wout_ref, wpool_ref, gnw_ref,
                       pscale_ref, ln1w_ref, ln1b_ref, wrt_ref, brt_ref,
                       x1_ref.at[j], xt_ref.at[j], route_ref.at[j], cnt_ref.at[j],
                       state_ref.at[j], halo_ref.at[j], dmat_ref, qdec_ref, kdec_ref, cat_ref.at[j])
             for j in range(SEQS_PER_STEP)]
    _round_robin(tiles)


def _mix_tile(s, x_ref, pos_ref, ada_ref, rope_ref, win_ref, wout_ref, wpool_ref, gnw_ref, pscale_ref,
              ln1w_ref, ln1b_ref, wrt_ref, brt_ref,
              x1_ref, xt_ref, route_ref, cnt_ref,
              state_ref, halo_ref, dmat_ref, qdec_ref, kdec_ref, cat_ref):
    ada = ada_ref[...]
    shift1, scale1, gate1 = ada[0:1], ada[1:2], ada[2:3]
    shift2, scale2 = ada[3:4], ada[4:5]

    x = x_ref[...]
    u = _bf(_ln(x) * (1.0 + scale1) + shift1)
    yield

    posf = jnp.broadcast_to(pos_ref[...].astype(jnp.float32), (LANES, TS)).T
    hl = lax.broadcasted_iota(jnp.int32, (TS // 2, HEAD_DIM), 1) < HEAD_DIM // 2
    ang = jnp.where(hl, posf[:TS // 2], posf[TS // 2:]) * rope_ref[0:1, :]
    cos_p, sin_p = jnp.cos(ang), jnp.sin(ang)
    cos_s, sin_s = pltpu.roll(cos_p, HEAD_DIM // 2, 1), pltpu.roll(sin_p, HEAD_DIM // 2, 1)
    cos_t = jnp.concatenate([jnp.where(hl, cos_p, cos_s), jnp.where(hl, cos_s, cos_p)], axis=0)
    sin_t = jnp.concatenate([jnp.where(hl, sin_p, sin_s), jnp.where(hl, sin_s, sin_p)], axis=0) * rope_ref[1:2, :]
    yield

    q = jnp.dot(u, win_ref[:, 0:RET_WIDTH], preferred_element_type=jnp.float32)
    k = jnp.dot(u, win_ref[:, RET_WIDTH:2 * RET_WIDTH], preferred_element_type=jnp.float32)
    v = jnp.dot(u, win_ref[:, 2 * RET_WIDTH:3 * RET_WIDTH], preferred_element_type=jnp.float32)
    g = jnp.dot(u, win_ref[:, 3 * RET_WIDTH:4 * RET_WIDTH], preferred_element_type=jnp.float32)
    p = jnp.dot(u, win_ref[:, 4 * RET_WIDTH:IN_COLS], preferred_element_type=jnp.float32)
    yield

    gnw = gnw_ref[...]
    for h in range(RET_HEADS):
        sl = slice(h * HEAD_DIM, (h + 1) * HEAD_DIM)
        qh, kh, vh = q[:, sl], k[:, sl], v[:, sl]
        qr = qh * cos_t + pltpu.roll(qh, HEAD_DIM // 2, 1) * sin_t
        kr = (kh * cos_t + pltpu.roll(kh, HEAD_DIM // 2, 1) * sin_t) * (HEAD_DIM ** -0.5)
        vb = _bf(vh)
        sc = lax.dot_general(_bf(qr), _bf(kr), (((1,), (1,)), ((), ())), preferred_element_type=jnp.float32)
        intra = jnp.dot(_bf(sc * dmat_ref[h]), vb, preferred_element_type=jnp.float32)
        st = state_ref[h]
        cross = jnp.dot(_bf(qr * qdec_ref[:, sl]), _bf(st), preferred_element_type=jnp.float32)
        kv = lax.dot_general(_bf(kr * kdec_ref[:, sl]), vb, (((0,), (0,)), ((), ())),
                             preferred_element_type=jnp.float32)
        state_ref[h] = st * math.exp(TS * _LOG_GAMMA[h]) + kv
        r = _ln(intra + cross) * gnw[:, sl]
        gh = g[:, sl]
        cat_ref[:, sl] = _bf(gh * jax.nn.sigmoid(gh) * r)
        yield

    pext = jnp.concatenate([halo_ref[...], p], axis=0)
    halo_ref[...] = p[TS - POOL_HALO:, :]
    t_abs = (s * TS + lax.broadcasted_iota(jnp.int32, (TS, 1), 0) + 1).astype(jnp.float32)
    pscale = pscale_ref[...]
    for grp, w in enumerate(POOL_WINDOWS):
        sl = slice(grp * POOL_GROUP_DIM, (grp + 1) * POOL_GROUP_DIM)
        acc = pext[:, sl]
        shift = 1
        while shift < w:
            acc = acc + pltpu.roll(acc, shift, 0)
            shift *= 2
        pooled = acc[POOL_HALO:, :] / jnp.minimum(t_abs, float(w)) - p[:, sl]
        po = jnp.dot(_bf(pooled), wpool_ref[grp], preferred_element_type=jnp.float32) * pscale[:, sl]
        cat_ref[:, RET_WIDTH + grp * POOL_GROUP_DIM:RET_WIDTH + (grp + 1) * POOL_GROUP_DIM] = _bf(po)
    yield

    mix = jnp.dot(cat_ref[...], wout_ref[...], preferred_element_type=jnp.float32)
    yield
    x1 = _ln(ALPHA * x + gate1 * mix) * ln1w_ref[...] + ln1b_ref[...]
    x1_ref[...] = x1
    u2 = _ln(x1) * (1.0 + scale2) + shift2
    yield

    w = wrt_ref[...]
    w_hi = _bf(w)
    w_lo = _bf(w - w_hi.astype(jnp.float32))
    u2_hi = _bf(u2)
    u2_lo = _bf(u2 - u2_hi.astype(jnp.float32))
    nt = (((1,), (1,)), ((), ()))
    logits = (lax.dot_general(w_hi, u2_hi, nt, preferred_element_type=jnp.float32)
              + lax.dot_general(w_hi, u2_lo, nt, preferred_element_type=jnp.float32)
              + lax.dot_general(w_lo, u2_hi, nt, preferred_element_type=jnp.float32)) + brt_ref[...]
    row8 = lax.broadcasted_iota(jnp.int32, (SUBLANES, TS), 0)
    neg = jnp.float32(-jnp.inf)
    gl = jnp.where(row8 < N_GROUPS, logits[0:SUBLANES], neg)
    gmax = jnp.max(gl, axis=0, keepdims=True)
    gidx = jnp.min(jnp.where(gl == gmax, row8, SUBLANES), axis=0, keepdims=True)
    gprob = 1.0 / jnp.sum(jnp.exp(gl - gmax), axis=0, keepdims=True)
    el = logits[SUBLANES:2 * SUBLANES]
    for grp in range(1, N_GROUPS):
        el = jnp.where(gidx == grp, logits[(grp + 1) * SUBLANES:(grp + 2) * SUBLANES], el)
    m1 = jnp.max(el, axis=0, keepdims=True)
    j1 = jnp.min(jnp.where(el == m1, row8, SUBLANES), axis=0, keepdims=True)
    el2 = jnp.where(row8 == j1, neg, el)
    m2 = jnp.max(el2, axis=0, keepdims=True)
    j2 = jnp.min(jnp.where(el2 == m2, row8, SUBLANES), axis=0, keepdims=True)
    e21 = jnp.exp(m2 - m1)
    den = 1.0 / (1.0 + e21)
    cw1 = gprob * den
    cw2 = gprob * e21 * den
    yield

    erow = lax.broadcasted_iota(jnp.int32, (N_EXPERTS, TS), 0)
    oh1 = erow == gidx * EXPERTS_PER_GROUP + j1
    oh2 = erow == gidx * EXPERTS_PER_GROUP + j2
    oh = jnp.where(oh1 | oh2, 1.0, 0.0)
    cnt = jnp.broadcast_to(jnp.sum(oh, axis=1, keepdims=True), (N_EXPERTS, LANES))
    run = jnp.floor((cnt + (ROW_GROUP - 1.0)) * (1.0 / ROW_GROUP)) * ROW_GROUP
    erow_l = lax.broadcasted_iota(jnp.int32, (N_EXPERTS, LANES), 0)
    run_end = run
    shift = 1
    while shift < N_EXPERTS:
        run_end = run_end + jnp.where(erow_l >= shift, pltpu.roll(run_end, shift, 0), 0.0)
        shift *= 2
    run_start = (run_end - run)[:, 0:1]
    ri = lax.broadcasted_iota(jnp.int32, (TS, TS), 0)
    ci = lax.broadcasted_iota(jnp.int32, (TS, TS), 1)
    earlier = _bf(jnp.where(ri < ci, 1.0, 0.0))
    before = jnp.dot(_bf(oh), earlier, preferred_element_type=jnp.float32) + run_start
    pos1 = jnp.sum(jnp.where(oh1, before, 0.0), axis=0, keepdims=True)
    pos2 = jnp.sum(jnp.where(oh2, before, 0.0), axis=0, keepdims=True)
    rr = lax.broadcasted_iota(jnp.int32, (TR, TS), 0).astype(jnp.float32)
    perm = _bf(jnp.where((rr == pos1) | (rr == pos2), 1.0, 0.0))
    yield
    xt_ref[...] = _pack_rows(jnp.dot(perm, u2_hi, preferred_element_type=jnp.float32))
    cnt_ref[...] = cnt

    rowl = lax.broadcasted_iota(jnp.int32, (LANES, TS), 0)
    rec = jnp.where(rowl == 0, pos1, 0.0)
    rec = jnp.where(rowl == 1, pos2, rec)
    rec = jnp.where(rowl == 2, cw1, rec)
    rec = jnp.where(rowl == 3, cw2, rec)
    route_ref[...] = rec.T


def _mix(x, positions, ada, rope, win, wout, wpool, gnw, pscale, ln1w, ln1b, wrt, brt):
    B, S, D = x.shape
    ns = S // TS
    assert B % SEQS_PER_STEP == 0
    P = SEQS_PER_STEP
    const2 = lambda b, s: (0, 0)
    const3 = lambda b, s: (0, 0, 0)
    tile = lambda b, s: (b, s, 0)
    flat = lambda b, s: (b, s, 0, 0)
    return pl.pallas_call(
        _mix_kernel,
        grid=(B // P, ns),
        in_specs=[
            pl.BlockSpec((P, TS, D), tile),
            pl.BlockSpec((P, None, 1, TS), flat),
            pl.BlockSpec((P, 6, D), lambda b, s: (b, 0, 0)),
            pl.BlockSpec((2, LANES), const2),
            pl.BlockSpec((D, IN_COLS), const2, pipeline_mode=pl.Buffered(1)),
            pl.BlockSpec((D, D), const2, pipeline_mode=pl.Buffered(1)),
            pl.BlockSpec((len(POOL_WINDOWS), POOL_GROUP_DIM, POOL_GROUP_DIM), const3),
            pl.BlockSpec((1, RET_WIDTH), const2),
            pl.BlockSpec((1, POOL_WIDTH), const2),
            pl.BlockSpec((1, D), const2),
            pl.BlockSpec((1, D), const2),
            pl.BlockSpec((ROUTE_ROWS, D), const2),
            pl.BlockSpec((ROUTE_ROWS, 1), const2),
        ],
        out_specs=[
            pl.BlockSpec((P, TS, D), tile),
            pl.BlockSpec((P, None, 2 * TR, HALF), flat),
            pl.BlockSpec((P, TS, LANES), tile),
            pl.BlockSpec((P, None, N_EXPERTS, LANES), flat),
        ],
        out_shape=[
            jax.ShapeDtypeStruct((B, S, D), jnp.float32),
            jax.ShapeDtypeStruct((B, ns, 2 * TR, HALF), jnp.bfloat16),
            jax.ShapeDtypeStruct((B, S, LANES), jnp.float32),
            jax.ShapeDtypeStruct((B, ns, N_EXPERTS, LANES), jnp.float32),
        ],
        scratch_shapes=[
            pltpu.VMEM((P, RET_HEADS, HEAD_DIM, HEAD_DIM), jnp.float32),
            pltpu.VMEM((P, POOL_HALO, POOL_WIDTH), jnp.float32),
            pltpu.VMEM((RET_HEADS, TS, TS), jnp.float32),
            pltpu.VMEM((TS, RET_WIDTH), jnp.float32),
            pltpu.VMEM((TS, RET_WIDTH), jnp.float32),
            pltpu.VMEM((P, TS, D), jnp.bfloat16),
        ],
        compiler_params=pltpu.CompilerParams(dimension_semantics=("arbitrary", "arbitrary"),
                                             vmem_limit_bytes=VMEM_LIMIT),
        name="mix",
    )(x, positions.reshape(B, ns, 1, TS), ada, rope, win, wout, wpool, gnw, pscale, ln1w, ln1b, wrt, brt)


def _fetch_groups(src_ref, group_of, dst_ref, sem):
    for u in range(dst_ref.shape[0]):
        pltpu.make_async_copy(src_ref.at[group_of(u)], dst_ref.at[u], sem).start()


def _fetch_groups_task(src_ref, group_of, dst_ref, sem, chunk):
    for u in range(dst_ref.shape[0]):
        pltpu.make_async_copy(src_ref.at[group_of(u)], dst_ref.at[u], sem).start()
        if u % chunk == chunk - 1:
            yield


def _wait_fetch(src_ref, dst_ref, sem):
    pltpu.make_async_copy(src_ref.at[pl.ds(0, dst_ref.shape[0])], dst_ref, sem).wait()


def _expert_rows(x_ref, w1_ref, w3_ref, w2_ref, y_ref):
    n = PAGE_SPLIT_ROWS
    xl, xh = _unpack_rows(x_ref[...].reshape(2 * n, HALF))
    yield
    a = (jnp.dot(xl, w1_ref[:HALF, :], preferred_element_type=jnp.float32)
         + jnp.dot(xh, w1_ref[HALF:, :], preferred_element_type=jnp.float32))
    c = (jnp.dot(xl, w3_ref[:HALF, :], preferred_element_type=jnp.float32)
         + jnp.dot(xh, w3_ref[HALF:, :], preferred_element_type=jnp.float32))
    yield
    h = _bf(a * jax.nn.sigmoid(a) * c)
    yield
    y = jnp.dot(h, w2_ref[...], preferred_element_type=jnp.float32)
    yield
    y_ref[...] = _pack_rows(y)


def _round_robin(tasks):
    while tasks:
        tasks = [t for t in tasks if next(t, True) is None]


def _expert_kernel(used_ref, pe_ref, pv_ref, first_ref, wslot_ref, next_ref, src_ref, xt_ref, w1_hbm, w3_hbm, w2_hbm,
                   ys_ref, xbuf, ybuf, zbuf, w1_buf, w3_buf, w2_buf, w1_bf, w3_bf, w2_bf, sem, osem, wsem, zsem):
    n_used = used_ref[0]
    n_pages = ys_ref.shape[0] // (2 * TM)
    chunks = TM // PAGE_CHUNK
    per_chunk = PAGE_CHUNK // ROW_GROUP

    def fetch(page):
        @pl.when(page < n_used)
        def _start():
            slot = page % FETCH_SLOTS
            for u in range(chunks):
                pltpu.make_async_copy(xt_ref.at[pl.ds(src_ref[page * chunks + u], per_chunk)],
                                      xbuf.at[slot, pl.ds(u * per_chunk, per_chunk)], sem.at[slot]).start()

    def weight_copies(expert, ws):
        return [pltpu.make_async_copy(hbm.at[expert], buf.at[ws], wsem.at[ws])
                for hbm, buf in ((w1_hbm, w1_buf), (w3_hbm, w3_buf), (w2_hbm, w2_buf))]

    def page_out(page, oslot):
        return pltpu.make_async_copy(ybuf.at[oslot], ys_ref.at[pl.ds(pl.multiple_of(page * 2 * TM, 2 * TM), 2 * TM)],
                                     osem.at[oslot])

    for k in range(FETCH_AHEAD):
        fetch(k)

    zbuf[...] = jnp.zeros_like(zbuf)

    def zero_page(page):
        return pltpu.make_async_copy(zbuf, ys_ref.at[pl.ds(pl.multiple_of(page * 2 * TM, 2 * TM), 2 * TM)], zsem)

    def zero_start(page, carry):
        zero_page(page).start()
        return carry

    lax.fori_loop(n_used, n_pages, zero_start, 0)

    @pl.when(n_used > 0)
    def _first_weights():
        for c in weight_copies(pe_ref[0], wslot_ref[0]):
            c.start()

    def page_body(g, carry):
        fetch(g + FETCH_AHEAD)
        ws = wslot_ref[g]

        @pl.when(first_ref[g] == 1)
        def _new_expert():
            for c in weight_copies(pe_ref[g], ws):
                c.wait()
            w1_bf[...] = _bf(w1_buf[ws])
            w3_bf[...] = _bf(w3_buf[ws])
            w2_bf[...] = _bf(w2_buf[ws])

            @pl.when(next_ref[g] >= 0)
            def _stream_next():
                for c in weight_copies(next_ref[g], 1 - ws):
                    c.start()

        slot = g % FETCH_SLOTS
        oslot = g % 2
        _wait_fetch(xt_ref, xbuf.at[slot], sem.at[slot])

        @pl.when(g >= 2)
        def _reuse_out_buffer():
            page_out(g - 2, oslot).wait()

        per = PAGE_SPLIT_ROWS // ROW_GROUP
        parts = TM // PAGE_SPLIT_ROWS
        part_rows = lambda k: pl.ds(k * 2 * PAGE_SPLIT_ROWS, 2 * PAGE_SPLIT_ROWS)
        part = lambda k: _expert_rows(xbuf.at[slot, pl.ds(k * per, per)], w1_bf, w3_bf, w2_bf,
                                      ybuf.at[oslot, part_rows(k)])
        n_parts = (pv_ref[g] + PAGE_SPLIT_ROWS - 1) // PAGE_SPLIT_ROWS
        for n in range(1, parts + 1):
            @pl.when(n_parts == n)
            def _parts(n=n):
                _round_robin([part(k) for k in range(n)])
                for k in range(n, parts):
                    ybuf[oslot, part_rows(k)] = jnp.zeros((2 * PAGE_SPLIT_ROWS, HALF), ybuf.dtype)
        page_out(g, oslot).start()
        return carry

    lax.fori_loop(0, n_used, page_body, 0)

    for back in (2, 1):
        @pl.when(n_used >= back)
        def _drain(back=back):
            page_out(n_used - back, (n_used - back) % 2).wait()

    def zero_wait(page, carry):
        zero_page(page).wait()
        return carry

    lax.fori_loop(n_used, n_pages, zero_wait, 0)


def _experts(page_tables, esrc, xt, w1, w3, w2):
    D = D_MODEL
    n_pages = page_tables[1].shape[0]
    assert TM % PAGE_SPLIT_ROWS == 0
    grid_spec = pltpu.PrefetchScalarGridSpec(
        num_scalar_prefetch=len(page_tables) + 1,
        grid=(1,),
        in_specs=[pl.BlockSpec(memory_space=pl.ANY)] * 4,
        out_specs=pl.BlockSpec(memory_space=pl.ANY),
        scratch_shapes=[pltpu.VMEM((FETCH_SLOTS, TM // ROW_GROUP, 2 * ROW_GROUP, HALF), jnp.bfloat16),
                        pltpu.VMEM((2, 2 * TM, HALF), jnp.bfloat16),
                        pltpu.VMEM((2 * TM, HALF), jnp.bfloat16),
                        pltpu.VMEM((2, D, D_EXPERT), jnp.float32),
                        pltpu.VMEM((2, D, D_EXPERT), jnp.float32),
                        pltpu.VMEM((2, D_EXPERT, D), jnp.float32),
                        pltpu.VMEM((D, D_EXPERT), jnp.bfloat16),
                        pltpu.VMEM((D, D_EXPERT), jnp.bfloat16),
                        pltpu.VMEM((D_EXPERT, D), jnp.bfloat16),
                        pltpu.SemaphoreType.DMA((FETCH_SLOTS,)),
                        pltpu.SemaphoreType.DMA((2,)),
                        pltpu.SemaphoreType.DMA((2,)),
                        pltpu.SemaphoreType.DMA(())],
    )
    return pl.pallas_call(
        _expert_kernel,
        grid_spec=grid_spec,
        out_shape=jax.ShapeDtypeStruct((n_pages * 2 * TM, HALF), jnp.bfloat16),
        compiler_params=pltpu.CompilerParams(dimension_semantics=("arbitrary",), vmem_limit_bytes=VMEM_LIMIT),
        name="experts",
    )(*page_tables, esrc, xt, w1, w3, w2)


def _final_tile(y_ref, x1_ref, route_ref, ada_ref, lnw_ref, lnb_ref, o_ref):
    route = route_ref[...]
    pos1, pos2, cw1, cw2 = route[:, 0:1], route[:, 1:2], route[:, 2:3], route[:, 3:4]
    col = lax.broadcasted_iota(jnp.int32, (TS, TR), 1).astype(jnp.float32)
    wmat = _bf(jnp.where(col == pos1, cw1, 0.0) + jnp.where(col == pos2, cw2, 0.0))
    yield
    yl, yh = _unpack_rows(y_ref[...].reshape(2 * TR, HALF))
    yield
    y = jnp.concatenate([jnp.dot(wmat, yl, preferred_element_type=jnp.float32),
                         jnp.dot(wmat, yh, preferred_element_type=jnp.float32)], axis=-1)
    yield
    gate2 = ada_ref[5:6, :]
    o_ref[...] = _ln(ALPHA * x1_ref[...] + gate2 * y) * lnw_ref[...] + lnb_ref[...]


def _final_kernel(src_ref, x1_ref, route_ref, ada_ref, lnw_ref, lnb_ref, ys_ref, o_ref, ybuf, sem):
    P = TILES_PER_STEP
    i = pl.program_id(0)
    last = pl.num_programs(0) - 1
    per = TR // ROW_GROUP
    groups = P * per

    def fetch(step, dst_slot):
        _fetch_groups(ys_ref, lambda u: src_ref[step * groups + u], ybuf.at[dst_slot], sem.at[dst_slot])

    @pl.when(i == 0)
    def _prime():
        for k in range(FETCH_AHEAD):
            fetch(k, k)

    slot = i % FETCH_SLOTS
    _wait_fetch(ys_ref, ybuf.at[slot], sem.at[slot])

    ahead = jnp.minimum(i + FETCH_AHEAD, last)
    ahead_slot = (i + FETCH_AHEAD) % FETCH_SLOTS
    rows = lambda k: pl.ds(k * TS, TS)
    _round_robin([_final_tile(ybuf.at[slot, pl.ds(k * per, per)], x1_ref.at[rows(k)], route_ref.at[rows(k)], ada_ref,
                              lnw_ref, lnb_ref, o_ref.at[rows(k)]) for k in range(P)]
                 + [_fetch_groups_task(ys_ref, lambda u: src_ref[ahead * groups + u], ybuf.at[ahead_slot],
                                       sem.at[ahead_slot], FETCH_CHUNK)])

    @pl.when(i == last)
    def _drain():
        for k in range(1, FETCH_SLOTS):
            other = (i + k) % FETCH_SLOTS
            _wait_fetch(ys_ref, ybuf.at[other], sem.at[other])


def _final(fsrc, x1, route, ada, lnw, lnb, ys, seq_len):
    T, D = x1.shape
    P = TILES_PER_STEP
    rows = P * TS
    assert seq_len % rows == 0
    steps_per_seq = seq_len // rows
    n_steps = T // rows
    assert n_steps > FETCH_AHEAD
    grid_spec = pltpu.PrefetchScalarGridSpec(
        num_scalar_prefetch=1,
        grid=(n_steps,),
        in_specs=[pl.BlockSpec((rows, D), lambda i, src: (i, 0)),
                  pl.BlockSpec((rows, LANES), lambda i, src: (i, 0)),
                  pl.BlockSpec((None, 6, D), lambda i, src: (i // steps_per_seq, 0, 0)),
                  pl.BlockSpec((1, D), lambda i, src: (0, 0)),
                  pl.BlockSpec((1, D), lambda i, src: (0, 0)),
                  pl.BlockSpec(memory_space=pl.ANY)],
        out_specs=pl.BlockSpec((rows, D), lambda i, src: (i, 0)),
        scratch_shapes=[pltpu.VMEM((FETCH_SLOTS, P * (TR // ROW_GROUP), 2 * ROW_GROUP, HALF), jnp.bfloat16),
                        pltpu.SemaphoreType.DMA((FETCH_SLOTS,))],
    )
    return pl.pallas_call(
        _final_kernel,
        grid_spec=grid_spec,
        out_shape=jax.ShapeDtypeStruct((T, D), jnp.float32),
        compiler_params=pltpu.CompilerParams(dimension_semantics=("arbitrary",), vmem_limit_bytes=VMEM_LIMIT),
        name="final",
    )(fsrc, x1, route, ada, lnw, lnb, ys)


def _gather_tables(cnt, n_pages):
    nt = cnt.shape[0]
    run = (cnt + ROW_GROUP - 1) // ROW_GROUP * ROW_GROUP
    so = jnp.cumsum(run, axis=1) - run
    run_e = (cnt + PAGE_CHUNK - 1) // PAGE_CHUNK * PAGE_CHUNK
    eo = jnp.cumsum(run_e, axis=0) - run_e
    tot = jnp.sum(run_e, axis=0)
    pages_e = (tot + TM - 1) // TM
    page_end = jnp.cumsum(pages_e)
    page_start = page_end - pages_e
    g = jnp.arange(n_pages, dtype=jnp.int32)
    pe = jnp.minimum(jnp.sum(g[:, None] >= page_end[None, :], axis=1), N_EXPERTS - 1).astype(jnp.int32)
    used = g < page_end[-1]
    owner = (g[None, :] >= page_start[:, None]) & (g[None, :] < page_end[:, None])
    of_page = lambda a: jnp.sum(jnp.where(owner, a[:, None], 0), axis=0)
    of_page2 = lambda a: jnp.sum(jnp.where(owner[:, None, :], a.T[:, :, None], 0), axis=0)
    pj = g - of_page(page_start)
    tot_p = of_page(tot)
    pv = jnp.where(used, jnp.clip(tot_p - pj * TM, 0, TM), 0).astype(jnp.int32)
    experts = jnp.arange(N_EXPERTS, dtype=jnp.int32)
    has_pages = pages_e > 0
    wslot_e = (jnp.cumsum(has_pages) - 1) % 2
    later = (experts[None, :] > experts[:, None]) & has_pages[None, :]
    next_e = jnp.min(jnp.where(later, experts[None, :], N_EXPERTS), axis=1)
    next_e = jnp.where(next_e < N_EXPERTS, next_e, -1)
    first = (used & (pj == 0)).astype(jnp.int32)
    page_tables = (page_end[-1:].astype(jnp.int32), pe, pv, first, of_page(wslot_e).astype(jnp.int32),
                   jnp.where(used, of_page(next_e), -1).astype(jnp.int32))

    q = (pj * TM)[:, None] + PAGE_CHUNK * jnp.arange(TM // PAGE_CHUNK, dtype=jnp.int32)[None, :]
    tiles = jnp.arange(nt, dtype=jnp.int32)
    offset = of_page2(tiles[:, None] * TR + so - eo)
    step = jnp.concatenate([offset[1:] - offset[:-1], jnp.zeros_like(offset[:1])], axis=0)
    run_end = of_page2(eo + run_e)
    src = q + offset[0][:, None] + jnp.sum(jnp.where(run_end[:, :, None] <= q[None], step[:, :, None], 0), axis=0)
    zero_xt = TR - PAGE_CHUNK
    esrc = jnp.where((q < tot_p[:, None]) & used[:, None], src, zero_xt).astype(jnp.int32)

    r = ROW_GROUP * jnp.arange(TR // ROW_GROUP, dtype=jnp.int32)
    end = (so + run).T
    offset_f = ((page_start * TM)[None, :] + eo - so).T
    step_f = jnp.concatenate([offset_f[1:] - offset_f[:-1], jnp.zeros_like(offset_f[:1])], axis=0)
    srcf = (r[None, :] + offset_f[0][:, None]
            + jnp.sum(jnp.where(end[:, :, None] <= r[None, None, :], step_f[:, :, None], 0), axis=0))
    zero_ys = (n_pages - 1) * TM
    fsrc = jnp.where(r[None, :] < end[-1][:, None], srcf, zero_ys).astype(jnp.int32)
    return page_tables, esrc.reshape(-1) // ROW_GROUP, fsrc.reshape(-1) // ROW_GROUP


def kernel(x, c, positions, w_ada, b_ada, w_in, ret_gn_w, w_pool, pool_scale, w_out, ln1_w, ln1_b, w_group, b_group,
           w_router, b_router, w1, w3, w2, ln2_w, ln2_b):
    B, S, D = x.shape
    T = B * S
    assert w_ada.shape[0] == DEPTH and D == D_MODEL and S % TS == 0

    inv_freq = ROPE_BASE ** (-jnp.arange(0, HEAD_DIM, 2, dtype=jnp.float32) / HEAD_DIM)
    half = HEAD_DIM // 2
    rope = jnp.stack([jnp.concatenate([inv_freq, inv_freq]),
                      jnp.concatenate([-jnp.ones((half,), jnp.float32), jnp.ones((half,), jnp.float32)])])

    nt = T // TS
    n_pages = (2 * T + nt * N_EXPERTS * (PAGE_CHUNK - 1)) // TM + N_EXPERTS + 1

    for l in range(DEPTH):
        ada = _ada(c, w_ada[l], b_ada[l]).reshape(B, 6, D)
        pad = SUBLANES - N_GROUPS
        wrt = jnp.concatenate([w_group[l].T, jnp.zeros((pad, D), jnp.float32), w_router[l].T], axis=0)
        brt = jnp.concatenate([b_group[l], jnp.zeros((pad,), jnp.float32), b_router[l]]).reshape(ROUTE_ROWS, 1)
        x1, xt, route, counts = _mix(
            x, positions, ada, rope, _bf(w_in[l]), _bf(w_out[l]), _bf(w_pool[l]),
            ret_gn_w[l].reshape(1, RET_WIDTH), pool_scale[l].reshape(1, POOL_WIDTH),
            ln1_w[l].reshape(1, D), ln1_b[l].reshape(1, D), wrt, brt)

        cnt = counts[:, :, :, 0].reshape(nt, N_EXPERTS).astype(jnp.int32)
        page_tables, esrc, fsrc = _gather_tables(cnt, n_pages)
        ys = _experts(page_tables, esrc, xt.reshape(nt * TR // ROW_GROUP, 2 * ROW_GROUP, HALF),
                      w1[l].reshape(N_EXPERTS, D, D_EXPERT), w3[l].reshape(N_EXPERTS, D, D_EXPERT),
                      w2[l].reshape(N_EXPERTS, D_EXPERT, D))
        x = _final(fsrc, x1.reshape(T, D), route.reshape(T, LANES), ada,
                   ln2_w[l].reshape(1, D), ln2_b[l].reshape(1, D),
                   ys.reshape(n_pages * TM // ROW_GROUP, 2 * ROW_GROUP, HALF), S).reshape(B, S, D)
    return x
```

```python
import math

import jax
import jax.numpy as jnp
from jax import lax
from jax.experimental import pallas as pl
from jax.experimental.pallas import tpu as pltpu

D_MODEL = 1024
RET_WIDTH = 512
RET_HEADS = 4
HEAD_DIM = 128
POOL_WIDTH = 512
POOL_WINDOWS = (2, 4, 8, 16)
POOL_GROUP_DIM = 128
IN_COLS = 4 * RET_WIDTH + POOL_WIDTH
N_GROUPS = 4
EXPERTS_PER_GROUP = 8
N_EXPERTS = 32
D_EXPERT = 256
DEPTH = 1
ALPHA = (2.0 * DEPTH) ** 0.25
LN_EPS = 1e-5
ROPE_BASE = 10000.0

LANES = 128
SUBLANES = 8
ROW_GROUP = 8
PAGE_CHUNK = 2 * ROW_GROUP
HALF = D_MODEL // 2
POOL_HALO = 16
TS = 256
TM = 512
PAGE_SPLIT_ROWS = 256
TR = -(-(2 * TS + N_EXPERTS * (ROW_GROUP - 1)) // LANES) * LANES
SEQS_PER_STEP = 4
TILES_PER_STEP = 2
ROUTE_ROWS = SUBLANES + N_EXPERTS
FETCH_AHEAD = 3
FETCH_SLOTS = FETCH_AHEAD + 1
FETCH_CHUNK = 64
VMEM_LIMIT = 60 * 1024 * 1024

_LOG_GAMMA = tuple(math.log1p(-(2.0 ** (-5.0 - h))) for h in range(RET_HEADS))
_HI = lax.Precision.HIGHEST


def _ln(x):
    mu = jnp.mean(x, axis=-1, keepdims=True)
    xc = x - mu
    var = jnp.mean(xc * xc, axis=-1, keepdims=True)
    return xc * lax.rsqrt(var + LN_EPS)


def _bf(x):
    return x.astype(jnp.bfloat16)


def _pack_rows(x):
    n = x.shape[0]
    lo = x[:, :HALF].reshape(n // ROW_GROUP, ROW_GROUP, HALF)
    hi = x[:, HALF:].reshape(n // ROW_GROUP, ROW_GROUP, HALF)
    return _bf(jnp.concatenate([lo, hi], axis=1).reshape(2 * n, HALF))


def _unpack_rows(z):
    n = z.shape[0] // 2
    zf = z.astype(jnp.float32).reshape(n // ROW_GROUP, 2 * ROW_GROUP, HALF)
    return _bf(zf[:, :ROW_GROUP, :].reshape(n, HALF)), _bf(zf[:, ROW_GROUP:, :].reshape(n, HALF))


def _ada_kernel(c_ref, w_ref, b_ref, o_ref):
    k = pl.program_id(0)

    @pl.when(k == 0)
    def _init():
        o_ref[...] = jnp.broadcast_to(b_ref[...], o_ref.shape)

    c = c_ref[...]
    ca = c * jax.nn.sigmoid(c)
    o_ref[...] += jnp.dot(ca, w_ref[...], precision=_HI, preferred_element_type=jnp.float32)


def _ada(c, w_ada, b_ada):
    B, D = c.shape
    n = w_ada.shape[1]
    kb = LANES
    c_blocks = c.reshape(B, D // kb, kb).transpose(1, 0, 2)
    return pl.pallas_call(
        _ada_kernel,
        grid=(D // kb,),
        in_specs=[pl.BlockSpec((None, B, kb), lambda k: (k, 0, 0)),
                  pl.BlockSpec((kb, n), lambda k: (k, 0)),
                  pl.BlockSpec((1, n), lambda k: (0, 0))],
        out_specs=pl.BlockSpec((B, n), lambda k: (0, 0)),
        out_shape=jax.ShapeDtypeStruct((B, n), jnp.float32),
        compiler_params=pltpu.CompilerParams(dimension_semantics=("arbitrary",), vmem_limit_bytes=VMEM_LIMIT),
        name="ada",
    )(c_blocks, w_ada, b_ada.reshape(1, n))


def _mix_kernel(x_ref, pos_ref, ada_ref, rope_ref, win_ref, wout_ref, wpool_ref, gnw_ref, pscale_ref,
                ln1w_ref, ln1b_ref, wrt_ref, brt_ref,
                x1_ref, xt_ref, route_ref, cnt_ref,
                state_ref, halo_ref, dmat_ref, qdec_ref, kdec_ref, cat_ref):
    b = pl.program_id(0)
    s = pl.program_id(1)

    @pl.when((b == 0) & (s == 0))
    def _init_tables():
        ri = lax.broadcasted_iota(jnp.int32, (TS, TS), 0)
        ci = lax.broadcasted_iota(jnp.int32, (TS, TS), 1)
        rel = (ri - ci).astype(jnp.float32)
        for h in range(RET_HEADS):
            dmat_ref[h] = jnp.where(rel >= 0.0, jnp.exp(jnp.maximum(rel, 0.0) * _LOG_GAMMA[h]), 0.0)
        row = lax.broadcasted_iota(jnp.int32, (TS, RET_WIDTH), 0).astype(jnp.float32)
        lane = lax.broadcasted_iota(jnp.int32, (TS, RET_WIDTH), 1)
        lg = jnp.full((TS, RET_WIDTH), _LOG_GAMMA[0], jnp.float32)
        for h in range(1, RET_HEADS):
            lg = jnp.where(lane >= h * HEAD_DIM, _LOG_GAMMA[h], lg)
        qdec_ref[...] = jnp.exp((row + 1.0) * lg)
        kdec_ref[...] = jnp.exp((TS - 1.0 - row) * lg)

    @pl.when(s == 0)
    def _init_carries():
        state_ref[...] = jnp.zeros_like(state_ref)
        halo_ref[...] = jnp.zeros_like(halo_ref)

    tiles = [_mix_tile(s, x_ref.at[j], pos_ref.at[j], ada_ref.at[j], rope_ref, win_ref, wout_ref, wpool_ref, gnw_ref,
                       pscale_ref, ln1w_ref, ln1b_ref, wrt_ref, brt_ref,
                       x1_ref.at[j], xt_ref.at[j], route_ref.at[j], cnt_ref.at[j],
                       state_ref.at[j], halo_ref.at[j], dmat_ref, qdec_ref, kdec_ref, cat_ref.at[j])
             for j in range(SEQS_PER_STEP)]
    _round_robin(tiles)


def _mix_tile(s, x_ref, pos_ref, ada_ref, rope_ref, win_ref, wout_ref, wpool_ref, gnw_ref, pscale_ref,
              ln1w_ref, ln1b_ref, wrt_ref, brt_ref,
              x1_ref, xt_ref, route_ref, cnt_ref,
              state_ref, halo_ref, dmat_ref, qdec_ref, kdec_ref, cat_ref):
    ada = ada_ref[...]
    shift1, scale1, gate1 = ada[0:1], ada[1:2], ada[2:3]
    shift2, scale2 = ada[3:4], ada[4:5]

    x = x_ref[...]
    u = _bf(_ln(x) * (1.0 + scale1) + shift1)
    yield

    posf = jnp.broadcast_to(pos_ref[...].astype(jnp.float32), (LANES, TS)).T
    hl = lax.broadcasted_iota(jnp.int32, (TS // 2, HEAD_DIM), 1) < HEAD_DIM // 2
    ang = jnp.where(hl, posf[:TS // 2], posf[TS // 2:]) * rope_ref[0:1, :]
    cos_p, sin_p = jnp.cos(ang), jnp.sin(ang)
    cos_s, sin_s = pltpu.roll(cos_p, HEAD_DIM // 2, 1), pltpu.roll(sin_p, HEAD_DIM // 2, 1)
    cos_t = jnp.concatenate([jnp.where(hl, cos_p, cos_s), jnp.where(hl, cos_s, cos_p)], axis=0)
    sin_t = jnp.concatenate([jnp.where(hl, sin_p, sin_s), jnp.where(hl, sin_s, sin_p)], axis=0) * rope_ref[1:2, :]
    yield

    q = jnp.dot(u, win_ref[:, 0:RET_WIDTH], preferred_element_type=jnp.float32)
    k = jnp.dot(u, win_ref[:, RET_WIDTH:2 * RET_WIDTH], preferred_element_type=jnp.float32)
    v = jnp.dot(u, win_ref[:, 2 * RET_WIDTH:3 * RET_WIDTH], preferred_element_type=jnp.float32)
    g = jnp.dot(u, win_ref[:, 3 * RET_WIDTH:4 * RET_WIDTH], preferred_element_type=jnp.float32)
    p = jnp.dot(u, win_ref[:, 4 * RET_WIDTH:IN_COLS], preferred_element_type=jnp.float32)
    yield

    gnw = gnw_ref[...]
    for h in range(RET_HEADS):
        sl = slice(h * HEAD_DIM, (h + 1) * HEAD_DIM)
        qh, kh, vh = q[:, sl], k[:, sl], v[:, sl]
        qr = qh * cos_t + pltpu.roll(qh, HEAD_DIM // 2, 1) * sin_t
        kr = (kh * cos_t + pltpu.roll(kh, HEAD_DIM // 2, 1) * sin_t) * (HEAD_DIM ** -0.5)
        vb = _bf(vh)
        sc = lax.dot_general(_bf(qr), _bf(kr), (((1,), (1,)), ((), ())), preferred_element_type=jnp.float32)
        intra = jnp.dot(_bf(sc * dmat_ref[h]), vb, preferred_element_type=jnp.float32)
        st = state_ref[h]
        cross = jnp.dot(_bf(qr * qdec_ref[:, sl]), _bf(st), preferred_element_type=jnp.float32)
        kv = lax.dot_general(_bf(kr * kdec_ref[:, sl]), vb, (((0,), (0,)), ((), ())),
                             preferred_element_type=jnp.float32)
        state_ref[h] = st * math.exp(TS * _LOG_GAMMA[h]) + kv
        r = _ln(intra + cross) * gnw[:, sl]
        gh = g[:, sl]
        cat_ref[:, sl] = _bf(gh * jax.nn.sigmoid(gh) * r)
        yield

    pext = jnp.concatenate([halo_ref[...], p], axis=0)
    halo_ref[...] = p[TS - POOL_HALO:, :]
    t_abs = (s * TS + lax.broadcasted_iota(jnp.int32, (TS, 1), 0) + 1).astype(jnp.float32)
    pscale = pscale_ref[...]
    for grp, w in enumerate(POOL_WINDOWS):
        sl = slice(grp * POOL_GROUP_DIM, (grp + 1) * POOL_GROUP_DIM)
        acc = pext[:, sl]
        shift = 1
        while shift < w:
            acc = acc + pltpu.roll(acc, shift, 0)
            shift *= 2
        pooled = acc[POOL_HALO:, :] / jnp.minimum(t_abs, float(w)) - p[:, sl]
        po = jnp.dot(_bf(pooled), wpool_ref[grp], preferred_element_type=jnp.float32) * pscale[:, sl]
        cat_ref[:, RET_WIDTH + grp * POOL_GROUP_DIM:RET_WIDTH + (grp + 1) * POOL_GROUP_DIM] = _bf(po)
    yield

    mix = jnp.dot(cat_ref[...], wout_ref[...], preferred_element_type=jnp.float32)
    yield
    x1 = _ln(ALPHA * x + gate1 * mix) * ln1w_ref[...] + ln1b_ref[...]
    x1_ref[...] = x1
    u2 = _ln(x1) * (1.0 + scale2) + shift2
    yield

    w = wrt_ref[...]
    w_hi = _bf(w)
    w_lo = _bf(w - w_hi.astype(jnp.float32))
    u2_hi = _bf(u2)
    u2_lo = _bf(u2 - u2_hi.astype(jnp.float32))
    nt = (((1,), (1,)), ((), ()))
    logits = (lax.dot_general(w_hi, u2_hi, nt, preferred_element_type=jnp.float32)
              + lax.dot_general(w_hi, u2_lo, nt, preferred_element_type=jnp.float32)
              + lax.dot_general(w_lo, u2_hi, nt, preferred_element_type=jnp.float32)) + brt_ref[...]
    row8 = lax.broadcasted_iota(jnp.int32, (SUBLANES, TS), 0)
    neg = jnp.float32(-jnp.inf)
    gl = jnp.where(row8 < N_GROUPS, logits[0:SUBLANES], neg)
    gmax = jnp.max(gl, axis=0, keepdims=True)
    gidx = jnp.min(jnp.where(gl == gmax, row8, SUBLANES), axis=0, keepdims=True)
    gprob = 1.0 / jnp.sum(jnp.exp(gl - gmax), axis=0, keepdims=True)
    el = logits[SUBLANES:2 * SUBLANES]
    for grp in range(1, N_GROUPS):
        el = jnp.where(gidx == grp, logits[(grp + 1) * SUBLANES:(grp + 2) * SUBLANES], el)
    m1 = jnp.max(el, axis=0, keepdims=True)
    j1 = jnp.min(jnp.where(el == m1, row8, SUBLANES), axis=0, keepdims=True)
    el2 = jnp.where(row8 == j1, neg, el)
    m2 = jnp.max(el2, axis=0, keepdims=True)
    j2 = jnp.min(jnp.where(el2 == m2, row8, SUBLANES), axis=0, keepdims=True)
    e21 = jnp.exp(m2 - m1)
    den = 1.0 / (1.0 + e21)
    cw1 = gprob * den
    cw2 = gprob * e21 * den
    yield

    erow = lax.broadcasted_iota(jnp.int32, (N_EXPERTS, TS), 0)
    oh1 = erow == gidx * EXPERTS_PER_GROUP + j1
    oh2 = erow == gidx * EXPERTS_PER_GROUP + j2
    oh = jnp.where(oh1 | oh2, 1.0, 0.0)
    cnt = jnp.broadcast_to(jnp.sum(oh, axis=1, keepdims=True), (N_EXPERTS, LANES))
    run = jnp.floor((cnt + (ROW_GROUP - 1.0)) * (1.0 / ROW_GROUP)) * ROW_GROUP
    erow_l = lax.broadcasted_iota(jnp.int32, (N_EXPERTS, LANES), 0)
    run_end = run
    shift = 1
    while shift < N_EXPERTS:
        run_end = run_end + jnp.where(erow_l >= shift, pltpu.roll(run_end, shift, 0), 0.0)
        shift *= 2
    run_start = (run_end - run)[:, 0:1]
    ri = lax.broadcasted_iota(jnp.int32, (TS, TS), 0)
    ci = lax.broadcasted_iota(jnp.int32, (TS, TS), 1)
    earlier = _bf(jnp.where(ri < ci, 1.0, 0.0))
    before = jnp.dot(_bf(oh), earlier, preferred_element_type=jnp.float32) + run_start
    pos1 = jnp.sum(jnp.where(oh1, before, 0.0), axis=0, keepdims=True)
    pos2 = jnp.sum(jnp.where(oh2, before, 0.0), axis=0, keepdims=True)
    rr = lax.broadcasted_iota(jnp.int32, (TR, TS), 0).astype(jnp.float32)
    perm = _bf(jnp.where((rr == pos1) | (rr == pos2), 1.0, 0.0))
    yield
    xt_ref[...] = _pack_rows(jnp.dot(perm, u2_hi, preferred_element_type=jnp.float32))
    cnt_ref[...] = cnt

    rowl = lax.broadcasted_iota(jnp.int32, (LANES, TS), 0)
    rec = jnp.where(rowl == 0, pos1, 0.0)
    rec = jnp.where(rowl == 1, pos2, rec)
    rec = jnp.where(rowl == 2, cw1, rec)
    rec = jnp.where(rowl == 3, cw2, rec)
    route_ref[...] = rec.T


def _mix(x, positions, ada, rope, win, wout, wpool, gnw, pscale, ln1w, ln1b, wrt, brt):
    B, S, D = x.shape
    ns = S // TS
    assert B % SEQS_PER_STEP == 0
    P = SEQS_PER_STEP
    const2 = lambda b, s: (0, 0)
    const3 = lambda b, s: (0, 0, 0)
    tile = lambda b, s: (b, s, 0)
    flat = lambda b, s: (b, s, 0, 0)
    return pl.pallas_call(
        _mix_kernel,
        grid=(B // P, ns),
        in_specs=[
            pl.BlockSpec((P, TS, D), tile),
            pl.BlockSpec((P, None, 1, TS), flat),
            pl.BlockSpec((P, 6, D), lambda b, s: (b, 0, 0)),
            pl.BlockSpec((2, LANES), const2),
            pl.BlockSpec((D, IN_COLS), const2, pipeline_mode=pl.Buffered(1)),
            pl.BlockSpec((D, D), const2, pipeline_mode=pl.Buffered(1)),
            pl.BlockSpec((len(POOL_WINDOWS), POOL_GROUP_DIM, POOL_GROUP_DIM), const3),
            pl.BlockSpec((1, RET_WIDTH), const2),
            pl.BlockSpec((1, POOL_WIDTH), const2),
            pl.BlockSpec((1, D), const2),
            pl.BlockSpec((1, D), const2),
            pl.BlockSpec((ROUTE_ROWS, D), const2),
            pl.BlockSpec((ROUTE_ROWS, 1), const2),
        ],
        out_specs=[
            pl.BlockSpec((P, TS, D), tile),
            pl.BlockSpec((P, None, 2 * TR, HALF), flat),
            pl.BlockSpec((P, TS, LANES), tile),
            pl.BlockSpec((P, None, N_EXPERTS, LANES), flat),
        ],
        out_shape=[
            jax.ShapeDtypeStruct((B, S, D), jnp.float32),
            jax.ShapeDtypeStruct((B, ns, 2 * TR, HALF), jnp.bfloat16),
            jax.ShapeDtypeStruct((B, S, LANES), jnp.float32),
            jax.ShapeDtypeStruct((B, ns, N_EXPERTS, LANES), jnp.float32),
        ],
        scratch_shapes=[
            pltpu.VMEM((P, RET_HEADS, HEAD_DIM, HEAD_DIM), jnp.float32),
            pltpu.VMEM((P, POOL_HALO, POOL_WIDTH), jnp.float32),
            pltpu.VMEM((RET_HEADS, TS, TS), jnp.float32),
            pltpu.VMEM((TS, RET_WIDTH), jnp.float32),
            pltpu.VMEM((TS, RET_WIDTH), jnp.float32),
            pltpu.VMEM((P, TS, D), jnp.bfloat16),
        ],
        compiler_params=pltpu.CompilerParams(dimension_semantics=("arbitrary", "arbitrary"),
                                             vmem_limit_bytes=VMEM_LIMIT),
        name="mix",
    )(x, positions.reshape(B, ns, 1, TS), ada, rope, win, wout, wpool, gnw, pscale, ln1w, ln1b, wrt, brt)


def _fetch_groups(src_ref, group_of, dst_ref, sem):
    for u in range(dst_ref.shape[0]):
        pltpu.make_async_copy(src_ref.at[group_of(u)], dst_ref.at[u], sem).start()


def _fetch_groups_task(src_ref, group_of, dst_ref, sem, chunk):
    for u in range(dst_ref.shape[0]):
        pltpu.make_async_copy(src_ref.at[group_of(u)], dst_ref.at[u], sem).start()
        if u % chunk == chunk - 1:
            yield


def _wait_fetch(src_ref, dst_ref, sem):
    pltpu.make_async_copy(src_ref.at[pl.ds(0, dst_ref.shape[0])], dst_ref, sem).wait()


def _expert_rows(x_ref, w1_ref, w3_ref, w2_ref, y_ref):
    n = PAGE_SPLIT_ROWS
    xl, xh = _unpack_rows(x_ref[...].reshape(2 * n, HALF))
    yield
    a = (jnp.dot(xl, w1_ref[:HALF, :], preferred_element_type=jnp.float32)
         + jnp.dot(xh, w1_ref[HALF:, :], preferred_element_type=jnp.float32))
    c = (jnp.dot(xl, w3_ref[:HALF, :], preferred_element_type=jnp.float32)
         + jnp.dot(xh, w3_ref[HALF:, :], preferred_element_type=jnp.float32))
    yield
    h = _bf(a * jax.nn.sigmoid(a) * c)
    yield
    y = jnp.dot(h, w2_ref[...], preferred_element_type=jnp.float32)
    yield
    y_ref[...] = _pack_rows(y)


def _round_robin(tasks):
    while tasks:
        tasks = [t for t in tasks if next(t, True) is None]


def _expert_kernel(used_ref, pe_ref, pv_ref, first_ref, wslot_ref, next_ref, src_ref, xt_ref, w1_hbm, w3_hbm, w2_hbm,
                   ys_ref, xbuf, ybuf, zbuf, w1_buf, w3_buf, w2_buf, w1_bf, w3_bf, w2_bf, sem, osem, wsem, zsem):
    n_used = used_ref[0]
    n_pages = ys_ref.shape[0] // (2 * TM)
    chunks = TM // PAGE_CHUNK
    per_chunk = PAGE_CHUNK // ROW_GROUP

    def fetch(page):
        @pl.when(page < n_used)
        def _start():
            slot = page % FETCH_SLOTS
            for u in range(chunks):
                pltpu.make_async_copy(xt_ref.at[pl.ds(src_ref[page * chunks + u], per_chunk)],
                                      xbuf.at[slot, pl.ds(u * per_chunk, per_chunk)], sem.at[slot]).start()

    def weight_copies(expert, ws):
        return [pltpu.make_async_copy(hbm.at[expert], buf.at[ws], wsem.at[ws])
                for hbm, buf in ((w1_hbm, w1_buf), (w3_hbm, w3_buf), (w2_hbm, w2_buf))]

    def page_out(page, oslot):
        return pltpu.make_async_copy(ybuf.at[oslot], ys_ref.at[pl.ds(pl.multiple_of(page * 2 * TM, 2 * TM), 2 * TM)],
                                     osem.at[oslot])

    for k in range(FETCH_AHEAD):
        fetch(k)

    zbuf[...] = jnp.zeros_like(zbuf)

    def zero_page(page):
        return pltpu.make_async_copy(zbuf, ys_ref.at[pl.ds(pl.multiple_of(page * 2 * TM, 2 * TM), 2 * TM)], zsem)

    def zero_start(page, carry):
        zero_page(page).start()
        return carry

    lax.fori_loop(n_used, n_pages, zero_start, 0)

    @pl.when(n_used > 0)
    def _first_weights():
        for c in weight_copies(pe_ref[0], wslot_ref[0]):
            c.start()

    def page_body(g, carry):
        fetch(g + FETCH_AHEAD)
        ws = wslot_ref[g]

        @pl.when(first_ref[g] == 1)
        def _new_expert():
            for c in weight_copies(pe_ref[g], ws):
                c.wait()
            w1_bf[...] = _bf(w1_buf[ws])
            w3_bf[...] = _bf(w3_buf[ws])
            w2_bf[...] = _bf(w2_buf[ws])

            @pl.when(next_ref[g] >= 0)
            def _stream_next():
                for c in weight_copies(next_ref[g], 1 - ws):
                    c.start()

        slot = g % FETCH_SLOTS
        oslot = g % 2
        _wait_fetch(xt_ref, xbuf.at[slot], sem.at[slot])

        @pl.when(g >= 2)
        def _reuse_out_buffer():
            page_out(g - 2, oslot).wait()

        per = PAGE_SPLIT_ROWS // ROW_GROUP
        parts = TM // PAGE_SPLIT_ROWS
        part_rows = lambda k: pl.ds(k * 2 * PAGE_SPLIT_ROWS, 2 * PAGE_SPLIT_ROWS)
        part = lambda k: _expert_rows(xbuf.at[slot, pl.ds(k * per, per)], w1_bf, w3_bf, w2_bf,
                                      ybuf.at[oslot, part_rows(k)])
        n_parts = (pv_ref[g] + PAGE_SPLIT_ROWS - 1) // PAGE_SPLIT_ROWS
        for n in range(1, parts + 1):
            @pl.when(n_parts == n)
            def _parts(n=n):
                _round_robin([part(k) for k in range(n)])
                for k in range(n, parts):
                    ybuf[oslot, part_rows(k)] = jnp.zeros((2 * PAGE_SPLIT_ROWS, HALF), ybuf.dtype)
        page_out(g, oslot).start()
        return carry

    lax.fori_loop(0, n_used, page_body, 0)

    for back in (2, 1):
        @pl.when(n_used >= back)
        def _drain(back=back):
            page_out(n_used - back, (n_used - back) % 2).wait()

    def zero_wait(page, carry):
        zero_page(page).wait()
        return carry

    lax.fori_loop(n_used, n_pages, zero_wait, 0)


def _experts(page_tables, esrc, xt, w1, w3, w2):
    D = D_MODEL
    n_pages = page_tables[1].shape[0]
    assert TM % PAGE_SPLIT_ROWS == 0
    grid_spec = pltpu.PrefetchScalarGridSpec(
        num_scalar_prefetch=len(page_tables) + 1,
        grid=(1,),
        in_specs=[pl.BlockSpec(memory_space=pl.ANY)] * 4,
        out_specs=pl.BlockSpec(memory_space=pl.ANY),
        scratch_shapes=[pltpu.VMEM((FETCH_SLOTS, TM // ROW_GROUP, 2 * ROW_GROUP, HALF), jnp.bfloat16),
                        pltpu.VMEM((2, 2 * TM, HALF), jnp.bfloat16),
                        pltpu.VMEM((2 * TM, HALF), jnp.bfloat16),
                        pltpu.VMEM((2, D, D_EXPERT), jnp.float32),
                        pltpu.VMEM((2, D, D_EXPERT), jnp.float32),
                        pltpu.VMEM((2, D_EXPERT, D), jnp.float32),
                        pltpu.VMEM((D, D_EXPERT), jnp.bfloat16),
                        pltpu.VMEM((D, D_EXPERT), jnp.bfloat16),
                        pltpu.VMEM((D_EXPERT, D), jnp.bfloat16),
                        pltpu.SemaphoreType.DMA((FETCH_SLOTS,)),
                        pltpu.SemaphoreType.DMA((2,)),
                        pltpu.SemaphoreType.DMA((2,)),
                        pltpu.SemaphoreType.DMA(())],
    )
    return pl.pallas_call(
        _expert_kernel,
        grid_spec=grid_spec,
        out_shape=jax.ShapeDtypeStruct((n_pages * 2 * TM, HALF), jnp.bfloat16),
        compiler_params=pltpu.CompilerParams(dimension_semantics=("arbitrary",), vmem_limit_bytes=VMEM_LIMIT),
        name="experts",
    )(*page_tables, esrc, xt, w1, w3, w2)


def _final_tile(y_ref, x1_ref, route_ref, ada_ref, lnw_ref, lnb_ref, o_ref):
    route = route_ref[...]
    pos1, pos2, cw1, cw2 = route[:, 0:1], route[:, 1:2], route[:, 2:3], route[:, 3:4]
    col = lax.broadcasted_iota(jnp.int32, (TS, TR), 1).astype(jnp.float32)
    wmat = _bf(jnp.where(col == pos1, cw1, 0.0) + jnp.where(col == pos2, cw2, 0.0))
    yield
    yl, yh = _unpack_rows(y_ref[...].reshape(2 * TR, HALF))
    yield
    y = jnp.concatenate([jnp.dot(wmat, yl, preferred_element_type=jnp.float32),
                         jnp.dot(wmat, yh, preferred_element_type=jnp.float32)], axis=-1)
    yield
    gate2 = ada_ref[5:6, :]
    o_ref[...] = _ln(ALPHA * x1_ref[...] + gate2 * y) * lnw_ref[...] + lnb_ref[...]


def _final_kernel(src_ref, x1_ref, route_ref, ada_ref, lnw_ref, lnb_ref, ys_ref, o_ref, ybuf, sem):
    P = TILES_PER_STEP
    i = pl.program_id(0)
    last = pl.num_programs(0) - 1
    per = TR // ROW_GROUP
    groups = P * per

    def fetch(step, dst_slot):
        _fetch_groups(ys_ref, lambda u: src_ref[step * groups + u], ybuf.at[dst_slot], sem.at[dst_slot])

    @pl.when(i == 0)
    def _prime():
        for k in range(FETCH_AHEAD):
            fetch(k, k)

    slot = i % FETCH_SLOTS
    _wait_fetch(ys_ref, ybuf.at[slot], sem.at[slot])

    ahead = jnp.minimum(i + FETCH_AHEAD, last)
    ahead_slot = (i + FETCH_AHEAD) % FETCH_SLOTS
    rows = lambda k: pl.ds(k * TS, TS)
    _round_robin([_final_tile(ybuf.at[slot, pl.ds(k * per, per)], x1_ref.at[rows(k)], route_ref.at[rows(k)], ada_ref,
                              lnw_ref, lnb_ref, o_ref.at[rows(k)]) for k in range(P)]
                 + [_fetch_groups_task(ys_ref, lambda u: src_ref[ahead * groups + u], ybuf.at[ahead_slot],
                                       sem.at[ahead_slot], FETCH_CHUNK)])

    @pl.when(i == last)
    def _drain():
        for k in range(1, FETCH_SLOTS):
            other = (i + k) % FETCH_SLOTS
            _wait_fetch(ys_ref, ybuf.at[other], sem.at[other])


def _final(fsrc, x1, route, ada, lnw, lnb, ys, seq_len):
    T, D = x1.shape
    P = TILES_PER_STEP
    rows = P * TS
    assert seq_len % rows == 0
    steps_per_seq = seq_len // rows
    n_steps = T // rows
    assert n_steps > FETCH_AHEAD
    grid_spec = pltpu.PrefetchScalarGridSpec(
        num_scalar_prefetch=1,
        grid=(n_steps,),
        in_specs=[pl.BlockSpec((rows, D), lambda i, src: (i, 0)),
                  pl.BlockSpec((rows, LANES), lambda i, src: (i, 0)),
                  pl.BlockSpec((None, 6, D), lambda i, src: (i // steps_per_seq, 0, 0)),
                  pl.BlockSpec((1, D), lambda i, src: (0, 0)),
                  pl.BlockSpec((1, D), lambda i, src: (0, 0)),
                  pl.BlockSpec(memory_space=pl.ANY)],
        out_specs=pl.BlockSpec((rows, D), lambda i, src: (i, 0)),
        scratch_shapes=[pltpu.VMEM((FETCH_SLOTS, P * (TR // ROW_GROUP), 2 * ROW_GROUP, HALF), jnp.bfloat16),
                        pltpu.SemaphoreType.DMA((FETCH_SLOTS,))],
    )
    return pl.pallas_call(
        _final_kernel,
        grid_spec=grid_spec,
        out_shape=jax.ShapeDtypeStruct((T, D), jnp.float32),
        compiler_params=pltpu.CompilerParams(dimension_semantics=("arbitrary",), vmem_limit_bytes=VMEM_LIMIT),
        name="final",
    )(fsrc, x1, route, ada, lnw, lnb, ys)


def _gather_tables(cnt, n_pages):
    nt = cnt.shape[0]
    run = (cnt + ROW_GROUP - 1) // ROW_GROUP * ROW_GROUP
    so = jnp.cumsum(run, axis=1) - run
    run_e = (cnt + PAGE_CHUNK - 1) // PAGE_CHUNK * PAGE_CHUNK
    eo = jnp.cumsum(run_e, axis=0) - run_e
    tot = jnp.sum(run_e, axis=0)
    pages_e = (tot + TM - 1) // TM
    page_end = jnp.cumsum(pages_e)
    page_start = page_end - pages_e
    g = jnp.arange(n_pages, dtype=jnp.int32)
    pe = jnp.minimum(jnp.sum(g[:, None] >= page_end[None, :], axis=1), N_EXPERTS - 1).astype(jnp.int32)
    used = g < page_end[-1]
    owner = (g[None, :] >= page_start[:, None]) & (g[None, :] < page_end[:, None])
    of_page = lambda a: jnp.sum(jnp.where(owner, a[:, None], 0), axis=0)
    of_page2 = lambda a: jnp.sum(jnp.where(owner[:, None, :], a.T[:, :, None], 0), axis=0)
    pj = g - of_page(page_start)
    tot_p = of_page(tot)
    pv = jnp.where(used, jnp.clip(tot_p - pj * TM, 0, TM), 0).astype(jnp.int32)
    experts = jnp.arange(N_EXPERTS, dtype=jnp.int32)
    has_pages = pages_e > 0
    wslot_e = (jnp.cumsum(has_pages) - 1) % 2
    later = (experts[None, :] > experts[:, None]) & has_pages[None, :]
    next_e = jnp.min(jnp.where(later, experts[None, :], N_EXPERTS), axis=1)
    next_e = jnp.where(next_e < N_EXPERTS, next_e, -1)
    first = (used & (pj == 0)).astype(jnp.int32)
    page_tables = (page_end[-1:].astype(jnp.int32), pe, pv, first, of_page(wslot_e).astype(jnp.int32),
                   jnp.where(used, of_page(next_e), -1).astype(jnp.int32))

    q = (pj * TM)[:, None] + PAGE_CHUNK * jnp.arange(TM // PAGE_CHUNK, dtype=jnp.int32)[None, :]
    tiles = jnp.arange(nt, dtype=jnp.int32)
    offset = of_page2(tiles[:, None] * TR + so - eo)
    step = jnp.concatenate([offset[1:] - offset[:-1], jnp.zeros_like(offset[:1])], axis=0)
    run_end = of_page2(eo + run_e)
    src = q + offset[0][:, None] + jnp.sum(jnp.where(run_end[:, :, None] <= q[None], step[:, :, None], 0), axis=0)
    zero_xt = TR - PAGE_CHUNK
    esrc = jnp.where((q < tot_p[:, None]) & used[:, None], src, zero_xt).astype(jnp.int32)

    r = ROW_GROUP * jnp.arange(TR // ROW_GROUP, dtype=jnp.int32)
    end = (so + run).T
    offset_f = ((page_start * TM)[None, :] + eo - so).T
    step_f = jnp.concatenate([offset_f[1:] - offset_f[:-1], jnp.zeros_like(offset_f[:1])], axis=0)
    srcf = (r[None, :] + offset_f[0][:, None]
            + jnp.sum(jnp.where(end[:, :, None] <= r[None, None, :], step_f[:, :, None], 0), axis=0))
    zero_ys = (n_pages - 1) * TM
    fsrc = jnp.where(r[None, :] < end[-1][:, None], srcf, zero_ys).astype(jnp.int32)
    return page_tables, esrc.reshape(-1) // ROW_GROUP, fsrc.reshape(-1) // ROW_GROUP


def kernel(x, c, positions, w_ada, b_ada, w_in, ret_gn_w, w_pool, pool_scale, w_out, ln1_w, ln1_b, w_group, b_group,
           w_router, b_router, w1, w3, w2, ln2_w, ln2_b):
    B, S, D = x.shape
    T = B * S
    assert w_ada.shape[0] == DEPTH and D == D_MODEL and S % TS == 0

    inv_freq = ROPE_BASE ** (-jnp.arange(0, HEAD_DIM, 2, dtype=jnp.float32) / HEAD_DIM)
    half = HEAD_DIM // 2
    rope = jnp.stack([jnp.concatenate([inv_freq, inv_freq]),
                      jnp.concatenate([-jnp.ones((half,), jnp.float32), jnp.ones((half,), jnp.float32)])])

    nt = T // TS
    n_pages = (2 * T + nt * N_EXPERTS * (PAGE_CHUNK - 1)) // TM + N_EXPERTS + 1

    for l in range(DEPTH):
        ada = _ada(c, w_ada[l], b_ada[l]).reshape(B, 6, D)
        pad = SUBLANES - N_GROUPS
        wrt = jnp.concatenate([w_group[l].T, jnp.zeros((pad, D), jnp.float32), w_router[l].T], axis=0)
        brt = jnp.concatenate([b_group[l], jnp.zeros((pad,), jnp.float32), b_router[l]]).reshape(ROUTE_ROWS, 1)
        x1, xt, route, counts = _mix(
            x, positions, ada, rope, _bf(w_in[l]), _bf(w_out[l]), _bf(w_pool[l]),
            ret_gn_w[l].reshape(1, RET_WIDTH), pool_scale[l].reshape(1, POOL_WIDTH),
            ln1_w[l].reshape(1, D), ln1_b[l].reshape(1, D), wrt, brt)

        cnt = counts[:, :, :, 0].reshape(nt, N_EXPERTS).astype(jnp.int32)
        page_tables, esrc, fsrc = _gather_tables(cnt, n_pages)
        ys = _experts(page_tables, esrc, xt.reshape(nt * TR // ROW_GROUP, 2 * ROW_GROUP, HALF),
                      w1[l].reshape(N_EXPERTS, D, D_EXPERT), w3[l].reshape(N_EXPERTS, D, D_EXPERT),
                      w2[l].reshape(N_EXPERTS, D_EXPERT, D))
        x = _final(fsrc, x1.reshape(T, D), route.reshape(T, LANES), ada,
                   ln2_w[l].reshape(1, D), ln2_b[l].reshape(1, D),
                   ys.reshape(n_pages * TM // ROW_GROUP, 2 * ROW_GROUP, HALF), S).reshape(B, S, D)
    return x
```

```python
import math

import jax
import jax.numpy as jnp
from jax import lax
from jax.experimental import pallas as pl
from jax.experimental.pallas import tpu as pltpu

D_MODEL = 1024
RET_WIDTH = 512
RET_HEADS = 4
HEAD_DIM = 128
POOL_WIDTH = 512
POOL_WINDOWS = (2, 4, 8, 16)
POOL_GROUP_DIM = 128
IN_COLS = 4 * RET_WIDTH + POOL_WIDTH
N_GROUPS = 4
EXPERTS_PER_GROUP = 8
N_EXPERTS = 32
D_EXPERT = 256
DEPTH = 1
ALPHA = (2.0 * DEPTH) ** 0.25
LN_EPS = 1e-5
ROPE_BASE = 10000.0

LANES = 128
SUBLANES = 8
ROW_GROUP = 8
PAGE_CHUNK = 2 * ROW_GROUP
HALF = D_MODEL // 2
POOL_HALO = 16
TS = 256
TM = 512
PAGE_SPLIT_ROWS = 256
TR = -(-(2 * TS + N_EXPERTS * (ROW_GROUP - 1)) // LANES) * LANES
SEQS_PER_STEP = 4
TILES_PER_STEP = 2
ROUTE_ROWS = SUBLANES + N_EXPERTS
FETCH_AHEAD = 2
FETCH_SLOTS = FETCH_AHEAD + 1
PAGE_FETCH_AHEAD = 4
PAGE_FETCH_SLOTS = PAGE_FETCH_AHEAD + 1
FETCH_CHUNK = 64
VMEM_LIMIT = 60 * 1024 * 1024

_LOG_GAMMA = tuple(math.log1p(-(2.0 ** (-5.0 - h))) for h in range(RET_HEADS))
_HI = lax.Precision.HIGHEST


def _ln(x):
    mu = jnp.mean(x, axis=-1, keepdims=True)
    xc = x - mu
    var = jnp.mean(xc * xc, axis=-1, keepdims=True)
    return xc * lax.rsqrt(var + LN_EPS)


def _bf(x):
    return x.astype(jnp.bfloat16)


def _pack_rows(x):
    n = x.shape[0]
    lo = x[:, :HALF].reshape(n // ROW_GROUP, ROW_GROUP, HALF)
    hi = x[:, HALF:].reshape(n // ROW_GROUP, ROW_GROUP, HALF)
    return _bf(jnp.concatenate([lo, hi], axis=1).reshape(2 * n, HALF))


def _unpack_rows(z):
    n = z.shape[0] // 2
    zf = z.astype(jnp.float32).reshape(n // ROW_GROUP, 2 * ROW_GROUP, HALF)
    return _bf(zf[:, :ROW_GROUP, :].reshape(n, HALF)), _bf(zf[:, ROW_GROUP:, :].reshape(n, HALF))


def _ada_kernel(c_ref, w_ref, b_ref, o_ref):
    k = pl.program_id(0)

    @pl.when(k == 0)
    def _init():
        o_ref[...] = jnp.broadcast_to(b_ref[...], o_ref.shape)

    c = c_ref[...]
    ca = c * jax.nn.sigmoid(c)
    o_ref[...] += jnp.dot(ca, w_ref[...], precision=_HI, preferred_element_type=jnp.float32)


def _ada(c, w_ada, b_ada):
    B, D = c.shape
    n = w_ada.shape[1]
    kb = LANES
    c_blocks = c.reshape(B, D // kb, kb).transpose(1, 0, 2)
    return pl.pallas_call(
        _ada_kernel,
        grid=(D // kb,),
        in_specs=[pl.BlockSpec((None, B, kb), lambda k: (k, 0, 0)),
                  pl.BlockSpec((kb, n), lambda k: (k, 0)),
                  pl.BlockSpec((1, n), lambda k: (0, 0))],
        out_specs=pl.BlockSpec((B, n), lambda k: (0, 0)),
        out_shape=jax.ShapeDtypeStruct((B, n), jnp.float32),
        compiler_params=pltpu.CompilerParams(dimension_semantics=("arbitrary",), vmem_limit_bytes=VMEM_LIMIT),
        name="ada",
    )(c_blocks, w_ada, b_ada.reshape(1, n))


def _mix_kernel(x_ref, pos_ref, ada_ref, rope_ref, win_ref, wout_ref, wpool_ref, gnw_ref, pscale_ref,
                ln1w_ref, ln1b_ref, wrt_ref, brt_ref,
                x1_ref, xt_ref, route_ref, cnt_ref,
                state_ref, halo_ref, dmat_ref, qdec_ref, kdec_ref, cat_ref):
    b = pl.program_id(0)
    s = pl.program_id(1)

    @pl.when((b == 0) & (s == 0))
    def _init_tables():
        ri = lax.broadcasted_iota(jnp.int32, (TS, TS), 0)
        ci = lax.broadcasted_iota(jnp.int32, (TS, TS), 1)
        rel = (ri - ci).astype(jnp.float32)
        for h in range(RET_HEADS):
            dmat_ref[h] = jnp.where(rel >= 0.0, jnp.exp(jnp.maximum(rel, 0.0) * _LOG_GAMMA[h]), 0.0)
        row = lax.broadcasted_iota(jnp.int32, (TS, RET_WIDTH), 0).astype(jnp.float32)
        lane = lax.broadcasted_iota(jnp.int32, (TS, RET_WIDTH), 1)
        lg = jnp.full((TS, RET_WIDTH), _LOG_GAMMA[0], jnp.float32)
        for h in range(1, RET_HEADS):
            lg = jnp.where(lane >= h * HEAD_DIM, _LOG_GAMMA[h], lg)
        qdec_ref[...] = jnp.exp((row + 1.0) * lg)
        kdec_ref[...] = jnp.exp((TS - 1.0 - row) * lg)

    @pl.when(s == 0)
    def _init_carries():
        state_ref[...] = jnp.zeros_like(state_ref)
        halo_ref[...] = jnp.zeros_like(halo_ref)

    tiles = [_mix_tile(s, x_ref.at[j], pos_ref.at[j], ada_ref.at[j], rope_ref, win_ref, wout_ref, wpool_ref, gnw_ref,
                       pscale_ref, ln1w_ref, ln1b_ref, wrt_ref, brt_ref,
                       x1_ref.at[j], xt_ref.at[j], route_ref.at[j], cnt_ref.at[j],
                       state_ref.at[j], halo_ref.at[j], dmat_ref, qdec_ref, kdec_ref, cat_ref.at[j])
             for j in range(SEQS_PER_STEP)]
    _round_robin(tiles)


def _mix_tile(s, x_ref, pos_ref, ada_ref, rope_ref, win_ref, wout_ref, wpool_ref, gnw_ref, pscale_ref,
              ln1w_ref, ln1b_ref, wrt_ref, brt_ref,
              x1_ref, xt_ref, route_ref, cnt_ref,
              state_ref, halo_ref, dmat_ref, qdec_ref, kdec_ref, cat_ref):
    ada = ada_ref[...]
    shift1, scale1, gate1 = ada[0:1], ada[1:2], ada[2:3]
    shift2, scale2 = ada[3:4], ada[4:5]

    x = x_ref[...]
    u = _bf(_ln(x) * (1.0 + scale1) + shift1)
    yield

    posf = jnp.broadcast_to(pos_ref[...].astype(jnp.float32), (LANES, TS)).T
    hl = lax.broadcasted_iota(jnp.int32, (TS // 2, HEAD_DIM), 1) < HEAD_DIM // 2
    ang = jnp.where(hl, posf[:TS // 2], posf[TS // 2:]) * rope_ref[0:1, :]
    cos_p, sin_p = jnp.cos(ang), jnp.sin(ang)
    cos_s, sin_s = pltpu.roll(cos_p, HEAD_DIM // 2, 1), pltpu.roll(sin_p, HEAD_DIM // 2, 1)
    cos_t = jnp.concatenate([jnp.where(hl, cos_p, cos_s), jnp.where(hl, cos_s, cos_p)], axis=0)
    sin_t = jnp.concatenate([jnp.where(hl, sin_p, sin_s), jnp.where(hl, sin_s, sin_p)], axis=0) * rope_ref[1:2, :]
    yield

    q = jnp.dot(u, win_ref[:, 0:RET_WIDTH], preferred_element_type=jnp.float32)
    k = jnp.dot(u, win_ref[:, RET_WIDTH:2 * RET_WIDTH], preferred_element_type=jnp.float32)
    v = jnp.dot(u, win_ref[:, 2 * RET_WIDTH:3 * RET_WIDTH], preferred_element_type=jnp.float32)
    g = jnp.dot(u, win_ref[:, 3 * RET_WIDTH:4 * RET_WIDTH], preferred_element_type=jnp.float32)
    p = jnp.dot(u, win_ref[:, 4 * RET_WIDTH:IN_COLS], preferred_element_type=jnp.float32)
    yield

    gnw = gnw_ref[...]
    for h in range(RET_HEADS):
        sl = slice(h * HEAD_DIM, (h + 1) * HEAD_DIM)
        qh, kh, vh = q[:, sl], k[:, sl], v[:, sl]
        qr = qh * cos_t + pltpu.roll(qh, HEAD_DIM // 2, 1) * sin_t
        kr = (kh * cos_t + pltpu.roll(kh, HEAD_DIM // 2, 1) * sin_t) * (HEAD_DIM ** -0.5)
        vb = _bf(vh)
        sc = lax.dot_general(_bf(qr), _bf(kr), (((1,), (1,)), ((), ())), preferred_element_type=jnp.float32)
        intra = jnp.dot(_bf(sc * dmat_ref[h]), vb, preferred_element_type=jnp.float32)
        st = state_ref[h]
        cross = jnp.dot(_bf(qr * qdec_ref[:, sl]), _bf(st), preferred_element_type=jnp.float32)
        kv = lax.dot_general(_bf(kr * kdec_ref[:, sl]), vb, (((0,), (0,)), ((), ())),
                             preferred_element_type=jnp.float32)
        state_ref[h] = st * math.exp(TS * _LOG_GAMMA[h]) + kv
        r = _ln(intra + cross) * gnw[:, sl]
        gh = g[:, sl]
        cat_ref[:, sl] = _bf(gh * jax.nn.sigmoid(gh) * r)
        yield

    pext = jnp.concatenate([halo_ref[...], p], axis=0)
    halo_ref[...] = p[TS - POOL_HALO:, :]
    t_abs = (s * TS + lax.broadcasted_iota(jnp.int32, (TS, 1), 0) + 1).astype(jnp.float32)
    pscale = pscale_ref[...]
    for grp, w in enumerate(POOL_WINDOWS):
        sl = slice(grp * POOL_GROUP_DIM, (grp + 1) * POOL_GROUP_DIM)
        acc = pext[:, sl]
        shift = 1
        while shift < w:
            acc = acc + pltpu.roll(acc, shift, 0)
            shift *= 2
        pooled = acc[POOL_HALO:, :] / jnp.minimum(t_abs, float(w)) - p[:, sl]
        po = jnp.dot(_bf(pooled), wpool_ref[grp], preferred_element_type=jnp.float32) * pscale[:, sl]
        cat_ref[:, RET_WIDTH + grp * POOL_GROUP_DIM:RET_WIDTH + (grp + 1) * POOL_GROUP_DIM] = _bf(po)
    yield

    mix = jnp.dot(cat_ref[...], wout_ref[...], preferred_element_type=jnp.float32)
    yield
    x1 = _ln(ALPHA * x + gate1 * mix) * ln1w_ref[...] + ln1b_ref[...]
    x1_ref[...] = x1
    u2 = _ln(x1) * (1.0 + scale2) + shift2
    yield

    w = wrt_ref[...]
    w_hi = _bf(w)
    w_lo = _bf(w - w_hi.astype(jnp.float32))
    u2_hi = _bf(u2)
    u2_lo = _bf(u2 - u2_hi.astype(jnp.float32))
    nt = (((1,), (1,)), ((), ()))
    logits = (lax.dot_general(w_hi, u2_hi, nt, preferred_element_type=jnp.float32)
              + lax.dot_general(w_hi, u2_lo, nt, preferred_element_type=jnp.float32)
              + lax.dot_general(w_lo, u2_hi, nt, preferred_element_type=jnp.float32)) + brt_ref[...]
    row8 = lax.broadcasted_iota(jnp.int32, (SUBLANES, TS), 0)
    neg = jnp.float32(-jnp.inf)
    gl = jnp.where(row8 < N_GROUPS, logits[0:SUBLANES], neg)
    gmax = jnp.max(gl, axis=0, keepdims=True)
    gidx = jnp.min(jnp.where(gl == gmax, row8, SUBLANES), axis=0, keepdims=True)
    gprob = 1.0 / jnp.sum(jnp.exp(gl - gmax), axis=0, keepdims=True)
    el = logits[SUBLANES:2 * SUBLANES]
    for grp in range(1, N_GROUPS):
        el = jnp.where(gidx == grp, logits[(grp + 1) * SUBLANES:(grp + 2) * SUBLANES], el)
    m1 = jnp.max(el, axis=0, keepdims=True)
    j1 = jnp.min(jnp.where(el == m1, row8, SUBLANES), axis=0, keepdims=True)
    el2 = jnp.where(row8 == j1, neg, el)
    m2 = jnp.max(el2, axis=0, keepdims=True)
    j2 = jnp.min(jnp.where(el2 == m2, row8, SUBLANES), axis=0, keepdims=True)
    e21 = jnp.exp(m2 - m1)
    den = 1.0 / (1.0 + e21)
    cw1 = gprob * den
    cw2 = gprob * e21 * den
    yield

    erow = lax.broadcasted_iota(jnp.int32, (N_EXPERTS, TS), 0)
    oh1 = erow == gidx * EXPERTS_PER_GROUP + j1
    oh2 = erow == gidx * EXPERTS_PER_GROUP + j2
    oh = jnp.where(oh1 | oh2, 1.0, 0.0)
    cnt = jnp.broadcast_to(jnp.sum(oh, axis=1, keepdims=True), (N_EXPERTS, LANES))
    run = jnp.floor((cnt + (ROW_GROUP - 1.0)) * (1.0 / ROW_GROUP)) * ROW_GROUP
    erow_l = lax.broadcasted_iota(jnp.int32, (N_EXPERTS, LANES), 0)
    run_end = run
    shift = 1
    while shift < N_EXPERTS:
        run_end = run_end + jnp.where(erow_l >= shift, pltpu.roll(run_end, shift, 0), 0.0)
        shift *= 2
    run_start = (run_end - run)[:, 0:1]
    ri = lax.broadcasted_iota(jnp.int32, (TS, TS), 0)
    ci = lax.broadcasted_iota(jnp.int32, (TS, TS), 1)
    earlier = _bf(jnp.where(ri < ci, 1.0, 0.0))
    before = jnp.dot(_bf(oh), earlier, preferred_element_type=jnp.float32) + run_start
    pos1 = jnp.sum(jnp.where(oh1, before, 0.0), axis=0, keepdims=True)
    pos2 = jnp.sum(jnp.where(oh2, before, 0.0), axis=0, keepdims=True)
    rr = lax.broadcasted_iota(jnp.int32, (TR, TS), 0).astype(jnp.float32)
    perm = _bf(jnp.where((rr == pos1) | (rr == pos2), 1.0, 0.0))
    yield
    xt_ref[...] = _pack_rows(jnp.dot(perm, u2_hi, preferred_element_type=jnp.float32))
    cnt_ref[...] = cnt

    rowl = lax.broadcasted_iota(jnp.int32, (LANES, TS), 0)
    rec = jnp.where(rowl == 0, pos1, 0.0)
    rec = jnp.where(rowl == 1, pos2, rec)
    rec = jnp.where(rowl == 2, cw1, rec)
    rec = jnp.where(rowl == 3, cw2, rec)
    route_ref[...] = rec.T


def _mix(x, positions, ada, rope, win, wout, wpool, gnw, pscale, ln1w, ln1b, wrt, brt):
    B, S, D = x.shape
    ns = S // TS
    assert B % SEQS_PER_STEP == 0
    P = SEQS_PER_STEP
    const2 = lambda b, s: (0, 0)
    const3 = lambda b, s: (0, 0, 0)
    tile = lambda b, s: (b, s, 0)
    flat = lambda b, s: (b, s, 0, 0)
    return pl.pallas_call(
        _mix_kernel,
        grid=(B // P, ns),
        in_specs=[
            pl.BlockSpec((P, TS, D), tile),
            pl.BlockSpec((P, None, 1, TS), flat),
            pl.BlockSpec((P, 6, D), lambda b, s: (b, 0, 0)),
            pl.BlockSpec((2, LANES), const2),
            pl.BlockSpec((D, IN_COLS), const2, pipeline_mode=pl.Buffered(1)),
            pl.BlockSpec((D, D), const2, pipeline_mode=pl.Buffered(1)),
            pl.BlockSpec((len(POOL_WINDOWS), POOL_GROUP_DIM, POOL_GROUP_DIM), const3),
            pl.BlockSpec((1, RET_WIDTH), const2),
            pl.BlockSpec((1, POOL_WIDTH), const2),
            pl.BlockSpec((1, D), const2),
            pl.BlockSpec((1, D), const2),
            pl.BlockSpec((ROUTE_ROWS, D), const2),
            pl.BlockSpec((ROUTE_ROWS, 1), const2),
        ],
        out_specs=[
            pl.BlockSpec((P, TS, D), tile),
            pl.BlockSpec((P, None, 2 * TR, HALF), flat),
            pl.BlockSpec((P, TS, LANES), tile),
            pl.BlockSpec((P, None, N_EXPERTS, LANES), flat),
        ],
        out_shape=[
            jax.ShapeDtypeStruct((B, S, D), jnp.float32),
            jax.ShapeDtypeStruct((B, ns, 2 * TR, HALF), jnp.bfloat16),
            jax.ShapeDtypeStruct((B, S, LANES), jnp.float32),
            jax.ShapeDtypeStruct((B, ns, N_EXPERTS, LANES), jnp.float32),
        ],
        scratch_shapes=[
            pltpu.VMEM((P, RET_HEADS, HEAD_DIM, HEAD_DIM), jnp.float32),
            pltpu.VMEM((P, POOL_HALO, POOL_WIDTH), jnp.float32),
            pltpu.VMEM((RET_HEADS, TS, TS), jnp.float32),
            pltpu.VMEM((TS, RET_WIDTH), jnp.float32),
            pltpu.VMEM((TS, RET_WIDTH), jnp.float32),
            pltpu.VMEM((P, TS, D), jnp.bfloat16),
        ],
        compiler_params=pltpu.CompilerParams(dimension_semantics=("arbitrary", "arbitrary"),
                                             vmem_limit_bytes=VMEM_LIMIT),
        name="mix",
    )(x, positions.reshape(B, ns, 1, TS), ada, rope, win, wout, wpool, gnw, pscale, ln1w, ln1b, wrt, brt)


def _fetch_groups(src_ref, group_of, dst_ref, sem):
    for u in range(dst_ref.shape[0]):
        pltpu.make_async_copy(src_ref.at[group_of(u)], dst_ref.at[u], sem).start()


def _fetch_groups_task(src_ref, group_of, dst_ref, sem, chunk):
    for u in range(dst_ref.shape[0]):
        pltpu.make_async_copy(src_ref.at[group_of(u)], dst_ref.at[u], sem).start()
        if u % chunk == chunk - 1:
            yield


def _wait_fetch(src_ref, dst_ref, sem):
    pltpu.make_async_copy(src_ref.at[pl.ds(0, dst_ref.shape[0])], dst_ref, sem).wait()


def _expert_rows(x_ref, w1_ref, w3_ref, w2_ref, y_ref):
    n = PAGE_SPLIT_ROWS
    xl, xh = _unpack_rows(x_ref[...].reshape(2 * n, HALF))
    yield
    a = (jnp.dot(xl, w1_ref[:HALF, :], preferred_element_type=jnp.float32)
         + jnp.dot(xh, w1_ref[HALF:, :], preferred_element_type=jnp.float32))
    c = (jnp.dot(xl, w3_ref[:HALF, :], preferred_element_type=jnp.float32)
         + jnp.dot(xh, w3_ref[HALF:, :], preferred_element_type=jnp.float32))
    yield
    h = _bf(a * jax.nn.sigmoid(a) * c)
    yield
    y = jnp.dot(h, w2_ref[...], preferred_element_type=jnp.float32)
    yield
    y_ref[...] = _pack_rows(y)


def _round_robin(tasks):
    while tasks:
        tasks = [t for t in tasks if next(t, True) is None]


def _expert_kernel(used_ref, pe_ref, pv_ref, first_ref, wslot_ref, next_ref, src_ref, xt_ref, w1_hbm, w3_hbm, w2_hbm,
                   ys_ref, xbuf, ybuf, zbuf, w1_buf, w3_buf, w2_buf, w1_bf, w3_bf, w2_bf, sem, osem, wsem, zsem):
    n_used = used_ref[0]
    n_pages = ys_ref.shape[0] // (2 * TM)
    chunks = TM // PAGE_CHUNK
    per_chunk = PAGE_CHUNK // ROW_GROUP

    def fetch(page):
        @pl.when(page < n_used)
        def _start():
            slot = page % PAGE_FETCH_SLOTS
            for u in range(chunks):
                pltpu.make_async_copy(xt_ref.at[pl.ds(src_ref[page * chunks + u], per_chunk)],
                                      xbuf.at[slot, pl.ds(u * per_chunk, per_chunk)], sem.at[slot]).start()

    def weight_copies(expert, ws):
        return [pltpu.make_async_copy(hbm.at[expert], buf.at[ws], wsem.at[ws])
                for hbm, buf in ((w1_hbm, w1_buf), (w3_hbm, w3_buf), (w2_hbm, w2_buf))]

    def page_out(page, oslot):
        return pltpu.make_async_copy(ybuf.at[oslot], ys_ref.at[pl.ds(pl.multiple_of(page * 2 * TM, 2 * TM), 2 * TM)],
                                     osem.at[oslot])

    for k in range(PAGE_FETCH_AHEAD):
        fetch(k)

    zbuf[...] = jnp.zeros_like(zbuf)

    def zero_page(page):
        return pltpu.make_async_copy(zbuf, ys_ref.at[pl.ds(pl.multiple_of(page * 2 * TM, 2 * TM), 2 * TM)], zsem)

    def zero_start(page, carry):
        zero_page(page).start()
        return carry

    lax.fori_loop(n_used, n_pages, zero_start, 0)

    @pl.when(n_used > 0)
    def _first_weights():
        for c in weight_copies(pe_ref[0], wslot_ref[0]):
            c.start()

    def page_body(g, carry):
        fetch(g + PAGE_FETCH_AHEAD)
        ws = wslot_ref[g]

        @pl.when(first_ref[g] == 1)
        def _new_expert():
            for c in weight_copies(pe_ref[g], ws):
                c.wait()
            w1_bf[...] = _bf(w1_buf[ws])
            w3_bf[...] = _bf(w3_buf[ws])
            w2_bf[...] = _bf(w2_buf[ws])

            @pl.when(next_ref[g] >= 0)
            def _stream_next():
                for c in weight_copies(next_ref[g], 1 - ws):
                    c.start()

        slot = g % PAGE_FETCH_SLOTS
        oslot = g % 2
        _wait_fetch(xt_ref, xbuf.at[slot], sem.at[slot])

        @pl.when(g >= 2)
        def _reuse_out_buffer():
            page_out(g - 2, oslot).wait()

        per = PAGE_SPLIT_ROWS // ROW_GROUP
        parts = TM // PAGE_SPLIT_ROWS
        part_rows = lambda k: pl.ds(k * 2 * PAGE_SPLIT_ROWS, 2 * PAGE_SPLIT_ROWS)
        part = lambda k: _expert_rows(xbuf.at[slot, pl.ds(k * per, per)], w1_bf, w3_bf, w2_bf,
                                      ybuf.at[oslot, part_rows(k)])
        n_parts = (pv_ref[g] + PAGE_SPLIT_ROWS - 1) // PAGE_SPLIT_ROWS
        for n in range(1, parts + 1):
            @pl.when(n_parts == n)
            def _parts(n=n):
                _round_robin([part(k) for k in range(n)])
                for k in range(n, parts):
                    ybuf[oslot, part_rows(k)] = jnp.zeros((2 * PAGE_SPLIT_ROWS, HALF), ybuf.dtype)
        page_out(g, oslot).start()
        return carry

    lax.fori_loop(0, n_used, page_body, 0)

    for back in (2, 1):
        @pl.when(n_used >= back)
        def _drain(back=back):
            page_out(n_used - back, (n_used - back) % 2).wait()

    def zero_wait(page, carry):
        zero_page(page).wait()
        return carry

    lax.fori_loop(n_used, n_pages, zero_wait, 0)


def _experts(page_tables, esrc, xt, w1, w3, w2):
    D = D_MODEL
    n_pages = page_tables[1].shape[0]
    assert TM % PAGE_SPLIT_ROWS == 0
    grid_spec = pltpu.PrefetchScalarGridSpec(
        num_scalar_prefetch=len(page_tables) + 1,
        grid=(1,),
        in_specs=[pl.BlockSpec(memory_space=pl.ANY)] * 4,
        out_specs=pl.BlockSpec(memory_space=pl.ANY),
        scratch_shapes=[pltpu.VMEM((PAGE_FETCH_SLOTS, TM // ROW_GROUP, 2 * ROW_GROUP, HALF), jnp.bfloat16),
                        pltpu.VMEM((2, 2 * TM, HALF), jnp.bfloat16),
                        pltpu.VMEM((2 * TM, HALF), jnp.bfloat16),
                        pltpu.VMEM((2, D, D_EXPERT), jnp.float32),
                        pltpu.VMEM((2, D, D_EXPERT), jnp.float32),
                        pltpu.VMEM((2, D_EXPERT, D), jnp.float32),
                        pltpu.VMEM((D, D_EXPERT), jnp.bfloat16),
                        pltpu.VMEM((D, D_EXPERT), jnp.bfloat16),
                        pltpu.VMEM((D_EXPERT, D), jnp.bfloat16),
                        pltpu.SemaphoreType.DMA((PAGE_FETCH_SLOTS,)),
                        pltpu.SemaphoreType.DMA((2,)),
                        pltpu.SemaphoreType.DMA((2,)),
                        pltpu.SemaphoreType.DMA(())],
    )
    return pl.pallas_call(
        _expert_kernel,
        grid_spec=grid_spec,
        out_shape=jax.ShapeDtypeStruct((n_pages * 2 * TM, HALF), jnp.bfloat16),
        compiler_params=pltpu.CompilerParams(dimension_semantics=("arbitrary",), vmem_limit_bytes=VMEM_LIMIT),
        name="experts",
    )(*page_tables, esrc, xt, w1, w3, w2)


def _final_tile(y_ref, x1_ref, route_ref, ada_ref, lnw_ref, lnb_ref, o_ref):
    route = route_ref[...]
    pos1, pos2, cw1, cw2 = route[:, 0:1], route[:, 1:2], route[:, 2:3], route[:, 3:4]
    col = lax.broadcasted_iota(jnp.int32, (TS, TR), 1).astype(jnp.float32)
    wmat = _bf(jnp.where(col == pos1, cw1, 0.0) + jnp.where(col == pos2, cw2, 0.0))
    yield
    yl, yh = _unpack_rows(y_ref[...].reshape(2 * TR, HALF))
    yield
    y = jnp.concatenate([jnp.dot(wmat, yl, preferred_element_type=jnp.float32),
                         jnp.dot(wmat, yh, preferred_element_type=jnp.float32)], axis=-1)
    yield
    gate2 = ada_ref[5:6, :]
    o_ref[...] = _ln(ALPHA * x1_ref[...] + gate2 * y) * lnw_ref[...] + lnb_ref[...]


def _final_kernel(src_ref, x1_ref, route_ref, ada_ref, lnw_ref, lnb_ref, ys_ref, o_ref, ybuf, sem):
    P = TILES_PER_STEP
    i = pl.program_id(0)
    last = pl.num_programs(0) - 1
    per = TR // ROW_GROUP
    groups = P * per

    def fetch(step, dst_slot):
        _fetch_groups(ys_ref, lambda u: src_ref[step * groups + u], ybuf.at[dst_slot], sem.at[dst_slot])

    @pl.when(i == 0)
    def _prime():
        for k in range(FETCH_AHEAD):
            fetch(k, k)

    slot = i % FETCH_SLOTS
    _wait_fetch(ys_ref, ybuf.at[slot], sem.at[slot])

    ahead = jnp.minimum(i + FETCH_AHEAD, last)
    ahead_slot = (i + FETCH_AHEAD) % FETCH_SLOTS
    rows = lambda k: pl.ds(k * TS, TS)
    _round_robin([_final_tile(ybuf.at[slot, pl.ds(k * per, per)], x1_ref.at[rows(k)], route_ref.at[rows(k)], ada_ref,
                              lnw_ref, lnb_ref, o_ref.at[rows(k)]) for k in range(P)]
                 + [_fetch_groups_task(ys_ref, lambda u: src_ref[ahead * groups + u], ybuf.at[ahead_slot],
                                       sem.at[ahead_slot], FETCH_CHUNK)])

    @pl.when(i == last)
    def _drain():
        for k in range(1, FETCH_SLOTS):
            other = (i + k) % FETCH_SLOTS
            _wait_fetch(ys_ref, ybuf.at[other], sem.at[other])


def _final(fsrc, x1, route, ada, lnw, lnb, ys, seq_len):
    T, D = x1.shape
    P = TILES_PER_STEP
    rows = P * TS
    assert seq_len % rows == 0
    steps_per_seq = seq_len // rows
    n_steps = T // rows
    assert n_steps > FETCH_AHEAD
    grid_spec = pltpu.PrefetchScalarGridSpec(
        num_scalar_prefetch=1,
        grid=(n_steps,),
        in_specs=[pl.BlockSpec((rows, D), lambda i, src: (i, 0)),
                  pl.BlockSpec((rows, LANES), lambda i, src: (i, 0)),
                  pl.BlockSpec((None, 6, D), lambda i, src: (i // steps_per_seq, 0, 0)),
                  pl.BlockSpec((1, D), lambda i, src: (0, 0)),
                  pl.BlockSpec((1, D), lambda i, src: (0, 0)),
                  pl.BlockSpec(memory_space=pl.ANY)],
        out_specs=pl.BlockSpec((rows, D), lambda i, src: (i, 0)),
        scratch_shapes=[pltpu.VMEM((FETCH_SLOTS, P * (TR // ROW_GROUP), 2 * ROW_GROUP, HALF), jnp.bfloat16),
                        pltpu.SemaphoreType.DMA((FETCH_SLOTS,))],
    )
    return pl.pallas_call(
        _final_kernel,
        grid_spec=grid_spec,
        out_shape=jax.ShapeDtypeStruct((T, D), jnp.float32),
        compiler_params=pltpu.CompilerParams(dimension_semantics=("arbitrary",), vmem_limit_bytes=VMEM_LIMIT),
        name="final",
    )(fsrc, x1, route, ada, lnw, lnb, ys)


def _gather_tables(cnt, n_pages):
    nt = cnt.shape[0]
    run = (cnt + ROW_GROUP - 1) // ROW_GROUP * ROW_GROUP
    so = jnp.cumsum(run, axis=1) - run
    run_e = (cnt + PAGE_CHUNK - 1) // PAGE_CHUNK * PAGE_CHUNK
    eo = jnp.cumsum(run_e, axis=0) - run_e
    tot = jnp.sum(run_e, axis=0)
    pages_e = (tot + TM - 1) // TM
    page_end = jnp.cumsum(pages_e)
    page_start = page_end - pages_e
    g = jnp.arange(n_pages, dtype=jnp.int32)
    pe = jnp.minimum(jnp.sum(g[:, None] >= page_end[None, :], axis=1), N_EXPERTS - 1).astype(jnp.int32)
    used = g < page_end[-1]
    owner = (g[None, :] >= page_start[:, None]) & (g[None, :] < page_end[:, None])
    of_page = lambda a: jnp.sum(jnp.where(owner, a[:, None], 0), axis=0)
    of_page2 = lambda a: jnp.sum(jnp.where(owner[:, None, :], a.T[:, :, None], 0), axis=0)
    pj = g - of_page(page_start)
    tot_p = of_page(tot)
    pv = jnp.where(used, jnp.clip(tot_p - pj * TM, 0, TM), 0).astype(jnp.int32)
    experts = jnp.arange(N_EXPERTS, dtype=jnp.int32)
    has_pages = pages_e > 0
    wslot_e = (jnp.cumsum(has_pages) - 1) % 2
    later = (experts[None, :] > experts[:, None]) & has_pages[None, :]
    next_e = jnp.min(jnp.where(later, experts[None, :], N_EXPERTS), axis=1)
    next_e = jnp.where(next_e < N_EXPERTS, next_e, -1)
    first = (used & (pj == 0)).astype(jnp.int32)
    page_tables = (page_end[-1:].astype(jnp.int32), pe, pv, first, of_page(wslot_e).astype(jnp.int32),
                   jnp.where(used, of_page(next_e), -1).astype(jnp.int32))

    q = (pj * TM)[:, None] + PAGE_CHUNK * jnp.arange(TM // PAGE_CHUNK, dtype=jnp.int32)[None, :]
    tiles = jnp.arange(nt, dtype=jnp.int32)
    offset = of_page2(tiles[:, None] * TR + so - eo)
    step = jnp.concatenate([offset[1:] - offset[:-1], jnp.zeros_like(offset[:1])], axis=0)
    run_end = of_page2(eo + run_e)
    src = q + offset[0][:, None] + jnp.sum(jnp.where(run_end[:, :, None] <= q[None], step[:, :, None], 0), axis=0)
    zero_xt = TR - PAGE_CHUNK
    esrc = jnp.where((q < tot_p[:, None]) & used[:, None], src, zero_xt).astype(jnp.int32)

    r = ROW_GROUP * jnp.arange(TR // ROW_GROUP, dtype=jnp.int32)
    end = (so + run).T
    offset_f = ((page_start * TM)[None, :] + eo - so).T
    step_f = jnp.concatenate([offset_f[1:] - offset_f[:-1], jnp.zeros_like(offset_f[:1])], axis=0)
    srcf = (r[None, :] + offset_f[0][:, None]
            + jnp.sum(jnp.where(end[:, :, None] <= r[None, None, :], step_f[:, :, None], 0), axis=0))
    zero_ys = (n_pages - 1) * TM
    fsrc = jnp.where(r[None, :] < end[-1][:, None], srcf, zero_ys).astype(jnp.int32)
    return page_tables, esrc.reshape(-1) // ROW_GROUP, fsrc.reshape(-1) // ROW_GROUP


def kernel(x, c, positions, w_ada, b_ada, w_in, ret_gn_w, w_pool, pool_scale, w_out, ln1_w, ln1_b, w_group, b_group,
           w_router, b_router, w1, w3, w2, ln2_w, ln2_b):
    B, S, D = x.shape
    T = B * S
    assert w_ada.shape[0] == DEPTH and D == D_MODEL and S % TS == 0

    inv_freq = ROPE_BASE ** (-jnp.arange(0, HEAD_DIM, 2, dtype=jnp.float32) / HEAD_DIM)
    half = HEAD_DIM // 2
    rope = jnp.stack([jnp.concatenate([inv_freq, inv_freq]),
                      jnp.concatenate([-jnp.ones((half,), jnp.float32), jnp.ones((half,), jnp.float32)])])

    nt = T // TS
    n_pages = (2 * T + nt * N_EXPERTS * (PAGE_CHUNK - 1)) // TM + N_EXPERTS + 1

    for l in range(DEPTH):
        ada = _ada(c, w_ada[l], b_ada[l]).reshape(B, 6, D)
        pad = SUBLANES - N_GROUPS
        wrt = jnp.concatenate([w_group[l].T, jnp.zeros((pad, D), jnp.float32), w_router[l].T], axis=0)
        brt = jnp.concatenate([b_group[l], jnp.zeros((pad,), jnp.float32), b_router[l]]).reshape(ROUTE_ROWS, 1)
        x1, xt, route, counts = _mix(
            x, positions, ada, rope, _bf(w_in[l]), _bf(w_out[l]), _bf(w_pool[l]),
            ret_gn_w[l].reshape(1, RET_WIDTH), pool_scale[l].reshape(1, POOL_WIDTH),
            ln1_w[l].reshape(1, D), ln1_b[l].reshape(1, D), wrt, brt)

        cnt = counts[:, :, :, 0].reshape(nt, N_EXPERTS).astype(jnp.int32)
        page_tables, esrc, fsrc = _gather_tables(cnt, n_pages)
        ys = _experts(page_tables, esrc, xt.reshape(nt * TR // ROW_GROUP, 2 * ROW_GROUP, HALF),
                      w1[l].reshape(N_EXPERTS, D, D_EXPERT), w3[l].reshape(N_EXPERTS, D, D_EXPERT),
                      w2[l].reshape(N_EXPERTS, D_EXPERT, D))
        x = _final(fsrc, x1.reshape(T, D), route.reshape(T, LANES), ada,
                   ln2_w[l].reshape(1, D), ln2_b[l].reshape(1, D),
                   ys.reshape(n_pages * TM // ROW_GROUP, 2 * ROW_GROUP, HALF), S).reshape(B, S, D)
    return x
```

```python
import math

import jax
import jax.numpy as jnp
from jax import lax
from jax.experimental import pallas as pl
from jax.experimental.pallas import tpu as pltpu

D_MODEL = 1024
RET_WIDTH = 512
RET_HEADS = 4
HEAD_DIM = 128
POOL_WIDTH = 512
POOL_WINDOWS = (2, 4, 8, 16)
POOL_GROUP_DIM = 128
IN_COLS = 4 * RET_WIDTH + POOL_WIDTH
N_GROUPS = 4
EXPERTS_PER_GROUP = 8
N_EXPERTS = 32
D_EXPERT = 256
DEPTH = 1
ALPHA = (2.0 * DEPTH) ** 0.25
LN_EPS = 1e-5
ROPE_BASE = 10000.0

LANES = 128
SUBLANES = 8
ROW_GROUP = 8
PAGE_CHUNK = 2 * ROW_GROUP
HALF = D_MODEL // 2
POOL_HALO = 16
TS = 256
TM = 512
PAGE_SPLIT_ROWS = 256
TR = -(-(2 * TS + N_EXPERTS * (ROW_GROUP - 1)) // LANES) * LANES
SEQS_PER_STEP = 4
TILES_PER_STEP = 2
ROUTE_ROWS = SUBLANES + N_EXPERTS
FETCH_AHEAD = 2
FETCH_SLOTS = FETCH_AHEAD + 1
PAGE_FETCH_AHEAD = 8
PAGE_FETCH_SLOTS = PAGE_FETCH_AHEAD + 1
FETCH_CHUNK = 64
VMEM_LIMIT = 60 * 1024 * 1024

_LOG_GAMMA = tuple(math.log1p(-(2.0 ** (-5.0 - h))) for h in range(RET_HEADS))
_HI = lax.Precision.HIGHEST


def _ln(x):
    mu = jnp.mean(x, axis=-1, keepdims=True)
    xc = x - mu
    var = jnp.mean(xc * xc, axis=-1, keepdims=True)
    return xc * lax.rsqrt(var + LN_EPS)


def _bf(x):
    return x.astype(jnp.bfloat16)


def _pack_rows(x):
    n = x.shape[0]
    lo = x[:, :HALF].reshape(n // ROW_GROUP, ROW_GROUP, HALF)
    hi = x[:, HALF:].reshape(n // ROW_GROUP, ROW_GROUP, HALF)
    return _bf(jnp.concatenate([lo, hi], axis=1).reshape(2 * n, HALF))


def _unpack_rows(z):
    n = z.shape[0] // 2
    zf = z.astype(jnp.float32).reshape(n // ROW_GROUP, 2 * ROW_GROUP, HALF)
    return _bf(zf[:, :ROW_GROUP, :].reshape(n, HALF)), _bf(zf[:, ROW_GROUP:, :].reshape(n, HALF))


def _ada_kernel(c_ref, w_ref, b_ref, o_ref):
    k = pl.program_id(0)

    @pl.when(k == 0)
    def _init():
        o_ref[...] = jnp.broadcast_to(b_ref[...], o_ref.shape)

    c = c_ref[...]
    ca = c * jax.nn.sigmoid(c)
    o_ref[...] += jnp.dot(ca, w_ref[...], precision=_HI, preferred_element_type=jnp.float32)


def _ada(c, w_ada, b_ada):
    B, D = c.shape
    n = w_ada.shape[1]
    kb = LANES
    c_blocks = c.reshape(B, D // kb, kb).transpose(1, 0, 2)
    return pl.pallas_call(
        _ada_kernel,
        grid=(D // kb,),
        in_specs=[pl.BlockSpec((None, B, kb), lambda k: (k, 0, 0)),
                  pl.BlockSpec((kb, n), lambda k: (k, 0)),
                  pl.BlockSpec((1, n), lambda k: (0, 0))],
        out_specs=pl.BlockSpec((B, n), lambda k: (0, 0)),
        out_shape=jax.ShapeDtypeStruct((B, n), jnp.float32),
        compiler_params=pltpu.CompilerParams(dimension_semantics=("arbitrary",), vmem_limit_bytes=VMEM_LIMIT),
        name="ada",
    )(c_blocks, w_ada, b_ada.reshape(1, n))


def _mix_kernel(x_ref, pos_ref, ada_ref, rope_ref, win_ref, wout_ref, wpool_ref, gnw_ref, pscale_ref,
                ln1w_ref, ln1b_ref, wrt_ref, brt_ref,
                x1_ref, xt_ref, route_ref, cnt_ref,
                state_ref, halo_ref, dmat_ref, qdec_ref, kdec_ref, cat_ref):
    b = pl.program_id(0)
    s = pl.program_id(1)

    @pl.when((b == 0) & (s == 0))
    def _init_tables():
        ri = lax.broadcasted_iota(jnp.int32, (TS, TS), 0)
        ci = lax.broadcasted_iota(jnp.int32, (TS, TS), 1)
        rel = (ri - ci).astype(jnp.float32)
        for h in range(RET_HEADS):
            dmat_ref[h] = jnp.where(rel >= 0.0, jnp.exp(jnp.maximum(rel, 0.0) * _LOG_GAMMA[h]), 0.0)
        row = lax.broadcasted_iota(jnp.int32, (TS, RET_WIDTH), 0).astype(jnp.float32)
        lane = lax.broadcasted_iota(jnp.int32, (TS, RET_WIDTH), 1)
        lg = jnp.full((TS, RET_WIDTH), _LOG_GAMMA[0], jnp.float32)
        for h in range(1, RET_HEADS):
            lg = jnp.where(lane >= h * HEAD_DIM, _LOG_GAMMA[h], lg)
        qdec_ref[...] = jnp.exp((row + 1.0) * lg)
        kdec_ref[...] = jnp.exp((TS - 1.0 - row) * lg)

    @pl.when(s == 0)
    def _init_carries():
        state_ref[...] = jnp.zeros_like(state_ref)
        halo_ref[...] = jnp.zeros_like(halo_ref)

    tiles = [_mix_tile(s, x_ref.at[j], pos_ref.at[j], ada_ref.at[j], rope_ref, win_ref, wout_ref, wpool_ref, gnw_ref,
                       pscale_ref, ln1w_ref, ln1b_ref, wrt_ref, brt_ref,
                       x1_ref.at[j], xt_ref.at[j], route_ref.at[j], cnt_ref.at[j],
                       state_ref.at[j], halo_ref.at[j], dmat_ref, qdec_ref, kdec_ref, cat_ref.at[j])
             for j in range(SEQS_PER_STEP)]
    _round_robin(tiles)


def _mix_tile(s, x_ref, pos_ref, ada_ref, rope_ref, win_ref, wout_ref, wpool_ref, gnw_ref, pscale_ref,
              ln1w_ref, ln1b_ref, wrt_ref, brt_ref,
              x1_ref, xt_ref, route_ref, cnt_ref,
              state_ref, halo_ref, dmat_ref, qdec_ref, kdec_ref, cat_ref):
    ada = ada_ref[...]
    shift1, scale1, gate1 = ada[0:1], ada[1:2], ada[2:3]
    shift2, scale2 = ada[3:4], ada[4:5]

    x = x_ref[...]
    u = _bf(_ln(x) * (1.0 + scale1) + shift1)
    yield

    posf = jnp.broadcast_to(pos_ref[...].astype(jnp.float32), (LANES, TS)).T
    hl = lax.broadcasted_iota(jnp.int32, (TS // 2, HEAD_DIM), 1) < HEAD_DIM // 2
    ang = jnp.where(hl, posf[:TS // 2], posf[TS // 2:]) * rope_ref[0:1, :]
    cos_p, sin_p = jnp.cos(ang), jnp.sin(ang)
    cos_s, sin_s = pltpu.roll(cos_p, HEAD_DIM // 2, 1), pltpu.roll(sin_p, HEAD_DIM // 2, 1)
    cos_t = jnp.concatenate([jnp.where(hl, cos_p, cos_s), jnp.where(hl, cos_s, cos_p)], axis=0)
    sin_t = jnp.concatenate([jnp.where(hl, sin_p, sin_s), jnp.where(hl, sin_s, sin_p)], axis=0) * rope_ref[1:2, :]
    yield

    q = jnp.dot(u, win_ref[:, 0:RET_WIDTH], preferred_element_type=jnp.float32)
    k = jnp.dot(u, win_ref[:, RET_WIDTH:2 * RET_WIDTH], preferred_element_type=jnp.float32)
    v = jnp.dot(u, win_ref[:, 2 * RET_WIDTH:3 * RET_WIDTH], preferred_element_type=jnp.float32)
    g = jnp.dot(u, win_ref[:, 3 * RET_WIDTH:4 * RET_WIDTH], preferred_element_type=jnp.float32)
    p = jnp.dot(u, win_ref[:, 4 * RET_WIDTH:IN_COLS], preferred_element_type=jnp.float32)
    yield

    gnw = gnw_ref[...]
    for h in range(RET_HEADS):
        sl = slice(h * HEAD_DIM, (h + 1) * HEAD_DIM)
        qh, kh, vh = q[:, sl], k[:, sl], v[:, sl]
        qr = qh * cos_t + pltpu.roll(qh, HEAD_DIM // 2, 1) * sin_t
        kr = (kh * cos_t + pltpu.roll(kh, HEAD_DIM // 2, 1) * sin_t) * (HEAD_DIM ** -0.5)
        vb = _bf(vh)
        sc = lax.dot_general(_bf(qr), _bf(kr), (((1,), (1,)), ((), ())), preferred_element_type=jnp.float32)
        intra = jnp.dot(_bf(sc * dmat_ref[h]), vb, preferred_element_type=jnp.float32)
        st = state_ref[h]
        cross = jnp.dot(_bf(qr * qdec_ref[:, sl]), _bf(st), preferred_element_type=jnp.float32)
        kv = lax.dot_general(_bf(kr * kdec_ref[:, sl]), vb, (((0,), (0,)), ((), ())),
                             preferred_element_type=jnp.float32)
        state_ref[h] = st * math.exp(TS * _LOG_GAMMA[h]) + kv
        r = _ln(intra + cross) * gnw[:, sl]
        gh = g[:, sl]
        cat_ref[:, sl] = _bf(gh * jax.nn.sigmoid(gh) * r)
        yield

    pext = jnp.concatenate([halo_ref[...], p], axis=0)
    halo_ref[...] = p[TS - POOL_HALO:, :]
    t_abs = (s * TS + lax.broadcasted_iota(jnp.int32, (TS, 1), 0) + 1).astype(jnp.float32)
    pscale = pscale_ref[...]
    for grp, w in enumerate(POOL_WINDOWS):
        sl = slice(grp * POOL_GROUP_DIM, (grp + 1) * POOL_GROUP_DIM)
        acc = pext[:, sl]
        shift = 1
        while shift < w:
            acc = acc + pltpu.roll(acc, shift, 0)
            shift *= 2
        pooled = acc[POOL_HALO:, :] / jnp.minimum(t_abs, float(w)) - p[:, sl]
        po = jnp.dot(_bf(pooled), wpool_ref[grp], preferred_element_type=jnp.float32) * pscale[:, sl]
        cat_ref[:, RET_WIDTH + grp * POOL_GROUP_DIM:RET_WIDTH + (grp + 1) * POOL_GROUP_DIM] = _bf(po)
    yield

    mix = jnp.dot(cat_ref[...], wout_ref[...], preferred_element_type=jnp.float32)
    yield
    x1 = _ln(ALPHA * x + gate1 * mix) * ln1w_ref[...] + ln1b_ref[...]
    x1_ref[...] = x1
    u2 = _ln(x1) * (1.0 + scale2) + shift2
    yield

    w = wrt_ref[...]
    w_hi = _bf(w)
    w_lo = _bf(w - w_hi.astype(jnp.float32))
    u2_hi = _bf(u2)
    u2_lo = _bf(u2 - u2_hi.astype(jnp.float32))
    nt = (((1,), (1,)), ((), ()))
    logits = (lax.dot_general(w_hi, u2_hi, nt, preferred_element_type=jnp.float32)
              + lax.dot_general(w_hi, u2_lo, nt, preferred_element_type=jnp.float32)
              + lax.dot_general(w_lo, u2_hi, nt, preferred_element_type=jnp.float32)) + brt_ref[...]
    row8 = lax.broadcasted_iota(jnp.int32, (SUBLANES, TS), 0)
    neg = jnp.float32(-jnp.inf)
    gl = jnp.where(row8 < N_GROUPS, logits[0:SUBLANES], neg)
    gmax = jnp.max(gl, axis=0, keepdims=True)
    gidx = jnp.min(jnp.where(gl == gmax, row8, SUBLANES), axis=0, keepdims=True)
    gprob = 1.0 / jnp.sum(jnp.exp(gl - gmax), axis=0, keepdims=True)
    el = logits[SUBLANES:2 * SUBLANES]
    for grp in range(1, N_GROUPS):
        el = jnp.where(gidx == grp, logits[(grp + 1) * SUBLANES:(grp + 2) * SUBLANES], el)
    m1 = jnp.max(el, axis=0, keepdims=True)
    j1 = jnp.min(jnp.where(el == m1, row8, SUBLANES), axis=0, keepdims=True)
    el2 = jnp.where(row8 == j1, neg, el)
    m2 = jnp.max(el2, axis=0, keepdims=True)
    j2 = jnp.min(jnp.where(el2 == m2, row8, SUBLANES), axis=0, keepdims=True)
    e21 = jnp.exp(m2 - m1)
    den = 1.0 / (1.0 + e21)
    cw1 = gprob * den
    cw2 = gprob * e21 * den
    yield

    erow = lax.broadcasted_iota(jnp.int32, (N_EXPERTS, TS), 0)
    oh1 = erow == gidx * EXPERTS_PER_GROUP + j1
    oh2 = erow == gidx * EXPERTS_PER_GROUP + j2
    oh = jnp.where(oh1 | oh2, 1.0, 0.0)
    cnt = jnp.broadcast_to(jnp.sum(oh, axis=1, keepdims=True), (N_EXPERTS, LANES))
    run = jnp.floor((cnt + (ROW_GROUP - 1.0)) * (1.0 / ROW_GROUP)) * ROW_GROUP
    erow_l = lax.broadcasted_iota(jnp.int32, (N_EXPERTS, LANES), 0)
    run_end = run
    shift = 1
    while shift < N_EXPERTS:
        run_end = run_end + jnp.where(erow_l >= shift, pltpu.roll(run_end, shift, 0), 0.0)
        shift *= 2
    run_start = (run_end - run)[:, 0:1]
    ri = lax.broadcasted_iota(jnp.int32, (TS, TS), 0)
    ci = lax.broadcasted_iota(jnp.int32, (TS, TS), 1)
    earlier = _bf(jnp.where(ri < ci, 1.0, 0.0))
    before = jnp.dot(_bf(oh), earlier, preferred_element_type=jnp.float32) + run_start
    pos1 = jnp.sum(jnp.where(oh1, before, 0.0), axis=0, keepdims=True)
    pos2 = jnp.sum(jnp.where(oh2, before, 0.0), axis=0, keepdims=True)
    rr = lax.broadcasted_iota(jnp.int32, (TR, TS), 0).astype(jnp.float32)
    perm = _bf(jnp.where((rr == pos1) | (rr == pos2), 1.0, 0.0))
    yield
    xt_ref[...] = _pack_rows(jnp.dot(perm, u2_hi, preferred_element_type=jnp.float32))
    cnt_ref[...] = cnt

    rowl = lax.broadcasted_iota(jnp.int32, (LANES, TS), 0)
    rec = jnp.where(rowl == 0, pos1, 0.0)
    rec = jnp.where(rowl == 1, pos2, rec)
    rec = jnp.where(rowl == 2, cw1, rec)
    rec = jnp.where(rowl == 3, cw2, rec)
    route_ref[...] = rec.T


def _mix(x, positions, ada, rope, win, wout, wpool, gnw, pscale, ln1w, ln1b, wrt, brt):
    B, S, D = x.shape
    ns = S // TS
    assert B % SEQS_PER_STEP == 0
    P = SEQS_PER_STEP
    const2 = lambda b, s: (0, 0)
    const3 = lambda b, s: (0, 0, 0)
    tile = lambda b, s: (b, s, 0)
    flat = lambda b, s: (b, s, 0, 0)
    return pl.pallas_call(
        _mix_kernel,
        grid=(B // P, ns),
        in_specs=[
            pl.BlockSpec((P, TS, D), tile),
            pl.BlockSpec((P, None, 1, TS), flat),
            pl.BlockSpec((P, 6, D), lambda b, s: (b, 0, 0)),
            pl.BlockSpec((2, LANES), const2),
            pl.BlockSpec((D, IN_COLS), const2, pipeline_mode=pl.Buffered(1)),
            pl.BlockSpec((D, D), const2, pipeline_mode=pl.Buffered(1)),
            pl.BlockSpec((len(POOL_WINDOWS), POOL_GROUP_DIM, POOL_GROUP_DIM), const3),
            pl.BlockSpec((1, RET_WIDTH), const2),
            pl.BlockSpec((1, POOL_WIDTH), const2),
            pl.BlockSpec((1, D), const2),
            pl.BlockSpec((1, D), const2),
            pl.BlockSpec((ROUTE_ROWS, D), const2),
            pl.BlockSpec((ROUTE_ROWS, 1), const2),
        ],
        out_specs=[
            pl.BlockSpec((P, TS, D), tile),
            pl.BlockSpec((P, None, 2 * TR, HALF), flat),
            pl.BlockSpec((P, TS, LANES), tile),
            pl.BlockSpec((P, None, N_EXPERTS, LANES), flat),
        ],
        out_shape=[
            jax.ShapeDtypeStruct((B, S, D), jnp.float32),
            jax.ShapeDtypeStruct((B, ns, 2 * TR, HALF), jnp.bfloat16),
            jax.ShapeDtypeStruct((B, S, LANES), jnp.float32),
            jax.ShapeDtypeStruct((B, ns, N_EXPERTS, LANES), jnp.float32),
        ],
        scratch_shapes=[
            pltpu.VMEM((P, RET_HEADS, HEAD_DIM, HEAD_DIM), jnp.float32),
            pltpu.VMEM((P, POOL_HALO, POOL_WIDTH), jnp.float32),
            pltpu.VMEM((RET_HEADS, TS, TS), jnp.float32),
            pltpu.VMEM((TS, RET_WIDTH), jnp.float32),
            pltpu.VMEM((TS, RET_WIDTH), jnp.float32),
            pltpu.VMEM((P, TS, D), jnp.bfloat16),
        ],
        compiler_params=pltpu.CompilerParams(dimension_semantics=("arbitrary", "arbitrary"),
                                             vmem_limit_bytes=VMEM_LIMIT),
        name="mix",
    )(x, positions.reshape(B, ns, 1, TS), ada, rope, win, wout, wpool, gnw, pscale, ln1w, ln1b, wrt, brt)


def _fetch_groups(src_ref, group_of, dst_ref, sem):
    for u in range(dst_ref.shape[0]):
        pltpu.make_async_copy(src_ref.at[group_of(u)], dst_ref.at[u], sem).start()


def _fetch_groups_task(src_ref, group_of, dst_ref, sem, chunk):
    for u in range(dst_ref.shape[0]):
        pltpu.make_async_copy(src_ref.at[group_of(u)], dst_ref.at[u], sem).start()
        if u % chunk == chunk - 1:
            yield


def _wait_fetch(src_ref, dst_ref, sem):
    pltpu.make_async_copy(src_ref.at[pl.ds(0, dst_ref.shape[0])], dst_ref, sem).wait()


def _expert_rows(x_ref, w1_ref, w3_ref, w2_ref, y_ref):
    n = PAGE_SPLIT_ROWS
    xl, xh = _unpack_rows(x_ref[...].reshape(2 * n, HALF))
    yield
    a = (jnp.dot(xl, w1_ref[:HALF, :], preferred_element_type=jnp.float32)
         + jnp.dot(xh, w1_ref[HALF:, :], preferred_element_type=jnp.float32))
    c = (jnp.dot(xl, w3_ref[:HALF, :], preferred_element_type=jnp.float32)
         + jnp.dot(xh, w3_ref[HALF:, :], preferred_element_type=jnp.float32))
    yield
    h = _bf(a * jax.nn.sigmoid(a) * c)
    yield
    y = jnp.dot(h, w2_ref[...], preferred_element_type=jnp.float32)
    yield
    y_ref[...] = _pack_rows(y)


def _round_robin(tasks):
    while tasks:
        tasks = [t for t in tasks if next(t, True) is None]


def _expert_kernel(used_ref, pe_ref, pv_ref, first_ref, wslot_ref, next_ref, src_ref, xt_ref, w1_hbm, w3_hbm, w2_hbm,
                   ys_ref, xbuf, ybuf, zbuf, w1_buf, w3_buf, w2_buf, w1_bf, w3_bf, w2_bf, sem, osem, wsem, zsem):
    n_used = used_ref[0]
    n_pages = ys_ref.shape[0] // (2 * TM)
    chunks = TM // PAGE_CHUNK
    per_chunk = PAGE_CHUNK // ROW_GROUP

    def fetch(page):
        @pl.when(page < n_used)
        def _start():
            slot = page % PAGE_FETCH_SLOTS
            for u in range(chunks):
                pltpu.make_async_copy(xt_ref.at[pl.ds(src_ref[page * chunks + u], per_chunk)],
                                      xbuf.at[slot, pl.ds(u * per_chunk, per_chunk)], sem.at[slot]).start()

    def weight_copies(expert, ws):
        return [pltpu.make_async_copy(hbm.at[expert], buf.at[ws], wsem.at[ws])
                for hbm, buf in ((w1_hbm, w1_buf), (w3_hbm, w3_buf), (w2_hbm, w2_buf))]

    def page_out(page, oslot):
        return pltpu.make_async_copy(ybuf.at[oslot], ys_ref.at[pl.ds(pl.multiple_of(page * 2 * TM, 2 * TM), 2 * TM)],
                                     osem.at[oslot])

    for k in range(PAGE_FETCH_AHEAD):
        fetch(k)

    zbuf[...] = jnp.zeros_like(zbuf)

    def zero_page(page):
        return pltpu.make_async_copy(zbuf, ys_ref.at[pl.ds(pl.multiple_of(page * 2 * TM, 2 * TM), 2 * TM)], zsem)

    def zero_start(page, carry):
        zero_page(page).start()
        return carry

    lax.fori_loop(n_used, n_pages, zero_start, 0)

    @pl.when(n_used > 0)
    def _first_weights():
        for c in weight_copies(pe_ref[0], wslot_ref[0]):
            c.start()

    def page_body(g, carry):
        fetch(g + PAGE_FETCH_AHEAD)
        ws = wslot_ref[g]

        @pl.when(first_ref[g] == 1)
        def _new_expert():
            for c in weight_copies(pe_ref[g], ws):
                c.wait()
            w1_bf[...] = _bf(w1_buf[ws])
            w3_bf[...] = _bf(w3_buf[ws])
            w2_bf[...] = _bf(w2_buf[ws])

            @pl.when(next_ref[g] >= 0)
            def _stream_next():
                for c in weight_copies(next_ref[g], 1 - ws):
                    c.start()

        slot = g % PAGE_FETCH_SLOTS
        oslot = g % 2
        _wait_fetch(xt_ref, xbuf.at[slot], sem.at[slot])

        @pl.when(g >= 2)
        def _reuse_out_buffer():
            page_out(g - 2, oslot).wait()

        per = PAGE_SPLIT_ROWS // ROW_GROUP
        parts = TM // PAGE_SPLIT_ROWS
        part_rows = lambda k: pl.ds(k * 2 * PAGE_SPLIT_ROWS, 2 * PAGE_SPLIT_ROWS)
        part = lambda k: _expert_rows(xbuf.at[slot, pl.ds(k * per, per)], w1_bf, w3_bf, w2_bf,
                                      ybuf.at[oslot, part_rows(k)])
        n_parts = (pv_ref[g] + PAGE_SPLIT_ROWS - 1) // PAGE_SPLIT_ROWS
        for n in range(1, parts + 1):
            @pl.when(n_parts == n)
            def _parts(n=n):
                _round_robin([part(k) for k in range(n)])
                for k in range(n, parts):
                    ybuf[oslot, part_rows(k)] = jnp.zeros((2 * PAGE_SPLIT_ROWS, HALF), ybuf.dtype)
        page_out(g, oslot).start()
        return carry

    lax.fori_loop(0, n_used, page_body, 0)

    for back in (2, 1):
        @pl.when(n_used >= back)
        def _drain(back=back):
            page_out(n_used - back, (n_used - back) % 2).wait()

    def zero_wait(page, carry):
        zero_page(page).wait()
        return carry

    lax.fori_loop(n_used, n_pages, zero_wait, 0)


def _experts(page_tables, esrc, xt, w1, w3, w2):
    D = D_MODEL
    n_pages = page_tables[1].shape[0]
    assert TM % PAGE_SPLIT_ROWS == 0
    grid_spec = pltpu.PrefetchScalarGridSpec(
        num_scalar_prefetch=len(page_tables) + 1,
        grid=(1,),
        in_specs=[pl.BlockSpec(memory_space=pl.ANY)] * 4,
        out_specs=pl.BlockSpec(memory_space=pl.ANY),
        scratch_shapes=[pltpu.VMEM((PAGE_FETCH_SLOTS, TM // ROW_GROUP, 2 * ROW_GROUP, HALF), jnp.bfloat16),
                        pltpu.VMEM((2, 2 * TM, HALF), jnp.bfloat16),
                        pltpu.VMEM((2 * TM, HALF), jnp.bfloat16),
                        pltpu.VMEM((2, D, D_EXPERT), jnp.float32),
                        pltpu.VMEM((2, D, D_EXPERT), jnp.float32),
                        pltpu.VMEM((2, D_EXPERT, D), jnp.float32),
                        pltpu.VMEM((D, D_EXPERT), jnp.bfloat16),
                        pltpu.VMEM((D, D_EXPERT), jnp.bfloat16),
                        pltpu.VMEM((D_EXPERT, D), jnp.bfloat16),
                        pltpu.SemaphoreType.DMA((PAGE_FETCH_SLOTS,)),
                        pltpu.SemaphoreType.DMA((2,)),
                        pltpu.SemaphoreType.DMA((2,)),
                        pltpu.SemaphoreType.DMA(())],
    )
    return pl.pallas_call(
        _expert_kernel,
        grid_spec=grid_spec,
        out_shape=jax.ShapeDtypeStruct((n_pages * 2 * TM, HALF), jnp.bfloat16),
        compiler_params=pltpu.CompilerParams(dimension_semantics=("arbitrary",), vmem_limit_bytes=VMEM_LIMIT),
        name="experts",
    )(*page_tables, esrc, xt, w1, w3, w2)


def _final_tile(y_ref, x1_ref, route_ref, ada_ref, lnw_ref, lnb_ref, o_ref):
    route = route_ref[...]
    pos1, pos2, cw1, cw2 = route[:, 0:1], route[:, 1:2], route[:, 2:3], route[:, 3:4]
    col = lax.broadcasted_iota(jnp.int32, (TS, TR), 1).astype(jnp.float32)
    wmat = _bf(jnp.where(col == pos1, cw1, 0.0) + jnp.where(col == pos2, cw2, 0.0))
    yield
    yl, yh = _unpack_rows(y_ref[...].reshape(2 * TR, HALF))
    yield
    y = jnp.concatenate([jnp.dot(wmat, yl, preferred_element_type=jnp.float32),
                         jnp.dot(wmat, yh, preferred_element_type=jnp.float32)], axis=-1)
    yield
    gate2 = ada_ref[5:6, :]
    o_ref[...] = _ln(ALPHA * x1_ref[...] + gate2 * y) * lnw_ref[...] + lnb_ref[...]


def _final_kernel(src_ref, x1_ref, route_ref, ada_ref, lnw_ref, lnb_ref, ys_ref, o_ref, ybuf, sem):
    P = TILES_PER_STEP
    i = pl.program_id(0)
    last = pl.num_programs(0) - 1
    per = TR // ROW_GROUP
    groups = P * per

    def fetch(step, dst_slot):
        _fetch_groups(ys_ref, lambda u: src_ref[step * groups + u], ybuf.at[dst_slot], sem.at[dst_slot])

    @pl.when(i == 0)
    def _prime():
        for k in range(FETCH_AHEAD):
            fetch(k, k)

    slot = i % FETCH_SLOTS
    _wait_fetch(ys_ref, ybuf.at[slot], sem.at[slot])

    ahead = jnp.minimum(i + FETCH_AHEAD, last)
    ahead_slot = (i + FETCH_AHEAD) % FETCH_SLOTS
    rows = lambda k: pl.ds(k * TS, TS)
    _round_robin([_final_tile(ybuf.at[slot, pl.ds(k * per, per)], x1_ref.at[rows(k)], route_ref.at[rows(k)], ada_ref,
                              lnw_ref, lnb_ref, o_ref.at[rows(k)]) for k in range(P)]
                 + [_fetch_groups_task(ys_ref, lambda u: src_ref[ahead * groups + u], ybuf.at[ahead_slot],
                                       sem.at[ahead_slot], FETCH_CHUNK)])

    @pl.when(i == last)
    def _drain():
        for k in range(1, FETCH_SLOTS):
            other = (i + k) % FETCH_SLOTS
            _wait_fetch(ys_ref, ybuf.at[other], sem.at[other])


def _final(fsrc, x1, route, ada, lnw, lnb, ys, seq_len):
    T, D = x1.shape
    P = TILES_PER_STEP
    rows = P * TS
    assert seq_len % rows == 0
    steps_per_seq = seq_len // rows
    n_steps = T // rows
    assert n_steps > FETCH_AHEAD
    grid_spec = pltpu.PrefetchScalarGridSpec(
        num_scalar_prefetch=1,
        grid=(n_steps,),
        in_specs=[pl.BlockSpec((rows, D), lambda i, src: (i, 0)),
                  pl.BlockSpec((rows, LANES), lambda i, src: (i, 0)),
                  pl.BlockSpec((None, 6, D), lambda i, src: (i // steps_per_seq, 0, 0)),
                  pl.BlockSpec((1, D), lambda i, src: (0, 0)),
                  pl.BlockSpec((1, D), lambda i, src: (0, 0)),
                  pl.BlockSpec(memory_space=pl.ANY)],
        out_specs=pl.BlockSpec((rows, D), lambda i, src: (i, 0)),
        scratch_shapes=[pltpu.VMEM((FETCH_SLOTS, P * (TR // ROW_GROUP), 2 * ROW_GROUP, HALF), jnp.bfloat16),
                        pltpu.SemaphoreType.DMA((FETCH_SLOTS,))],
    )
    return pl.pallas_call(
        _final_kernel,
        grid_spec=grid_spec,
        out_shape=jax.ShapeDtypeStruct((T, D), jnp.float32),
        compiler_params=pltpu.CompilerParams(dimension_semantics=("arbitrary",), vmem_limit_bytes=VMEM_LIMIT),
        name="final",
    )(fsrc, x1, route, ada, lnw, lnb, ys)


def _gather_tables(cnt, n_pages):
    nt = cnt.shape[0]
    run = (cnt + ROW_GROUP - 1) // ROW_GROUP * ROW_GROUP
    so = jnp.cumsum(run, axis=1) - run
    run_e = (cnt + PAGE_CHUNK - 1) // PAGE_CHUNK * PAGE_CHUNK
    eo = jnp.cumsum(run_e, axis=0) - run_e
    tot = jnp.sum(run_e, axis=0)
    pages_e = (tot + TM - 1) // TM
    page_end = jnp.cumsum(pages_e)
    page_start = page_end - pages_e
    g = jnp.arange(n_pages, dtype=jnp.int32)
    pe = jnp.minimum(jnp.sum(g[:, None] >= page_end[None, :], axis=1), N_EXPERTS - 1).astype(jnp.int32)
    used = g < page_end[-1]
    owner = (g[None, :] >= page_start[:, None]) & (g[None, :] < page_end[:, None])
    of_page = lambda a: jnp.sum(jnp.where(owner, a[:, None], 0), axis=0)
    of_page2 = lambda a: jnp.sum(jnp.where(owner[:, None, :], a.T[:, :, None], 0), axis=0)
    pj = g - of_page(page_start)
    tot_p = of_page(tot)
    pv = jnp.where(used, jnp.clip(tot_p - pj * TM, 0, TM), 0).astype(jnp.int32)
    experts = jnp.arange(N_EXPERTS, dtype=jnp.int32)
    has_pages = pages_e > 0
    wslot_e = (jnp.cumsum(has_pages) - 1) % 2
    later = (experts[None, :] > experts[:, None]) & has_pages[None, :]
    next_e = jnp.min(jnp.where(later, experts[None, :], N_EXPERTS), axis=1)
    next_e = jnp.where(next_e < N_EXPERTS, next_e, -1)
    first = (used & (pj == 0)).astype(jnp.int32)
    page_tables = (page_end[-1:].astype(jnp.int32), pe, pv, first, of_page(wslot_e).astype(jnp.int32),
                   jnp.where(used, of_page(next_e), -1).astype(jnp.int32))

    q = (pj * TM)[:, None] + PAGE_CHUNK * jnp.arange(TM // PAGE_CHUNK, dtype=jnp.int32)[None, :]
    tiles = jnp.arange(nt, dtype=jnp.int32)
    offset = of_page2(tiles[:, None] * TR + so - eo)
    step = jnp.concatenate([offset[1:] - offset[:-1], jnp.zeros_like(offset[:1])], axis=0)
    run_end = of_page2(eo + run_e)
    src = q + offset[0][:, None] + jnp.sum(jnp.where(run_end[:, :, None] <= q[None], step[:, :, None], 0), axis=0)
    zero_xt = TR - PAGE_CHUNK
    esrc = jnp.where((q < tot_p[:, None]) & used[:, None], src, zero_xt).astype(jnp.int32)

    r = ROW_GROUP * jnp.arange(TR // ROW_GROUP, dtype=jnp.int32)
    end = (so + run).T
    offset_f = ((page_start * TM)[None, :] + eo - so).T
    step_f = jnp.concatenate([offset_f[1:] - offset_f[:-1], jnp.zeros_like(offset_f[:1])], axis=0)
    srcf = (r[None, :] + offset_f[0][:, None]
            + jnp.sum(jnp.where(end[:, :, None] <= r[None, None, :], step_f[:, :, None], 0), axis=0))
    zero_ys = (n_pages - 1) * TM
    fsrc = jnp.where(r[None, :] < end[-1][:, None], srcf, zero_ys).astype(jnp.int32)
    return page_tables, esrc.reshape(-1) // ROW_GROUP, fsrc.reshape(-1) // ROW_GROUP


def kernel(x, c, positions, w_ada, b_ada, w_in, ret_gn_w, w_pool, pool_scale, w_out, ln1_w, ln1_b, w_group, b_group,
           w_router, b_router, w1, w3, w2, ln2_w, ln2_b):
    B, S, D = x.shape
    T = B * S
    assert w_ada.shape[0] == DEPTH and D == D_MODEL and S % TS == 0

    inv_freq = ROPE_BASE ** (-jnp.arange(0, HEAD_DIM, 2, dtype=jnp.float32) / HEAD_DIM)
    half = HEAD_DIM // 2
    rope = jnp.stack([jnp.concatenate([inv_freq, inv_freq]),
                      jnp.concatenate([-jnp.ones((half,), jnp.float32), jnp.ones((half,), jnp.float32)])])

    nt = T // TS
    n_pages = (2 * T + nt * N_EXPERTS * (PAGE_CHUNK - 1)) // TM + N_EXPERTS + 1

    for l in range(DEPTH):
        ada = _ada(c, w_ada[l], b_ada[l]).reshape(B, 6, D)
        pad = SUBLANES - N_GROUPS
        wrt = jnp.concatenate([w_group[l].T, jnp.zeros((pad, D), jnp.float32), w_router[l].T], axis=0)
        brt = jnp.concatenate([b_group[l], jnp.zeros((pad,), jnp.float32), b_router[l]]).reshape(ROUTE_ROWS, 1)
        x1, xt, route, counts = _mix(
            x, positions, ada, rope, _bf(w_in[l]), _bf(w_out[l]), _bf(w_pool[l]),
            ret_gn_w[l].reshape(1, RET_WIDTH), pool_scale[l].reshape(1, POOL_WIDTH),
            ln1_w[l].reshape(1, D), ln1_b[l].reshape(1, D), wrt, brt)

        cnt = counts[:, :, :, 0].reshape(nt, N_EXPERTS).astype(jnp.int32)
        page_tables, esrc, fsrc = _gather_tables(cnt, n_pages)
        ys = _experts(page_tables, esrc, xt.reshape(nt * TR // ROW_GROUP, 2 * ROW_GROUP, HALF),
                      w1[l].reshape(N_EXPERTS, D, D_EXPERT), w3[l].reshape(N_EXPERTS, D, D_EXPERT),
                      w2[l].reshape(N_EXPERTS, D_EXPERT, D))
        x = _final(fsrc, x1.reshape(T, D), route.reshape(T, LANES), ada,
                   ln2_w[l].reshape(1, D), ln2_b[l].reshape(1, D),
                   ys.reshape(n_pages * TM // ROW_GROUP, 2 * ROW_GROUP, HALF), S).reshape(B, S, D)
    return x
```

```python
import math

import jax
import jax.numpy as jnp
from jax import lax
from jax.experimental import pallas as pl
from jax.experimental.pallas import tpu as pltpu

D_MODEL = 1024
RET_WIDTH = 512
RET_HEADS = 4
HEAD_DIM = 128
POOL_WIDTH = 512
POOL_WINDOWS = (2, 4, 8, 16)
POOL_GROUP_DIM = 128
IN_COLS = 4 * RET_WIDTH + POOL_WIDTH
N_GROUPS = 4
EXPERTS_PER_GROUP = 8
N_EXPERTS = 32
D_EXPERT = 256
DEPTH = 1
ALPHA = (2.0 * DEPTH) ** 0.25
LN_EPS = 1e-5
ROPE_BASE = 10000.0

LANES = 128
SUBLANES = 8
ROW_GROUP = 8
PAGE_CHUNK = 2 * ROW_GROUP
HALF = D_MODEL // 2
POOL_HALO = 16
TS = 256
TM = 512
PAGE_SPLIT_ROWS = 256
TR = -(-(2 * TS + N_EXPERTS * (ROW_GROUP - 1)) // LANES) * LANES
SEQS_PER_STEP = 4
TILES_PER_STEP = 2
ROUTE_ROWS = SUBLANES + N_EXPERTS
FETCH_AHEAD = 2
FETCH_SLOTS = FETCH_AHEAD + 1
PAGE_FETCH_AHEAD = 4
PAGE_FETCH_SLOTS = PAGE_FETCH_AHEAD + 1
FETCH_CHUNK = 64
VMEM_LIMIT = 60 * 1024 * 1024

_LOG_GAMMA = tuple(math.log1p(-(2.0 ** (-5.0 - h))) for h in range(RET_HEADS))
_HI = lax.Precision.HIGHEST


def _ln(x):
    mu = jnp.mean(x, axis=-1, keepdims=True)
    xc = x - mu
    var = jnp.mean(xc * xc, axis=-1, keepdims=True)
    return xc * lax.rsqrt(var + LN_EPS)


def _bf(x):
    return x.astype(jnp.bfloat16)


def _pack_rows(x):
    n = x.shape[0]
    lo = x[:, :HALF].reshape(n // ROW_GROUP, ROW_GROUP, HALF)
    hi = x[:, HALF:].reshape(n // ROW_GROUP, ROW_GROUP, HALF)
    return _bf(jnp.concatenate([lo, hi], axis=1).reshape(2 * n, HALF))


def _unpack_rows(z):
    n = z.shape[0] // 2
    zf = z.astype(jnp.float32).reshape(n // ROW_GROUP, 2 * ROW_GROUP, HALF)
    return _bf(zf[:, :ROW_GROUP, :].reshape(n, HALF)), _bf(zf[:, ROW_GROUP:, :].reshape(n, HALF))


def _ada_kernel(c_ref, w_ref, b_ref, o_ref):
    k = pl.program_id(0)

    @pl.when(k == 0)
    def _init():
        o_ref[...] = jnp.broadcast_to(b_ref[...], o_ref.shape)

    c = c_ref[...]
    ca = c * jax.nn.sigmoid(c)
    o_ref[...] += jnp.dot(ca, w_ref[...], precision=_HI, preferred_element_type=jnp.float32)


def _ada(c, w_ada, b_ada):
    B, D = c.shape
    n = w_ada.shape[1]
    kb = LANES
    c_blocks = c.reshape(B, D // kb, kb).transpose(1, 0, 2)
    return pl.pallas_call(
        _ada_kernel,
        grid=(D // kb,),
        in_specs=[pl.BlockSpec((None, B, kb), lambda k: (k, 0, 0)),
                  pl.BlockSpec((kb, n), lambda k: (k, 0)),
                  pl.BlockSpec((1, n), lambda k: (0, 0))],
        out_specs=pl.BlockSpec((B, n), lambda k: (0, 0)),
        out_shape=jax.ShapeDtypeStruct((B, n), jnp.float32),
        compiler_params=pltpu.CompilerParams(dimension_semantics=("arbitrary",), vmem_limit_bytes=VMEM_LIMIT),
        name="ada",
    )(c_blocks, w_ada, b_ada.reshape(1, n))


def _mix_kernel(x_ref, pos_ref, ada_ref, rope_ref, win_ref, wout_ref, wpool_ref, gnw_ref, pscale_ref,
                ln1w_ref, ln1b_ref, wrt_ref, brt_ref,
                x1_ref, xt_ref, route_ref, cnt_ref,
                state_ref, halo_ref, dmat_ref, qdec_ref, kdec_ref, cat_ref):
    b = pl.program_id(0)
    s = pl.program_id(1)

    @pl.when((b == 0) & (s == 0))
    def _init_tables():
        ri = lax.broadcasted_iota(jnp.int32, (TS, TS), 0)
        ci = lax.broadcasted_iota(jnp.int32, (TS, TS), 1)
        rel = (ri - ci).astype(jnp.float32)
        for h in range(RET_HEADS):
            dmat_ref[h] = jnp.where(rel >= 0.0, jnp.exp(jnp.maximum(rel, 0.0) * _LOG_GAMMA[h]), 0.0)
        row = lax.broadcasted_iota(jnp.int32, (TS, RET_WIDTH), 0).astype(jnp.float32)
        lane = lax.broadcasted_iota(jnp.int32, (TS, RET_WIDTH), 1)
        lg = jnp.full((TS, RET_WIDTH), _LOG_GAMMA[0], jnp.float32)
        for h in range(1, RET_HEADS):
            lg = jnp.where(lane >= h * HEAD_DIM, _LOG_GAMMA[h], lg)
        qdec_ref[...] = jnp.exp((row + 1.0) * lg)
        kdec_ref[...] = jnp.exp((TS - 1.0 - row) * lg)

    @pl.when(s == 0)
    def _init_carries():
        state_ref[...] = jnp.zeros_like(state_ref)
        halo_ref[...] = jnp.zeros_like(halo_ref)

    tiles = [_mix_tile(s, x_ref.at[j], pos_ref.at[j], ada_ref.at[j], rope_ref, win_ref, wout_ref, wpool_ref, gnw_ref,
                       pscale_ref, ln1w_ref, ln1b_ref, wrt_ref, brt_ref,
                       x1_ref.at[j], xt_ref.at[j], route_ref.at[j], cnt_ref.at[j],
                       state_ref.at[j], halo_ref.at[j], dmat_ref, qdec_ref, kdec_ref, cat_ref.at[j])
             for j in range(SEQS_PER_STEP)]
    _round_robin(tiles)


def _mix_tile(s, x_ref, pos_ref, ada_ref, rope_ref, win_ref, wout_ref, wpool_ref, gnw_ref, pscale_ref,
              ln1w_ref, ln1b_ref, wrt_ref, brt_ref,
              x1_ref, xt_ref, route_ref, cnt_ref,
              state_ref, halo_ref, dmat_ref, qdec_ref, kdec_ref, cat_ref):
    ada = ada_ref[...]
    shift1, scale1, gate1 = ada[0:1], ada[1:2], ada[2:3]
    shift2, scale2 = ada[3:4], ada[4:5]

    x = x_ref[...]
    u = _bf(_ln(x) * (1.0 + scale1) + shift1)
    yield

    posf = jnp.broadcast_to(pos_ref[...].astype(jnp.float32), (LANES, TS)).T
    hl = lax.broadcasted_iota(jnp.int32, (TS // 2, HEAD_DIM), 1) < HEAD_DIM // 2
    ang = jnp.where(hl, posf[:TS // 2], posf[TS // 2:]) * rope_ref[0:1, :]
    cos_p, sin_p = jnp.cos(ang), jnp.sin(ang)
    cos_s, sin_s = pltpu.roll(cos_p, HEAD_DIM // 2, 1), pltpu.roll(sin_p, HEAD_DIM // 2, 1)
    cos_t = jnp.concatenate([jnp.where(hl, cos_p, cos_s), jnp.where(hl, cos_s, cos_p)], axis=0)
    sin_t = jnp.concatenate([jnp.where(hl, sin_p, sin_s), jnp.where(hl, sin_s, sin_p)], axis=0) * rope_ref[1:2, :]
    yield

    q = jnp.dot(u, win_ref[:, 0:RET_WIDTH], preferred_element_type=jnp.float32)
    k = jnp.dot(u, win_ref[:, RET_WIDTH:2 * RET_WIDTH], preferred_element_type=jnp.float32)
    v = jnp.dot(u, win_ref[:, 2 * RET_WIDTH:3 * RET_WIDTH], preferred_element_type=jnp.float32)
    g = jnp.dot(u, win_ref[:, 3 * RET_WIDTH:4 * RET_WIDTH], preferred_element_type=jnp.float32)
    p = jnp.dot(u, win_ref[:, 4 * RET_WIDTH:IN_COLS], preferred_element_type=jnp.float32)
    yield

    gnw = gnw_ref[...]
    for h in range(RET_HEADS):
        sl = slice(h * HEAD_DIM, (h + 1) * HEAD_DIM)
        qh, kh, vh = q[:, sl], k[:, sl], v[:, sl]
        qr = qh * cos_t + pltpu.roll(qh, HEAD_DIM // 2, 1) * sin_t
        kr = (kh * cos_t + pltpu.roll(kh, HEAD_DIM // 2, 1) * sin_t) * (HEAD_DIM ** -0.5)
        vb = _bf(vh)
        sc = lax.dot_general(_bf(qr), _bf(kr), (((1,), (1,)), ((), ())), preferred_element_type=jnp.float32)
        intra = jnp.dot(_bf(sc * dmat_ref[h]), vb, preferred_element_type=jnp.float32)
        st = state_ref[h]
        cross = jnp.dot(_bf(qr * qdec_ref[:, sl]), _bf(st), preferred_element_type=jnp.float32)
        kv = lax.dot_general(_bf(kr * kdec_ref[:, sl]), vb, (((0,), (0,)), ((), ())),
                             preferred_element_type=jnp.float32)
        state_ref[h] = st * math.exp(TS * _LOG_GAMMA[h]) + kv
        r = _ln(intra + cross) * gnw[:, sl]
        gh = g[:, sl]
        cat_ref[:, sl] = _bf(gh * jax.nn.sigmoid(gh) * r)
        yield

    pext = jnp.concatenate([halo_ref[...], p], axis=0)
    halo_ref[...] = p[TS - POOL_HALO:, :]
    t_abs = (s * TS + lax.broadcasted_iota(jnp.int32, (TS, 1), 0) + 1).astype(jnp.float32)
    pscale = pscale_ref[...]
    for grp, w in enumerate(POOL_WINDOWS):
        sl = slice(grp * POOL_GROUP_DIM, (grp + 1) * POOL_GROUP_DIM)
        acc = pext[:, sl]
        shift = 1
        while shift < w:
            acc = acc + pltpu.roll(acc, shift, 0)
            shift *= 2
        pooled = acc[POOL_HALO:, :] / jnp.minimum(t_abs, float(w)) - p[:, sl]
        po = jnp.dot(_bf(pooled), wpool_ref[grp], preferred_element_type=jnp.float32) * pscale[:, sl]
        cat_ref[:, RET_WIDTH + grp * POOL_GROUP_DIM:RET_WIDTH + (grp + 1) * POOL_GROUP_DIM] = _bf(po)
    yield

    mix = jnp.dot(cat_ref[...], wout_ref[...], preferred_element_type=jnp.float32)
    yield
    x1 = _ln(ALPHA * x + gate1 * mix) * ln1w_ref[...] + ln1b_ref[...]
    x1_ref[...] = x1
    u2 = _ln(x1) * (1.0 + scale2) + shift2
    yield

    w = wrt_ref[...]
    w_hi = _bf(w)
    w_lo = _bf(w - w_hi.astype(jnp.float32))
    u2_hi = _bf(u2)
    u2_lo = _bf(u2 - u2_hi.astype(jnp.float32))
    nt = (((1,), (1,)), ((), ()))
    logits = (lax.dot_general(w_hi, u2_hi, nt, preferred_element_type=jnp.float32)
              + lax.dot_general(w_hi, u2_lo, nt, preferred_element_type=jnp.float32)
              + lax.dot_general(w_lo, u2_hi, nt, preferred_element_type=jnp.float32)) + brt_ref[...]
    row8 = lax.broadcasted_iota(jnp.int32, (SUBLANES, TS), 0)
    neg = jnp.float32(-jnp.inf)
    gl = jnp.where(row8 < N_GROUPS, logits[0:SUBLANES], neg)
    gmax = jnp.max(gl, axis=0, keepdims=True)
    gidx = jnp.min(jnp.where(gl == gmax, row8, SUBLANES), axis=0, keepdims=True)
    gprob = 1.0 / jnp.sum(jnp.exp(gl - gmax), axis=0, keepdims=True)
    el = logits[SUBLANES:2 * SUBLANES]
    for grp in range(1, N_GROUPS):
        el = jnp.where(gidx == grp, logits[(grp + 1) * SUBLANES:(grp + 2) * SUBLANES], el)
    m1 = jnp.max(el, axis=0, keepdims=True)
    j1 = jnp.min(jnp.where(el == m1, row8, SUBLANES), axis=0, keepdims=True)
    el2 = jnp.where(row8 == j1, neg, el)
    m2 = jnp.max(el2, axis=0, keepdims=True)
    j2 = jnp.min(jnp.where(el2 == m2, row8, SUBLANES), axis=0, keepdims=True)
    e21 = jnp.exp(m2 - m1)
    den = 1.0 / (1.0 + e21)
    cw1 = gprob * den
    cw2 = gprob * e21 * den
    yield

    erow = lax.broadcasted_iota(jnp.int32, (N_EXPERTS, TS), 0)
    oh1 = erow == gidx * EXPERTS_PER_GROUP + j1
    oh2 = erow == gidx * EXPERTS_PER_GROUP + j2
    oh = jnp.where(oh1 | oh2, 1.0, 0.0)
    cnt = jnp.broadcast_to(jnp.sum(oh, axis=1, keepdims=True), (N_EXPERTS, LANES))
    run = jnp.floor((cnt + (ROW_GROUP - 1.0)) * (1.0 / ROW_GROUP)) * ROW_GROUP
    erow_l = lax.broadcasted_iota(jnp.int32, (N_EXPERTS, LANES), 0)
    run_end = run
    shift = 1
    while shift < N_EXPERTS:
        run_end = run_end + jnp.where(erow_l >= shift, pltpu.roll(run_end, shift, 0), 0.0)
        shift *= 2
    run_start = (run_end - run)[:, 0:1]
    ri = lax.broadcasted_iota(jnp.int32, (TS, TS), 0)
    ci = lax.broadcasted_iota(jnp.int32, (TS, TS), 1)
    earlier = _bf(jnp.where(ri < ci, 1.0, 0.0))
    before = jnp.dot(_bf(oh), earlier, preferred_element_type=jnp.float32) + run_start
    pos1 = jnp.sum(jnp.where(oh1, before, 0.0), axis=0, keepdims=True)
    pos2 = jnp.sum(jnp.where(oh2, before, 0.0), axis=0, keepdims=True)
    rr = lax.broadcasted_iota(jnp.int32, (TR, TS), 0).astype(jnp.float32)
    perm = _bf(jnp.where((rr == pos1) | (rr == pos2), 1.0, 0.0))
    yield
    xt_ref[...] = _pack_rows(jnp.dot(perm, u2_hi, preferred_element_type=jnp.float32))
    cnt_ref[...] = cnt

    rowl = lax.broadcasted_iota(jnp.int32, (LANES, TS), 0)
    rec = jnp.where(rowl == 0, pos1, 0.0)
    rec = jnp.where(rowl == 1, pos2, rec)
    rec = jnp.where(rowl == 2, cw1, rec)
    rec = jnp.where(rowl == 3, cw2, rec)
    route_ref[...] = rec.T


def _mix(x, positions, ada, rope, win, wout, wpool, gnw, pscale, ln1w, ln1b, wrt, brt):
    B, S, D = x.shape
    ns = S // TS
    assert B % SEQS_PER_STEP == 0
    P = SEQS_PER_STEP
    const2 = lambda b, s: (0, 0)
    const3 = lambda b, s: (0, 0, 0)
    tile = lambda b, s: (b, s, 0)
    flat = lambda b, s: (b, s, 0, 0)
    return pl.pallas_call(
        _mix_kernel,
        grid=(B // P, ns),
        in_specs=[
            pl.BlockSpec((P, TS, D), tile),
            pl.BlockSpec((P, None, 1, TS), flat),
            pl.BlockSpec((P, 6, D), lambda b, s: (b, 0, 0)),
            pl.BlockSpec((2, LANES), const2),
            pl.BlockSpec((D, IN_COLS), const2, pipeline_mode=pl.Buffered(1)),
            pl.BlockSpec((D, D), const2, pipeline_mode=pl.Buffered(1)),
            pl.BlockSpec((len(POOL_WINDOWS), POOL_GROUP_DIM, POOL_GROUP_DIM), const3),
            pl.BlockSpec((1, RET_WIDTH), const2),
            pl.BlockSpec((1, POOL_WIDTH), const2),
            pl.BlockSpec((1, D), const2),
            pl.BlockSpec((1, D), const2),
            pl.BlockSpec((ROUTE_ROWS, D), const2),
            pl.BlockSpec((ROUTE_ROWS, 1), const2),
        ],
        out_specs=[
            pl.BlockSpec((P, TS, D), tile),
            pl.BlockSpec((P, None, 2 * TR, HALF), flat),
            pl.BlockSpec((P, TS, LANES), tile),
            pl.BlockSpec((P, None, N_EXPERTS, LANES), flat),
        ],
        out_shape=[
            jax.ShapeDtypeStruct((B, S, D), jnp.float32),
            jax.ShapeDtypeStruct((B, ns, 2 * TR, HALF), jnp.bfloat16),
            jax.ShapeDtypeStruct((B, S, LANES), jnp.float32),
            jax.ShapeDtypeStruct((B, ns, N_EXPERTS, LANES), jnp.float32),
        ],
        scratch_shapes=[
            pltpu.VMEM((P, RET_HEADS, HEAD_DIM, HEAD_DIM), jnp.float32),
            pltpu.VMEM((P, POOL_HALO, POOL_WIDTH), jnp.float32),
            pltpu.VMEM((RET_HEADS, TS, TS), jnp.float32),
            pltpu.VMEM((TS, RET_WIDTH), jnp.float32),
            pltpu.VMEM((TS, RET_WIDTH), jnp.float32),
            pltpu.VMEM((P, TS, D), jnp.bfloat16),
        ],
        compiler_params=pltpu.CompilerParams(dimension_semantics=("arbitrary", "arbitrary"),
                                             vmem_limit_bytes=VMEM_LIMIT),
        name="mix",
    )(x, positions.reshape(B, ns, 1, TS), ada, rope, win, wout, wpool, gnw, pscale, ln1w, ln1b, wrt, brt)


def _fetch_groups(src_ref, group_of, dst_ref, sem):
    for u in range(dst_ref.shape[0]):
        pltpu.make_async_copy(src_ref.at[group_of(u)], dst_ref.at[u], sem).start()


def _fetch_groups_task(src_ref, group_of, dst_ref, sem, chunk):
    for u in range(dst_ref.shape[0]):
        pltpu.make_async_copy(src_ref.at[group_of(u)], dst_ref.at[u], sem).start()
        if u % chunk == chunk - 1:
            yield


def _wait_fetch(src_ref, dst_ref, sem):
    pltpu.make_async_copy(src_ref.at[pl.ds(0, dst_ref.shape[0])], dst_ref, sem).wait()


def _expert_rows(x_ref, w1_ref, w3_ref, w2_ref, y_ref):
    n = PAGE_SPLIT_ROWS
    xl, xh = _unpack_rows(x_ref[...].reshape(2 * n, HALF))
    yield
    a = (jnp.dot(xl, w1_ref[:HALF, :], preferred_element_type=jnp.float32)
         + jnp.dot(xh, w1_ref[HALF:, :], preferred_element_type=jnp.float32))
    c = (jnp.dot(xl, w3_ref[:HALF, :], preferred_element_type=jnp.float32)
         + jnp.dot(xh, w3_ref[HALF:, :], preferred_element_type=jnp.float32))
    yield
    h = _bf(a * jax.nn.sigmoid(a) * c)
    yield
    y = jnp.dot(h, w2_ref[...], preferred_element_type=jnp.float32)
    yield
    y_ref[...] = _pack_rows(y)


def _round_robin(tasks):
    while tasks:
        tasks = [t for t in tasks if next(t, True) is None]


def _expert_kernel(used_ref, pe_ref, pv_ref, first_ref, wslot_ref, next_ref, src_ref, xt_ref, w1_hbm, w3_hbm, w2_hbm,
                   ys_ref, xbuf, ybuf, zbuf, w1_buf, w3_buf, w2_buf, w1_bf, w3_bf, w2_bf, sem, osem, wsem, zsem):
    n_used = used_ref[0]
    n_pages = ys_ref.shape[0] // (2 * TM)
    chunks = TM // PAGE_CHUNK
    per_chunk = PAGE_CHUNK // ROW_GROUP

    def fetch(page):
        @pl.when(page < n_used)
        def _start():
            slot = page % PAGE_FETCH_SLOTS
            for u in range(chunks):
                pltpu.make_async_copy(xt_ref.at[pl.ds(src_ref[page * chunks + u], per_chunk)],
                                      xbuf.at[slot, pl.ds(u * per_chunk, per_chunk)], sem.at[slot]).start()

    def weight_copies(expert, ws):
        return [pltpu.make_async_copy(hbm.at[expert], buf.at[ws], wsem.at[ws])
                for hbm, buf in ((w1_hbm, w1_buf), (w3_hbm, w3_buf), (w2_hbm, w2_buf))]

    def page_out(page, oslot):
        return pltpu.make_async_copy(ybuf.at[oslot], ys_ref.at[pl.ds(pl.multiple_of(page * 2 * TM, 2 * TM), 2 * TM)],
                                     osem.at[oslot])

    for k in range(PAGE_FETCH_AHEAD):
        fetch(k)

    zbuf[...] = jnp.zeros_like(zbuf)

    def zero_page(page):
        return pltpu.make_async_copy(zbuf, ys_ref.at[pl.ds(pl.multiple_of(page * 2 * TM, 2 * TM), 2 * TM)], zsem)

    def zero_start(page, carry):
        zero_page(page).start()
        return carry

    lax.fori_loop(n_used, n_pages, zero_start, 0)

    @pl.when(n_used > 0)
    def _first_weights():
        for c in weight_copies(pe_ref[0], wslot_ref[0]):
            c.start()

    def page_body(g, carry):
        fetch(g + PAGE_FETCH_AHEAD)
        ws = wslot_ref[g]

        @pl.when(first_ref[g] == 1)
        def _new_expert():
            for c in weight_copies(pe_ref[g], ws):
                c.wait()
            w1_bf[...] = _bf(w1_buf[ws])
            w3_bf[...] = _bf(w3_buf[ws])
            w2_bf[...] = _bf(w2_buf[ws])

            @pl.when(next_ref[g] >= 0)
            def _stream_next():
                for c in weight_copies(next_ref[g], 1 - ws):
                    c.start()

        slot = g % PAGE_FETCH_SLOTS
        oslot = g % 2
        _wait_fetch(xt_ref, xbuf.at[slot], sem.at[slot])

        @pl.when(g >= 2)
        def _reuse_out_buffer():
            page_out(g - 2, oslot).wait()

        per = PAGE_SPLIT_ROWS // ROW_GROUP
        parts = TM // PAGE_SPLIT_ROWS
        part_rows = lambda k: pl.ds(k * 2 * PAGE_SPLIT_ROWS, 2 * PAGE_SPLIT_ROWS)
        part = lambda k: _expert_rows(xbuf.at[slot, pl.ds(k * per, per)], w1_bf, w3_bf, w2_bf,
                                      ybuf.at[oslot, part_rows(k)])
        n_parts = (pv_ref[g] + PAGE_SPLIT_ROWS - 1) // PAGE_SPLIT_ROWS
        for n in range(1, parts + 1):
            @pl.when(n_parts == n)
            def _parts(n=n):
                _round_robin([part(k) for k in range(n)])
                for k in range(n, parts):
                    ybuf[oslot, part_rows(k)] = jnp.zeros((2 * PAGE_SPLIT_ROWS, HALF), ybuf.dtype)
        page_out(g, oslot).start()
        return carry

    lax.fori_loop(0, n_used, page_body, 0)

    for back in (2, 1):
        @pl.when(n_used >= back)
        def _drain(back=back):
            page_out(n_used - back, (n_used - back) % 2).wait()

    def zero_wait(page, carry):
        zero_page(page).wait()
        return carry

    lax.fori_loop(n_used, n_pages, zero_wait, 0)


def _experts(page_tables, esrc, xt, w1, w3, w2):
    D = D_MODEL
    n_pages = page_tables[1].shape[0]
    assert TM % PAGE_SPLIT_ROWS == 0
    grid_spec = pltpu.PrefetchScalarGridSpec(
        num_scalar_prefetch=len(page_tables) + 1,
        grid=(1,),
        in_specs=[pl.BlockSpec(memory_space=pl.ANY)] * 4,
        out_specs=pl.BlockSpec(memory_space=pl.ANY),
        scratch_shapes=[pltpu.VMEM((PAGE_FETCH_SLOTS, TM // ROW_GROUP, 2 * ROW_GROUP, HALF), jnp.bfloat16),
                        pltpu.VMEM((2, 2 * TM, HALF), jnp.bfloat16),
                        pltpu.VMEM((2 * TM, HALF), jnp.bfloat16),
                        pltpu.VMEM((2, D, D_EXPERT), jnp.float32),
                        pltpu.VMEM((2, D, D_EXPERT), jnp.float32),
                        pltpu.VMEM((2, D_EXPERT, D), jnp.float32),
                        pltpu.VMEM((D, D_EXPERT), jnp.bfloat16),
                        pltpu.VMEM((D, D_EXPERT), jnp.bfloat16),
                        pltpu.VMEM((D_EXPERT, D), jnp.bfloat16),
                        pltpu.SemaphoreType.DMA((PAGE_FETCH_SLOTS,)),
                        pltpu.SemaphoreType.DMA((2,)),
                        pltpu.SemaphoreType.DMA((2,)),
                        pltpu.SemaphoreType.DMA(())],
    )
    return pl.pallas_call(
        _expert_kernel,
        grid_spec=grid_spec,
        out_shape=jax.ShapeDtypeStruct((n_pages * 2 * TM, HALF), jnp.bfloat16),
        compiler_params=pltpu.CompilerParams(dimension_semantics=("arbitrary",), vmem_limit_bytes=VMEM_LIMIT),
        name="experts",
    )(*page_tables, esrc, xt, w1, w3, w2)


def _final_tile(y_ref, x1_ref, route_ref, ada_ref, lnw_ref, lnb_ref, o_ref):
    route = route_ref[...]
    pos1, pos2, cw1, cw2 = route[:, 0:1], route[:, 1:2], route[:, 2:3], route[:, 3:4]
    col = lax.broadcasted_iota(jnp.int32, (TS, TR), 1).astype(jnp.float32)
    wmat = _bf(jnp.where(col == pos1, cw1, 0.0) + jnp.where(col == pos2, cw2, 0.0))
    yield
    yl, yh = _unpack_rows(y_ref[...].reshape(2 * TR, HALF))
    yield
    y = jnp.concatenate([jnp.dot(wmat, yl, preferred_element_type=jnp.float32),
                         jnp.dot(wmat, yh, preferred_element_type=jnp.float32)], axis=-1)
    yield
    gate2 = ada_ref[5:6, :]
    o_ref[...] = _ln(ALPHA * x1_ref[...] + gate2 * y) * lnw_ref[...] + lnb_ref[...]


def _final_kernel(src_ref, x1_ref, route_ref, ada_ref, lnw_ref, lnb_ref, ys_ref, o_ref, ybuf, sem):
    P = TILES_PER_STEP
    i = pl.program_id(0)
    last = pl.num_programs(0) - 1
    per = TR // ROW_GROUP
    groups = P * per

    def fetch(step, dst_slot):
        _fetch_groups(ys_ref, lambda u: src_ref[step * groups + u], ybuf.at[dst_slot], sem.at[dst_slot])

    @pl.when(i == 0)
    def _prime():
        for k in range(FETCH_AHEAD):
            fetch(k, k)

    slot = i % FETCH_SLOTS
    _wait_fetch(ys_ref, ybuf.at[slot], sem.at[slot])

    ahead = jnp.minimum(i + FETCH_AHEAD, last)
    ahead_slot = (i + FETCH_AHEAD) % FETCH_SLOTS
    rows = lambda k: pl.ds(k * TS, TS)
    _round_robin([_final_tile(ybuf.at[slot, pl.ds(k * per, per)], x1_ref.at[rows(k)], route_ref.at[rows(k)], ada_ref,
                              lnw_ref, lnb_ref, o_ref.at[rows(k)]) for k in range(P)]
                 + [_fetch_groups_task(ys_ref, lambda u: src_ref[ahead * groups + u], ybuf.at[ahead_slot],
                                       sem.at[ahead_slot], FETCH_CHUNK)])

    @pl.when(i == last)
    def _drain():
        for k in range(1, FETCH_SLOTS):
            other = (i + k) % FETCH_SLOTS
            _wait_fetch(ys_ref, ybuf.at[other], sem.at[other])


def _final(fsrc, x1, route, ada, lnw, lnb, ys, seq_len):
    T, D = x1.shape
    P = TILES_PER_STEP
    rows = P * TS
    assert seq_len % rows == 0
    steps_per_seq = seq_len // rows
    n_steps = T // rows
    assert n_steps > FETCH_AHEAD
    grid_spec = pltpu.PrefetchScalarGridSpec(
        num_scalar_prefetch=1,
        grid=(n_steps,),
        in_specs=[pl.BlockSpec((rows, D), lambda i, src: (i, 0)),
                  pl.BlockSpec((rows, LANES), lambda i, src: (i, 0)),
                  pl.BlockSpec((None, 6, D), lambda i, src: (i // steps_per_seq, 0, 0)),
                  pl.BlockSpec((1, D), lambda i, src: (0, 0)),
                  pl.BlockSpec((1, D), lambda i, src: (0, 0)),
                  pl.BlockSpec(memory_space=pl.ANY)],
        out_specs=pl.BlockSpec((rows, D), lambda i, src: (i, 0)),
        scratch_shapes=[pltpu.VMEM((FETCH_SLOTS, P * (TR // ROW_GROUP), 2 * ROW_GROUP, HALF), jnp.bfloat16),
                        pltpu.SemaphoreType.DMA((FETCH_SLOTS,))],
    )
    return pl.pallas_call(
        _final_kernel,
        grid_spec=grid_spec,
        out_shape=jax.ShapeDtypeStruct((T, D), jnp.float32),
        compiler_params=pltpu.CompilerParams(dimension_semantics=("arbitrary",), vmem_limit_bytes=VMEM_LIMIT),
        name="final",
    )(fsrc, x1, route, ada, lnw, lnb, ys)


def _gather_tables(cnt, n_pages):
    nt = cnt.shape[0]
    run = (cnt + ROW_GROUP - 1) // ROW_GROUP * ROW_GROUP
    so = jnp.cumsum(run, axis=1) - run
    run_e = (cnt + PAGE_CHUNK - 1) // PAGE_CHUNK * PAGE_CHUNK
    eo = jnp.cumsum(run_e, axis=0) - run_e
    tot = jnp.sum(run_e, axis=0)
    pages_e = (tot + TM - 1) // TM
    page_end = jnp.cumsum(pages_e)
    page_start = page_end - pages_e
    g = jnp.arange(n_pages, dtype=jnp.int32)
    pe = jnp.minimum(jnp.sum(g[:, None] >= page_end[None, :], axis=1), N_EXPERTS - 1).astype(jnp.int32)
    used = g < page_end[-1]
    owner = (g[None, :] >= page_start[:, None]) & (g[None, :] < page_end[:, None])
    of_page = lambda a: jnp.sum(jnp.where(owner, a[:, None], 0), axis=0)
    of_page2 = lambda a: jnp.sum(jnp.where(owner[:, None, :], a.T[:, :, None], 0), axis=0)
    pj = g - of_page(page_start)
    tot_p = of_page(tot)
    pv = jnp.where(used, jnp.clip(tot_p - pj * TM, 0, TM), 0).astype(jnp.int32)
    experts = jnp.arange(N_EXPERTS, dtype=jnp.int32)
    has_pages = pages_e > 0
    wslot_e = (jnp.cumsum(has_pages) - 1) % 2
    later = (experts[None, :] > experts[:, None]) & has_pages[None, :]
    next_e = jnp.min(jnp.where(later, experts[None, :], N_EXPERTS), axis=1)
    next_e = jnp.where(next_e < N_EXPERTS, next_e, -1)
    first = (used & (pj == 0)).astype(jnp.int32)
    page_tables = (page_end[-1:].astype(jnp.int32), pe, pv, first, of_page(wslot_e).astype(jnp.int32),
                   jnp.where(used, of_page(next_e), -1).astype(jnp.int32))

    q = (pj * TM)[:, None] + PAGE_CHUNK * jnp.arange(TM // PAGE_CHUNK, dtype=jnp.int32)[None, :]
    tiles = jnp.arange(nt, dtype=jnp.int32)
    offset = of_page2(tiles[:, None] * TR + so - eo)
    step = jnp.concatenate([offset[1:] - offset[:-1], jnp.zeros_like(offset[:1])], axis=0)
    run_end = of_page2(eo + run_e)
    src = q + offset[0][:, None] + jnp.sum(jnp.where(run_end[:, :, None] <= q[None], step[:, :, None], 0), axis=0)
    zero_xt = TR - PAGE_CHUNK
    esrc = jnp.where((q < tot_p[:, None]) & used[:, None], src, zero_xt).astype(jnp.int32)

    r = ROW_GROUP * jnp.arange(TR // ROW_GROUP, dtype=jnp.int32)
    end = (so + run).T
    offset_f = ((page_start * TM)[None, :] + eo - so).T
    step_f = jnp.concatenate([offset_f[1:] - offset_f[:-1], jnp.zeros_like(offset_f[:1])], axis=0)
    srcf = (r[None, :] + offset_f[0][:, None]
            + jnp.sum(jnp.where(end[:, :, None] <= r[None, None, :], step_f[:, :, None], 0), axis=0))
    zero_ys = (n_pages - 1) * TM
    fsrc = jnp.where(r[None, :] < end[-1][:, None], srcf, zero_ys).astype(jnp.int32)
    return page_tables, esrc.reshape(-1) // ROW_GROUP, fsrc.reshape(-1) // ROW_GROUP


def kernel(x, c, positions, w_ada, b_ada, w_in, ret_gn_w, w_pool, pool_scale, w_out, ln1_w, ln1_b, w_group, b_group,
           w_router, b_router, w1, w3, w2, ln2_w, ln2_b):
    B, S, D = x.shape
    T = B * S
    assert w_ada.shape[0] == DEPTH and D == D_MODEL and S % TS == 0

    inv_freq = ROPE_BASE ** (-jnp.arange(0, HEAD_DIM, 2, dtype=jnp.float32) / HEAD_DIM)
    half = HEAD_DIM // 2
    rope = jnp.stack([jnp.concatenate([inv_freq, inv_freq]),
                      jnp.concatenate([-jnp.ones((half,), jnp.float32), jnp.ones((half,), jnp.float32)])])

    nt = T // TS
    n_pages = (2 * T + nt * N_EXPERTS * (PAGE_CHUNK - 1)) // TM + N_EXPERTS + 1

    for l in range(DEPTH):
        ada = _ada(c, w_ada[l], b_ada[l]).reshape(B, 6, D)
        pad = SUBLANES - N_GROUPS
        wrt = jnp.concatenate([w_group[l].T, jnp.zeros((pad, D), jnp.float32), w_router[l].T], axis=0)
        brt = jnp.concatenate([b_group[l], jnp.zeros((pad,), jnp.float32), b_router[l]]).reshape(ROUTE_ROWS, 1)
        x1, xt, route, counts = _mix(
            x, positions, ada, rope, _bf(w_in[l]), _bf(w_out[l]), _bf(w_pool[l]),
            ret_gn_w[l].reshape(1, RET_WIDTH), pool_scale[l].reshape(1, POOL_WIDTH),
            ln1_w[l].reshape(1, D), ln1_b[l].reshape(1, D), wrt, brt)

        cnt = counts[:, :, :, 0].reshape(nt, N_EXPERTS).astype(jnp.int32)
        page_tables, esrc, fsrc = _gather_tables(cnt, n_pages)
        ys = _experts(page_tables, esrc, xt.reshape(nt * TR // ROW_GROUP, 2 * ROW_GROUP, HALF),
                      w1[l].reshape(N_EXPERTS, D, D_EXPERT), w3[l].reshape(N_EXPERTS, D, D_EXPERT),
                      w2[l].reshape(N_EXPERTS, D_EXPERT, D))
        x = _final(fsrc, x1.reshape(T, D), route.reshape(T, LANES), ada,
                   ln2_w[l].reshape(1, D), ln2_b[l].reshape(1, D),
                   ys.reshape(n_pages * TM // ROW_GROUP, 2 * ROW_GROUP, HALF), S).reshape(B, S, D)
    return x
```
